```python
import math
import jax, jax.numpy as jnp
from jax import lax
import numpy as np

D_MODEL = 2048
BATCH = 4
SEQ = 2048
DEPTH = 2
DEC_BATCH = 128
DEC_SEQ = 4
PAST_LEN = 16384
PAGE_SIZE = 128

N_EVEN = (DEPTH + 1) // 2
N_ODD = DEPTH // 2

HG_HEADS = 16
HG_DK = 128
HG_DV = D_MODEL // HG_HEADS
HG_KW = HG_HEADS * HG_DK
HG_VW = HG_HEADS * HG_DV
HG_CHUNK = 64

M_DI = D_MODEL
M_HEADDIM = 64
M_HEADS = M_DI // M_HEADDIM
M_GROUPS = 8
M_HPG = M_HEADS // M_GROUPS
M_STATE = 128
M_CONV = 4
M_CONV_DIM = M_DI + 2 * M_GROUPS * M_STATE
M_CHUNK = 128

IN0_WIDTH = 2 * HG_KW + 2 * HG_VW + M_DI + M_CONV_DIM + M_HEADS
MIX0_WIDTH = HG_VW + M_DI

S5_GROUP = 16
S5_GROUPS = D_MODEL // S5_GROUP
S5_P = 64

N_EXPERTS = 16
N_EXP_GROUPS = 4
EXP_PER_GROUP = N_EXPERTS // N_EXP_GROUPS
TOP_K = 2
D_EXPERT = 512

ALPHA = (2 * DEPTH) ** 0.25
BETA = (8 * DEPTH) ** -0.25
LN_EPS = 1e-5
RMS_EPS = 1e-6

kernel_name = "hgrn2_mamba2_s5_deepnorm_moe_step"

F32 = jnp.float32


def layer_norm(x, g, b):
    xf = x.astype(F32)
    mu = jnp.mean(xf, -1, keepdims=True)
    var = jnp.mean(jnp.square(xf - mu), -1, keepdims=True)
    return ((xf - mu) * lax.rsqrt(var + LN_EPS) * g + b).astype(x.dtype)


def hgrn2_recurrence(q, k, v, logf, s0):
    b, L = q.shape[:2]
    c = math.gcd(L, HG_CHUNK)
    n = L // c

    def to_chunks(t):
        return t.reshape(b, n, c, HG_HEADS, t.shape[-1]).transpose(1, 0, 3, 2, 4)

    causal = jnp.tril(jnp.ones((c, c), bool))

    def step(s, inp):
        qi, ki, vi, gi = inp
        cum = jnp.cumsum(gi, axis=2)
        o_inter = jnp.einsum('bhtd,bhde->bhte', qi * jnp.exp(cum), s)
        diff = cum[:, :, :, None, :] - cum[:, :, None, :, :]
        decay = jnp.exp(jnp.where(causal[:, :, None], diff, -jnp.inf))
        scores = jnp.einsum('bhtd,bhsd,bhtsd->bhts', qi, ki, decay)
        o_intra = jnp.einsum('bhts,bhse->bhte', scores, vi)
        last = cum[:, :, -1:, :]
        s_new = jnp.exp(last[:, :, 0, :, None]) * s + jnp.einsum('bhsd,bhse->bhde', ki * jnp.exp(last - cum), vi)
        return s_new, o_inter + o_intra

    s_fin, o = lax.scan(step, s0, (to_chunks(q), to_chunks(k), to_chunks(v), to_chunks(logf)))
    o = o.transpose(1, 0, 3, 2, 4).reshape(b, L, HG_HEADS, HG_DV)
    return o, s_fin


def ssd_scan(xdt, dA, Bm, Cm, s0):
    b, L = xdt.shape[:2]
    c = math.gcd(L, M_CHUNK)
    n = L // c
    xc = xdt.reshape(b, n, c, M_GROUPS, M_HPG, M_HEADDIM)
    Bc = Bm.reshape(b, n, c, M_GROUPS, M_STATE)
    Cc = Cm.reshape(b, n, c, M_GROUPS, M_STATE)
    cum = jnp.cumsum(dA.reshape(b, n, c, M_GROUPS, M_HPG).transpose(0, 1, 3, 4, 2), axis=-1)
    causal = jnp.tril(jnp.ones((c, c), bool))
    decay = jnp.exp(jnp.where(causal, cum[..., :, None] - cum[..., None, :], -jnp.inf))
    cb = jnp.einsum('bnlgk,bnsgk->bngls', Cc, Bc)
    y_intra = jnp.einsum('bngls,bnghls,bnsghp->bnlghp', cb, decay, xc)
    to_end = jnp.exp(cum[..., -1:] - cum)
    chunk_states = jnp.einsum('bnsgk,bnghs,bnsghp->bnghpk', Bc, to_end, xc)
    chunk_decay = jnp.exp(cum[..., -1])

    def step(s, inp):
        st, dec = inp
        return dec[..., None, None] * s + st, s

    s_fin, s_in = lax.scan(step, s0, (chunk_states.transpose(1, 0, 2, 3, 4, 5), chunk_decay.transpose(1, 0, 2, 3)))
    s_in = s_in.transpose(1, 0, 2, 3, 4, 5)
    y_inter = jnp.einsum('bnlgk,bnghl,bnghpk->bnlghp', Cc, jnp.exp(cum), s_in)
    y = (y_intra + y_inter).reshape(b, L, M_GROUPS, M_HPG, M_HEADDIM)
    return y, s_fin


def causal_conv(u, buf, w, bias):
    full = jnp.concatenate([buf.astype(u.dtype), u], axis=1)
    L = u.shape[1]
    out = sum(full[:, j:j + L] * w[j] for j in range(M_CONV)) + bias
    return jax.nn.silu(out), full[:, -(M_CONV - 1):]


def hybrid_mixer(x, s_hg, s_ssm, conv_buf, w_in, lb, hg_norm_w, conv_w, conv_b,
                 dt_bias, a_log, m_d, m_norm_w, w_out):
    b, L, _ = x.shape
    proj = x @ w_in
    o1 = HG_KW
    o2 = o1 + HG_KW
    o3 = o2 + HG_VW
    o4 = o3 + HG_VW
    o5 = o4 + M_DI
    o6 = o5 + M_CONV_DIM
    q, f, v, g, z, xbc, dt = jnp.split(proj, [o1, o2, o3, o4, o5, o6], axis=-1)

    fg = lb + (1.0 - lb) * jax.nn.sigmoid(f.astype(F32))
    hshape = (b, L, HG_HEADS, HG_DK)
    qh = jax.nn.silu(q.astype(F32)).reshape(hshape)
    kh = (1.0 - fg).reshape(hshape)
    logf = jnp.log(fg).reshape(hshape)
    vh = v.astype(F32).reshape(b, L, HG_HEADS, HG_DV)
    o, s_hg_new = hgrn2_recurrence(qh, kh, vh, logf, s_hg.astype(F32))
    o = o * lax.rsqrt(jnp.mean(o * o, -1, keepdims=True) + RMS_EPS)
    o = o.reshape(b, L, HG_VW) * hg_norm_w * jax.nn.silu(g.astype(F32))

    xbc, conv_new = causal_conv(xbc, conv_buf, conv_w, conv_b)
    xm, Bm, Cm = jnp.split(xbc.astype(F32), [M_DI, M_DI + M_GROUPS * M_STATE], axis=-1)
    dtp = jax.nn.softplus(dt.astype(F32) + dt_bias).reshape(b, L, M_GROUPS, M_HPG)
    A = -jnp.exp(a_log.astype(F32)).reshape(M_GROUPS, M_HPG)
    xm = xm.reshape(b, L, M_GROUPS, M_HPG, M_HEADDIM)
    y, s_ssm_new = ssd_scan(xm * dtp[..., None], dtp * A,
                            Bm.reshape(b, L, M_GROUPS, M_STATE), Cm.reshape(b, L, M_GROUPS, M_STATE),
                            s_ssm.astype(F32).reshape(b, M_GROUPS, M_HPG, M_HEADDIM, M_STATE))
    y = y + m_d.astype(F32).reshape(M_GROUPS, M_HPG)[..., None] * xm
    y = y.reshape(b, L, M_DI) * jax.nn.silu(z.astype(F32))
    yg = y.reshape(b, L, M_GROUPS, M_DI // M_GROUPS)
    yg = yg * lax.rsqrt(jnp.mean(yg * yg, -1, keepdims=True) + RMS_EPS)
    y = yg.reshape(b, L, M_DI) * m_norm_w

    out = jnp.concatenate([o, y], axis=-1).astype(x.dtype) @ w_out
    return (out, s_hg_new.astype(s_hg.dtype),
            s_ssm_new.reshape(b, M_HEADS, M_HEADDIM, M_STATE).astype(s_ssm.dtype),
            conv_new.astype(conv_buf.dtype))


def s5_mixer(x, s_re, s_im, a_re, a_im, log_dt, b_re, b_im, c_re, c_im, d_skip,
             w_a, b_a, w_b, b_b):
    bsz, L, _ = x.shape
    xf = x.astype(F32)
    u = xf.reshape(bsz, L, S5_GROUPS, S5_GROUP)
    lam_re = jnp.minimum(a_re.astype(F32), -1e-4)
    lam_im = a_im.astype(F32)
    dt = jnp.exp(log_dt.astype(F32))[:, None]
    mag = jnp.exp(lam_re * dt)
    ab_re = mag * jnp.cos(lam_im * dt)
    ab_im = mag * jnp.sin(lam_im * dt)
    den = lam_re * lam_re + lam_im * lam_im
    zr = ((ab_re - 1.0) * lam_re + ab_im * lam_im) / den
    zi = (ab_im * lam_re - (ab_re - 1.0) * lam_im) / den
    br = b_re.astype(F32)
    bi = b_im.astype(F32)
    bb_re = zr[..., None] * br - zi[..., None] * bi
    bb_im = zr[..., None] * bi + zi[..., None] * br
    bu_re = jnp.einsum('gph,blgh->blgp', bb_re, u)
    bu_im = jnp.einsum('gph,blgh->blgp', bb_im, u)
    sr = s_re.astype(F32)
    si = s_im.astype(F32)
    bu_re = bu_re.at[:, 0].add(ab_re * sr - ab_im * si)
    bu_im = bu_im.at[:, 0].add(ab_re * si + ab_im * sr)
    a_re_b = jnp.broadcast_to(ab_re, bu_re.shape)
    a_im_b = jnp.broadcast_to(ab_im, bu_im.shape)

    def combine(e1, e2):
        a1r, a1i, b1r, b1i = e1
        a2r, a2i, b2r, b2i = e2
        return (a2r * a1r - a2i * a1i, a2r * a1i + a2i * a1r,
                a2r * b1r - a2i * b1i + b2r, a2r * b1i + a2i * b1r + b2i)

    _, _, hr, hi = lax.associative_scan(combine, (a_re_b, a_im_b, bu_re, bu_im), axis=1)
    y = jnp.einsum('ghp,blgp->blgh', c_re.astype(F32), hr) - jnp.einsum('ghp,blgp->blgh', c_im.astype(F32), hi)
    y = y.reshape(bsz, L, D_MODEL) + d_skip * xf
    act = jax.nn.gelu(y)
    out = (act @ w_a + b_a) * jax.nn.sigmoid(act @ w_b + b_b)
    return out.astype(x.dtype), hr[:, -1].astype(s_re.dtype), hi[:, -1].astype(s_im.dtype)


def moe(x, w_router, b_router, w_gate, w_up, w_down):
    bsz, L, D = x.shape
    t = x.reshape(-1, D)
    logits = (t @ w_router).astype(F32) + b_router
    probs = jax.nn.softmax(logits, axis=-1)
    pg = probs.reshape(-1, N_EXP_GROUPS, EXP_PER_GROUP)
    group_score = jnp.sum(lax.top_k(pg, TOP_K)[0], axis=-1)
    gsel = jnp.argmax(group_score, axis=-1)
    in_group = jnp.take_along_axis(pg, gsel[:, None, None], axis=1)[:, 0]
    top_p, top_i = lax.top_k(in_group, TOP_K)
    top_w = top_p / jnp.sum(top_p, -1, keepdims=True)
    expert_idx = gsel[:, None] * EXP_PER_GROUP + top_i
    gates = jnp.sum(jax.nn.one_hot(expert_idx, N_EXPERTS, dtype=F32) * top_w[..., None], axis=1)
    h = jnp.einsum('td,edf->tef', t, w_gate)
    up = jnp.einsum('td,edf->tef', t, w_up)
    act = jax.nn.silu(h) * up * gates[..., None].astype(t.dtype)
    y = jnp.einsum('tef,efd->td', act, w_down)
    return y.reshape(bsz, L, D).astype(x.dtype)


def trunk(x, st_hg, st_ssm, st_conv, st_s5r, st_s5i,
          w_in0, hg_lb_logits, hg_norm_w, conv_w, conv_b, dt_bias, a_log, m_d, m_norm_w, w_out0,
          s5_a_re, s5_a_im, s5_log_dt, s5_b_re, s5_b_im, s5_c_re, s5_c_im, s5_d,
          glu_w_a, glu_b_a, glu_w_b, glu_b_b,
          w_router, b_router, w_gate, w_up, w_down, ln1_g, ln1_b, ln2_g, ln2_b):
    lower_bounds = jnp.cumsum(jax.nn.softmax(hg_lb_logits.astype(F32), axis=0), axis=0)
    new_hg, new_ssm, new_conv, new_s5r, new_s5i = [], [], [], [], []
    for layer in range(DEPTH):
        if layer % 2 == 0:
            e = layer // 2
            mix, s_hg, s_ssm, s_conv = hybrid_mixer(
                x, st_hg[e], st_ssm[e], st_conv[e], w_in0[e], lower_bounds[layer], hg_norm_w[e],
                conv_w[e], conv_b[e], dt_bias[e], a_log[e], m_d[e], m_norm_w[e], w_out0[e])
            new_hg.append(s_hg)
            new_ssm.append(s_ssm)
            new_conv.append(s_conv)
        else:
            o = layer // 2
            mix, s_r, s_i = s5_mixer(
                x, st_s5r[o], st_s5i[o], s5_a_re[o], s5_a_im[o], s5_log_dt[o], s5_b_re[o], s5_b_im[o],
                s5_c_re[o], s5_c_im[o], s5_d[o], glu_w_a[o], glu_b_a[o], glu_w_b[o], glu_b_b[o])
            new_s5r.append(s_r)
            new_s5i.append(s_i)
        x = layer_norm(ALPHA * x + mix, ln1_g[layer], ln1_b[layer])
        x = layer_norm(ALPHA * x + moe(x, w_router, b_router, w_gate[layer], w_up[layer], w_down[layer]),
                       ln2_g[layer], ln2_b[layer])
    return (x, jnp.stack(new_hg), jnp.stack(new_ssm), jnp.stack(new_conv),
            jnp.stack(new_s5r), jnp.stack(new_s5i))


def setup_inputs(seed: int = 0) -> dict:
    key = jax.random.key(seed)
    ks = iter(jax.random.split(key, 48))

    def nrm(shape, scale):
        return jax.random.normal(next(ks), shape, F32) * scale

    dt0 = jnp.exp(jax.random.uniform(next(ks), (N_EVEN, M_HEADS), F32, math.log(1e-3), math.log(1e-1)))
    dt_bias = dt0 + jnp.log(-jnp.expm1(-dt0))
    a_log = jnp.log(jax.random.uniform(next(ks), (N_EVEN, M_HEADS), F32, 1.0, 16.0))
    s5_log_dt = jax.random.uniform(next(ks), (N_ODD, S5_GROUPS), F32, math.log(1e-3), math.log(1e-1))
    s5_a_im = jnp.broadcast_to(jnp.pi * jnp.arange(S5_P, dtype=F32), (N_ODD, S5_GROUPS, S5_P)) \
        + nrm((N_ODD, S5_GROUPS, S5_P), 0.01)
    return {
        "x_prompt": nrm((BATCH, SEQ, D_MODEL), 1.0),
        "x_sample": nrm((DEC_BATCH, DEC_SEQ, D_MODEL), 1.0),
        "state_hgrn": nrm((N_EVEN, DEC_BATCH, HG_HEADS, HG_DK, HG_DV), 0.5),
        "state_ssm": nrm((N_EVEN, DEC_BATCH, M_HEADS, M_HEADDIM, M_STATE), 0.1),
        "state_conv": nrm((N_EVEN, DEC_BATCH, M_CONV - 1, M_CONV_DIM), 1.0),
        "state_s5_re": nrm((N_ODD, DEC_BATCH, S5_GROUPS, S5_P), 0.1),
        "state_s5_im": nrm((N_ODD, DEC_BATCH, S5_GROUPS, S5_P), 0.1),
        "w_in0": nrm((N_EVEN, D_MODEL, IN0_WIDTH), D_MODEL ** -0.5),
        "hg_lb_logits": nrm((DEPTH + 1, HG_KW), 0.1),
        "hg_norm_w": 1.0 + nrm((N_EVEN, HG_VW), 0.02),
        "conv_w": nrm((N_EVEN, M_CONV, M_CONV_DIM), M_CONV ** -0.5),
        "conv_b": nrm((N_EVEN, M_CONV_DIM), 0.02),
        "dt_bias": dt_bias,
        "a_log": a_log,
        "m_d": 1.0 + nrm((N_EVEN, M_HEADS), 0.1),
        "m_norm_w": 1.0 + nrm((N_EVEN, M_DI), 0.02),
        "w_out0": nrm((N_EVEN, MIX0_WIDTH, D_MODEL), MIX0_WIDTH ** -0.5 * BETA),
        "s5_a_re": -0.5 + nrm((N_ODD, S5_GROUPS, S5_P), 0.01),
        "s5_a_im": s5_a_im,
        "s5_log_dt": s5_log_dt,
        "s5_b_re": nrm((N_ODD, S5_GROUPS, S5_P, S5_GROUP), (2 * S5_GROUP) ** -0.5),
        "s5_b_im": nrm((N_ODD, S5_GROUPS, S5_P, S5_GROUP), (2 * S5_GROUP) ** -0.5),
        "s5_c_re": nrm((N_ODD, S5_GROUPS, S5_GROUP, S5_P), S5_P ** -0.5),
        "s5_c_im": nrm((N_ODD, S5_GROUPS, S5_GROUP, S5_P), S5_P ** -0.5),
        "s5_d": nrm((N_ODD, D_MODEL), 1.0),
        "glu_w_a": nrm((N_ODD, D_MODEL, D_MODEL), D_MODEL ** -0.5 * BETA),
        "glu_b_a": nrm((N_ODD, D_MODEL), 0.02),
        "glu_w_b": nrm((N_ODD, D_MODEL, D_MODEL), D_MODEL ** -0.5),
        "glu_b_b": nrm((N_ODD, D_MODEL), 0.02),
        "w_router": nrm((D_MODEL, N_EXPERTS), D_MODEL ** -0.5),
        "b_router": nrm((N_EXPERTS,), 0.01),
        "w_gate": nrm((DEPTH, N_EXPERTS, D_MODEL, D_EXPERT), D_MODEL ** -0.5),
        "w_up": nrm((DEPTH, N_EXPERTS, D_MODEL, D_EXPERT), D_MODEL ** -0.5),
        "w_down": nrm((DEPTH, N_EXPERTS, D_EXPERT, D_MODEL), D_EXPERT ** -0.5 * BETA),
        "ln1_g": 1.0 + nrm((DEPTH, D_MODEL), 0.02),
        "ln1_b": nrm((DEPTH, D_MODEL), 0.02),
        "ln2_g": 1.0 + nrm((DEPTH, D_MODEL), 0.02),
        "ln2_b": nrm((DEPTH, D_MODEL), 0.02),
    }


def reference(x_prompt, x_sample, state_hgrn, state_ssm, state_conv, state_s5_re, state_s5_im,
              w_in0, hg_lb_logits, hg_norm_w, conv_w, conv_b, dt_bias, a_log, m_d, m_norm_w, w_out0,
              s5_a_re, s5_a_im, s5_log_dt, s5_b_re, s5_b_im, s5_c_re, s5_c_im, s5_d,
              glu_w_a, glu_b_a, glu_w_b, glu_b_b,
              w_router, b_router, w_gate, w_up, w_down, ln1_g, ln1_b, ln2_g, ln2_b):
    params = (w_in0, hg_lb_logits, hg_norm_w, conv_w, conv_b, dt_bias, a_log, m_d, m_norm_w, w_out0,
              s5_a_re, s5_a_im, s5_log_dt, s5_b_re, s5_b_im, s5_c_re, s5_c_im, s5_d,
              glu_w_a, glu_b_a, glu_w_b, glu_b_b,
              w_router, b_router, w_gate, w_up, w_down, ln1_g, ln1_b, ln2_g, ln2_b)
    bp = x_prompt.shape[0]
    zero_hg = jnp.zeros((N_EVEN, bp) + state_hgrn.shape[2:], state_hgrn.dtype)
    zero_ssm = jnp.zeros((N_EVEN, bp) + state_ssm.shape[2:], state_ssm.dtype)
    zero_conv = jnp.zeros((N_EVEN, bp) + state_conv.shape[2:], state_conv.dtype)
    zero_s5r = jnp.zeros((N_ODD, bp) + state_s5_re.shape[2:], state_s5_re.dtype)
    zero_s5i = jnp.zeros((N_ODD, bp) + state_s5_im.shape[2:], state_s5_im.dtype)
    y_prompt, hg_p, ssm_p, conv_p, s5r_p, s5i_p = trunk(
        x_prompt, zero_hg, zero_ssm, zero_conv, zero_s5r, zero_s5i, *params)
    y_sample, hg_s, ssm_s, conv_s, s5r_s, s5i_s = trunk(
        x_sample, state_hgrn, state_ssm, state_conv, state_s5_re, state_s5_im, *params)
    return (y_prompt, y_sample, hg_p, hg_s, ssm_p, ssm_s, conv_p, conv_s, s5r_p, s5r_s, s5i_p, s5i_s)
```

```python
import functools
import math

import jax
import jax.numpy as jnp
from jax import lax
from jax.experimental import pallas as pl
from jax.experimental.pallas import tpu as pltpu

F32 = jnp.float32
BF16 = jnp.bfloat16
HIGHEST = lax.Precision.HIGHEST

D_MODEL = 2048
DEPTH = 2
HG_HEADS = 16
HG_DK = 128
HG_DV = 128
HG_KW = HG_HEADS * HG_DK
HG_VW = HG_HEADS * HG_DV
HG_CHUNK = 64
M_DI = 2048
M_HEADDIM = 64
M_HEADS = 32
M_GROUPS = 8
M_HPG = 4
M_STATE = 128
M_CONV = 4
M_CONV_DIM = M_DI + 2 * M_GROUPS * M_STATE
M_CHUNK = 128
IN0_MAIN = 2 * HG_KW + 2 * HG_VW + M_DI + M_CONV_DIM
S5_GROUP = 16
S5_GROUPS = 128
S5_P = 64
S5_STATE = S5_GROUPS * S5_P
N_EXPERTS = 16
N_EXP_GROUPS = 4
EXP_PER_GROUP = 4
D_EXPERT = 512
ALPHA = (2 * DEPTH) ** 0.25
LN_EPS = 1e-5
RMS_EPS = 1e-6

V7X_LANES = 128
V7X_SUBLANES = 8
V7X_VMEM_BYTES = 64 * 1024 * 1024
V7X_SCOPED_VMEM_CAP = 60000 * 1024
COMPILER_SCRATCH_BYTES = 16 * 1024 * 1024


def _params(semantics, block_bytes):
    limit = min(2 * block_bytes + COMPILER_SCRATCH_BYTES, V7X_SCOPED_VMEM_CAP)
    return pltpu.CompilerParams(dimension_semantics=semantics, vmem_limit_bytes=int(limit))


def _nbytes(shape, dtype):
    return math.prod(shape) * jnp.dtype(dtype).itemsize


def _silu(x):
    return x * jax.nn.sigmoid(x)


def _dot(a, b, dims, precision=None):
    return lax.dot_general(a, b, (dims, ((), ())), precision=precision, preferred_element_type=F32)


NN = ((1,), (0,))
NT = ((1,), (1,))
TN = ((0,), (0,))


def _mm_body(x_ref, w_ref, o_ref):
    o_ref[...] = _dot(x_ref[...], w_ref[...], NN).astype(o_ref.dtype)


def _matmul(x, w, *, tm, tn, n_cols=None, out_dtype=F32, name):
    m, k = x.shape
    n = n_cols if n_cols is not None else w.shape[1]
    blocks = _nbytes((tm, k), x.dtype) + _nbytes((k, tn), w.dtype) + _nbytes((tm, tn), out_dtype)
    return pl.pallas_call(
        _mm_body,
        grid=(n // tn, m // tm),
        in_specs=[pl.BlockSpec((tm, k), lambda j, i: (i, 0)), pl.BlockSpec((k, tn), lambda j, i: (0, j))],
        out_specs=pl.BlockSpec((tm, tn), lambda j, i: (i, j)),
        out_shape=jax.ShapeDtypeStruct((m, n), out_dtype),
        compiler_params=_params(("arbitrary", "arbitrary"), blocks),
        name=name,
    )(x, w)


def _ln_body(x_ref, mix_ref, g_ref, b_ref, o_ref, ob_ref):
    z = ALPHA * x_ref[...] + mix_ref[...]
    mu = jnp.mean(z, -1, keepdims=True)
    zc = z - mu
    var = jnp.mean(zc * zc, -1, keepdims=True)
    y = zc * lax.rsqrt(var + LN_EPS) * g_ref[...] + b_ref[...]
    o_ref[...] = y
    ob_ref[...] = y.astype(BF16)


def _res_layernorm(x, mix, g, b, *, tm, name):
    t, d = x.shape
    row = pl.BlockSpec((tm, d), lambda i: (i, 0))
    vec = pl.BlockSpec((1, d), lambda i: (0, 0))
    blocks = 3 * _nbytes((tm, d), F32) + _nbytes((tm, d), BF16)
    return pl.pallas_call(
        _ln_body,
        grid=(t // tm,),
        in_specs=[row, row, vec, vec],
        out_specs=[row, row],
        out_shape=[jax.ShapeDtypeStruct((t, d), F32), jax.ShapeDtypeStruct((t, d), BF16)],
        compiler_params=_params(("arbitrary",), blocks),
        name=name,
    )(x, mix, g.reshape(1, d), b.reshape(1, d))


def _hgrn_gates(q, f, lb):
    fg = lb + (1.0 - lb) * jax.nn.sigmoid(f)
    return _silu(q), 1.0 - fg, jnp.log(fg)


def _hgrn_out(o, g, nw):
    o = o * lax.rsqrt(jnp.mean(o * o, -1, keepdims=True) + RMS_EPS)
    return o * nw * _silu(g)


def _hgrn_prompt_body(q_ref, f_ref, v_ref, g_ref, lb_ref, nw_ref, o_ref, sfin_ref, st_scr):
    n = pl.program_id(1)
    c = HG_CHUNK

    @pl.when(n == 0)
    def _():
        st_scr[...] = jnp.zeros_like(st_scr)

    row = lax.broadcasted_iota(jnp.int32, (c, c), 0)
    col = lax.broadcasted_iota(jnp.int32, (c, c), 1)
    causal = row >= col
    tri = causal.astype(F32)
    for h in range(HG_HEADS):
        sl = slice(h * HG_DK, (h + 1) * HG_DK)
        q, k, logf = _hgrn_gates(q_ref[:, sl], f_ref[:, sl], lb_ref[:, sl])
        v = v_ref[:, sl].astype(BF16)
        cum = _dot(tri, logf, NN, precision=HIGHEST)
        mid = cum[c // 2 - 1:c // 2, :]
        last = cum[c - 1:c, :]
        qm = (q * jnp.exp(cum - mid)).astype(BF16)
        km = (k * jnp.exp(mid - cum)).astype(BF16)
        scores = jnp.where(causal, _dot(qm, km, NT), 0.0).astype(BF16)
        st = st_scr[h]
        qi = (q * jnp.exp(cum)).astype(BF16)
        o = _dot(qi, st.astype(BF16), NT) + _dot(scores, v, NN)
        kl = (k * jnp.exp(last - cum)).astype(BF16)
        st_scr[h] = jnp.exp(last) * st + _dot(v, kl, TN)
        o_ref[:, sl] = _hgrn_out(o, g_ref[:, sl], nw_ref[:, sl]).astype(o_ref.dtype)

    @pl.when(n == pl.num_programs(1) - 1)
    def _():
        sfin_ref[0] = st_scr[...]


def _hgrn_prompt(proj, lb, nw, *, batch, seq):
    c = HG_CHUNK
    nchunk = seq // c
    blk = lambda j: pl.BlockSpec((c, HG_KW), lambda b, n: (b * nchunk + n, j))
    vec = pl.BlockSpec((1, HG_KW), lambda b, n: (0, 0))
    blocks = 4 * _nbytes((c, HG_KW), F32) + _nbytes((c, HG_VW), BF16) + 2 * _nbytes((HG_HEADS, HG_DV, HG_DK), F32)
    o, st = pl.pallas_call(
        _hgrn_prompt_body,
        grid=(batch, nchunk),
        in_specs=[blk(0), blk(1), blk(2), blk(3), vec, vec],
        out_specs=[pl.BlockSpec((c, HG_VW), lambda b, n: (b * nchunk + n, 0)),
                   pl.BlockSpec((1, HG_HEADS, HG_DV, HG_DK), lambda b, n: (b, 0, 0, 0))],
        out_shape=[jax.ShapeDtypeStruct((batch * seq, HG_VW), BF16),
                   jax.ShapeDtypeStruct((batch, HG_HEADS, HG_DV, HG_DK), F32)],
        scratch_shapes=[pltpu.VMEM((HG_HEADS, HG_DV, HG_DK), F32)],
        compiler_params=_params(("arbitrary", "arbitrary"), blocks),
        name="hgrn_prompt",
    )(proj, proj, proj, proj, lb.reshape(1, HG_KW), nw.reshape(1, HG_VW))
    return o, jnp.swapaxes(st, -1, -2)


HG_SAMPLE_BB = 8


def _hgrn_sample_body(q_ref, f_ref, v_ref, g_ref, lb_ref, nw_ref, s_ref, o_ref, so_ref, *, seq):
    rows = 2 * seq
    assert rows == V7X_SUBLANES
    row = lax.broadcasted_iota(jnp.int32, (rows, rows), 0)
    col = lax.broadcasted_iota(jnp.int32, (rows, rows), 1)
    causal = (row >= col) & ((row // seq) == (col // seq))
    tri = causal.astype(F32)
    rvec = lax.broadcasted_iota(jnp.int32, (rows, 1), 0)
    r16 = lax.broadcasted_iota(jnp.int32, (2 * rows, HG_DV), 0)
    ones_rows = jnp.where((r16 == rows) | (r16 == rows + 1), 1.0, 0.0).astype(BF16)

    def pair(p, carry):
        r0 = pl.multiple_of(p * rows, rows)
        for h in range(HG_HEADS):
            sl = slice(h * HG_DK, (h + 1) * HG_DK)
            q, k, logf = _hgrn_gates(q_ref[pl.ds(r0, rows), sl], f_ref[pl.ds(r0, rows), sl], lb_ref[:, sl])
            v = v_ref[pl.ds(r0, rows), sl].astype(BF16)
            cum = _dot(tri, logf, NN, precision=HIGHEST)
            ecum = jnp.exp(cum)
            qi = (q * ecum).astype(BF16)
            km = (k / ecum).astype(BF16)
            scores = jnp.where(causal, _dot(qi, km, NT), 0.0).astype(BF16)
            o = _dot(scores, v, NN)
            for j in range(2):
                b = 2 * p + j
                mine = (rvec // seq) == j
                last = cum[(j + 1) * seq - 1:(j + 1) * seq, :]
                a = jnp.exp(last)
                a_hi = a.astype(BF16).astype(F32)
                a_lo = a - a_hi
                kl = jnp.where(mine, k * jnp.exp(last - cum), 0.0)
                lhs = jnp.concatenate([kl, a_hi, a_lo, jnp.zeros((rows - 2, HG_DK), F32)], axis=0).astype(BF16)
                rhs = jnp.concatenate(
                    [jnp.concatenate([v, jnp.zeros((rows, HG_DV), BF16)], axis=0), ones_rows], axis=1)
                both = _dot(lhs, rhs, TN)
                s0 = s_ref[b, h]
                o = o + jnp.where(mine, _dot(qi, s0.astype(BF16), NN), 0.0)
                so_ref[b, h] = both[:, HG_DV:] * s0 + both[:, :HG_DV]
            o_ref[pl.ds(r0, rows), sl] = _hgrn_out(o, g_ref[pl.ds(r0, rows), sl], nw_ref[:, sl]).astype(o_ref.dtype)
        return carry

    lax.fori_loop(0, HG_SAMPLE_BB // 2, pair, 0)


def _hgrn_sample(proj, row0, state, lb, nw, *, batch, seq):
    bb = HG_SAMPLE_BB
    tr = bb * seq
    blk0 = row0 // tr
    blk = lambda j: pl.BlockSpec((tr, HG_KW), lambda i: (blk0 + i, j))
    vec = pl.BlockSpec((1, HG_KW), lambda i: (0, 0))
    sblk = pl.BlockSpec((bb, HG_HEADS, HG_DK, HG_DV), lambda i: (i, 0, 0, 0))
    blocks = 4 * _nbytes((tr, HG_KW), F32) + 2 * _nbytes((bb, HG_HEADS, HG_DK, HG_DV), F32)
    return pl.pallas_call(
        functools.partial(_hgrn_sample_body, seq=seq),
        grid=(batch // bb,),
        in_specs=[blk(0), blk(1), blk(2), blk(3), vec, vec, sblk],
        out_specs=[pl.BlockSpec((tr, HG_VW), lambda i: (i, 0)), sblk],
        out_shape=[jax.ShapeDtypeStruct((batch * seq, HG_VW), BF16),
                   jax.ShapeDtypeStruct(state.shape, F32)],
        compiler_params=_params(("arbitrary",), blocks),
        name="hgrn_sample",
    )(proj, proj, proj, proj, lb.reshape(1, HG_KW), nw.reshape(1, HG_VW), state)


def _head_expand_matrix():
    r = jnp.arange(2 * V7X_LANES)[:, None] % V7X_LANES
    c = jnp.arange(M_DI)[None, :] // M_HEADDIM
    return (r == c).astype(BF16)


def _expand_heads(coef, e2):
    hi = coef.astype(BF16)
    lo = (coef - hi.astype(F32)).astype(BF16)
    return _dot(jnp.concatenate([hi, lo], axis=1), e2, NN)


def _softplus(x):
    return jnp.maximum(x, 0.0) + jnp.log(1.0 + jnp.exp(-jnp.abs(x)))


def _group_rmsnorm(y, width):
    outs = []
    for g in range(y.shape[1] // width):
        yg = y[:, g * width:(g + 1) * width]
        outs.append(yg * lax.rsqrt(jnp.mean(yg * yg, -1, keepdims=True) + RMS_EPS))
    return jnp.concatenate(outs, axis=1)


def _ssd_prompt_body(z_ref, x_ref, bc_ref, dt_ref, cwx_ref, cwb_ref, cbx_ref, cbb_ref, dtb_ref, alog_ref,
                     dexp_ref, nw_ref, e2_ref, y_ref, sfin_ref, conv_ref, s_scr, cx_scr, cb_scr):
    n = pl.program_id(1)
    c = M_CHUNK
    tail = V7X_SUBLANES

    @pl.when(n == 0)
    def _():
        s_scr[...] = jnp.zeros_like(s_scr)
        cx_scr[...] = jnp.zeros_like(cx_scr)
        cb_scr[...] = jnp.zeros_like(cb_scr)

    def conv(u, carry_scr, w_ref, b_ref):
        ext = jnp.concatenate([carry_scr[...], u], axis=0)
        acc = b_ref[...] + w_ref[M_CONV - 1:M_CONV, :] * u
        for j in range(1, M_CONV):
            acc = acc + w_ref[M_CONV - 1 - j:M_CONV - j, :] * ext[tail - j:tail - j + c, :]
        carry_scr[...] = u[c - tail:, :]
        return _silu(acc)

    ux = x_ref[...]
    ubc = bc_ref[...]
    xc = conv(ux, cx_scr, cwx_ref, cbx_ref)
    bcc = conv(ubc, cb_scr, cwb_ref, cbb_ref)
    ngn = M_GROUPS * M_STATE

    row = lax.broadcasted_iota(jnp.int32, (c, c), 0)
    col = lax.broadcasted_iota(jnp.int32, (c, c), 1)
    causal = row >= col
    tri = causal.astype(F32)
    e2 = e2_ref[...]

    dtp = _softplus(dt_ref[...] + dtb_ref[...])
    da = dtp * (-jnp.exp(alog_ref[...]))
    cum = _dot(tri, da, NN, precision=HIGHEST)
    cum_t = cum.T
    last = cum[c - 1:c, :]
    xdt = xc * _expand_heads(dtp, e2)
    xend = (xdt * _expand_heads(jnp.exp(last - cum), e2)).astype(BF16)
    ecum = _expand_heads(jnp.exp(cum), e2)
    xdt_b = xdt.astype(BF16)
    gw = M_HPG * M_HEADDIM

    ys = []
    for g in range(M_GROUPS):
        bg = bcc[:, g * M_STATE:(g + 1) * M_STATE].astype(BF16)
        cg = bcc[:, ngn + g * M_STATE:ngn + (g + 1) * M_STATE].astype(BF16)
        cb = _dot(cg, bg, NT)
        sg = s_scr[g * M_HPG:(g + 1) * M_HPG].reshape(gw, M_STATE)
        y_inter = _dot(cg, sg.astype(BF16), NT) * ecum[:, g * gw:(g + 1) * gw]
        upd = _dot(xend[:, g * gw:(g + 1) * gw], bg, TN)
        parts = []
        for hh in range(M_HPG):
            h = g * M_HPG + hh
            decay = jnp.exp(jnp.where(causal, cum[:, h:h + 1] - cum_t[h:h + 1, :], -jnp.inf))
            m = (cb * decay).astype(BF16)
            parts.append(_dot(m, xdt_b[:, h * M_HEADDIM:(h + 1) * M_HEADDIM], NN))
            s_scr[h] = jnp.exp(last[:, h:h + 1]) * s_scr[h] + upd[hh * M_HEADDIM:(hh + 1) * M_HEADDIM, :]
        ys.append(jnp.concatenate(parts, axis=1) + y_inter)
    y = jnp.concatenate(ys, axis=1) + dexp_ref[...] * xc
    y = y * _silu(z_ref[...])
    y_ref[...] = (_group_rmsnorm(y, gw) * nw_ref[...]).astype(y_ref.dtype)

    @pl.when(n == pl.num_programs(1) - 1)
    def _():
        sfin_ref[0] = s_scr[...]
        conv_ref[0, :, 0:M_DI] = ux[c - (M_CONV - 1):, :]
        conv_ref[0, :, M_DI:] = ubc[c - (M_CONV - 1):, :]


def _ssd_prompt(proj, dt, conv_w, conv_b, dt_bias, a_log, d_exp, norm_w, e2, *, batch, seq):
    c = M_CHUNK
    nchunk = seq // c
    tok = lambda j: pl.BlockSpec((c, M_DI), lambda b, n: (b * nchunk + n, j))
    const = lambda shape, j=0: pl.BlockSpec(shape, lambda b, n: (0, j))
    blocks = (3 * _nbytes((c, M_DI), F32) + _nbytes((c, M_DI), BF16) + _nbytes((2 * V7X_LANES, M_DI), BF16)
              + 2 * _nbytes((M_HEADS, M_HEADDIM, M_STATE), F32) + 40 * _nbytes((c, M_DI), F32))
    return pl.pallas_call(
        _ssd_prompt_body,
        grid=(batch, nchunk),
        in_specs=[tok(4), tok(5), tok(6),
                  pl.BlockSpec((c, V7X_LANES), lambda b, n: (b * nchunk + n, 0)),
                  const((M_CONV, M_DI), 0), const((M_CONV, M_DI), 1), const((1, M_DI), 0), const((1, M_DI), 1),
                  const((1, V7X_LANES)), const((1, V7X_LANES)), const((1, M_DI)), const((1, M_DI)),
                  const((2 * V7X_LANES, M_DI))],
        out_specs=[pl.BlockSpec((c, M_DI), lambda b, n: (b * nchunk + n, 0)),
                   pl.BlockSpec((1, M_HEADS, M_HEADDIM, M_STATE), lambda b, n: (b, 0, 0, 0)),
                   pl.BlockSpec((1, M_CONV - 1, M_CONV_DIM), lambda b, n: (b, 0, 0))],
        out_shape=[jax.ShapeDtypeStruct((batch * seq, M_DI), BF16),
                   jax.ShapeDtypeStruct((batch, M_HEADS, M_HEADDIM, M_STATE), F32),
                   jax.ShapeDtypeStruct((batch, M_CONV - 1, M_CONV_DIM), F32)],
        scratch_shapes=[pltpu.VMEM((M_HEADS, M_HEADDIM, M_STATE), F32),
                        pltpu.VMEM((V7X_SUBLANES, M_DI), F32), pltpu.VMEM((V7X_SUBLANES, M_DI), F32)],
        compiler_params=_params(("arbitrary", "arbitrary"), blocks),
        name="ssd_prompt",
    )(proj, proj, proj, dt, conv_w, conv_w, conv_b, conv_b, dt_bias, a_log, d_exp, norm_w, e2)


SSD_SAMPLE_BB = 8


def _shift_rows(a, j):
    return a if j == 0 else pltpu.roll(a, j, 0)


def _ssd_sample_body(z_ref, x_ref, bc_ref, dt_ref, s_ref, cs_ref, cwx_ref, cwb_ref, cbx_ref, cbb_ref, dtb_ref,
                     alog_ref, dexp_ref, nw_ref, e2_ref, y_ref, so_ref, co_ref, *, seq):
    rows = V7X_SUBLANES
    hist = M_CONV - 1
    assert rows == 2 * seq and hist <= seq and hist <= rows - seq
    rvec = lax.broadcasted_iota(jnp.int32, (rows, 1), 0)
    valid = rvec < seq
    lane = lax.broadcasted_iota(jnp.int32, (1, V7X_LANES), 1)
    r16 = lax.broadcasted_iota(jnp.int32, (2 * rows, M_STATE), 0)
    ones_rows = jnp.where((r16 == rows) | (r16 == rows + 1), 1.0, 0.0).astype(BF16)
    e2 = e2_ref[...]
    a_neg = -jnp.exp(alog_ref[...])
    gw = M_HPG * M_HEADDIM
    ngn = M_GROUPS * M_STATE

    def conv(u8, buf, w_ref, b_ref):
        buf8 = jnp.concatenate([buf, jnp.zeros((rows - hist, buf.shape[1]), F32)], axis=0)
        ext = jnp.where(valid, u8, _shift_rows(buf8, rows - hist))
        acc = b_ref[...] + w_ref[hist:hist + 1, :] * ext
        for j in range(1, M_CONV):
            acc = acc + w_ref[hist - j:hist - j + 1, :] * _shift_rows(ext, j)
        new_hist = _shift_rows(ext, rows - (seq - hist))[0:hist, :]
        return _silu(acc), new_hist

    def one(b, u_x, u_bc, z8, dt8):
        xc, nhx = conv(u_x, cs_ref[b, :, 0:M_DI], cwx_ref, cbx_ref)
        bcc, nhb = conv(u_bc, cs_ref[b, :, M_DI:], cwb_ref, cbb_ref)
        co_ref[b, :, 0:M_DI] = nhx
        co_ref[b, :, M_DI:] = nhb
        dtp = jnp.where(valid, _softplus(dt8 + dtb_ref[...]), 0.0)
        cum = dtp * a_neg
        k = 1
        while k < seq:
            cum = cum + jnp.where(rvec >= k, _shift_rows(cum, k), 0.0)
            k *= 2
        last = cum[seq - 1:seq, :]
        xdt = xc * _expand_heads(dtp, e2)
        ecum = _expand_heads(jnp.exp(cum), e2)
        bmat = bcc[:, :ngn]
        cmat = bcc[:, ngn:]
        y = dexp_ref[...] * xc
        for j in range(seq):
            prod = cmat * _shift_rows(bmat, j)
            cbh = jnp.zeros((rows, V7X_LANES), F32)
            for g in range(M_GROUPS):
                cbg = jnp.sum(prod[:, g * M_STATE:(g + 1) * M_STATE], axis=-1, keepdims=True)
                cbh = jnp.where((lane // M_HPG) == g, cbg, cbh)
            coef = cbh * jnp.exp(cum - _shift_rows(cum, j))
            y = y + _expand_heads(coef, e2) * _shift_rows(xdt, j)
        xs = xdt * _expand_heads(jnp.exp(last - cum), e2)
        dec = _expand_heads(jnp.broadcast_to(jnp.exp(last), (rows, V7X_LANES)), e2)[0:1, :]
        dec_hi = dec.astype(BF16).astype(F32)
        lhs = jnp.concatenate([xs, dec_hi, dec - dec_hi, jnp.zeros((rows - 2, M_DI), F32)], axis=0).astype(BF16)
        parts = []
        for g in range(M_GROUPS):
            bg = bmat[:, g * M_STATE:(g + 1) * M_STATE].astype(BF16)
            cg = cmat[:, g * M_STATE:(g + 1) * M_STATE].astype(BF16)
            s0 = s_ref[b, g * M_HPG:(g + 1) * M_HPG].reshape(gw, M_STATE)
            parts.append(_dot(cg, s0.astype(BF16), NT) * ecum[:, g * gw:(g + 1) * gw])
            rhs = jnp.concatenate([jnp.concatenate([bg, jnp.zeros((rows, M_STATE), BF16)], axis=0), ones_rows], axis=1)
            both = _dot(lhs[:, g * gw:(g + 1) * gw], rhs, TN)
            s_new = both[:, M_STATE:] * s0 + both[:, :M_STATE]
            so_ref[b, g * M_HPG:(g + 1) * M_HPG] = s_new.reshape(M_HPG, M_HEADDIM, M_STATE)
        y = (y + jnp.concatenate(parts, axis=1)) * _silu(z8)
        return _group_rmsnorm(y, gw) * nw_ref[...]

    def pair(p, carry):
        r0 = pl.multiple_of(p * rows, rows)
        tiles = [ref[pl.ds(r0, rows), :] for ref in (x_ref, bc_ref, z_ref, dt_ref)]
        ys = []
        for j in range(2):
            ys.append(one(2 * p + j, *[_shift_rows(t, j * (rows - seq)) for t in tiles]))
        y_ref[pl.ds(r0, rows), :] = jnp.where(valid, ys[0], _shift_rows(ys[1], seq)).astype(y_ref.dtype)
        return carry

    lax.fori_loop(0, SSD_SAMPLE_BB // 2, pair, 0)


def _ssd_sample(proj, row0, dt, dt_row0, state, conv_state, conv_w, conv_b, dt_bias, a_log, d_exp, norm_w, e2, *,
                batch, seq):
    bb = SSD_SAMPLE_BB
    tr = bb * seq
    tok = lambda j: pl.BlockSpec((tr, M_DI), lambda i: (row0 // tr + i, j))
    const = lambda shape, j=0: pl.BlockSpec(shape, lambda i: (0, j))
    sblk = pl.BlockSpec((bb, M_HEADS, M_HEADDIM, M_STATE), lambda i: (i, 0, 0, 0))
    cblk = pl.BlockSpec((bb, M_CONV - 1, M_CONV_DIM), lambda i: (i, 0, 0))
    blocks = (3 * _nbytes((tr, M_DI), F32) + 2 * _nbytes((bb, M_HEADS, M_HEADDIM, M_STATE), F32)
              + 2 * _nbytes((bb, V7X_SUBLANES, M_CONV_DIM), F32) + _nbytes((2 * V7X_LANES, M_DI), BF16))
    return pl.pallas_call(
        functools.partial(_ssd_sample_body, seq=seq),
        grid=(batch // bb,),
        in_specs=[tok(4), tok(5), tok(6),
                  pl.BlockSpec((tr, V7X_LANES), lambda i: (dt_row0 // tr + i, 0)),
                  sblk, cblk,
                  const((M_CONV, M_DI), 0), const((M_CONV, M_DI), 1), const((1, M_DI), 0), const((1, M_DI), 1),
                  const((1, V7X_LANES)), const((1, V7X_LANES)), const((1, M_DI)), const((1, M_DI)),
                  const((2 * V7X_LANES, M_DI))],
        out_specs=[pl.BlockSpec((tr, M_DI), lambda i: (i, 0)), sblk, cblk],
        out_shape=[jax.ShapeDtypeStruct((batch * seq, M_DI), BF16),
                   jax.ShapeDtypeStruct(state.shape, F32),
                   jax.ShapeDtypeStruct(conv_state.shape, F32)],
        compiler_params=_params(("arbitrary",), blocks),
        name="ssd_sample",
    )(proj, proj, proj, dt, state, conv_state, conv_w, conv_w, conv_b, conv_b, dt_bias, a_log, d_exp, norm_w, e2)


S5_TILES = 8
S5_TILE_IN = D_MODEL // S5_TILES
S5_TILE_ST = S5_STATE // S5_TILES


def _s5_discretize(a_re, a_im, log_dt, b_re, b_im, c_re, c_im):
    lam_re = jnp.minimum(a_re, -1e-4)
    lam_im = a_im
    dt = jnp.exp(log_dt)[:, None]
    mag = jnp.exp(lam_re * dt)
    ab_re = mag * jnp.cos(lam_im * dt)
    ab_im = mag * jnp.sin(lam_im * dt)
    den = lam_re * lam_re + lam_im * lam_im
    zr = ((ab_re - 1.0) * lam_re + ab_im * lam_im) / den
    zi = (ab_im * lam_re - (ab_re - 1.0) * lam_im) / den
    bb_re = zr[..., None] * b_re - zi[..., None] * b_im
    bb_im = zr[..., None] * b_im + zi[..., None] * b_re
    gpt = S5_GROUPS // S5_TILES
    eye = jnp.eye(gpt, dtype=F32)

    def pack_b(bb):
        return jnp.einsum('kgph,gm->kghmp', bb.reshape(S5_TILES, gpt, S5_P, S5_GROUP), eye).reshape(
            S5_TILES, S5_TILE_IN, S5_TILE_ST).astype(BF16)

    def pack_c(c):
        return jnp.einsum('kghp,gm->kgpmh', c.reshape(S5_TILES, gpt, S5_GROUP, S5_P), eye).reshape(
            S5_TILES, S5_TILE_ST, S5_TILE_IN).astype(BF16)

    tile = lambda a: a.reshape(V7X_SUBLANES, S5_STATE // V7X_SUBLANES)
    return tile(ab_re), tile(ab_im), pack_b(bb_re), pack_b(bb_im), pack_c(c_re), pack_c(c_im)


def _s5_bu_body(x_ref, wr_ref, wi_ref, or_ref, oi_ref):
    x = x_ref[...]
    or_ref[...] = _dot(x, wr_ref[0], NN)
    oi_ref[...] = _dot(x, wi_ref[0], NN)


def _s5_bu(xb, wb_re, wb_im, *, tm):
    t = xb.shape[0]
    wspec = pl.BlockSpec((1, S5_TILE_IN, S5_TILE_ST), lambda k, i: (k, 0, 0))
    ospec = pl.BlockSpec((tm, S5_TILE_ST), lambda k, i: (i, k))
    blocks = _nbytes((tm, S5_TILE_IN), BF16) + 2 * _nbytes((S5_TILE_IN, S5_TILE_ST), BF16) + 2 * _nbytes((tm, S5_TILE_ST), F32)
    return pl.pallas_call(
        _s5_bu_body,
        grid=(S5_TILES, t // tm),
        in_specs=[pl.BlockSpec((tm, S5_TILE_IN), lambda k, i: (i, k)), wspec, wspec],
        out_specs=[ospec, ospec],
        out_shape=[jax.ShapeDtypeStruct((t, S5_STATE), F32)] * 2,
        compiler_params=_params(("arbitrary", "arbitrary"), blocks),
        name="s5_bu",
    )(xb, wb_re, wb_im)


def _s5_scan_body(br_ref, bi_ref, h0r_ref, h0i_ref, ar_ref, ai_ref, *rest, bb, tc):
    hr_ref, hi_ref, fr_ref, fi_ref, cr_scr, ci_scr = rest[-6:]
    n = pl.program_id(1)

    @pl.when(n == 0)
    def _():
        cr_scr[...] = h0r_ref[...]
        ci_scr[...] = h0i_ref[...]

    ar = ar_ref[...]
    ai = ai_ref[...]

    def seq_body(b, carry):
        def step(t, h):
            hr, hi = h
            r = b * tc + t
            nr = ar * hr - ai * hi + br_ref[r]
            ni = ar * hi + ai * hr + bi_ref[r]
            hr_ref[r] = nr
            hi_ref[r] = ni
            return nr, ni

        hr, hi = lax.fori_loop(0, tc, step, (cr_scr[b], ci_scr[b]))
        cr_scr[b] = hr
        ci_scr[b] = hi
        return carry

    lax.fori_loop(0, bb, seq_body, 0)

    @pl.when(n == pl.num_programs(1) - 1)
    def _():
        fr_ref[...] = cr_scr[...]
        fi_ref[...] = ci_scr[...]


def _s5_scan(bu_re, bu_im, h0_re, h0_im, ab_re, ab_im, *, row0, batch, seq, bb, tc, prev=None, name):
    t, sub, w = bu_re.shape
    assert bb == 1 or tc == seq
    rb = bb * tc
    blk = pl.BlockSpec((rb, sub, w), lambda i, n: ((row0 + i * bb * seq) // rb + n, 0, 0))
    sblk = pl.BlockSpec((bb, sub, w), lambda i, n: (i, 0, 0))
    ablk = pl.BlockSpec((sub, w), lambda i, n: (0, 0))
    blocks = 4 * _nbytes((rb, sub, w), F32) + 6 * _nbytes((bb, sub, w), F32)
    keep = [] if prev is None else list(prev)
    n_in = 6
    return pl.pallas_call(
        functools.partial(_s5_scan_body, bb=bb, tc=tc),
        grid=(batch // bb, seq // tc),
        in_specs=[blk, blk, sblk, sblk, ablk, ablk] + [pl.BlockSpec(memory_space=pl.ANY)] * len(keep),
        out_specs=[blk, blk, sblk, sblk],
        out_shape=[jax.ShapeDtypeStruct(bu_re.shape, F32)] * 2 + [jax.ShapeDtypeStruct(h0_re.shape, F32)] * 2,
        scratch_shapes=[pltpu.VMEM((bb, sub, w), F32)] * 2,
        input_output_aliases={n_in + k: k for k in range(len(keep))},
        compiler_params=_params(("arbitrary", "arbitrary"), blocks),
        name=name,
    )(bu_re, bu_im, h0_re, h0_im, ab_re, ab_im, *keep)


def _s5_out_body(hr_ref, hi_ref, x_ref, d_ref, wr_ref, wi_ref, o_ref):
    y = _dot(hr_ref[...].astype(BF16), wr_ref[0], NN) - _dot(hi_ref[...].astype(BF16), wi_ref[0], NN)
    o_ref[...] = jax.nn.gelu(y + d_ref[...] * x_ref[...]).astype(o_ref.dtype)


def _s5_out(h_re, h_im, x, d_skip, wc_re, wc_im, *, tm):
    t = x.shape[0]
    hspec = pl.BlockSpec((tm, S5_TILE_ST), lambda k, i: (i, k))
    wspec = pl.BlockSpec((1, S5_TILE_ST, S5_TILE_IN), lambda k, i: (k, 0, 0))
    xspec = pl.BlockSpec((tm, S5_TILE_IN), lambda k, i: (i, k))
    blocks = (2 * _nbytes((tm, S5_TILE_ST), F32) + 2 * _nbytes((S5_TILE_ST, S5_TILE_IN), BF16)
              + _nbytes((tm, S5_TILE_IN), F32) + _nbytes((tm, S5_TILE_IN), BF16))
    return pl.pallas_call(
        _s5_out_body,
        grid=(S5_TILES, t // tm),
        in_specs=[hspec, hspec, xspec, pl.BlockSpec((1, S5_TILE_IN), lambda k, i: (0, k)), wspec, wspec],
        out_specs=xspec,
        out_shape=jax.ShapeDtypeStruct((t, D_MODEL), BF16),
        compiler_params=_params(("arbitrary", "arbitrary"), blocks),
        name="s5_out",
    )(h_re, h_im, x, d_skip.reshape(1, D_MODEL), wc_re, wc_im)


def _glu_body(a_ref, wa_ref, wb_ref, ba_ref, bb_ref, o_ref):
    a = a_ref[...]
    o_ref[...] = (_dot(a, wa_ref[...], NN) + ba_ref[...]) * jax.nn.sigmoid(_dot(a, wb_ref[...], NN) + bb_ref[...])


def _glu(a, wa, wb, ba, bb, *, tm, tn):
    t, k = a.shape
    n = wa.shape[1]
    wspec = pl.BlockSpec((k, tn), lambda j, i: (0, j))
    bspec = pl.BlockSpec((1, tn), lambda j, i: (0, j))
    blocks = _nbytes((tm, k), BF16) + 2 * _nbytes((k, tn), BF16) + _nbytes((tm, tn), F32)
    return pl.pallas_call(
        _glu_body,
        grid=(n // tn, t // tm),
        in_specs=[pl.BlockSpec((tm, k), lambda j, i: (i, 0)), wspec, wspec, bspec, bspec],
        out_specs=pl.BlockSpec((tm, tn), lambda j, i: (i, j)),
        out_shape=jax.ShapeDtypeStruct((t, n), F32),
        compiler_params=_params(("arbitrary", "arbitrary"), blocks),
        name="glu",
    )(a, wa, wb, ba.reshape(1, n), bb.reshape(1, n))


def _router_body(x_ref, w_ref, b_ref, g_ref):
    logits = _dot(w_ref[...], x_ref[...], NT, precision=HIGHEST) + b_ref[...]
    rows = [logits[e:e + 1, :] for e in range(N_EXPERTS)]
    m = functools.reduce(jnp.maximum, rows)
    ex = [jnp.exp(r - m) for r in rows]
    z = functools.reduce(jnp.add, ex)
    p = [v / z for v in ex]

    def top2_sum(a, b, c, d):
        hi1, lo1, hi2, lo2 = jnp.maximum(a, b), jnp.minimum(a, b), jnp.maximum(c, d), jnp.minimum(c, d)
        return jnp.maximum(hi1, hi2) + jnp.maximum(jnp.minimum(hi1, hi2), jnp.maximum(lo1, lo2))

    assert EXP_PER_GROUP == 4
    best = top2_sum(*p[0:EXP_PER_GROUP])
    gsel = jnp.zeros_like(best, dtype=jnp.int32)
    for g in range(1, N_EXP_GROUPS):
        s = top2_sum(*p[g * EXP_PER_GROUP:(g + 1) * EXP_PER_GROUP])
        better = s > best
        gsel = jnp.where(better, g, gsel)
        best = jnp.where(better, s, best)
    inner = []
    for i in range(EXP_PER_GROUP):
        v = p[i]
        for g in range(1, N_EXP_GROUPS):
            v = jnp.where(gsel == g, p[g * EXP_PER_GROUP + i], v)
        inner.append(v)

    def first_argmax(vals, skip=None):
        bv = bi = None
        for i, v in enumerate(vals):
            v = v if skip is None else jnp.where(skip == i, -jnp.inf, v)
            if bv is None:
                bv, bi = v, jnp.zeros_like(gsel)
            else:
                better = v > bv
                bi = jnp.where(better, i, bi)
                bv = jnp.where(better, v, bv)
        return bv, bi

    p1, i1 = first_argmax(inner)
    p2, i2 = first_argmax(inner, skip=i1)
    tot = p1 + p2
    e1 = gsel * EXP_PER_GROUP + i1
    e2 = gsel * EXP_PER_GROUP + i2
    for e in range(N_EXPERTS):
        g_ref[e:e + 1, :] = jnp.where(e1 == e, p1 / tot, 0.0) + jnp.where(e2 == e, p2 / tot, 0.0)


def _router(x, w_router, b_router, *, tm):
    t, d = x.shape
    blocks = _nbytes((tm, d), F32) + _nbytes((N_EXPERTS, d), F32) + _nbytes((N_EXPERTS, tm), F32)
    return pl.pallas_call(
        _router_body,
        grid=(t // tm,),
        in_specs=[pl.BlockSpec((tm, d), lambda i: (i, 0)), pl.BlockSpec((N_EXPERTS, d), lambda i: (0, 0)),
                  pl.BlockSpec((N_EXPERTS, 1), lambda i: (0, 0))],
        out_specs=pl.BlockSpec((N_EXPERTS, tm), lambda i: (0, i)),
        out_shape=jax.ShapeDtypeStruct((N_EXPERTS, t), F32),
        compiler_params=_params(("arbitrary",), blocks),
        name="router",
    )(x, w_router.T, b_router.reshape(N_EXPERTS, 1))


def _moe_dense_body(x_ref, g_ref, wg_ref, wu_ref, wd_ref, o_ref):
    e = pl.program_id(1)

    @pl.when(e == 0)
    def _():
        o_ref[...] = jnp.zeros_like(o_ref)

    x = x_ref[...]
    h = _dot(x, wg_ref[0], NN)
    u = _dot(x, wu_ref[0], NN)
    lane = lax.broadcasted_iota(jnp.int32, (1, N_EXPERTS), 1)
    gate = jnp.sum(jnp.where(lane == e, g_ref[...], 0.0), axis=1, keepdims=True)
    act = (_silu(h) * u * gate).astype(BF16)
    o_ref[...] += _dot(act, wd_ref[0], NN)


def _moe_dense(xb, gates, wg, wu, wd, *, tm):
    t, d = xb.shape
    f = wg.shape[2]
    blocks = (_nbytes((tm, d), BF16) + _nbytes((tm, V7X_LANES), F32) + 3 * _nbytes((d, f), BF16)
              + _nbytes((tm, d), F32) + 4 * _nbytes((tm, f), F32))
    return pl.pallas_call(
        _moe_dense_body,
        grid=(t // tm, N_EXPERTS),
        in_specs=[pl.BlockSpec((tm, d), lambda i, e: (i, 0)), pl.BlockSpec((tm, N_EXPERTS), lambda i, e: (i, 0)),
                  pl.BlockSpec((1, d, f), lambda i, e: (e, 0, 0)), pl.BlockSpec((1, d, f), lambda i, e: (e, 0, 0)),
                  pl.BlockSpec((1, f, d), lambda i, e: (e, 0, 0))],
        out_specs=pl.BlockSpec((tm, d), lambda i, e: (i, 0)),
        out_shape=jax.ShapeDtypeStruct((t, d), F32),
        compiler_params=_params(("arbitrary", "arbitrary"), blocks),
        name="moe_dense",
    )(xb, gates, wg, wu, wd)


TOKEN_TILE = 1088
ROUTER_TILE = 512
LN_TILE = 272


def _moe_block(x, xb, layer, w_router, b_router, w_gate, w_up, w_down):
    gates = _router(x, w_router, b_router, tm=ROUTER_TILE).T
    return _moe_dense(xb, gates, w_gate[layer].astype(BF16), w_up[layer].astype(BF16), w_down[layer].astype(BF16),
                      tm=TOKEN_TILE)


def kernel(x_prompt, x_sample, state_hgrn, state_ssm, state_conv, state_s5_re, state_s5_im, w_in0, hg_lb_logits, hg_norm_w, conv_w, conv_b, dt_bias, a_log, m_d, m_norm_w, w_out0, s5_a_re, s5_a_im, s5_log_dt, s5_b_re, s5_b_im, s5_c_re, s5_c_im, s5_d, glu_w_a, glu_b_a, glu_w_b, glu_b_b, w_router, b_router, w_gate, w_up, w_down, ln1_g, ln1_b, ln2_g, ln2_b):
    bp, lp, d = x_prompt.shape
    bs, ls, _ = x_sample.shape
    tp, ts = bp * lp, bs * ls
    tm = TOKEN_TILE
    x0 = jnp.concatenate([x_prompt.reshape(tp, d), x_sample.reshape(ts, d)], axis=0)
    x0b = x0.astype(BF16)
    lower_bounds = jnp.cumsum(jax.nn.softmax(hg_lb_logits.astype(F32), axis=0), axis=0)
    pad_lanes = lambda v: jnp.pad(v, (0, V7X_LANES - v.shape[0])).reshape(1, V7X_LANES)

    w_in = w_in0[0]
    proj = _matmul(x0b, w_in.astype(BF16), tm=tm, tn=1024, n_cols=IN0_MAIN, name="in_proj")
    w_dt = jnp.pad(w_in[:, IN0_MAIN:], ((0, 0), (0, V7X_LANES - M_HEADS))).astype(BF16)
    dt = _matmul(x0b, w_dt, tm=tm, tn=V7X_LANES, name="dt_proj")
    lb0 = lower_bounds[0]
    o_p, hg_p = _hgrn_prompt(proj, lb0, hg_norm_w[0], batch=bp, seq=lp)
    o_s, hg_s = _hgrn_sample(proj, tp, state_hgrn[0], lb0, hg_norm_w[0], batch=bs, seq=ls)
    e2 = _head_expand_matrix()
    ssd_consts = (conv_w[0], conv_b[0].reshape(1, -1), pad_lanes(dt_bias[0]), pad_lanes(a_log[0]),
                  jnp.repeat(m_d[0], M_HEADDIM).reshape(1, M_DI), m_norm_w[0].reshape(1, M_DI), e2)
    y_p, ssm_p, conv_p = _ssd_prompt(proj, dt, *ssd_consts, batch=bp, seq=lp)
    y_s, ssm_s, conv_s = _ssd_sample(proj, tp, dt, tp, state_ssm[0], state_conv[0], *ssd_consts, batch=bs, seq=ls)
    mixed = jnp.concatenate([jnp.concatenate([o_p, o_s], axis=0), jnp.concatenate([y_p, y_s], axis=0)], axis=1)
    mix = _matmul(mixed, w_out0[0].astype(BF16), tm=tm, tn=512, name="out_proj")
    x1, x1b = _res_layernorm(x0, mix, ln1_g[0], ln1_b[0], tm=LN_TILE, name="ln1_0")
    moe0 = _moe_block(x1, x1b, 0, w_router, b_router, w_gate, w_up, w_down)
    x2, x2b = _res_layernorm(x1, moe0, ln2_g[0], ln2_b[0], tm=LN_TILE, name="ln2_0")

    ab_re, ab_im, wb_re, wb_im, wc_re, wc_im = _s5_discretize(
        s5_a_re[0], s5_a_im[0], s5_log_dt[0], s5_b_re[0], s5_b_im[0], s5_c_re[0], s5_c_im[0])
    bu_re, bu_im = _s5_bu(x2b, wb_re, wb_im, tm=tm)
    sub, w = V7X_SUBLANES, S5_STATE // V7X_SUBLANES
    zeros = jnp.zeros((bp, sub, w), F32)
    bu_re, bu_im = bu_re.reshape(tp + ts, sub, w), bu_im.reshape(tp + ts, sub, w)
    h_re, h_im, s5r_p, s5i_p = _s5_scan(bu_re, bu_im, zeros, zeros, ab_re, ab_im,
                                        row0=0, batch=bp, seq=lp, bb=1, tc=64, name="s5_scan_prompt")
    h_re, h_im, s5r_s, s5i_s = _s5_scan(bu_re, bu_im, state_s5_re[0].reshape(bs, sub, w),
                                        state_s5_im[0].reshape(bs, sub, w), ab_re, ab_im,
                                        row0=tp, batch=bs, seq=ls, bb=16, tc=ls, prev=(h_re, h_im), name="s5_scan_sample")
    act = _s5_out(h_re.reshape(tp + ts, S5_STATE), h_im.reshape(tp + ts, S5_STATE), x2, s5_d[0], wc_re, wc_im, tm=tm)
    mix1 = _glu(act, glu_w_a[0].astype(BF16), glu_w_b[0].astype(BF16), glu_b_a[0], glu_b_b[0], tm=tm, tn=1024)
    x3, x3b = _res_layernorm(x2, mix1, ln1_g[1], ln1_b[1], tm=LN_TILE, name="ln1_1")
    moe1 = _moe_block(x3, x3b, 1, w_router, b_router, w_gate, w_up, w_down)
    x4, _ = _res_layernorm(x3, moe1, ln2_g[1], ln2_b[1], tm=LN_TILE, name="ln2_1")

    st = lambda a, b, shape: a.reshape((1, b) + shape)
    return (x4[:tp].reshape(bp, lp, d), x4[tp:].reshape(bs, ls, d),
            hg_p[None], hg_s[None], ssm_p[None], ssm_s[None], conv_p[None], conv_s[None],
            st(s5r_p, bp, (S5_GROUPS, S5_P)), st(s5r_s, bs, (S5_GROUPS, S5_P)),
            st(s5i_p, bp, (S5_GROUPS, S5_P)), st(s5i_s, bs, (S5_GROUPS, S5_P)))
```

```python
import functools
import math

import jax
import jax.numpy as jnp
from jax import lax
from jax.experimental import pallas as pl
from jax.experimental.pallas import tpu as pltpu

F32 = jnp.float32
BF16 = jnp.bfloat16
HIGHEST = lax.Precision.HIGHEST

D_MODEL = 2048
DEPTH = 2
HG_HEADS = 16
HG_DK = 128
HG_DV = 128
HG_KW = HG_HEADS * HG_DK
HG_VW = HG_HEADS * HG_DV
HG_CHUNK = 64
M_DI = 2048
M_HEADDIM = 64
M_HEADS = 32
M_GROUPS = 8
M_HPG = 4
M_STATE = 128
M_CONV = 4
M_CONV_DIM = M_DI + 2 * M_GROUPS * M_STATE
M_CHUNK = 128
IN0_MAIN = 2 * HG_KW + 2 * HG_VW + M_DI + M_CONV_DIM
S5_GROUP = 16
S5_GROUPS = 128
S5_P = 64
S5_STATE = S5_GROUPS * S5_P
N_EXPERTS = 16
N_EXP_GROUPS = 4
EXP_PER_GROUP = 4
D_EXPERT = 512
ALPHA = (2 * DEPTH) ** 0.25
LN_EPS = 1e-5
RMS_EPS = 1e-6

V7X_LANES = 128
V7X_SUBLANES = 8
V7X_VMEM_BYTES = 64 * 1024 * 1024
V7X_SCOPED_VMEM_CAP = 60000 * 1024
COMPILER_SCRATCH_BYTES = 16 * 1024 * 1024


def _params(semantics, block_bytes):
    limit = min(2 * block_bytes + COMPILER_SCRATCH_BYTES, V7X_SCOPED_VMEM_CAP)
    return pltpu.CompilerParams(dimension_semantics=semantics, vmem_limit_bytes=int(limit))


def _nbytes(shape, dtype):
    return math.prod(shape) * jnp.dtype(dtype).itemsize


def _silu(x):
    return x * jax.nn.sigmoid(x)


def _dot(a, b, dims, precision=None):
    return lax.dot_general(a, b, (dims, ((), ())), precision=precision, preferred_element_type=F32)


NN = ((1,), (0,))
NT = ((1,), (1,))
TN = ((0,), (0,))

MIX0_WIDTH = HG_VW + M_DI


def _shared_output(body, n_in, prev):
    if prev is None:
        return body, [], [], {}

    def with_prev(*refs):
        return body(*refs[:n_in], *refs[n_in + 1:])

    return with_prev, [pl.BlockSpec(memory_space=pl.ANY)], [prev], {n_in: 0}


def _mm_body(x_ref, w_ref, o_ref):
    o_ref[...] = _dot(x_ref[...], w_ref[...], NN).astype(o_ref.dtype)


def _matmul(x, w, *, tm, tn, n_cols=None, out_dtype=F32, name):
    m, k = x.shape
    n = n_cols if n_cols is not None else w.shape[1]
    blocks = _nbytes((tm, k), x.dtype) + _nbytes((k, tn), w.dtype) + _nbytes((tm, tn), out_dtype)
    return pl.pallas_call(
        _mm_body,
        grid=(n // tn, m // tm),
        in_specs=[pl.BlockSpec((tm, k), lambda j, i: (i, 0)), pl.BlockSpec((k, tn), lambda j, i: (0, j))],
        out_specs=pl.BlockSpec((tm, tn), lambda j, i: (i, j)),
        out_shape=jax.ShapeDtypeStruct((m, n), out_dtype),
        compiler_params=_params(("arbitrary", "arbitrary"), blocks),
        name=name,
    )(x, w)


def _mm_w32_body(x_ref, w_ref, o_ref, wb_scr):
    @pl.when(pl.program_id(1) == 0)
    def _():
        wb_scr[...] = w_ref[0].astype(BF16)

    o_ref[...] = _dot(x_ref[...], wb_scr[...], NN).astype(o_ref.dtype)


def _matmul_w32(x, w, layer, *, tm, tn, n_cols=None, name):
    m, k = x.shape
    n = n_cols if n_cols is not None else w.shape[2]
    blocks = _nbytes((tm, k), BF16) + _nbytes((k, tn), F32) + _nbytes((tm, tn), F32) + _nbytes((k, tn), BF16)
    return pl.pallas_call(
        _mm_w32_body,
        grid=(n // tn, m // tm),
        in_specs=[pl.BlockSpec((tm, k), lambda j, i: (i, 0)), pl.BlockSpec((1, k, tn), lambda j, i: (layer, 0, j))],
        out_specs=pl.BlockSpec((tm, tn), lambda j, i: (i, j)),
        out_shape=jax.ShapeDtypeStruct((m, n), F32),
        scratch_shapes=[pltpu.VMEM((k, tn), BF16)],
        compiler_params=_params(("arbitrary", "arbitrary"), blocks),
        name=name,
    )(x, w)


def _ln_body(x_ref, mix_ref, g_ref, b_ref, o_ref, ob_ref):
    z = ALPHA * x_ref[...] + mix_ref[...]
    mu = jnp.mean(z, -1, keepdims=True)
    zc = z - mu
    var = jnp.mean(zc * zc, -1, keepdims=True)
    y = zc * lax.rsqrt(var + LN_EPS) * g_ref[...] + b_ref[...]
    o_ref[...] = y
    ob_ref[...] = y.astype(BF16)


def _res_layernorm(x, mix, g, b, *, tm, name, row0=0, nrows=None):
    d = x.shape[1]
    nrows = x.shape[0] if nrows is None else nrows
    src = pl.BlockSpec((tm, d), lambda i: (row0 // tm + i, 0))
    dst = pl.BlockSpec((tm, d), lambda i: (i, 0))
    vec = pl.BlockSpec((1, d), lambda i: (0, 0))
    blocks = 3 * _nbytes((tm, d), F32) + _nbytes((tm, d), BF16)
    return pl.pallas_call(
        _ln_body,
        grid=(nrows // tm,),
        in_specs=[src, src, vec, vec],
        out_specs=[dst, dst],
        out_shape=[jax.ShapeDtypeStruct((nrows, d), F32), jax.ShapeDtypeStruct((nrows, d), BF16)],
        compiler_params=_params(("arbitrary",), blocks),
        name=name,
    )(x, mix, g.reshape(1, d), b.reshape(1, d))


def _hgrn_gates(q, f, lb):
    fg = lb + (1.0 - lb) * jax.nn.sigmoid(f)
    return _silu(q), 1.0 - fg, jnp.log(fg)


def _hgrn_out(o, g, nw):
    o = o * lax.rsqrt(jnp.mean(o * o, -1, keepdims=True) + RMS_EPS)
    return o * nw * _silu(g)


def _hgrn_prompt_body(q_ref, f_ref, v_ref, g_ref, lb_ref, nw_ref, o_ref, sfin_ref, st_scr):
    n = pl.program_id(1)
    c = HG_CHUNK

    @pl.when(n == 0)
    def _():
        st_scr[...] = jnp.zeros_like(st_scr)

    row = lax.broadcasted_iota(jnp.int32, (c, c), 0)
    col = lax.broadcasted_iota(jnp.int32, (c, c), 1)
    causal = row >= col
    tri = causal.astype(F32)
    q, k, logf = _hgrn_gates(q_ref[...], f_ref[...], lb_ref[...])
    v = v_ref[...].astype(BF16)
    cum = _dot(tri, logf, NN, precision=HIGHEST)
    mid = cum[c // 2 - 1:c // 2, :]
    last = cum[c - 1:c, :]
    qm = (q * jnp.exp(cum - mid)).astype(BF16)
    km = (k * jnp.exp(mid - cum)).astype(BF16)
    qi = (q * jnp.exp(cum)).astype(BF16)
    kl = (k * jnp.exp(last - cum)).astype(BF16)
    a_last = jnp.exp(last)
    gate = nw_ref[...] * _silu(g_ref[...])
    heads = [slice(h * HG_DK, (h + 1) * HG_DK) for h in range(HG_HEADS)]
    scores = [jnp.where(causal, _dot(qm[:, sl], km[:, sl], NT), 0.0).astype(BF16) for sl in heads]
    sts = [st_scr[h] for h in range(HG_HEADS)]
    outs = [_dot(qi[:, sl], sts[h].astype(BF16), NT) + _dot(scores[h], v[:, sl], NN) for h, sl in enumerate(heads)]
    for h, sl in enumerate(heads):
        st_scr[h] = a_last[:, sl] * sts[h] + _dot(v[:, sl], kl[:, sl], TN)
    for h, sl in enumerate(heads):
        o = outs[h]
        o = o * lax.rsqrt(jnp.mean(o * o, -1, keepdims=True) + RMS_EPS)
        o_ref[:, sl] = (o * gate[:, sl]).astype(o_ref.dtype)

    @pl.when(n == pl.num_programs(1) - 1)
    def _():
        sfin_ref[0] = st_scr[...]


def _hgrn_prompt(proj, lb, nw, *, batch, seq, prev=None):
    c = HG_CHUNK
    nchunk = seq // c
    blk = lambda j: pl.BlockSpec((c, HG_KW), lambda b, n: (b * nchunk + n, j))
    vec = pl.BlockSpec((1, HG_KW), lambda b, n: (0, 0))
    blocks = 4 * _nbytes((c, HG_KW), F32) + _nbytes((c, HG_VW), BF16) + 2 * _nbytes((HG_HEADS, HG_DV, HG_DK), F32)
    body, xspecs, xops, alias = _shared_output(_hgrn_prompt_body, 6, prev)
    o, st = pl.pallas_call(
        body,
        grid=(batch, nchunk),
        in_specs=[blk(0), blk(1), blk(2), blk(3), vec, vec] + xspecs,
        out_specs=[pl.BlockSpec((c, HG_VW), lambda b, n: (b * nchunk + n, 0)),
                   pl.BlockSpec((1, HG_HEADS, HG_DV, HG_DK), lambda b, n: (b, 0, 0, 0))],
        out_shape=[jax.ShapeDtypeStruct((proj.shape[0], MIX0_WIDTH), BF16),
                   jax.ShapeDtypeStruct((batch, HG_HEADS, HG_DV, HG_DK), F32)],
        scratch_shapes=[pltpu.VMEM((HG_HEADS, HG_DV, HG_DK), F32)],
        input_output_aliases=alias,
        compiler_params=_params(("arbitrary", "arbitrary"), blocks),
        name="hgrn_prompt",
    )(proj, proj, proj, proj, lb.reshape(1, HG_KW), nw.reshape(1, HG_VW), *xops)
    return o, jnp.swapaxes(st, -1, -2)


HG_SAMPLE_BB = 8


def _hgrn_sample_body(q_ref, f_ref, v_ref, g_ref, lb_ref, nw_ref, s_ref, o_ref, so_ref, *, seq):
    rows = 2 * seq
    assert rows == V7X_SUBLANES
    row = lax.broadcasted_iota(jnp.int32, (rows, rows), 0)
    col = lax.broadcasted_iota(jnp.int32, (rows, rows), 1)
    causal = (row >= col) & ((row // seq) == (col // seq))
    tri = causal.astype(F32)
    rvec = lax.broadcasted_iota(jnp.int32, (rows, 1), 0)
    r16 = lax.broadcasted_iota(jnp.int32, (2 * rows, HG_DV), 0)
    ones_rows = jnp.where((r16 == rows) | (r16 == rows + 1), 1.0, 0.0).astype(BF16)

    def pair(p, carry):
        r0 = pl.multiple_of(p * rows, rows)
        for h in range(HG_HEADS):
            sl = slice(h * HG_DK, (h + 1) * HG_DK)
            q, k, logf = _hgrn_gates(q_ref[pl.ds(r0, rows), sl], f_ref[pl.ds(r0, rows), sl], lb_ref[:, sl])
            v = v_ref[pl.ds(r0, rows), sl].astype(BF16)
            cum = _dot(tri, logf, NN, precision=HIGHEST)
            ecum = jnp.exp(cum)
            qi = (q * ecum).astype(BF16)
            km = (k / ecum).astype(BF16)
            scores = jnp.where(causal, _dot(qi, km, NT), 0.0).astype(BF16)
            o = _dot(scores, v, NN)
            for j in range(2):
                b = 2 * p + j
                mine = (rvec // seq) == j
                last = cum[(j + 1) * seq - 1:(j + 1) * seq, :]
                a = jnp.exp(last)
                a_hi = a.astype(BF16).astype(F32)
                a_lo = a - a_hi
                kl = jnp.where(mine, k * jnp.exp(last - cum), 0.0)
                lhs = jnp.concatenate([kl, a_hi, a_lo, jnp.zeros((rows - 2, HG_DK), F32)], axis=0).astype(BF16)
                rhs = jnp.concatenate(
                    [jnp.concatenate([v, jnp.zeros((rows, HG_DV), BF16)], axis=0), ones_rows], axis=1)
                both = _dot(lhs, rhs, TN)
                s0 = s_ref[b, h]
                o = o + jnp.where(mine, _dot(qi, s0.astype(BF16), NN), 0.0)
                so_ref[b, h] = both[:, HG_DV:] * s0 + both[:, :HG_DV]
            o_ref[pl.ds(r0, rows), sl] = _hgrn_out(o, g_ref[pl.ds(r0, rows), sl], nw_ref[:, sl]).astype(o_ref.dtype)
        return carry

    lax.fori_loop(0, HG_SAMPLE_BB // 2, pair, 0)


def _hgrn_sample(proj, row0, state, lb, nw, *, batch, seq, prev=None):
    bb = HG_SAMPLE_BB
    tr = bb * seq
    blk0 = row0 // tr
    blk = lambda j: pl.BlockSpec((tr, HG_KW), lambda i: (blk0 + i, j))
    vec = pl.BlockSpec((1, HG_KW), lambda i: (0, 0))
    sblk = pl.BlockSpec((bb, HG_HEADS, HG_DK, HG_DV), lambda i: (i, 0, 0, 0))
    blocks = 4 * _nbytes((tr, HG_KW), F32) + 2 * _nbytes((bb, HG_HEADS, HG_DK, HG_DV), F32)
    body, xspecs, xops, alias = _shared_output(functools.partial(_hgrn_sample_body, seq=seq), 7, prev)
    return pl.pallas_call(
        body,
        grid=(batch // bb,),
        in_specs=[blk(0), blk(1), blk(2), blk(3), vec, vec, sblk] + xspecs,
        out_specs=[pl.BlockSpec((tr, HG_VW), lambda i: (blk0 + i, 0)), sblk],
        out_shape=[jax.ShapeDtypeStruct((proj.shape[0], MIX0_WIDTH), BF16),
                   jax.ShapeDtypeStruct(state.shape, F32)],
        input_output_aliases=alias,
        compiler_params=_params(("arbitrary",), blocks),
        name="hgrn_sample",
    )(proj, proj, proj, proj, lb.reshape(1, HG_KW), nw.reshape(1, HG_VW), state, *xops)


def _head_expand_matrix():
    r = jnp.arange(2 * V7X_LANES)[:, None] % V7X_LANES
    c = jnp.arange(M_DI)[None, :] // M_HEADDIM
    return (r == c).astype(BF16)


def _expand_heads(coef, e2):
    hi = coef.astype(BF16)
    lo = (coef - hi.astype(F32)).astype(BF16)
    return _dot(jnp.concatenate([hi, lo], axis=1), e2, NN)


def _softplus(x):
    return jnp.maximum(x, 0.0) + jnp.log(1.0 + jnp.exp(-jnp.abs(x)))


def _group_rmsnorm(y, width):
    outs = []
    for g in range(y.shape[1] // width):
        yg = y[:, g * width:(g + 1) * width]
        outs.append(yg * lax.rsqrt(jnp.mean(yg * yg, -1, keepdims=True) + RMS_EPS))
    return jnp.concatenate(outs, axis=1)


def _ssd_prompt_body(z_ref, x_ref, bc_ref, dt_ref, cwx_ref, cwb_ref, cbx_ref, cbb_ref, dtb_ref, alog_ref,
                     dexp_ref, nw_ref, e2_ref, y_ref, sfin_ref, conv_ref, s_scr, cx_scr, cb_scr):
    n = pl.program_id(1)
    c = M_CHUNK
    tail = V7X_SUBLANES

    @pl.when(n == 0)
    def _():
        s_scr[...] = jnp.zeros_like(s_scr)
        cx_scr[...] = jnp.zeros_like(cx_scr)
        cb_scr[...] = jnp.zeros_like(cb_scr)

    def conv(u, carry_scr, w_ref, b_ref):
        ext = jnp.concatenate([carry_scr[...], u], axis=0)
        acc = b_ref[...] + w_ref[M_CONV - 1:M_CONV, :] * u
        for j in range(1, M_CONV):
            acc = acc + w_ref[M_CONV - 1 - j:M_CONV - j, :] * ext[tail - j:tail - j + c, :]
        carry_scr[...] = u[c - tail:, :]
        return _silu(acc)

    ux = x_ref[...]
    ubc = bc_ref[...]
    xc = conv(ux, cx_scr, cwx_ref, cbx_ref)
    bcc = conv(ubc, cb_scr, cwb_ref, cbb_ref)
    ngn = M_GROUPS * M_STATE

    row = lax.broadcasted_iota(jnp.int32, (c, c), 0)
    col = lax.broadcasted_iota(jnp.int32, (c, c), 1)
    causal = row >= col
    tri = causal.astype(F32)
    e2 = e2_ref[...]

    dtp = _softplus(dt_ref[...] + dtb_ref[...])
    da = dtp * (-jnp.exp(alog_ref[...]))
    cum = _dot(tri, da, NN, precision=HIGHEST)
    cum_t = cum.T
    last = cum[c - 1:c, :]
    xdt = xc * _expand_heads(dtp, e2)
    xend = (xdt * _expand_heads(jnp.exp(last - cum), e2)).astype(BF16)
    ecum = _expand_heads(jnp.exp(cum), e2)
    xdt_b = xdt.astype(BF16)
    gw = M_HPG * M_HEADDIM

    ys = []
    for g in range(M_GROUPS):
        bg = bcc[:, g * M_STATE:(g + 1) * M_STATE].astype(BF16)
        cg = bcc[:, ngn + g * M_STATE:ngn + (g + 1) * M_STATE].astype(BF16)
        cb = _dot(cg, bg, NT)
        sg = s_scr[g * M_HPG:(g + 1) * M_HPG].reshape(gw, M_STATE)
        y_inter = _dot(cg, sg.astype(BF16), NT) * ecum[:, g * gw:(g + 1) * gw]
        upd = _dot(xend[:, g * gw:(g + 1) * gw], bg, TN)
        parts = []
        for hh in range(M_HPG):
            h = g * M_HPG + hh
            decay = jnp.exp(jnp.where(causal, cum[:, h:h + 1] - cum_t[h:h + 1, :], -jnp.inf))
            m = (cb * decay).astype(BF16)
            parts.append(_dot(m, xdt_b[:, h * M_HEADDIM:(h + 1) * M_HEADDIM], NN))
            s_scr[h] = jnp.exp(last[:, h:h + 1]) * s_scr[h] + upd[hh * M_HEADDIM:(hh + 1) * M_HEADDIM, :]
        ys.append(jnp.concatenate(parts, axis=1) + y_inter)
    y = jnp.concatenate(ys, axis=1) + dexp_ref[...] * xc
    y = y * _silu(z_ref[...])
    y_ref[...] = (_group_rmsnorm(y, gw) * nw_ref[...]).astype(y_ref.dtype)

    @pl.when(n == pl.num_programs(1) - 1)
    def _():
        sfin_ref[0] = s_scr[...]
        conv_ref[0, :, 0:M_DI] = ux[c - (M_CONV - 1):, :]
        conv_ref[0, :, M_DI:] = ubc[c - (M_CONV - 1):, :]


def _ssd_prompt(proj, dt, conv_w, conv_b, dt_bias, a_log, d_exp, norm_w, e2, *, batch, seq, prev=None):
    c = M_CHUNK
    nchunk = seq // c
    tok = lambda j: pl.BlockSpec((c, M_DI), lambda b, n: (b * nchunk + n, j))
    const = lambda shape, j=0: pl.BlockSpec(shape, lambda b, n: (0, j))
    blocks = (3 * _nbytes((c, M_DI), F32) + _nbytes((c, M_DI), BF16) + _nbytes((2 * V7X_LANES, M_DI), BF16)
              + 2 * _nbytes((M_HEADS, M_HEADDIM, M_STATE), F32) + 12 * _nbytes((c, M_DI), F32))
    body, xspecs, xops, alias = _shared_output(_ssd_prompt_body, 13, prev)
    return pl.pallas_call(
        body,
        grid=(batch, nchunk),
        in_specs=[tok(4), tok(5), tok(6),
                  pl.BlockSpec((c, V7X_LANES), lambda b, n: (b * nchunk + n, 0)),
                  const((M_CONV, M_DI), 0), const((M_CONV, M_DI), 1), const((1, M_DI), 0), const((1, M_DI), 1),
                  const((1, V7X_LANES)), const((1, V7X_LANES)), const((1, M_DI)), const((1, M_DI)),
                  const((2 * V7X_LANES, M_DI))] + xspecs,
        out_specs=[pl.BlockSpec((c, M_DI), lambda b, n: (b * nchunk + n, HG_VW // M_DI)),
                   pl.BlockSpec((1, M_HEADS, M_HEADDIM, M_STATE), lambda b, n: (b, 0, 0, 0)),
                   pl.BlockSpec((1, M_CONV - 1, M_CONV_DIM), lambda b, n: (b, 0, 0))],
        out_shape=[jax.ShapeDtypeStruct((proj.shape[0], MIX0_WIDTH), BF16),
                   jax.ShapeDtypeStruct((batch, M_HEADS, M_HEADDIM, M_STATE), F32),
                   jax.ShapeDtypeStruct((batch, M_CONV - 1, M_CONV_DIM), F32)],
        scratch_shapes=[pltpu.VMEM((M_HEADS, M_HEADDIM, M_STATE), F32),
                        pltpu.VMEM((V7X_SUBLANES, M_DI), F32), pltpu.VMEM((V7X_SUBLANES, M_DI), F32)],
        input_output_aliases=alias,
        compiler_params=_params(("arbitrary", "arbitrary"), blocks),
        name="ssd_prompt",
    )(proj, proj, proj, dt, conv_w, conv_w, conv_b, conv_b, dt_bias, a_log, d_exp, norm_w, e2, *xops)


SSD_SAMPLE_BB = 8


def _shift_rows(a, j):
    return a if j == 0 else pltpu.roll(a, j, 0)


def _ssd_sample_body(z_ref, x_ref, bc_ref, dt_ref, s_ref, cs_ref, cwx_ref, cwb_ref, cbx_ref, cbb_ref, dtb_ref,
                     alog_ref, dexp_ref, nw_ref, e2_ref, y_ref, so_ref, co_ref, *, seq):
    rows = V7X_SUBLANES
    hist = M_CONV - 1
    assert rows == 2 * seq and hist <= seq and hist <= rows - seq
    rvec = lax.broadcasted_iota(jnp.int32, (rows, 1), 0)
    valid = rvec < seq
    lane = lax.broadcasted_iota(jnp.int32, (1, V7X_LANES), 1)
    r16 = lax.broadcasted_iota(jnp.int32, (2 * rows, M_STATE), 0)
    ones_rows = jnp.where((r16 == rows) | (r16 == rows + 1), 1.0, 0.0).astype(BF16)
    e2 = e2_ref[...]
    a_neg = -jnp.exp(alog_ref[...])
    gw = M_HPG * M_HEADDIM
    ngn = M_GROUPS * M_STATE

    def conv(u8, buf, w_ref, b_ref):
        buf8 = jnp.concatenate([buf, jnp.zeros((rows - hist, buf.shape[1]), F32)], axis=0)
        ext = jnp.where(valid, u8, _shift_rows(buf8, rows - hist))
        acc = b_ref[...] + w_ref[hist:hist + 1, :] * ext
        for j in range(1, M_CONV):
            acc = acc + w_ref[hist - j:hist - j + 1, :] * _shift_rows(ext, j)
        new_hist = _shift_rows(ext, rows - (seq - hist))[0:hist, :]
        return _silu(acc), new_hist

    def one(b, u_x, u_bc, z8, dt8):
        xc, nhx = conv(u_x, cs_ref[b, :, 0:M_DI], cwx_ref, cbx_ref)
        bcc, nhb = conv(u_bc, cs_ref[b, :, M_DI:], cwb_ref, cbb_ref)
        co_ref[b, :, 0:M_DI] = nhx
        co_ref[b, :, M_DI:] = nhb
        dtp = jnp.where(valid, _softplus(dt8 + dtb_ref[...]), 0.0)
        cum = dtp * a_neg
        k = 1
        while k < seq:
            cum = cum + jnp.where(rvec >= k, _shift_rows(cum, k), 0.0)
            k *= 2
        last = cum[seq - 1:seq, :]
        xdt = xc * _expand_heads(dtp, e2)
        ecum = _expand_heads(jnp.exp(cum), e2)
        bmat = bcc[:, :ngn]
        cmat = bcc[:, ngn:]
        y = dexp_ref[...] * xc
        for j in range(seq):
            prod = cmat * _shift_rows(bmat, j)
            cbh = jnp.zeros((rows, V7X_LANES), F32)
            for g in range(M_GROUPS):
                cbg = jnp.sum(prod[:, g * M_STATE:(g + 1) * M_STATE], axis=-1, keepdims=True)
                cbh = jnp.where((lane // M_HPG) == g, cbg, cbh)
            coef = cbh * jnp.exp(cum - _shift_rows(cum, j))
            y = y + _expand_heads(coef, e2) * _shift_rows(xdt, j)
        xs = xdt * _expand_heads(jnp.exp(last - cum), e2)
        dec = _expand_heads(jnp.broadcast_to(jnp.exp(last), (rows, V7X_LANES)), e2)[0:1, :]
        dec_hi = dec.astype(BF16).astype(F32)
        lhs = jnp.concatenate([xs, dec_hi, dec - dec_hi, jnp.zeros((rows - 2, M_DI), F32)], axis=0).astype(BF16)
        parts = []
        for g in range(M_GROUPS):
            bg = bmat[:, g * M_STATE:(g + 1) * M_STATE].astype(BF16)
            cg = cmat[:, g * M_STATE:(g + 1) * M_STATE].astype(BF16)
            s0 = s_ref[b, g * M_HPG:(g + 1) * M_HPG].reshape(gw, M_STATE)
            parts.append(_dot(cg, s0.astype(BF16), NT) * ecum[:, g * gw:(g + 1) * gw])
            rhs = jnp.concatenate([jnp.concatenate([bg, jnp.zeros((rows, M_STATE), BF16)], axis=0), ones_rows], axis=1)
            both = _dot(lhs[:, g * gw:(g + 1) * gw], rhs, TN)
            s_new = both[:, M_STATE:] * s0 + both[:, :M_STATE]
            so_ref[b, g * M_HPG:(g + 1) * M_HPG] = s_new.reshape(M_HPG, M_HEADDIM, M_STATE)
        y = (y + jnp.concatenate(parts, axis=1)) * _silu(z8)
        return _group_rmsnorm(y, gw) * nw_ref[...]

    def pair(p, carry):
        r0 = pl.multiple_of(p * rows, rows)
        tiles = [ref[pl.ds(r0, rows), :] for ref in (x_ref, bc_ref, z_ref, dt_ref)]
        ys = []
        for j in range(2):
            ys.append(one(2 * p + j, *[_shift_rows(t, j * (rows - seq)) for t in tiles]))
        y_ref[pl.ds(r0, rows), :] = jnp.where(valid, ys[0], _shift_rows(ys[1], seq)).astype(y_ref.dtype)
        return carry

    lax.fori_loop(0, SSD_SAMPLE_BB // 2, pair, 0)


def _ssd_sample(proj, row0, dt, state, conv_state, conv_w, conv_b, dt_bias, a_log, d_exp, norm_w, e2, *,
                batch, seq, prev=None):
    bb = SSD_SAMPLE_BB
    tr = bb * seq
    tok = lambda j: pl.BlockSpec((tr, M_DI), lambda i: (row0 // tr + i, j))
    const = lambda shape, j=0: pl.BlockSpec(shape, lambda i: (0, j))
    sblk = pl.BlockSpec((bb, M_HEADS, M_HEADDIM, M_STATE), lambda i: (i, 0, 0, 0))
    cblk = pl.BlockSpec((bb, M_CONV - 1, M_CONV_DIM), lambda i: (i, 0, 0))
    blocks = (3 * _nbytes((tr, M_DI), F32) + 2 * _nbytes((bb, M_HEADS, M_HEADDIM, M_STATE), F32)
              + 2 * _nbytes((bb, V7X_SUBLANES, M_CONV_DIM), F32) + _nbytes((2 * V7X_LANES, M_DI), BF16))
    body, xspecs, xops, alias = _shared_output(functools.partial(_ssd_sample_body, seq=seq), 15, prev)
    return pl.pallas_call(
        body,
        grid=(batch // bb,),
        in_specs=[tok(4), tok(5), tok(6),
                  pl.BlockSpec((tr, V7X_LANES), lambda i: (row0 // tr + i, 0)),
                  sblk, cblk,
                  const((M_CONV, M_DI), 0), const((M_CONV, M_DI), 1), const((1, M_DI), 0), const((1, M_DI), 1),
                  const((1, V7X_LANES)), const((1, V7X_LANES)), const((1, M_DI)), const((1, M_DI)),
                  const((2 * V7X_LANES, M_DI))] + xspecs,
        out_specs=[pl.BlockSpec((tr, M_DI), lambda i: (row0 // tr + i, HG_VW // M_DI)), sblk, cblk],
        out_shape=[jax.ShapeDtypeStruct((proj.shape[0], MIX0_WIDTH), BF16),
                   jax.ShapeDtypeStruct(state.shape, F32),
                   jax.ShapeDtypeStruct(conv_state.shape, F32)],
        input_output_aliases=alias,
        compiler_params=_params(("arbitrary",), blocks),
        name="ssd_sample",
    )(proj, proj, proj, dt, state, conv_state, conv_w, conv_w, conv_b, conv_b, dt_bias, a_log, d_exp, norm_w, e2, *xops)


S5_TILES = 8
S5_TILE_IN = D_MODEL // S5_TILES
S5_TILE_ST = S5_STATE // S5_TILES


def _s5_discretize(a_re, a_im, log_dt, b_re, b_im, c_re, c_im):
    lam_re = jnp.minimum(a_re, -1e-4)
    lam_im = a_im
    dt = jnp.exp(log_dt)[:, None]
    mag = jnp.exp(lam_re * dt)
    ab_re = mag * jnp.cos(lam_im * dt)
    ab_im = mag * jnp.sin(lam_im * dt)
    den = lam_re * lam_re + lam_im * lam_im
    zr = ((ab_re - 1.0) * lam_re + ab_im * lam_im) / den
    zi = (ab_im * lam_re - (ab_re - 1.0) * lam_im) / den
    bb_re = zr[..., None] * b_re - zi[..., None] * b_im
    bb_im = zr[..., None] * b_im + zi[..., None] * b_re
    gpt = S5_GROUPS // S5_TILES
    eye = jnp.eye(gpt, dtype=F32)

    def pack_b(bb):
        return jnp.einsum('kgph,gm->kghmp', bb.reshape(S5_TILES, gpt, S5_P, S5_GROUP), eye).reshape(
            S5_TILES, S5_TILE_IN, S5_TILE_ST).astype(BF16)

    def pack_c(c):
        return jnp.einsum('kghp,gm->kgpmh', c.reshape(S5_TILES, gpt, S5_GROUP, S5_P), eye).reshape(
            S5_TILES, S5_TILE_ST, S5_TILE_IN).astype(BF16)

    slab = lambda a: _s5_slab_layout(a.reshape(S5_STATE))
    return slab(ab_re), slab(ab_im), pack_b(bb_re), pack_b(bb_im), pack_c(c_re), pack_c(c_im)


def _s5_bu_body(x_ref, wr_ref, wi_ref, or_ref, oi_ref):
    x = x_ref[...]
    or_ref[...] = _dot(x, wr_ref[0], NN)
    oi_ref[...] = _dot(x, wi_ref[0], NN)


def _s5_bu(xb, wb_re, wb_im, *, tm):
    t = xb.shape[0]
    wspec = pl.BlockSpec((1, S5_TILE_IN, S5_TILE_ST), lambda k, i: (k, 0, 0))
    ospec = pl.BlockSpec((tm, S5_TILE_ST), lambda k, i: (i, k))
    blocks = _nbytes((tm, S5_TILE_IN), BF16) + 2 * _nbytes((S5_TILE_IN, S5_TILE_ST), BF16) + 2 * _nbytes((tm, S5_TILE_ST), F32)
    return pl.pallas_call(
        _s5_bu_body,
        grid=(S5_TILES, t // tm),
        in_specs=[pl.BlockSpec((tm, S5_TILE_IN), lambda k, i: (i, k)), wspec, wspec],
        out_specs=[ospec, ospec],
        out_shape=[jax.ShapeDtypeStruct((t, S5_STATE), F32)] * 2,
        compiler_params=_params(("arbitrary", "arbitrary"), blocks),
        name="s5_bu",
    )(xb, wb_re, wb_im)


def _s5_scan_body(br_ref, bi_ref, h0r_ref, h0i_ref, ar_ref, ai_ref, *rest, bb, tc):
    hr_ref, hi_ref, fr_ref, fi_ref, cr_scr, ci_scr = rest[-6:]
    n = pl.program_id(1)

    @pl.when(n == 0)
    def _():
        cr_scr[...] = h0r_ref[...]
        ci_scr[...] = h0i_ref[...]

    ar = ar_ref[...]
    ai = ai_ref[...]

    def seq_body(b, carry):
        def step(t, h):
            hr, hi = h
            r = b * tc + t
            nr = ar * hr - ai * hi + br_ref[r]
            ni = ar * hi + ai * hr + bi_ref[r]
            hr_ref[r] = nr
            hi_ref[r] = ni
            return nr, ni

        hr, hi = lax.fori_loop(0, tc, step, (cr_scr[b], ci_scr[b]))
        cr_scr[b] = hr
        ci_scr[b] = hi
        return carry

    lax.fori_loop(0, bb, seq_body, 0)

    @pl.when(n == pl.num_programs(1) - 1)
    def _():
        fr_ref[...] = cr_scr[...]
        fi_ref[...] = ci_scr[...]


def _s5_scan(bu_re, bu_im, h0_re, h0_im, ab_re, ab_im, *, row0, batch, seq, bb, tc, prev=None, name):
    t, sub, w = bu_re.shape
    assert bb == 1 or tc == seq
    rb = bb * tc
    blk = pl.BlockSpec((rb, sub, w), lambda i, n: ((row0 + i * bb * seq) // rb + n, 0, 0))
    sblk = pl.BlockSpec((bb, sub, w), lambda i, n: (i, 0, 0))
    ablk = pl.BlockSpec((sub, w), lambda i, n: (0, 0))
    blocks = 4 * _nbytes((rb, sub, w), F32) + 6 * _nbytes((bb, sub, w), F32)
    keep = [] if prev is None else list(prev)
    n_in = 6
    return pl.pallas_call(
        functools.partial(_s5_scan_body, bb=bb, tc=tc),
        grid=(batch // bb, seq // tc),
        in_specs=[blk, blk, sblk, sblk, ablk, ablk] + [pl.BlockSpec(memory_space=pl.ANY)] * len(keep),
        out_specs=[blk, blk, sblk, sblk],
        out_shape=[jax.ShapeDtypeStruct(bu_re.shape, F32)] * 2 + [jax.ShapeDtypeStruct(h0_re.shape, F32)] * 2,
        scratch_shapes=[pltpu.VMEM((bb, sub, w), F32)] * 2,
        input_output_aliases={n_in + k: k for k in range(len(keep))},
        compiler_params=_params(("arbitrary", "arbitrary"), blocks),
        name=name,
    )(bu_re, bu_im, h0_re, h0_im, ab_re, ab_im, *keep)


def _s5_out_body(hr_ref, hi_ref, x_ref, d_ref, wr_ref, wi_ref, o_ref):
    y = _dot(hr_ref[...].astype(BF16), wr_ref[0], NN) - _dot(hi_ref[...].astype(BF16), wi_ref[0], NN)
    o_ref[...] = jax.nn.gelu(y + d_ref[...] * x_ref[...]).astype(o_ref.dtype)


def _s5_out(h_re, h_im, x, d_skip, wc_re, wc_im, *, tm):
    t = x.shape[0]
    hspec = pl.BlockSpec((tm, S5_TILE_ST), lambda k, i: (i, k))
    wspec = pl.BlockSpec((1, S5_TILE_ST, S5_TILE_IN), lambda k, i: (k, 0, 0))
    xspec = pl.BlockSpec((tm, S5_TILE_IN), lambda k, i: (i, k))
    blocks = (2 * _nbytes((tm, S5_TILE_ST), F32) + 2 * _nbytes((S5_TILE_ST, S5_TILE_IN), BF16)
              + _nbytes((tm, S5_TILE_IN), F32) + _nbytes((tm, S5_TILE_IN), BF16))
    return pl.pallas_call(
        _s5_out_body,
        grid=(S5_TILES, t // tm),
        in_specs=[hspec, hspec, xspec, pl.BlockSpec((1, S5_TILE_IN), lambda k, i: (0, k)), wspec, wspec],
        out_specs=xspec,
        out_shape=jax.ShapeDtypeStruct((t, D_MODEL), BF16),
        compiler_params=_params(("arbitrary", "arbitrary"), blocks),
        name="s5_out",
    )(h_re, h_im, x, d_skip.reshape(1, D_MODEL), wc_re, wc_im)


S5_PROMPT_ROWS = 256
S5_SAMPLE_ROWS = 128
S5_NBLK = S5_TILE_ST // V7X_LANES


def _s5_pitch(rows):
    return rows + (V7X_SUBLANES if (rows // V7X_SUBLANES) % 2 == 0 else 0)


def _s5_slab_layout(a):
    lead = a.shape[:-1]
    return jnp.swapaxes(a.reshape(lead + (S5_TILES, S5_NBLK, V7X_LANES)), -3, -2)


def _s5_from_slab_layout(a):
    lead = a.shape[:-3]
    return jnp.swapaxes(a, -3, -2).reshape(lead + (S5_STATE,))


def _s5_fused_body(xb_ref, x_ref, h0r_ref, h0i_ref, ar_ref, ai_ref, wbr_ref, wbi_ref, wcr_ref, wci_ref, d_ref, *rest,
                   n_seq, steps):
    act_ref, fr_ref, fi_ref, bur, bui, cr, ci = rest[-7:]
    hrs, his = bur, bui
    n = pl.program_id(1)
    rows = n_seq * steps
    S5_PITCH = _s5_pitch(rows)
    for s in range(S5_TILES):
        u = xb_ref[:, s * S5_TILE_IN:(s + 1) * S5_TILE_IN]
        br = _dot(u, wbr_ref[s], NN)
        bi = _dot(u, wbi_ref[s], NN)
        for j in range(S5_NBLK):
            bur[j, s * S5_PITCH:s * S5_PITCH + rows, :] = br[:, j * V7X_LANES:(j + 1) * V7X_LANES]
            bui[j, s * S5_PITCH:s * S5_PITCH + rows, :] = bi[:, j * V7X_LANES:(j + 1) * V7X_LANES]

    @pl.when(n == 0)
    def _():
        cr[...] = h0r_ref[...]
        ci[...] = h0i_ref[...]

    ar = [ar_ref[j] for j in range(S5_NBLK)]
    ai = [ai_ref[j] for j in range(S5_NBLK)]

    def seq_body(q, carry):
        def step(t, h):
            hr, hi = h
            idx = pl.ds(q * steps + t, S5_TILES, stride=S5_PITCH)
            nr, ni = [], []
            for j in range(S5_NBLK):
                nr.append(ar[j] * hr[j] - ai[j] * hi[j] + bur[j, idx, :])
                ni.append(ar[j] * hi[j] + ai[j] * hr[j] + bui[j, idx, :])
                hrs[j, idx, :] = nr[j]
                his[j, idx, :] = ni[j]
            return tuple(nr), tuple(ni)

        h0 = (tuple(cr[q, j] for j in range(S5_NBLK)), tuple(ci[q, j] for j in range(S5_NBLK)))
        hr, hi = lax.fori_loop(0, steps, step, h0, unroll=min(steps, 4))
        for j in range(S5_NBLK):
            cr[q, j] = hr[j]
            ci[q, j] = hi[j]
        return carry

    lax.fori_loop(0, n_seq, seq_body, 0)

    for s in range(S5_TILES):
        seg = slice(s * S5_PITCH, s * S5_PITCH + rows)
        hre = jnp.concatenate([hrs[j, seg, :] for j in range(S5_NBLK)], axis=1).astype(BF16)
        him = jnp.concatenate([his[j, seg, :] for j in range(S5_NBLK)], axis=1).astype(BF16)
        y = _dot(hre, wcr_ref[s], NN) - _dot(him, wci_ref[s], NN)
        cs = slice(s * S5_TILE_IN, (s + 1) * S5_TILE_IN)
        act_ref[:, cs] = jax.nn.gelu(y + d_ref[:, cs] * x_ref[:, cs]).astype(act_ref.dtype)

    @pl.when(n == pl.num_programs(1) - 1)
    def _():
        fr_ref[...] = cr[...]
        fi_ref[...] = ci[...]


def _s5_fused(xb, x, h0_re, h0_im, ab_re, ab_im, wb_re, wb_im, wc_re, wc_im, d_skip, *, row0, batch, seq, rows, n_seq,
              prev=None, name):
    t, d = x.shape
    steps = rows // n_seq
    assert n_seq == 1 or steps == seq
    S5_PITCH = _s5_pitch(rows)
    tok = lambda width: pl.BlockSpec((rows, width), lambda i, n: ((row0 + i * n_seq * seq) // rows + n, 0))
    st = pl.BlockSpec((n_seq, S5_NBLK, S5_TILES, V7X_LANES), lambda i, n: (i, 0, 0, 0))
    const = lambda shape: pl.BlockSpec(shape, lambda i, n: (0,) * len(shape), pipeline_mode=pl.Buffered(1))
    slab = pltpu.VMEM((S5_NBLK, S5_TILES * S5_PITCH, V7X_LANES), F32)
    stv = pltpu.VMEM((n_seq, S5_NBLK, S5_TILES, V7X_LANES), F32)
    blocks = (2 * (_nbytes((rows, d), BF16) * 2 + _nbytes((rows, d), F32))
              + 2 * 6 * _nbytes((n_seq, S5_STATE), F32) + 4 * _nbytes(wb_re.shape, BF16)
              + 2 * _nbytes((S5_NBLK, S5_TILES * S5_PITCH, V7X_LANES), F32))
    keep = [] if prev is None else [prev]
    n_in = 11
    return pl.pallas_call(
        functools.partial(_s5_fused_body, n_seq=n_seq, steps=steps),
        grid=(batch // n_seq, seq // steps),
        in_specs=[tok(d), tok(d), st, st, const(ab_re.shape), const(ab_im.shape), const(wb_re.shape),
                  const(wb_im.shape), const(wc_re.shape), const(wc_im.shape), const((1, d))]
                 + [pl.BlockSpec(memory_space=pl.ANY)] * len(keep),
        out_specs=[tok(d), st, st],
        out_shape=[jax.ShapeDtypeStruct((t, d), BF16), jax.ShapeDtypeStruct(h0_re.shape, F32),
                   jax.ShapeDtypeStruct(h0_im.shape, F32)],
        scratch_shapes=[slab, slab, stv, stv],
        input_output_aliases={n_in + k: k for k in range(len(keep))},
        compiler_params=pltpu.CompilerParams(dimension_semantics=("arbitrary", "arbitrary"),
                                             vmem_limit_bytes=int(min(blocks + COMPILER_SCRATCH_BYTES, V7X_SCOPED_VMEM_CAP))),
        name=name,
    )(xb, x, h0_re, h0_im, ab_re, ab_im, wb_re, wb_im, wc_re, wc_im, d_skip.reshape(1, d), *keep)


def _glu_body(a_ref, wa_ref, wb_ref, ba_ref, bb_ref, o_ref, wa_scr, wb_scr):
    @pl.when(pl.program_id(1) == 0)
    def _():
        wa_scr[...] = wa_ref[0].astype(BF16)
        wb_scr[...] = wb_ref[0].astype(BF16)

    a = a_ref[...]
    o_ref[...] = (_dot(a, wa_scr[...], NN) + ba_ref[...]) * jax.nn.sigmoid(_dot(a, wb_scr[...], NN) + bb_ref[...])


def _glu(a, wa, wb, ba, bb, layer, *, tm, tn):
    t, k = a.shape
    n = wa.shape[2]
    wspec = pl.BlockSpec((1, k, tn), lambda j, i: (layer, 0, j))
    bspec = pl.BlockSpec((1, tn), lambda j, i: (layer, j))
    blocks = _nbytes((tm, k), BF16) + 2 * _nbytes((k, tn), F32) + _nbytes((tm, tn), F32) + 2 * _nbytes((k, tn), BF16)
    return pl.pallas_call(
        _glu_body,
        grid=(n // tn, t // tm),
        in_specs=[pl.BlockSpec((tm, k), lambda j, i: (i, 0)), wspec, wspec, bspec, bspec],
        out_specs=pl.BlockSpec((tm, tn), lambda j, i: (i, j)),
        out_shape=jax.ShapeDtypeStruct((t, n), F32),
        scratch_shapes=[pltpu.VMEM((k, tn), BF16)] * 2,
        compiler_params=_params(("arbitrary", "arbitrary"), blocks),
        name="glu",
    )(a, wa, wb, ba, bb)


def _router_body(x_ref, w_ref, b_ref, g_ref):
    logits = _dot(w_ref[...], x_ref[...], NT, precision=HIGHEST) + b_ref[...]
    rows = [logits[e:e + 1, :] for e in range(N_EXPERTS)]
    m = functools.reduce(jnp.maximum, rows)
    ex = [jnp.exp(r - m) for r in rows]
    z = functools.reduce(jnp.add, ex)
    p = [v / z for v in ex]

    def top2_sum(a, b, c, d):
        hi1, lo1, hi2, lo2 = jnp.maximum(a, b), jnp.minimum(a, b), jnp.maximum(c, d), jnp.minimum(c, d)
        return jnp.maximum(hi1, hi2) + jnp.maximum(jnp.minimum(hi1, hi2), jnp.maximum(lo1, lo2))

    assert EXP_PER_GROUP == 4
    best = top2_sum(*p[0:EXP_PER_GROUP])
    gsel = jnp.zeros_like(best, dtype=jnp.int32)
    for g in range(1, N_EXP_GROUPS):
        s = top2_sum(*p[g * EXP_PER_GROUP:(g + 1) * EXP_PER_GROUP])
        better = s > best
        gsel = jnp.where(better, g, gsel)
        best = jnp.where(better, s, best)
    inner = []
    for i in range(EXP_PER_GROUP):
        v = p[i]
        for g in range(1, N_EXP_GROUPS):
            v = jnp.where(gsel == g, p[g * EXP_PER_GROUP + i], v)
        inner.append(v)

    def first_argmax(vals, skip=None):
        bv = bi = None
        for i, v in enumerate(vals):
            v = v if skip is None else jnp.where(skip == i, -jnp.inf, v)
            if bv is None:
                bv, bi = v, jnp.zeros_like(gsel)
            else:
                better = v > bv
                bi = jnp.where(better, i, bi)
                bv = jnp.where(better, v, bv)
        return bv, bi

    p1, i1 = first_argmax(inner)
    p2, i2 = first_argmax(inner, skip=i1)
    tot = p1 + p2
    e1 = gsel * EXP_PER_GROUP + i1
    e2 = gsel * EXP_PER_GROUP + i2
    for e in range(N_EXPERTS):
        g_ref[e:e + 1, :] = jnp.where(e1 == e, p1 / tot, 0.0) + jnp.where(e2 == e, p2 / tot, 0.0)


def _router(x, w_router, b_router, *, tm):
    t, d = x.shape
    blocks = _nbytes((tm, d), F32) + _nbytes((N_EXPERTS, d), F32) + _nbytes((N_EXPERTS, tm), F32)
    return pl.pallas_call(
        _router_body,
        grid=(t // tm,),
        in_specs=[pl.BlockSpec((tm, d), lambda i: (i, 0)), pl.BlockSpec((N_EXPERTS, d), lambda i: (0, 0)),
                  pl.BlockSpec((N_EXPERTS, 1), lambda i: (0, 0))],
        out_specs=pl.BlockSpec((N_EXPERTS, tm), lambda i: (0, i)),
        out_shape=jax.ShapeDtypeStruct((N_EXPERTS, t), F32),
        compiler_params=_params(("arbitrary",), blocks),
        name="router",
    )(x, w_router.T, b_router.reshape(N_EXPERTS, 1))


def _moe_dense_body(x_ref, g_ref, wg_ref, wu_ref, wd_ref, o_ref):
    e = pl.program_id(1)

    @pl.when(e == 0)
    def _():
        o_ref[...] = jnp.zeros_like(o_ref)

    x = x_ref[...]
    h = _dot(x, wg_ref[0], NN)
    u = _dot(x, wu_ref[0], NN)
    lane = lax.broadcasted_iota(jnp.int32, (1, N_EXPERTS), 1)
    gate = jnp.sum(jnp.where(lane == e, g_ref[...], 0.0), axis=1, keepdims=True)
    act = (_silu(h) * u * gate).astype(BF16)
    o_ref[...] += _dot(act, wd_ref[0], NN)


def _moe_dense(xb, gates, wg, wu, wd, *, tm):
    t, d = xb.shape
    f = wg.shape[2]
    blocks = (_nbytes((tm, d), BF16) + _nbytes((tm, V7X_LANES), F32) + 3 * _nbytes((d, f), BF16)
              + _nbytes((tm, d), F32) + 4 * _nbytes((tm, f), F32))
    return pl.pallas_call(
        _moe_dense_body,
        grid=(t // tm, N_EXPERTS),
        in_specs=[pl.BlockSpec((tm, d), lambda i, e: (i, 0)), pl.BlockSpec((tm, N_EXPERTS), lambda i, e: (i, 0)),
                  pl.BlockSpec((1, d, f), lambda i, e: (e, 0, 0)), pl.BlockSpec((1, d, f), lambda i, e: (e, 0, 0)),
                  pl.BlockSpec((1, f, d), lambda i, e: (e, 0, 0))],
        out_specs=pl.BlockSpec((tm, d), lambda i, e: (i, 0)),
        out_shape=jax.ShapeDtypeStruct((t, d), F32),
        compiler_params=_params(("arbitrary", "arbitrary"), blocks),
        name="moe_dense",
    )(xb, gates, wg, wu, wd)


TOKEN_TILE = 1088
ROUTER_TILE = 512
LN_TILE = 256


def _moe_block(x, xb, layer, w_router, b_router, w_gate, w_up, w_down):
    gates = _router(x, w_router, b_router, tm=ROUTER_TILE).T
    return _moe_dense(xb, gates, w_gate[layer].astype(BF16), w_up[layer].astype(BF16), w_down[layer].astype(BF16),
                      tm=TOKEN_TILE)


def kernel(x_prompt, x_sample, state_hgrn, state_ssm, state_conv, state_s5_re, state_s5_im, w_in0, hg_lb_logits, hg_norm_w, conv_w, conv_b, dt_bias, a_log, m_d, m_norm_w, w_out0, s5_a_re, s5_a_im, s5_log_dt, s5_b_re, s5_b_im, s5_c_re, s5_c_im, s5_d, glu_w_a, glu_b_a, glu_w_b, glu_b_b, w_router, b_router, w_gate, w_up, w_down, ln1_g, ln1_b, ln2_g, ln2_b):
    bp, lp, d = x_prompt.shape
    bs, ls, _ = x_sample.shape
    tp, ts = bp * lp, bs * ls
    tm = TOKEN_TILE
    x0 = jnp.concatenate([x_prompt.reshape(tp, d), x_sample.reshape(ts, d)], axis=0)
    x0b = x0.astype(BF16)
    lower_bounds = jnp.cumsum(jax.nn.softmax(hg_lb_logits.astype(F32), axis=0), axis=0)
    pad_lanes = lambda v: jnp.pad(v, (0, V7X_LANES - v.shape[0])).reshape(1, V7X_LANES)

    proj = _matmul_w32(x0b, w_in0, 0, tm=tm, tn=1024, n_cols=IN0_MAIN, name="in_proj")
    w_dt = jnp.pad(w_in0[0, :, IN0_MAIN:], ((0, 0), (0, V7X_LANES - M_HEADS))).astype(BF16)
    dt = _matmul(x0b, w_dt, tm=tm, tn=V7X_LANES, name="dt_proj")
    lb0 = lower_bounds[0]
    e2 = _head_expand_matrix()
    ssd_consts = (conv_w[0], conv_b[0].reshape(1, -1), pad_lanes(dt_bias[0]), pad_lanes(a_log[0]),
                  jnp.repeat(m_d[0], M_HEADDIM).reshape(1, M_DI), m_norm_w[0].reshape(1, M_DI), e2)
    mixed, hg_p = _hgrn_prompt(proj, lb0, hg_norm_w[0], batch=bp, seq=lp)
    mixed, hg_s = _hgrn_sample(proj, tp, state_hgrn[0], lb0, hg_norm_w[0], batch=bs, seq=ls, prev=mixed)
    mixed, ssm_p, conv_p = _ssd_prompt(proj, dt, *ssd_consts, batch=bp, seq=lp, prev=mixed)
    mixed, ssm_s, conv_s = _ssd_sample(proj, tp, dt, state_ssm[0], state_conv[0], *ssd_consts, batch=bs, seq=ls,
                                       prev=mixed)
    mix = _matmul_w32(mixed, w_out0, 0, tm=tm, tn=512, name="out_proj")
    x1, x1b = _res_layernorm(x0, mix, ln1_g[0], ln1_b[0], tm=LN_TILE, name="ln1_0")
    moe0 = _moe_block(x1, x1b, 0, w_router, b_router, w_gate, w_up, w_down)
    x2, x2b = _res_layernorm(x1, moe0, ln2_g[0], ln2_b[0], tm=LN_TILE, name="ln2_0")

    ab_re, ab_im, wb_re, wb_im, wc_re, wc_im = _s5_discretize(
        s5_a_re[0], s5_a_im[0], s5_log_dt[0], s5_b_re[0], s5_b_im[0], s5_c_re[0], s5_c_im[0])
    s5_consts = (ab_re, ab_im, wb_re, wb_im, wc_re, wc_im, s5_d[0])
    zeros = jnp.zeros((bp, S5_NBLK, S5_TILES, V7X_LANES), F32)
    act, s5r_p, s5i_p = _s5_fused(x2b, x2, zeros, zeros, *s5_consts, row0=0, batch=bp, seq=lp,
                                  rows=S5_PROMPT_ROWS, n_seq=1, name="s5_prompt")
    act, s5r_s, s5i_s = _s5_fused(x2b, x2, _s5_slab_layout(state_s5_re[0].reshape(bs, S5_STATE)),
                                  _s5_slab_layout(state_s5_im[0].reshape(bs, S5_STATE)), *s5_consts,
                                  row0=tp, batch=bs, seq=ls, rows=S5_SAMPLE_ROWS, n_seq=S5_SAMPLE_ROWS // ls,
                                  prev=act, name="s5_sample")
    mix1 = _glu(act, glu_w_a, glu_w_b, glu_b_a, glu_b_b, 0, tm=tm, tn=512)
    x3, x3b = _res_layernorm(x2, mix1, ln1_g[1], ln1_b[1], tm=LN_TILE, name="ln1_1")
    moe1 = _moe_block(x3, x3b, 1, w_router, b_router, w_gate, w_up, w_down)
    y_p, _ = _res_layernorm(x3, moe1, ln2_g[1], ln2_b[1], tm=LN_TILE, name="ln2_1p", row0=0, nrows=tp)
    y_s, _ = _res_layernorm(x3, moe1, ln2_g[1], ln2_b[1], tm=LN_TILE, name="ln2_1s", row0=tp, nrows=ts)

    s5_state = lambda a, b: _s5_from_slab_layout(a).reshape(1, b, S5_GROUPS, S5_P)
    return (y_p.reshape(bp, lp, d), y_s.reshape(bs, ls, d),
            hg_p[None], hg_s[None], ssm_p[None], ssm_s[None], conv_p[None], conv_s[None],
            s5_state(s5r_p, bp), s5_state(s5r_s, bs), s5_state(s5i_p, bp), s5_state(s5i_s, bs))
```

```python
import functools
import math

import jax
import jax.numpy as jnp
from jax import lax
from jax.experimental import pallas as pl
from jax.experimental.pallas import tpu as pltpu

F32 = jnp.float32
BF16 = jnp.bfloat16
HIGHEST = lax.Precision.HIGHEST

D_MODEL = 2048
DEPTH = 2
HG_HEADS = 16
HG_DK = 128
HG_DV = 128
HG_KW = HG_HEADS * HG_DK
HG_VW = HG_HEADS * HG_DV
HG_CHUNK = 64
M_DI = 2048
M_HEADDIM = 64
M_HEADS = 32
M_GROUPS = 8
M_HPG = 4
M_STATE = 128
M_CONV = 4
M_CONV_DIM = M_DI + 2 * M_GROUPS * M_STATE
M_CHUNK = 128
IN0_MAIN = 2 * HG_KW + 2 * HG_VW + M_DI + M_CONV_DIM
S5_GROUP = 16
S5_GROUPS = 128
S5_P = 64
S5_STATE = S5_GROUPS * S5_P
N_EXPERTS = 16
N_EXP_GROUPS = 4
EXP_PER_GROUP = 4
TOP_K = 2
D_EXPERT = 512
ALPHA = (2 * DEPTH) ** 0.25
LN_EPS = 1e-5
RMS_EPS = 1e-6

V7X_LANES = 128
V7X_SUBLANES = 8
V7X_VMEM_BYTES = 64 * 1024 * 1024
V7X_SCOPED_VMEM_CAP = 60000 * 1024
COMPILER_SCRATCH_BYTES = 16 * 1024 * 1024


def _params(semantics, block_bytes):
    limit = min(2 * block_bytes + COMPILER_SCRATCH_BYTES, V7X_SCOPED_VMEM_CAP)
    return pltpu.CompilerParams(dimension_semantics=semantics, vmem_limit_bytes=int(limit))


def _nbytes(shape, dtype):
    return math.prod(shape) * jnp.dtype(dtype).itemsize


def _silu(x):
    return x * jax.nn.sigmoid(x)


def _dot(a, b, dims, precision=None):
    return lax.dot_general(a, b, (dims, ((), ())), precision=precision, preferred_element_type=F32)


NN = ((1,), (0,))
NT = ((1,), (1,))
TN = ((0,), (0,))

MIX0_WIDTH = HG_VW + M_DI


def _shared_output(body, n_in, prev):
    if prev is None:
        return body, [], [], {}

    def with_prev(*refs):
        return body(*refs[:n_in], *refs[n_in + 1:])

    return with_prev, [pl.BlockSpec(memory_space=pl.ANY)], [prev], {n_in: 0}


def _mm_body(x_ref, w_ref, o_ref):
    o_ref[...] = _dot(x_ref[...], w_ref[...], NN).astype(o_ref.dtype)


def _matmul(x, w, *, tm, tn, n_cols=None, out_dtype=F32, name):
    m, k = x.shape
    n = n_cols if n_cols is not None else w.shape[1]
    blocks = _nbytes((tm, k), x.dtype) + _nbytes((k, tn), w.dtype) + _nbytes((tm, tn), out_dtype)
    return pl.pallas_call(
        _mm_body,
        grid=(n // tn, m // tm),
        in_specs=[pl.BlockSpec((tm, k), lambda j, i: (i, 0)), pl.BlockSpec((k, tn), lambda j, i: (0, j))],
        out_specs=pl.BlockSpec((tm, tn), lambda j, i: (i, j)),
        out_shape=jax.ShapeDtypeStruct((m, n), out_dtype),
        compiler_params=_params(("arbitrary", "arbitrary"), blocks),
        name=name,
    )(x, w)


def _mm_w32_body(x_ref, w_ref, o_ref, wb_scr):
    @pl.when(pl.program_id(1) == 0)
    def _():
        wb_scr[...] = w_ref[0].astype(BF16)

    o_ref[...] = _dot(x_ref[...], wb_scr[...], NN).astype(o_ref.dtype)


def _matmul_w32(x, w, layer, *, tm, tn, n_cols=None, name):
    m, k = x.shape
    n = n_cols if n_cols is not None else w.shape[2]
    blocks = _nbytes((tm, k), BF16) + _nbytes((k, tn), F32) + _nbytes((tm, tn), F32) + _nbytes((k, tn), BF16)
    return pl.pallas_call(
        _mm_w32_body,
        grid=(n // tn, m // tm),
        in_specs=[pl.BlockSpec((tm, k), lambda j, i: (i, 0)), pl.BlockSpec((1, k, tn), lambda j, i: (layer, 0, j))],
        out_specs=pl.BlockSpec((tm, tn), lambda j, i: (i, j)),
        out_shape=jax.ShapeDtypeStruct((m, n), F32),
        scratch_shapes=[pltpu.VMEM((k, tn), BF16)],
        compiler_params=_params(("arbitrary", "arbitrary"), blocks),
        name=name,
    )(x, w)


def _ln_body(x_ref, mix_ref, g_ref, b_ref, o_ref, ob_ref):
    z = ALPHA * x_ref[...] + mix_ref[...]
    mu = jnp.mean(z, -1, keepdims=True)
    zc = z - mu
    var = jnp.mean(zc * zc, -1, keepdims=True)
    y = zc * lax.rsqrt(var + LN_EPS) * g_ref[...] + b_ref[...]
    o_ref[...] = y
    ob_ref[...] = y.astype(BF16)


def _res_layernorm(x, mix, g, b, *, tm, name, row0=0, nrows=None):
    d = x.shape[1]
    nrows = x.shape[0] if nrows is None else nrows
    src = pl.BlockSpec((tm, d), lambda i: (row0 // tm + i, 0))
    dst = pl.BlockSpec((tm, d), lambda i: (i, 0))
    vec = pl.BlockSpec((1, d), lambda i: (0, 0))
    blocks = 3 * _nbytes((tm, d), F32) + _nbytes((tm, d), BF16)
    return pl.pallas_call(
        _ln_body,
        grid=(nrows // tm,),
        in_specs=[src, src, vec, vec],
        out_specs=[dst, dst],
        out_shape=[jax.ShapeDtypeStruct((nrows, d), F32), jax.ShapeDtypeStruct((nrows, d), BF16)],
        compiler_params=_params(("arbitrary",), blocks),
        name=name,
    )(x, mix, g.reshape(1, d), b.reshape(1, d))


def _hgrn_gates(q, f, lb):
    fg = lb + (1.0 - lb) * jax.nn.sigmoid(f)
    return _silu(q), 1.0 - fg, jnp.log(fg)


def _hgrn_out(o, g, nw):
    o = o * lax.rsqrt(jnp.mean(o * o, -1, keepdims=True) + RMS_EPS)
    return o * nw * _silu(g)


def _hgrn_prompt_body(q_ref, f_ref, v_ref, g_ref, lb_ref, nw_ref, o_ref, sfin_ref, st_scr):
    n = pl.program_id(1)
    c = HG_CHUNK

    @pl.when(n == 0)
    def _():
        st_scr[...] = jnp.zeros_like(st_scr)

    row = lax.broadcasted_iota(jnp.int32, (c, c), 0)
    col = lax.broadcasted_iota(jnp.int32, (c, c), 1)
    causal = row >= col
    tri = causal.astype(F32)
    q, k, logf = _hgrn_gates(q_ref[...], f_ref[...], lb_ref[...])
    v = v_ref[...].astype(BF16)
    cum = _dot(tri, logf, NN, precision=HIGHEST)
    mid = cum[c // 2 - 1:c // 2, :]
    last = cum[c - 1:c, :]
    qm = (q * jnp.exp(cum - mid)).astype(BF16)
    km = (k * jnp.exp(mid - cum)).astype(BF16)
    qi = (q * jnp.exp(cum)).astype(BF16)
    kl = (k * jnp.exp(last - cum)).astype(BF16)
    a_last = jnp.exp(last)
    gate = nw_ref[...] * _silu(g_ref[...])
    heads = [slice(h * HG_DK, (h + 1) * HG_DK) for h in range(HG_HEADS)]
    scores = [jnp.where(causal, _dot(qm[:, sl], km[:, sl], NT), 0.0).astype(BF16) for sl in heads]
    sts = [st_scr[h] for h in range(HG_HEADS)]
    outs = [_dot(qi[:, sl], sts[h].astype(BF16), NT) + _dot(scores[h], v[:, sl], NN) for h, sl in enumerate(heads)]
    for h, sl in enumerate(heads):
        st_scr[h] = a_last[:, sl] * sts[h] + _dot(v[:, sl], kl[:, sl], TN)
    for h, sl in enumerate(heads):
        o = outs[h]
        o = o * lax.rsqrt(jnp.mean(o * o, -1, keepdims=True) + RMS_EPS)
        o_ref[:, sl] = (o * gate[:, sl]).astype(o_ref.dtype)

    @pl.when(n == pl.num_programs(1) - 1)
    def _():
        sfin_ref[0] = st_scr[...]


def _hgrn_prompt(proj, lb, nw, *, batch, seq, prev=None):
    c = HG_CHUNK
    nchunk = seq // c
    blk = lambda j: pl.BlockSpec((c, HG_KW), lambda b, n: (b * nchunk + n, j))
    vec = pl.BlockSpec((1, HG_KW), lambda b, n: (0, 0))
    blocks = 4 * _nbytes((c, HG_KW), F32) + _nbytes((c, HG_VW), BF16) + 2 * _nbytes((HG_HEADS, HG_DV, HG_DK), F32)
    body, xspecs, xops, alias = _shared_output(_hgrn_prompt_body, 6, prev)
    o, st = pl.pallas_call(
        body,
        grid=(batch, nchunk),
        in_specs=[blk(0), blk(1), blk(2), blk(3), vec, vec] + xspecs,
        out_specs=[pl.BlockSpec((c, HG_VW), lambda b, n: (b * nchunk + n, 0)),
                   pl.BlockSpec((1, HG_HEADS, HG_DV, HG_DK), lambda b, n: (b, 0, 0, 0))],
        out_shape=[jax.ShapeDtypeStruct((proj.shape[0], MIX0_WIDTH), BF16),
                   jax.ShapeDtypeStruct((batch, HG_HEADS, HG_DV, HG_DK), F32)],
        scratch_shapes=[pltpu.VMEM((HG_HEADS, HG_DV, HG_DK), F32)],
        input_output_aliases=alias,
        compiler_params=_params(("arbitrary", "arbitrary"), blocks),
        name="hgrn_prompt",
    )(proj, proj, proj, proj, lb.reshape(1, HG_KW), nw.reshape(1, HG_VW), *xops)
    return o, jnp.swapaxes(st, -1, -2)


HG_SAMPLE_BB = 8


def _hgrn_sample_body(q_ref, f_ref, v_ref, g_ref, lb_ref, nw_ref, s_ref, o_ref, so_ref, *, seq):
    rows = 2 * seq
    assert rows == V7X_SUBLANES
    row = lax.broadcasted_iota(jnp.int32, (rows, rows), 0)
    col = lax.broadcasted_iota(jnp.int32, (rows, rows), 1)
    causal = (row >= col) & ((row // seq) == (col // seq))
    tri = causal.astype(F32)
    rvec = lax.broadcasted_iota(jnp.int32, (rows, 1), 0)
    r16 = lax.broadcasted_iota(jnp.int32, (2 * rows, HG_DV), 0)
    ones_rows = jnp.where((r16 == rows) | (r16 == rows + 1), 1.0, 0.0).astype(BF16)

    def pair(p, carry):
        r0 = pl.multiple_of(p * rows, rows)
        for h in range(HG_HEADS):
            sl = slice(h * HG_DK, (h + 1) * HG_DK)
            q, k, logf = _hgrn_gates(q_ref[pl.ds(r0, rows), sl], f_ref[pl.ds(r0, rows), sl], lb_ref[:, sl])
            v = v_ref[pl.ds(r0, rows), sl].astype(BF16)
            cum = _dot(tri, logf, NN, precision=HIGHEST)
            ecum = jnp.exp(cum)
            qi = (q * ecum).astype(BF16)
            km = (k / ecum).astype(BF16)
            scores = jnp.where(causal, _dot(qi, km, NT), 0.0).astype(BF16)
            o = _dot(scores, v, NN)
            for j in range(2):
                b = 2 * p + j
                mine = (rvec // seq) == j
                last = cum[(j + 1) * seq - 1:(j + 1) * seq, :]
                a = jnp.exp(last)
                a_hi = a.astype(BF16).astype(F32)
                a_lo = a - a_hi
                kl = jnp.where(mine, k * jnp.exp(last - cum), 0.0)
                lhs = jnp.concatenate([kl, a_hi, a_lo, jnp.zeros((rows - 2, HG_DK), F32)], axis=0).astype(BF16)
                rhs = jnp.concatenate(
                    [jnp.concatenate([v, jnp.zeros((rows, HG_DV), BF16)], axis=0), ones_rows], axis=1)
                both = _dot(lhs, rhs, TN)
                s0 = s_ref[b, h]
                o = o + jnp.where(mine, _dot(qi, s0.astype(BF16), NN), 0.0)
                so_ref[b, h] = both[:, HG_DV:] * s0 + both[:, :HG_DV]
            o_ref[pl.ds(r0, rows), sl] = _hgrn_out(o, g_ref[pl.ds(r0, rows), sl], nw_ref[:, sl]).astype(o_ref.dtype)
        return carry

    lax.fori_loop(0, HG_SAMPLE_BB // 2, pair, 0)


def _hgrn_sample(proj, row0, state, lb, nw, *, batch, seq, prev=None):
    bb = HG_SAMPLE_BB
    tr = bb * seq
    blk0 = row0 // tr
    blk = lambda j: pl.BlockSpec((tr, HG_KW), lambda i: (blk0 + i, j))
    vec = pl.BlockSpec((1, HG_KW), lambda i: (0, 0))
    sblk = pl.BlockSpec((bb, HG_HEADS, HG_DK, HG_DV), lambda i: (i, 0, 0, 0))
    blocks = 4 * _nbytes((tr, HG_KW), F32) + 2 * _nbytes((bb, HG_HEADS, HG_DK, HG_DV), F32)
    body, xspecs, xops, alias = _shared_output(functools.partial(_hgrn_sample_body, seq=seq), 7, prev)
    return pl.pallas_call(
        body,
        grid=(batch // bb,),
        in_specs=[blk(0), blk(1), blk(2), blk(3), vec, vec, sblk] + xspecs,
        out_specs=[pl.BlockSpec((tr, HG_VW), lambda i: (blk0 + i, 0)), sblk],
        out_shape=[jax.ShapeDtypeStruct((proj.shape[0], MIX0_WIDTH), BF16),
                   jax.ShapeDtypeStruct(state.shape, F32)],
        input_output_aliases=alias,
        compiler_params=_params(("arbitrary",), blocks),
        name="hgrn_sample",
    )(proj, proj, proj, proj, lb.reshape(1, HG_KW), nw.reshape(1, HG_VW), state, *xops)


def _head_expand_matrix():
    r = jnp.arange(2 * V7X_LANES)[:, None] % V7X_LANES
    c = jnp.arange(M_DI)[None, :] // M_HEADDIM
    return (r == c).astype(BF16)


def _expand_heads(coef, e2):
    hi = coef.astype(BF16)
    lo = (coef - hi.astype(F32)).astype(BF16)
    return _dot(jnp.concatenate([hi, lo], axis=1), e2, NN)


def _softplus(x):
    return jnp.maximum(x, 0.0) + jnp.log(1.0 + jnp.exp(-jnp.abs(x)))


def _group_rmsnorm(y, width):
    outs = []
    for g in range(y.shape[1] // width):
        yg = y[:, g * width:(g + 1) * width]
        outs.append(yg * lax.rsqrt(jnp.mean(yg * yg, -1, keepdims=True) + RMS_EPS))
    return jnp.concatenate(outs, axis=1)


def _ssd_prompt_body(z_ref, x_ref, bc_ref, dt_ref, cwx_ref, cwb_ref, cbx_ref, cbb_ref, dtb_ref, alog_ref,
                     dexp_ref, nw_ref, e2_ref, y_ref, sfin_ref, conv_ref, s_scr, cx_scr, cb_scr):
    n = pl.program_id(1)
    c = M_CHUNK
    tail = V7X_SUBLANES

    @pl.when(n == 0)
    def _():
        s_scr[...] = jnp.zeros_like(s_scr)
        cx_scr[...] = jnp.zeros_like(cx_scr)
        cb_scr[...] = jnp.zeros_like(cb_scr)

    def conv(u, carry_scr, w_ref, b_ref):
        ext = jnp.concatenate([carry_scr[...], u], axis=0)
        acc = b_ref[...] + w_ref[M_CONV - 1:M_CONV, :] * u
        for j in range(1, M_CONV):
            acc = acc + w_ref[M_CONV - 1 - j:M_CONV - j, :] * ext[tail - j:tail - j + c, :]
        carry_scr[...] = u[c - tail:, :]
        return _silu(acc)

    ux = x_ref[...]
    ubc = bc_ref[...]
    xc = conv(ux, cx_scr, cwx_ref, cbx_ref)
    bcc = conv(ubc, cb_scr, cwb_ref, cbb_ref)
    ngn = M_GROUPS * M_STATE

    row = lax.broadcasted_iota(jnp.int32, (c, c), 0)
    col = lax.broadcasted_iota(jnp.int32, (c, c), 1)
    causal = row >= col
    tri = causal.astype(F32)
    e2 = e2_ref[...]

    dtp = _softplus(dt_ref[...] + dtb_ref[...])
    da = dtp * (-jnp.exp(alog_ref[...]))
    cum = _dot(tri, da, NN, precision=HIGHEST)
    cum_t = cum.T
    last = cum[c - 1:c, :]
    xdt = xc * _expand_heads(dtp, e2)
    xend = (xdt * _expand_heads(jnp.exp(last - cum), e2)).astype(BF16)
    ecum = _expand_heads(jnp.exp(cum), e2)
    xdt_b = xdt.astype(BF16)
    gw = M_HPG * M_HEADDIM

    ys = []
    for g in range(M_GROUPS):
        bg = bcc[:, g * M_STATE:(g + 1) * M_STATE].astype(BF16)
        cg = bcc[:, ngn + g * M_STATE:ngn + (g + 1) * M_STATE].astype(BF16)
        cb = _dot(cg, bg, NT)
        sg = s_scr[g * M_HPG:(g + 1) * M_HPG].reshape(gw, M_STATE)
        y_inter = _dot(cg, sg.astype(BF16), NT) * ecum[:, g * gw:(g + 1) * gw]
        upd = _dot(xend[:, g * gw:(g + 1) * gw], bg, TN)
        parts = []
        for hh in range(M_HPG):
            h = g * M_HPG + hh
            decay = jnp.exp(jnp.where(causal, cum[:, h:h + 1] - cum_t[h:h + 1, :], -jnp.inf))
            m = (cb * decay).astype(BF16)
            parts.append(_dot(m, xdt_b[:, h * M_HEADDIM:(h + 1) * M_HEADDIM], NN))
            s_scr[h] = jnp.exp(last[:, h:h + 1]) * s_scr[h] + upd[hh * M_HEADDIM:(hh + 1) * M_HEADDIM, :]
        ys.append(jnp.concatenate(parts, axis=1) + y_inter)
    y = jnp.concatenate(ys, axis=1) + dexp_ref[...] * xc
    y = y * _silu(z_ref[...])
    y_ref[...] = (_group_rmsnorm(y, gw) * nw_ref[...]).astype(y_ref.dtype)

    @pl.when(n == pl.num_programs(1) - 1)
    def _():
        sfin_ref[0] = s_scr[...]
        conv_ref[0, :, 0:M_DI] = ux[c - (M_CONV - 1):, :]
        conv_ref[0, :, M_DI:] = ubc[c - (M_CONV - 1):, :]


def _ssd_prompt(proj, dt, conv_w, conv_b, dt_bias, a_log, d_exp, norm_w, e2, *, batch, seq, prev=None):
    c = M_CHUNK
    nchunk = seq // c
    tok = lambda j: pl.BlockSpec((c, M_DI), lambda b, n: (b * nchunk + n, j))
    const = lambda shape, j=0: pl.BlockSpec(shape, lambda b, n: (0, j))
    blocks = (3 * _nbytes((c, M_DI), F32) + _nbytes((c, M_DI), BF16) + _nbytes((2 * V7X_LANES, M_DI), BF16)
              + 2 * _nbytes((M_HEADS, M_HEADDIM, M_STATE), F32) + 12 * _nbytes((c, M_DI), F32))
    body, xspecs, xops, alias = _shared_output(_ssd_prompt_body, 13, prev)
    return pl.pallas_call(
        body,
        grid=(batch, nchunk),
        in_specs=[tok(4), tok(5), tok(6),
                  pl.BlockSpec((c, V7X_LANES), lambda b, n: (b * nchunk + n, 0)),
                  const((M_CONV, M_DI), 0), const((M_CONV, M_DI), 1), const((1, M_DI), 0), const((1, M_DI), 1),
                  const((1, V7X_LANES)), const((1, V7X_LANES)), const((1, M_DI)), const((1, M_DI)),
                  const((2 * V7X_LANES, M_DI))] + xspecs,
        out_specs=[pl.BlockSpec((c, M_DI), lambda b, n: (b * nchunk + n, HG_VW // M_DI)),
                   pl.BlockSpec((1, M_HEADS, M_HEADDIM, M_STATE), lambda b, n: (b, 0, 0, 0)),
                   pl.BlockSpec((1, M_CONV - 1, M_CONV_DIM), lambda b, n: (b, 0, 0))],
        out_shape=[jax.ShapeDtypeStruct((proj.shape[0], MIX0_WIDTH), BF16),
                   jax.ShapeDtypeStruct((batch, M_HEADS, M_HEADDIM, M_STATE), F32),
                   jax.ShapeDtypeStruct((batch, M_CONV - 1, M_CONV_DIM), F32)],
        scratch_shapes=[pltpu.VMEM((M_HEADS, M_HEADDIM, M_STATE), F32),
                        pltpu.VMEM((V7X_SUBLANES, M_DI), F32), pltpu.VMEM((V7X_SUBLANES, M_DI), F32)],
        input_output_aliases=alias,
        compiler_params=_params(("arbitrary", "arbitrary"), blocks),
        name="ssd_prompt",
    )(proj, proj, proj, dt, conv_w, conv_w, conv_b, conv_b, dt_bias, a_log, d_exp, norm_w, e2, *xops)


SSD_SAMPLE_BB = 8


def _shift_rows(a, j):
    return a if j == 0 else pltpu.roll(a, j, 0)


def _ssd_sample_body(z_ref, x_ref, bc_ref, dt_ref, s_ref, cs_ref, cwx_ref, cwb_ref, cbx_ref, cbb_ref, dtb_ref,
                     alog_ref, dexp_ref, nw_ref, e2_ref, y_ref, so_ref, co_ref, *, seq):
    rows = V7X_SUBLANES
    hist = M_CONV - 1
    assert rows == 2 * seq and hist <= seq and hist <= rows - seq
    rvec = lax.broadcasted_iota(jnp.int32, (rows, 1), 0)
    valid = rvec < seq
    lane = lax.broadcasted_iota(jnp.int32, (1, V7X_LANES), 1)
    r16 = lax.broadcasted_iota(jnp.int32, (2 * rows, M_STATE), 0)
    ones_rows = jnp.where((r16 == rows) | (r16 == rows + 1), 1.0, 0.0).astype(BF16)
    e2 = e2_ref[...]
    a_neg = -jnp.exp(alog_ref[...])
    gw = M_HPG * M_HEADDIM
    ngn = M_GROUPS * M_STATE

    def conv(u8, buf, w_ref, b_ref):
        buf8 = jnp.concatenate([buf, jnp.zeros((rows - hist, buf.shape[1]), F32)], axis=0)
        ext = jnp.where(valid, u8, _shift_rows(buf8, rows - hist))
        acc = b_ref[...] + w_ref[hist:hist + 1, :] * ext
        for j in range(1, M_CONV):
            acc = acc + w_ref[hist - j:hist - j + 1, :] * _shift_rows(ext, j)
        new_hist = _shift_rows(ext, rows - (seq - hist))[0:hist, :]
        return _silu(acc), new_hist

    def one(b, u_x, u_bc, z8, dt8):
        xc, nhx = conv(u_x, cs_ref[b, :, 0:M_DI], cwx_ref, cbx_ref)
        bcc, nhb = conv(u_bc, cs_ref[b, :, M_DI:], cwb_ref, cbb_ref)
        co_ref[b, :, 0:M_DI] = nhx
        co_ref[b, :, M_DI:] = nhb
        dtp = jnp.where(valid, _softplus(dt8 + dtb_ref[...]), 0.0)
        cum = dtp * a_neg
        k = 1
        while k < seq:
            cum = cum + jnp.where(rvec >= k, _shift_rows(cum, k), 0.0)
            k *= 2
        last = cum[seq - 1:seq, :]
        xdt = xc * _expand_heads(dtp, e2)
        ecum = _expand_heads(jnp.exp(cum), e2)
        bmat = bcc[:, :ngn]
        cmat = bcc[:, ngn:]
        y = dexp_ref[...] * xc
        for j in range(seq):
            prod = cmat * _shift_rows(bmat, j)
            cbh = jnp.zeros((rows, V7X_LANES), F32)
            for g in range(M_GROUPS):
                cbg = jnp.sum(prod[:, g * M_STATE:(g + 1) * M_STATE], axis=-1, keepdims=True)
                cbh = jnp.where((lane // M_HPG) == g, cbg, cbh)
            coef = cbh * jnp.exp(cum - _shift_rows(cum, j))
            y = y + _expand_heads(coef, e2) * _shift_rows(xdt, j)
        xs = xdt * _expand_heads(jnp.exp(last - cum), e2)
        dec = _expand_heads(jnp.broadcast_to(jnp.exp(last), (rows, V7X_LANES)), e2)[0:1, :]
        dec_hi = dec.astype(BF16).astype(F32)
        lhs = jnp.concatenate([xs, dec_hi, dec - dec_hi, jnp.zeros((rows - 2, M_DI), F32)], axis=0).astype(BF16)
        parts = []
        for g in range(M_GROUPS):
            bg = bmat[:, g * M_STATE:(g + 1) * M_STATE].astype(BF16)
            cg = cmat[:, g * M_STATE:(g + 1) * M_STATE].astype(BF16)
            s0 = s_ref[b, g * M_HPG:(g + 1) * M_HPG].reshape(gw, M_STATE)
            parts.append(_dot(cg, s0.astype(BF16), NT) * ecum[:, g * gw:(g + 1) * gw])
            rhs = jnp.concatenate([jnp.concatenate([bg, jnp.zeros((rows, M_STATE), BF16)], axis=0), ones_rows], axis=1)
            both = _dot(lhs[:, g * gw:(g + 1) * gw], rhs, TN)
            s_new = both[:, M_STATE:] * s0 + both[:, :M_STATE]
            so_ref[b, g * M_HPG:(g + 1) * M_HPG] = s_new.reshape(M_HPG, M_HEADDIM, M_STATE)
        y = (y + jnp.concatenate(parts, axis=1)) * _silu(z8)
        return _group_rmsnorm(y, gw) * nw_ref[...]

    def pair(p, carry):
        r0 = pl.multiple_of(p * rows, rows)
        tiles = [ref[pl.ds(r0, rows), :] for ref in (x_ref, bc_ref, z_ref, dt_ref)]
        ys = []
        for j in range(2):
            ys.append(one(2 * p + j, *[_shift_rows(t, j * (rows - seq)) for t in tiles]))
        y_ref[pl.ds(r0, rows), :] = jnp.where(valid, ys[0], _shift_rows(ys[1], seq)).astype(y_ref.dtype)
        return carry

    lax.fori_loop(0, SSD_SAMPLE_BB // 2, pair, 0)


def _ssd_sample(proj, row0, dt, state, conv_state, conv_w, conv_b, dt_bias, a_log, d_exp, norm_w, e2, *,
                batch, seq, prev=None):
    bb = SSD_SAMPLE_BB
    tr = bb * seq
    tok = lambda j: pl.BlockSpec((tr, M_DI), lambda i: (row0 // tr + i, j))
    const = lambda shape, j=0: pl.BlockSpec(shape, lambda i: (0, j))
    sblk = pl.BlockSpec((bb, M_HEADS, M_HEADDIM, M_STATE), lambda i: (i, 0, 0, 0))
    cblk = pl.BlockSpec((bb, M_CONV - 1, M_CONV_DIM), lambda i: (i, 0, 0))
    blocks = (3 * _nbytes((tr, M_DI), F32) + 2 * _nbytes((bb, M_HEADS, M_HEADDIM, M_STATE), F32)
              + 2 * _nbytes((bb, V7X_SUBLANES, M_CONV_DIM), F32) + _nbytes((2 * V7X_LANES, M_DI), BF16))
    body, xspecs, xops, alias = _shared_output(functools.partial(_ssd_sample_body, seq=seq), 15, prev)
    return pl.pallas_call(
        body,
        grid=(batch // bb,),
        in_specs=[tok(4), tok(5), tok(6),
                  pl.BlockSpec((tr, V7X_LANES), lambda i: (row0 // tr + i, 0)),
                  sblk, cblk,
                  const((M_CONV, M_DI), 0), const((M_CONV, M_DI), 1), const((1, M_DI), 0), const((1, M_DI), 1),
                  const((1, V7X_LANES)), const((1, V7X_LANES)), const((1, M_DI)), const((1, M_DI)),
                  const((2 * V7X_LANES, M_DI))] + xspecs,
        out_specs=[pl.BlockSpec((tr, M_DI), lambda i: (row0 // tr + i, HG_VW // M_DI)), sblk, cblk],
        out_shape=[jax.ShapeDtypeStruct((proj.shape[0], MIX0_WIDTH), BF16),
                   jax.ShapeDtypeStruct(state.shape, F32),
                   jax.ShapeDtypeStruct(conv_state.shape, F32)],
        input_output_aliases=alias,
        compiler_params=_params(("arbitrary",), blocks),
        name="ssd_sample",
    )(proj, proj, proj, dt, state, conv_state, conv_w, conv_w, conv_b, conv_b, dt_bias, a_log, d_exp, norm_w, e2, *xops)


S5_TILES = 8
S5_TILE_IN = D_MODEL // S5_TILES
S5_TILE_ST = S5_STATE // S5_TILES


def _s5_discretize(a_re, a_im, log_dt, b_re, b_im, c_re, c_im):
    lam_re = jnp.minimum(a_re, -1e-4)
    lam_im = a_im
    dt = jnp.exp(log_dt)[:, None]
    mag = jnp.exp(lam_re * dt)
    ab_re = mag * jnp.cos(lam_im * dt)
    ab_im = mag * jnp.sin(lam_im * dt)
    den = lam_re * lam_re + lam_im * lam_im
    zr = ((ab_re - 1.0) * lam_re + ab_im * lam_im) / den
    zi = (ab_im * lam_re - (ab_re - 1.0) * lam_im) / den
    bb_re = zr[..., None] * b_re - zi[..., None] * b_im
    bb_im = zr[..., None] * b_im + zi[..., None] * b_re
    gpt = S5_GROUPS // S5_TILES
    eye = jnp.eye(gpt, dtype=F32)

    def pack_b(bb):
        return jnp.einsum('kgph,gm->kghmp', bb.reshape(S5_TILES, gpt, S5_P, S5_GROUP), eye).reshape(
            S5_TILES, S5_TILE_IN, S5_TILE_ST).astype(BF16)

    def pack_c(c):
        return jnp.einsum('kghp,gm->kgpmh', c.reshape(S5_TILES, gpt, S5_GROUP, S5_P), eye).reshape(
            S5_TILES, S5_TILE_ST, S5_TILE_IN).astype(BF16)

    slab = lambda a: _s5_slab_layout(a.reshape(S5_STATE))
    return slab(ab_re), slab(ab_im), pack_b(bb_re), pack_b(bb_im), pack_c(c_re), pack_c(c_im)


def _s5_bu_body(x_ref, wr_ref, wi_ref, or_ref, oi_ref):
    x = x_ref[...]
    or_ref[...] = _dot(x, wr_ref[0], NN)
    oi_ref[...] = _dot(x, wi_ref[0], NN)


def _s5_bu(xb, wb_re, wb_im, *, tm):
    t = xb.shape[0]
    wspec = pl.BlockSpec((1, S5_TILE_IN, S5_TILE_ST), lambda k, i: (k, 0, 0))
    ospec = pl.BlockSpec((tm, S5_TILE_ST), lambda k, i: (i, k))
    blocks = _nbytes((tm, S5_TILE_IN), BF16) + 2 * _nbytes((S5_TILE_IN, S5_TILE_ST), BF16) + 2 * _nbytes((tm, S5_TILE_ST), F32)
    return pl.pallas_call(
        _s5_bu_body,
        grid=(S5_TILES, t // tm),
        in_specs=[pl.BlockSpec((tm, S5_TILE_IN), lambda k, i: (i, k)), wspec, wspec],
        out_specs=[ospec, ospec],
        out_shape=[jax.ShapeDtypeStruct((t, S5_STATE), F32)] * 2,
        compiler_params=_params(("arbitrary", "arbitrary"), blocks),
        name="s5_bu",
    )(xb, wb_re, wb_im)


def _s5_scan_body(br_ref, bi_ref, h0r_ref, h0i_ref, ar_ref, ai_ref, *rest, bb, tc):
    hr_ref, hi_ref, fr_ref, fi_ref, cr_scr, ci_scr = rest[-6:]
    n = pl.program_id(1)

    @pl.when(n == 0)
    def _():
        cr_scr[...] = h0r_ref[...]
        ci_scr[...] = h0i_ref[...]

    ar = ar_ref[...]
    ai = ai_ref[...]

    def seq_body(b, carry):
        def step(t, h):
            hr, hi = h
            r = b * tc + t
            nr = ar * hr - ai * hi + br_ref[r]
            ni = ar * hi + ai * hr + bi_ref[r]
            hr_ref[r] = nr
            hi_ref[r] = ni
            return nr, ni

        hr, hi = lax.fori_loop(0, tc, step, (cr_scr[b], ci_scr[b]))
        cr_scr[b] = hr
        ci_scr[b] = hi
        return carry

    lax.fori_loop(0, bb, seq_body, 0)

    @pl.when(n == pl.num_programs(1) - 1)
    def _():
        fr_ref[...] = cr_scr[...]
        fi_ref[...] = ci_scr[...]


def _s5_scan(bu_re, bu_im, h0_re, h0_im, ab_re, ab_im, *, row0, batch, seq, bb, tc, prev=None, name):
    t, sub, w = bu_re.shape
    assert bb == 1 or tc == seq
    rb = bb * tc
    blk = pl.BlockSpec((rb, sub, w), lambda i, n: ((row0 + i * bb * seq) // rb + n, 0, 0))
    sblk = pl.BlockSpec((bb, sub, w), lambda i, n: (i, 0, 0))
    ablk = pl.BlockSpec((sub, w), lambda i, n: (0, 0))
    blocks = 4 * _nbytes((rb, sub, w), F32) + 6 * _nbytes((bb, sub, w), F32)
    keep = [] if prev is None else list(prev)
    n_in = 6
    return pl.pallas_call(
        functools.partial(_s5_scan_body, bb=bb, tc=tc),
        grid=(batch // bb, seq // tc),
        in_specs=[blk, blk, sblk, sblk, ablk, ablk] + [pl.BlockSpec(memory_space=pl.ANY)] * len(keep),
        out_specs=[blk, blk, sblk, sblk],
        out_shape=[jax.ShapeDtypeStruct(bu_re.shape, F32)] * 2 + [jax.ShapeDtypeStruct(h0_re.shape, F32)] * 2,
        scratch_shapes=[pltpu.VMEM((bb, sub, w), F32)] * 2,
        input_output_aliases={n_in + k: k for k in range(len(keep))},
        compiler_params=_params(("arbitrary", "arbitrary"), blocks),
        name=name,
    )(bu_re, bu_im, h0_re, h0_im, ab_re, ab_im, *keep)


def _s5_out_body(hr_ref, hi_ref, x_ref, d_ref, wr_ref, wi_ref, o_ref):
    y = _dot(hr_ref[...].astype(BF16), wr_ref[0], NN) - _dot(hi_ref[...].astype(BF16), wi_ref[0], NN)
    o_ref[...] = jax.nn.gelu(y + d_ref[...] * x_ref[...]).astype(o_ref.dtype)


def _s5_out(h_re, h_im, x, d_skip, wc_re, wc_im, *, tm):
    t = x.shape[0]
    hspec = pl.BlockSpec((tm, S5_TILE_ST), lambda k, i: (i, k))
    wspec = pl.BlockSpec((1, S5_TILE_ST, S5_TILE_IN), lambda k, i: (k, 0, 0))
    xspec = pl.BlockSpec((tm, S5_TILE_IN), lambda k, i: (i, k))
    blocks = (2 * _nbytes((tm, S5_TILE_ST), F32) + 2 * _nbytes((S5_TILE_ST, S5_TILE_IN), BF16)
              + _nbytes((tm, S5_TILE_IN), F32) + _nbytes((tm, S5_TILE_IN), BF16))
    return pl.pallas_call(
        _s5_out_body,
        grid=(S5_TILES, t // tm),
        in_specs=[hspec, hspec, xspec, pl.BlockSpec((1, S5_TILE_IN), lambda k, i: (0, k)), wspec, wspec],
        out_specs=xspec,
        out_shape=jax.ShapeDtypeStruct((t, D_MODEL), BF16),
        compiler_params=_params(("arbitrary", "arbitrary"), blocks),
        name="s5_out",
    )(h_re, h_im, x, d_skip.reshape(1, D_MODEL), wc_re, wc_im)


S5_PROMPT_ROWS = 256
S5_SAMPLE_ROWS = 128
S5_NBLK = S5_TILE_ST // V7X_LANES


def _s5_pitch(rows):
    return rows + (V7X_SUBLANES if (rows // V7X_SUBLANES) % 2 == 0 else 0)


def _s5_slab_layout(a):
    lead = a.shape[:-1]
    return jnp.swapaxes(a.reshape(lead + (S5_TILES, S5_NBLK, V7X_LANES)), -3, -2)


def _s5_from_slab_layout(a):
    lead = a.shape[:-3]
    return jnp.swapaxes(a, -3, -2).reshape(lead + (S5_STATE,))


def _s5_fused_body(xb_ref, x_ref, h0r_ref, h0i_ref, ar_ref, ai_ref, wbr_ref, wbi_ref, wcr_ref, wci_ref, d_ref, *rest,
                   n_seq, steps):
    act_ref, fr_ref, fi_ref, bur, bui, cr, ci = rest[-7:]
    hrs, his = bur, bui
    n = pl.program_id(1)
    rows = n_seq * steps
    S5_PITCH = _s5_pitch(rows)
    for s in range(S5_TILES):
        u = xb_ref[:, s * S5_TILE_IN:(s + 1) * S5_TILE_IN]
        br = _dot(u, wbr_ref[s], NN)
        bi = _dot(u, wbi_ref[s], NN)
        for j in range(S5_NBLK):
            bur[j, s * S5_PITCH:s * S5_PITCH + rows, :] = br[:, j * V7X_LANES:(j + 1) * V7X_LANES]
            bui[j, s * S5_PITCH:s * S5_PITCH + rows, :] = bi[:, j * V7X_LANES:(j + 1) * V7X_LANES]

    @pl.when(n == 0)
    def _():
        cr[...] = h0r_ref[...]
        ci[...] = h0i_ref[...]

    ar = [ar_ref[j] for j in range(S5_NBLK)]
    ai = [ai_ref[j] for j in range(S5_NBLK)]

    def seq_body(q, carry):
        def step(t, h):
            hr, hi = h
            idx = pl.ds(q * steps + t, S5_TILES, stride=S5_PITCH)
            nr, ni = [], []
            for j in range(S5_NBLK):
                nr.append(ar[j] * hr[j] - ai[j] * hi[j] + bur[j, idx, :])
                ni.append(ar[j] * hi[j] + ai[j] * hr[j] + bui[j, idx, :])
                hrs[j, idx, :] = nr[j]
                his[j, idx, :] = ni[j]
            return tuple(nr), tuple(ni)

        h0 = (tuple(cr[q, j] for j in range(S5_NBLK)), tuple(ci[q, j] for j in range(S5_NBLK)))
        hr, hi = lax.fori_loop(0, steps, step, h0, unroll=min(steps, 4))
        for j in range(S5_NBLK):
            cr[q, j] = hr[j]
            ci[q, j] = hi[j]
        return carry

    lax.fori_loop(0, n_seq, seq_body, 0)

    for s in range(S5_TILES):
        seg = slice(s * S5_PITCH, s * S5_PITCH + rows)
        hre = jnp.concatenate([hrs[j, seg, :] for j in range(S5_NBLK)], axis=1).astype(BF16)
        him = jnp.concatenate([his[j, seg, :] for j in range(S5_NBLK)], axis=1).astype(BF16)
        y = _dot(hre, wcr_ref[s], NN) - _dot(him, wci_ref[s], NN)
        cs = slice(s * S5_TILE_IN, (s + 1) * S5_TILE_IN)
        act_ref[:, cs] = jax.nn.gelu(y + d_ref[:, cs] * x_ref[:, cs]).astype(act_ref.dtype)

    @pl.when(n == pl.num_programs(1) - 1)
    def _():
        fr_ref[...] = cr[...]
        fi_ref[...] = ci[...]


def _s5_fused(xb, x, h0_re, h0_im, ab_re, ab_im, wb_re, wb_im, wc_re, wc_im, d_skip, *, row0, batch, seq, rows, n_seq,
              prev=None, name):
    t, d = x.shape
    steps = rows // n_seq
    assert n_seq == 1 or steps == seq
    S5_PITCH = _s5_pitch(rows)
    tok = lambda width: pl.BlockSpec((rows, width), lambda i, n: ((row0 + i * n_seq * seq) // rows + n, 0))
    st = pl.BlockSpec((n_seq, S5_NBLK, S5_TILES, V7X_LANES), lambda i, n: (i, 0, 0, 0))
    const = lambda shape: pl.BlockSpec(shape, lambda i, n: (0,) * len(shape), pipeline_mode=pl.Buffered(1))
    slab = pltpu.VMEM((S5_NBLK, S5_TILES * S5_PITCH, V7X_LANES), F32)
    stv = pltpu.VMEM((n_seq, S5_NBLK, S5_TILES, V7X_LANES), F32)
    blocks = (2 * (_nbytes((rows, d), BF16) * 2 + _nbytes((rows, d), F32))
              + 2 * 6 * _nbytes((n_seq, S5_STATE), F32) + 4 * _nbytes(wb_re.shape, BF16)
              + 2 * _nbytes((S5_NBLK, S5_TILES * S5_PITCH, V7X_LANES), F32))
    keep = [] if prev is None else [prev]
    n_in = 11
    return pl.pallas_call(
        functools.partial(_s5_fused_body, n_seq=n_seq, steps=steps),
        grid=(batch // n_seq, seq // steps),
        in_specs=[tok(d), tok(d), st, st, const(ab_re.shape), const(ab_im.shape), const(wb_re.shape),
                  const(wb_im.shape), const(wc_re.shape), const(wc_im.shape), const((1, d))]
                 + [pl.BlockSpec(memory_space=pl.ANY)] * len(keep),
        out_specs=[tok(d), st, st],
        out_shape=[jax.ShapeDtypeStruct((t, d), BF16), jax.ShapeDtypeStruct(h0_re.shape, F32),
                   jax.ShapeDtypeStruct(h0_im.shape, F32)],
        scratch_shapes=[slab, slab, stv, stv],
        input_output_aliases={n_in + k: k for k in range(len(keep))},
        compiler_params=pltpu.CompilerParams(dimension_semantics=("arbitrary", "arbitrary"),
                                             vmem_limit_bytes=int(min(blocks + COMPILER_SCRATCH_BYTES, V7X_SCOPED_VMEM_CAP))),
        name=name,
    )(xb, x, h0_re, h0_im, ab_re, ab_im, wb_re, wb_im, wc_re, wc_im, d_skip.reshape(1, d), *keep)


def _glu_body(a_ref, wa_ref, wb_ref, ba_ref, bb_ref, o_ref, wa_scr, wb_scr):
    @pl.when(pl.program_id(1) == 0)
    def _():
        wa_scr[...] = wa_ref[0].astype(BF16)
        wb_scr[...] = wb_ref[0].astype(BF16)

    a = a_ref[...]
    o_ref[...] = (_dot(a, wa_scr[...], NN) + ba_ref[...]) * jax.nn.sigmoid(_dot(a, wb_scr[...], NN) + bb_ref[...])


def _glu(a, wa, wb, ba, bb, layer, *, tm, tn):
    t, k = a.shape
    n = wa.shape[2]
    wspec = pl.BlockSpec((1, k, tn), lambda j, i: (layer, 0, j))
    bspec = pl.BlockSpec((1, tn), lambda j, i: (layer, j))
    blocks = _nbytes((tm, k), BF16) + 2 * _nbytes((k, tn), F32) + _nbytes((tm, tn), F32) + 2 * _nbytes((k, tn), BF16)
    return pl.pallas_call(
        _glu_body,
        grid=(n // tn, t // tm),
        in_specs=[pl.BlockSpec((tm, k), lambda j, i: (i, 0)), wspec, wspec, bspec, bspec],
        out_specs=pl.BlockSpec((tm, tn), lambda j, i: (i, j)),
        out_shape=jax.ShapeDtypeStruct((t, n), F32),
        scratch_shapes=[pltpu.VMEM((k, tn), BF16)] * 2,
        compiler_params=_params(("arbitrary", "arbitrary"), blocks),
        name="glu",
    )(a, wa, wb, ba, bb)


def _router_body(x_ref, w_ref, b_ref, e_ref, g_ref):
    logits = _dot(w_ref[...], x_ref[...], NT, precision=HIGHEST) + b_ref[...]
    rows = [logits[e:e + 1, :] for e in range(N_EXPERTS)]
    m = functools.reduce(jnp.maximum, rows)
    ex = [jnp.exp(r - m) for r in rows]
    z = functools.reduce(jnp.add, ex)
    p = [v / z for v in ex]

    def top2_sum(a, b, c, d):
        hi1, lo1, hi2, lo2 = jnp.maximum(a, b), jnp.minimum(a, b), jnp.maximum(c, d), jnp.minimum(c, d)
        return jnp.maximum(hi1, hi2) + jnp.maximum(jnp.minimum(hi1, hi2), jnp.maximum(lo1, lo2))

    assert EXP_PER_GROUP == 4
    best = top2_sum(*p[0:EXP_PER_GROUP])
    gsel = jnp.zeros_like(best, dtype=jnp.int32)
    for g in range(1, N_EXP_GROUPS):
        s = top2_sum(*p[g * EXP_PER_GROUP:(g + 1) * EXP_PER_GROUP])
        better = s > best
        gsel = jnp.where(better, g, gsel)
        best = jnp.where(better, s, best)
    inner = []
    for i in range(EXP_PER_GROUP):
        v = p[i]
        for g in range(1, N_EXP_GROUPS):
            v = jnp.where(gsel == g, p[g * EXP_PER_GROUP + i], v)
        inner.append(v)

    def first_argmax(vals, skip=None):
        bv = bi = None
        for i, v in enumerate(vals):
            v = v if skip is None else jnp.where(skip == i, -jnp.inf, v)
            if bv is None:
                bv, bi = v, jnp.zeros_like(gsel)
            else:
                better = v > bv
                bi = jnp.where(better, i, bi)
                bv = jnp.where(better, v, bv)
        return bv, bi

    p1, i1 = first_argmax(inner)
    p2, i2 = first_argmax(inner, skip=i1)
    tot = p1 + p2
    e1 = gsel * EXP_PER_GROUP + i1
    e2 = gsel * EXP_PER_GROUP + i2
    e_ref[0:1, :] = e1
    e_ref[1:2, :] = e2
    g_ref[0:1, :] = p1 / tot
    g_ref[1:2, :] = p2 / tot


def _router(x, w_router, b_router, *, tm):
    t, d = x.shape
    blocks = _nbytes((tm, d), F32) + _nbytes((N_EXPERTS, d), F32) + 2 * _nbytes((V7X_SUBLANES, tm), F32)
    out = pl.BlockSpec((TOP_K, tm), lambda i: (0, i))
    return pl.pallas_call(
        _router_body,
        grid=(t // tm,),
        in_specs=[pl.BlockSpec((tm, d), lambda i: (i, 0)), pl.BlockSpec((N_EXPERTS, d), lambda i: (0, 0)),
                  pl.BlockSpec((N_EXPERTS, 1), lambda i: (0, 0))],
        out_specs=[out, out],
        out_shape=[jax.ShapeDtypeStruct((TOP_K, t), jnp.int32), jax.ShapeDtypeStruct((TOP_K, t), F32)],
        compiler_params=_params(("arbitrary",), blocks),
        name="router",
    )(x, w_router.T, b_router.reshape(N_EXPERTS, 1))


MOE_TILE = 256
COMBINE_TILE = 256


def _moe_plan(eidx):
    k, t = eidx.shape
    ns = k * t
    n_tiles = ns // MOE_TILE + N_EXPERTS
    e_flat = eidx.reshape(ns)
    order = jnp.argsort(e_flat, stable=True).astype(jnp.int32)
    rank = jnp.argsort(order).astype(jnp.int32)
    counts = jnp.sum((e_flat[None, :] == jnp.arange(N_EXPERTS, dtype=jnp.int32)[:, None]).astype(jnp.int32), axis=1)
    start = jnp.cumsum(counts) - counts
    tiles = (counts + MOE_TILE - 1) // MOE_TILE
    tile_end = jnp.cumsum(tiles)
    tile_start = tile_end - tiles
    tile_ids = jnp.arange(n_tiles, dtype=jnp.int32)
    tile_expert = jnp.minimum(jnp.sum((tile_ids[:, None] >= tile_end[None, :]).astype(jnp.int32), axis=1),
                              N_EXPERTS - 1)
    pos = jnp.arange(n_tiles * MOE_TILE, dtype=jnp.int32)
    pe = tile_expert[pos // MOE_TILE]
    r = pos - tile_start[pe] * MOE_TILE
    valid = (r < counts[pe]) & (pos // MOE_TILE < tile_end[-1])
    token_of_position = jnp.where(valid, order[jnp.where(valid, start[pe] + r, 0)] % t, 0)
    position_of_slot = tile_start[e_flat] * MOE_TILE + rank - start[e_flat]
    return (tile_expert.astype(jnp.int32), tile_end[-1:].astype(jnp.int32), token_of_position.astype(jnp.int32),
            position_of_slot.astype(jnp.int32))


def _moe_expert_body(te_ref, nt_ref, tok_ref, x_hbm, wg_ref, wu_ref, wd_ref, y_ref, xbuf, sem, wg_scr, wu_scr, wd_scr):
    i = pl.program_id(0)
    nt = nt_ref[0]
    tm = MOE_TILE

    def gather_copy(tok, slot, r):
        return pltpu.make_async_copy(x_hbm.at[pl.ds(tok, 1)], xbuf.at[slot, pl.ds(r, 1)], sem.at[slot])

    def start_gather(tile, slot):
        def body(r, c):
            gather_copy(tok_ref[tile * tm + r], slot, r).start()
            return c

        lax.fori_loop(0, tm, body, 0, unroll=8)

    @pl.when(i == 0)
    def _():
        start_gather(0, 0)

    @pl.when(i + 1 < nt)
    def _():
        start_gather(i + 1, (i + 1) % 2)

    @pl.when(i < nt)
    def _():
        slot = i % 2
        pltpu.make_async_copy(x_hbm.at[pl.ds(0, tm)], xbuf.at[slot], sem.at[slot]).wait()

        @pl.when((i == 0) | (te_ref[i] != te_ref[jnp.maximum(i - 1, 0)]))
        def _():
            wg_scr[...] = wg_ref[0, 0].astype(BF16)
            wu_scr[...] = wu_ref[0, 0].astype(BF16)
            wd_scr[...] = wd_ref[0, 0].astype(BF16)

        x = xbuf[slot].astype(BF16)
        act = (_silu(_dot(x, wg_scr[...], NN)) * _dot(x, wu_scr[...], NN)).astype(BF16)
        y_ref[...] = _dot(act, wd_scr[...], NN)

    @pl.when(i >= nt)
    def _():
        y_ref[...] = jnp.zeros_like(y_ref)


def _moe_experts(x, plan, w_gate, w_up, w_down, layer):
    tile_expert, n_valid, token_of_position, _ = plan
    t, d = x.shape
    f = w_gate.shape[3]
    n_tiles = tile_expert.shape[0]
    wspec = lambda a, b: pl.BlockSpec((1, 1, a, b), lambda i, te, nt, tok: (layer, te[i], 0, 0))
    blocks = (2 * 3 * _nbytes((d, f), F32) + 3 * _nbytes((d, f), BF16) + 2 * _nbytes((MOE_TILE, d), F32)
              + 2 * _nbytes((MOE_TILE, d), F32) + 4 * _nbytes((MOE_TILE, f), F32))
    return pl.pallas_call(
        _moe_expert_body,
        grid_spec=pltpu.PrefetchScalarGridSpec(
            num_scalar_prefetch=3,
            grid=(n_tiles,),
            in_specs=[pl.BlockSpec(memory_space=pl.ANY), wspec(d, f), wspec(d, f), wspec(f, d)],
            out_specs=pl.BlockSpec((MOE_TILE, d), lambda i, te, nt, tok: (i, 0)),
            scratch_shapes=[pltpu.VMEM((2, MOE_TILE, d), F32), pltpu.SemaphoreType.DMA((2,)),
                            pltpu.VMEM((d, f), BF16), pltpu.VMEM((d, f), BF16), pltpu.VMEM((f, d), BF16)]),
        out_shape=jax.ShapeDtypeStruct((n_tiles * MOE_TILE, d), F32),
        compiler_params=pltpu.CompilerParams(dimension_semantics=("arbitrary",),
                                             vmem_limit_bytes=int(min(blocks + COMPILER_SCRATCH_BYTES, V7X_SCOPED_VMEM_CAP))),
        name="moe_experts",
    )(tile_expert, n_valid, token_of_position, x, w_gate, w_up, w_down)


def _moe_combine_body(pos_ref, x_ref, w_ref, g_ref, b_ref, y_hbm, o_ref, ob_ref, ybuf, sem, *, n_tok):
    i = pl.program_id(0)
    tc = COMBINE_TILE

    def gather_copy(p, slot, k, r):
        return pltpu.make_async_copy(y_hbm.at[pl.ds(p, 1)], ybuf.at[slot, k, pl.ds(r, 1)], sem.at[slot])

    def start_gather(tile, slot):
        def body(r, c):
            for k in range(TOP_K):
                gather_copy(pos_ref[k * n_tok + tile * tc + r], slot, k, r).start()
            return c

        lax.fori_loop(0, tc, body, 0, unroll=4)

    @pl.when(i == 0)
    def _():
        start_gather(0, 0)

    @pl.when(i + 1 < pl.num_programs(0))
    def _():
        start_gather(i + 1, (i + 1) % 2)

    slot = i % 2
    for k in range(TOP_K):
        pltpu.make_async_copy(y_hbm.at[pl.ds(0, tc)], ybuf.at[slot, k], sem.at[slot]).wait()
    z = ALPHA * x_ref[...]
    for k in range(TOP_K):
        z = z + w_ref[:, k:k + 1] * ybuf[slot, k]
    mu = jnp.mean(z, -1, keepdims=True)
    zc = z - mu
    var = jnp.mean(zc * zc, -1, keepdims=True)
    y = zc * lax.rsqrt(var + LN_EPS) * g_ref[...] + b_ref[...]
    o_ref[...] = y
    ob_ref[...] = y.astype(BF16)


def _moe_combine_layernorm(x, y_slots, plan, gate_w, g, b, *, name):
    position_of_slot = plan[3]
    t, d = x.shape
    tc = COMBINE_TILE
    row = pl.BlockSpec((tc, d), lambda i, pos: (i, 0))
    vec = pl.BlockSpec((1, d), lambda i, pos: (0, 0))
    blocks = 2 * (3 * _nbytes((tc, d), F32) + _nbytes((tc, d), BF16)) + 2 * TOP_K * _nbytes((tc, d), F32)
    return pl.pallas_call(
        functools.partial(_moe_combine_body, n_tok=t),
        grid_spec=pltpu.PrefetchScalarGridSpec(
            num_scalar_prefetch=1,
            grid=(t // tc,),
            in_specs=[row, pl.BlockSpec((tc, TOP_K), lambda i, pos: (i, 0)), vec, vec,
                      pl.BlockSpec(memory_space=pl.ANY)],
            out_specs=[row, row],
            scratch_shapes=[pltpu.VMEM((2, TOP_K, tc, d), F32), pltpu.SemaphoreType.DMA((2,))]),
        out_shape=[jax.ShapeDtypeStruct((t, d), F32), jax.ShapeDtypeStruct((t, d), BF16)],
        compiler_params=pltpu.CompilerParams(dimension_semantics=("arbitrary",),
                                             vmem_limit_bytes=int(min(blocks + COMPILER_SCRATCH_BYTES, V7X_SCOPED_VMEM_CAP))),
        name=name,
    )(position_of_slot, x, gate_w, g.reshape(1, d), b.reshape(1, d), y_slots)


def _moe_dense_body(x_ref, g_ref, wg_ref, wu_ref, wd_ref, o_ref):
    e = pl.program_id(1)

    @pl.when(e == 0)
    def _():
        o_ref[...] = jnp.zeros_like(o_ref)

    x = x_ref[...]
    h = _dot(x, wg_ref[0], NN)
    u = _dot(x, wu_ref[0], NN)
    lane = lax.broadcasted_iota(jnp.int32, (1, N_EXPERTS), 1)
    gate = jnp.sum(jnp.where(lane == e, g_ref[...], 0.0), axis=1, keepdims=True)
    act = (_silu(h) * u * gate).astype(BF16)
    o_ref[...] += _dot(act, wd_ref[0], NN)


def _moe_dense(xb, gates, wg, wu, wd, *, tm):
    t, d = xb.shape
    f = wg.shape[2]
    blocks = (_nbytes((tm, d), BF16) + _nbytes((tm, V7X_LANES), F32) + 3 * _nbytes((d, f), BF16)
              + _nbytes((tm, d), F32) + 4 * _nbytes((tm, f), F32))
    return pl.pallas_call(
        _moe_dense_body,
        grid=(t // tm, N_EXPERTS),
        in_specs=[pl.BlockSpec((tm, d), lambda i, e: (i, 0)), pl.BlockSpec((tm, N_EXPERTS), lambda i, e: (i, 0)),
                  pl.BlockSpec((1, d, f), lambda i, e: (e, 0, 0)), pl.BlockSpec((1, d, f), lambda i, e: (e, 0, 0)),
                  pl.BlockSpec((1, f, d), lambda i, e: (e, 0, 0))],
        out_specs=pl.BlockSpec((tm, d), lambda i, e: (i, 0)),
        out_shape=jax.ShapeDtypeStruct((t, d), F32),
        compiler_params=_params(("arbitrary", "arbitrary"), blocks),
        name="moe_dense",
    )(xb, gates, wg, wu, wd)


TOKEN_TILE = 1088
ROUTER_TILE = 512
LN_TILE = 256


def _moe_layernorm(x, layer, w_router, b_router, w_gate, w_up, w_down, g, b):
    eidx, gate_w = _router(x, w_router, b_router, tm=ROUTER_TILE)
    plan = _moe_plan(eidx)
    y_slots = _moe_experts(x, plan, w_gate, w_up, w_down, layer)
    return _moe_combine_layernorm(x, y_slots, plan, gate_w.T, g, b, name=f"moe_combine_ln_{layer}")


def kernel(x_prompt, x_sample, state_hgrn, state_ssm, state_conv, state_s5_re, state_s5_im, w_in0, hg_lb_logits, hg_norm_w, conv_w, conv_b, dt_bias, a_log, m_d, m_norm_w, w_out0, s5_a_re, s5_a_im, s5_log_dt, s5_b_re, s5_b_im, s5_c_re, s5_c_im, s5_d, glu_w_a, glu_b_a, glu_w_b, glu_b_b, w_router, b_router, w_gate, w_up, w_down, ln1_g, ln1_b, ln2_g, ln2_b):
    bp, lp, d = x_prompt.shape
    bs, ls, _ = x_sample.shape
    tp, ts = bp * lp, bs * ls
    tm = TOKEN_TILE
    x0 = jnp.concatenate([x_prompt.reshape(tp, d), x_sample.reshape(ts, d)], axis=0)
    x0b = x0.astype(BF16)
    lower_bounds = jnp.cumsum(jax.nn.softmax(hg_lb_logits.astype(F32), axis=0), axis=0)
    pad_lanes = lambda v: jnp.pad(v, (0, V7X_LANES - v.shape[0])).reshape(1, V7X_LANES)

    proj = _matmul_w32(x0b, w_in0, 0, tm=tm, tn=1024, n_cols=IN0_MAIN, name="in_proj")
    w_dt = jnp.pad(w_in0[0, :, IN0_MAIN:], ((0, 0), (0, V7X_LANES - M_HEADS))).astype(BF16)
    dt = _matmul(x0b, w_dt, tm=tm, tn=V7X_LANES, name="dt_proj")
    lb0 = lower_bounds[0]
    e2 = _head_expand_matrix()
    ssd_consts = (conv_w[0], conv_b[0].reshape(1, -1), pad_lanes(dt_bias[0]), pad_lanes(a_log[0]),
                  jnp.repeat(m_d[0], M_HEADDIM).reshape(1, M_DI), m_norm_w[0].reshape(1, M_DI), e2)
    mixed, hg_p = _hgrn_prompt(proj, lb0, hg_norm_w[0], batch=bp, seq=lp)
    mixed, hg_s = _hgrn_sample(proj, tp, state_hgrn[0], lb0, hg_norm_w[0], batch=bs, seq=ls, prev=mixed)
    mixed, ssm_p, conv_p = _ssd_prompt(proj, dt, *ssd_consts, batch=bp, seq=lp, prev=mixed)
    mixed, ssm_s, conv_s = _ssd_sample(proj, tp, dt, state_ssm[0], state_conv[0], *ssd_consts, batch=bs, seq=ls,
                                       prev=mixed)
    mix = _matmul_w32(mixed, w_out0, 0, tm=tm, tn=512, name="out_proj")
    x1, x1b = _res_layernorm(x0, mix, ln1_g[0], ln1_b[0], tm=LN_TILE, name="ln1_0")
    x2, x2b = _moe_layernorm(x1, 0, w_router, b_router, w_gate, w_up, w_down, ln2_g[0], ln2_b[0])

    ab_re, ab_im, wb_re, wb_im, wc_re, wc_im = _s5_discretize(
        s5_a_re[0], s5_a_im[0], s5_log_dt[0], s5_b_re[0], s5_b_im[0], s5_c_re[0], s5_c_im[0])
    s5_consts = (ab_re, ab_im, wb_re, wb_im, wc_re, wc_im, s5_d[0])
    zeros = jnp.zeros((bp, S5_NBLK, S5_TILES, V7X_LANES), F32)
    act, s5r_p, s5i_p = _s5_fused(x2b, x2, zeros, zeros, *s5_consts, row0=0, batch=bp, seq=lp,
                                  rows=S5_PROMPT_ROWS, n_seq=1, name="s5_prompt")
    act, s5r_s, s5i_s = _s5_fused(x2b, x2, _s5_slab_layout(state_s5_re[0].reshape(bs, S5_STATE)),
                                  _s5_slab_layout(state_s5_im[0].reshape(bs, S5_STATE)), *s5_consts,
                                  row0=tp, batch=bs, seq=ls, rows=S5_SAMPLE_ROWS, n_seq=S5_SAMPLE_ROWS // ls,
                                  prev=act, name="s5_sample")
    mix1 = _glu(act, glu_w_a, glu_w_b, glu_b_a, glu_b_b, 0, tm=tm, tn=512)
    x3, x3b = _res_layernorm(x2, mix1, ln1_g[1], ln1_b[1], tm=LN_TILE, name="ln1_1")
    x4, _ = _moe_layernorm(x3, 1, w_router, b_router, w_gate, w_up, w_down, ln2_g[1], ln2_b[1])

    s5_state = lambda a, b: _s5_from_slab_layout(a).reshape(1, b, S5_GROUPS, S5_P)
    return (x4[:tp].reshape(bp, lp, d), x4[tp:].reshape(bs, ls, d),
            hg_p[None], hg_s[None], ssm_p[None], ssm_s[None], conv_p[None], conv_s[None],
            s5_state(s5r_p, bp), s5_state(s5r_s, bs), s5_state(s5i_p, bp), s5_state(s5i_s, bs))
```

```python
import functools
import math

import jax
import jax.numpy as jnp
from jax import lax
from jax.experimental import pallas as pl
from jax.experimental.pallas import tpu as pltpu

F32 = jnp.float32
BF16 = jnp.bfloat16
HIGHEST = lax.Precision.HIGHEST

D_MODEL = 2048
DEPTH = 2
HG_HEADS = 16
HG_DK = 128
HG_DV = 128
HG_KW = HG_HEADS * HG_DK
HG_VW = HG_HEADS * HG_DV
HG_CHUNK = 64
M_DI = 2048
M_HEADDIM = 64
M_HEADS = 32
M_GROUPS = 8
M_HPG = 4
M_STATE = 128
M_CONV = 4
M_CONV_DIM = M_DI + 2 * M_GROUPS * M_STATE
M_CHUNK = 128
IN0_MAIN = 2 * HG_KW + 2 * HG_VW + M_DI + M_CONV_DIM
S5_GROUP = 16
S5_GROUPS = 128
S5_P = 64
S5_STATE = S5_GROUPS * S5_P
N_EXPERTS = 16
N_EXP_GROUPS = 4
EXP_PER_GROUP = 4
TOP_K = 2
D_EXPERT = 512
ALPHA = (2 * DEPTH) ** 0.25
LN_EPS = 1e-5
RMS_EPS = 1e-6

V7X_LANES = 128
V7X_SUBLANES = 8
V7X_VMEM_BYTES = 64 * 1024 * 1024
V7X_SCOPED_VMEM_CAP = 60000 * 1024
COMPILER_SCRATCH_BYTES = 16 * 1024 * 1024


def _params(semantics, block_bytes):
    limit = min(2 * block_bytes + COMPILER_SCRATCH_BYTES, V7X_SCOPED_VMEM_CAP)
    return pltpu.CompilerParams(dimension_semantics=semantics, vmem_limit_bytes=int(limit))


def _nbytes(shape, dtype):
    return math.prod(shape) * jnp.dtype(dtype).itemsize


def _silu(x):
    return x * jax.nn.sigmoid(x)


def _dot(a, b, dims, precision=None):
    return lax.dot_general(a, b, (dims, ((), ())), precision=precision, preferred_element_type=F32)


NN = ((1,), (0,))
NT = ((1,), (1,))
TN = ((0,), (0,))

MIX0_WIDTH = HG_VW + M_DI


def _shared_output(body, n_in, prev):
    if prev is None:
        return body, [], [], {}

    def with_prev(*refs):
        return body(*refs[:n_in], *refs[n_in + 1:])

    return with_prev, [pl.BlockSpec(memory_space=pl.ANY)], [prev], {n_in: 0}


def _mm_body(x_ref, w_ref, o_ref):
    o_ref[...] = _dot(x_ref[...], w_ref[...], NN).astype(o_ref.dtype)


def _matmul(x, w, *, tm, tn, n_cols=None, out_dtype=F32, name):
    m, k = x.shape
    n = n_cols if n_cols is not None else w.shape[1]
    blocks = _nbytes((tm, k), x.dtype) + _nbytes((k, tn), w.dtype) + _nbytes((tm, tn), out_dtype)
    return pl.pallas_call(
        _mm_body,
        grid=(n // tn, m // tm),
        in_specs=[pl.BlockSpec((tm, k), lambda j, i: (i, 0)), pl.BlockSpec((k, tn), lambda j, i: (0, j))],
        out_specs=pl.BlockSpec((tm, tn), lambda j, i: (i, j)),
        out_shape=jax.ShapeDtypeStruct((m, n), out_dtype),
        compiler_params=_params(("arbitrary", "arbitrary"), blocks),
        name=name,
    )(x, w)


def _mm_w32_body(x_ref, w_ref, o_ref, wb_scr):
    @pl.when(pl.program_id(1) == 0)
    def _():
        wb_scr[...] = w_ref[0].astype(BF16)

    o_ref[...] = _dot(x_ref[...], wb_scr[...], NN).astype(o_ref.dtype)


def _mm_w32t_body(x_ref, w_ref, o_ref, wb_scr):
    @pl.when(pl.program_id(1) == 0)
    def _():
        wb_scr[...] = w_ref[0].astype(BF16)

    o_ref[...] = _dot(x_ref[...], wb_scr[...], NT).astype(o_ref.dtype)


def _matmul_w32(x, w, layer, *, tm, tn, n_cols=None, transposed=False, name):
    m, k = x.shape
    n = n_cols if n_cols is not None else w.shape[1 if transposed else 2]
    blocks = _nbytes((tm, k), BF16) + _nbytes((k, tn), F32) + _nbytes((tm, tn), F32) + _nbytes((k, tn), BF16)
    if transposed:
        body, wblock, wspec = _mm_w32t_body, (tn, k), pl.BlockSpec((1, tn, k), lambda j, i: (layer, j, 0))
    else:
        body, wblock, wspec = _mm_w32_body, (k, tn), pl.BlockSpec((1, k, tn), lambda j, i: (layer, 0, j))
    return pl.pallas_call(
        body,
        grid=(n // tn, m // tm),
        in_specs=[pl.BlockSpec((tm, k), lambda j, i: (i, 0)), wspec],
        out_specs=pl.BlockSpec((tm, tn), lambda j, i: (i, j)),
        out_shape=jax.ShapeDtypeStruct((m, n), F32),
        scratch_shapes=[pltpu.VMEM(wblock, BF16)],
        compiler_params=_params(("arbitrary", "arbitrary"), blocks),
        name=name,
    )(x, w)


def _ln_body(x_ref, mix_ref, g_ref, b_ref, o_ref, ob_ref):
    z = ALPHA * x_ref[...] + mix_ref[...]
    mu = jnp.mean(z, -1, keepdims=True)
    zc = z - mu
    var = jnp.mean(zc * zc, -1, keepdims=True)
    y = zc * lax.rsqrt(var + LN_EPS) * g_ref[...] + b_ref[...]
    o_ref[...] = y
    ob_ref[...] = y.astype(BF16)


def _res_layernorm(x, mix, g, b, *, tm, name, row0=0, nrows=None):
    d = x.shape[1]
    nrows = x.shape[0] if nrows is None else nrows
    src = pl.BlockSpec((tm, d), lambda i: (row0 // tm + i, 0))
    dst = pl.BlockSpec((tm, d), lambda i: (i, 0))
    vec = pl.BlockSpec((1, d), lambda i: (0, 0))
    blocks = 3 * _nbytes((tm, d), F32) + _nbytes((tm, d), BF16)
    return pl.pallas_call(
        _ln_body,
        grid=(nrows // tm,),
        in_specs=[src, src, vec, vec],
        out_specs=[dst, dst],
        out_shape=[jax.ShapeDtypeStruct((nrows, d), F32), jax.ShapeDtypeStruct((nrows, d), BF16)],
        compiler_params=_params(("arbitrary",), blocks),
        name=name,
    )(x, mix, g.reshape(1, d), b.reshape(1, d))


def _hgrn_gates(q, f, lb):
    fg = lb + (1.0 - lb) * jax.nn.sigmoid(f)
    return _silu(q), 1.0 - fg, jnp.log(fg)


def _hgrn_out(o, g, nw):
    o = o * lax.rsqrt(jnp.mean(o * o, -1, keepdims=True) + RMS_EPS)
    return o * nw * _silu(g)


def _hgrn_prompt_body(q_ref, f_ref, v_ref, g_ref, lb_ref, nw_ref, o_ref, sfin_ref, st_scr):
    n = pl.program_id(1)
    c = HG_CHUNK

    @pl.when(n == 0)
    def _():
        st_scr[...] = jnp.zeros_like(st_scr)

    row = lax.broadcasted_iota(jnp.int32, (c, c), 0)
    col = lax.broadcasted_iota(jnp.int32, (c, c), 1)
    causal = row >= col
    tri = causal.astype(F32)
    q, k, logf = _hgrn_gates(q_ref[...], f_ref[...], lb_ref[...])
    v = v_ref[...].astype(BF16)
    cum = _dot(tri, logf, NN, precision=HIGHEST)
    mid = cum[c // 2 - 1:c // 2, :]
    last = cum[c - 1:c, :]
    qm = (q * jnp.exp(cum - mid)).astype(BF16)
    km = (k * jnp.exp(mid - cum)).astype(BF16)
    qi = (q * jnp.exp(cum)).astype(BF16)
    kl = (k * jnp.exp(last - cum)).astype(BF16)
    a_last = jnp.exp(last)
    gate = nw_ref[...] * _silu(g_ref[...])
    heads = [slice(h * HG_DK, (h + 1) * HG_DK) for h in range(HG_HEADS)]
    scores = [jnp.where(causal, _dot(qm[:, sl], km[:, sl], NT), 0.0).astype(BF16) for sl in heads]
    sts = [st_scr[h] for h in range(HG_HEADS)]
    outs = [_dot(qi[:, sl], sts[h].astype(BF16), NT) + _dot(scores[h], v[:, sl], NN) for h, sl in enumerate(heads)]
    for h, sl in enumerate(heads):
        st_scr[h] = a_last[:, sl] * sts[h] + _dot(v[:, sl], kl[:, sl], TN)
    for h, sl in enumerate(heads):
        o = outs[h]
        o = o * lax.rsqrt(jnp.mean(o * o, -1, keepdims=True) + RMS_EPS)
        o_ref[:, sl] = (o * gate[:, sl]).astype(o_ref.dtype)

    @pl.when(n == pl.num_programs(1) - 1)
    def _():
        sfin_ref[0] = st_scr[...]


def _hgrn_prompt(proj, lb, nw, *, batch, seq, prev=None):
    c = HG_CHUNK
    nchunk = seq // c
    blk = lambda j: pl.BlockSpec((c, HG_KW), lambda b, n: (b * nchunk + n, j))
    vec = pl.BlockSpec((1, HG_KW), lambda b, n: (0, 0))
    blocks = 4 * _nbytes((c, HG_KW), F32) + _nbytes((c, HG_VW), BF16) + 2 * _nbytes((HG_HEADS, HG_DV, HG_DK), F32)
    body, xspecs, xops, alias = _shared_output(_hgrn_prompt_body, 6, prev)
    o, st = pl.pallas_call(
        body,
        grid=(batch, nchunk),
        in_specs=[blk(0), blk(1), blk(2), blk(3), vec, vec] + xspecs,
        out_specs=[pl.BlockSpec((c, HG_VW), lambda b, n: (b * nchunk + n, 0)),
                   pl.BlockSpec((1, HG_HEADS, HG_DV, HG_DK), lambda b, n: (b, 0, 0, 0))],
        out_shape=[jax.ShapeDtypeStruct((proj.shape[0], MIX0_WIDTH), BF16),
                   jax.ShapeDtypeStruct((batch, HG_HEADS, HG_DV, HG_DK), F32)],
        scratch_shapes=[pltpu.VMEM((HG_HEADS, HG_DV, HG_DK), F32)],
        input_output_aliases=alias,
        compiler_params=_params(("arbitrary", "arbitrary"), blocks),
        name="hgrn_prompt",
    )(proj, proj, proj, proj, lb.reshape(1, HG_KW), nw.reshape(1, HG_VW), *xops)
    return o, jnp.swapaxes(st, -1, -2)


HG_SAMPLE_BB = 8


def _hgrn_sample_body(q_ref, f_ref, v_ref, g_ref, lb_ref, nw_ref, s_ref, o_ref, so_ref, *, seq):
    rows = 2 * seq
    assert rows == V7X_SUBLANES
    row = lax.broadcasted_iota(jnp.int32, (rows, rows), 0)
    col = lax.broadcasted_iota(jnp.int32, (rows, rows), 1)
    causal = (row >= col) & ((row // seq) == (col // seq))
    rvec = lax.broadcasted_iota(jnp.int32, (rows, 1), 0)
    r16 = lax.broadcasted_iota(jnp.int32, (2 * rows, HG_DV), 0)
    ones_rows = jnp.where((r16 == rows) | (r16 == rows + 1), 1.0, 0.0).astype(BF16)

    heads = [slice(h * HG_DK, (h + 1) * HG_DK) for h in range(HG_HEADS)]
    in_seq = rvec % seq

    def pair(p, carry):
        r0 = pl.multiple_of(p * rows, rows)
        tile = pl.ds(r0, rows)
        q, k, logf = _hgrn_gates(q_ref[tile, :], f_ref[tile, :], lb_ref[...])
        v = v_ref[tile, :].astype(BF16)
        cum = logf
        step = 1
        while step < seq:
            cum = cum + jnp.where(in_seq >= step, _shift_rows(cum, step), 0.0)
            step *= 2
        ecum = jnp.exp(cum)
        qi = (q * ecum).astype(BF16)
        km = (k / ecum).astype(BF16)
        gate = nw_ref[...] * _silu(g_ref[tile, :])
        v16 = jnp.concatenate([v, jnp.zeros_like(v)], axis=0)
        lhs = []
        for j in range(2):
            mine = (rvec // seq) == j
            last = cum[(j + 1) * seq - 1:(j + 1) * seq, :]
            a = jnp.exp(last)
            a_hi = a.astype(BF16).astype(F32)
            kl = jnp.where(mine, k * jnp.exp(last - cum), 0.0)
            lhs.append(jnp.concatenate([kl, a_hi, a - a_hi, jnp.zeros((rows - 2, HG_KW), F32)], axis=0).astype(BF16))
        scores = [jnp.where(causal, _dot(qi[:, sl], km[:, sl], NT), 0.0).astype(BF16) for sl in heads]
        outs = [_dot(scores[h], v[:, sl], NN) for h, sl in enumerate(heads)]
        for j in range(2):
            b = 2 * p + j
            mine = (rvec // seq) == j
            s0 = [s_ref[b, h] for h in range(HG_HEADS)]
            inter = [_dot(qi[:, sl], s0[h].astype(BF16), NN) for h, sl in enumerate(heads)]
            both = [_dot(lhs[j][:, sl], jnp.concatenate([v16[:, sl], ones_rows], axis=1), TN)
                    for sl in heads]
            for h in range(HG_HEADS):
                outs[h] = outs[h] + jnp.where(mine, inter[h], 0.0)
                so_ref[b, h] = both[h][:, HG_DV:] * s0[h] + both[h][:, :HG_DV]
        for h, sl in enumerate(heads):
            o = outs[h]
            o = o * lax.rsqrt(jnp.mean(o * o, -1, keepdims=True) + RMS_EPS)
            o_ref[tile, sl] = (o * gate[:, sl]).astype(o_ref.dtype)
        return carry

    lax.fori_loop(0, HG_SAMPLE_BB // 2, pair, 0)


def _hgrn_sample(proj, row0, state, lb, nw, *, batch, seq, prev=None):
    bb = HG_SAMPLE_BB
    tr = bb * seq
    blk0 = row0 // tr
    blk = lambda j: pl.BlockSpec((tr, HG_KW), lambda i: (blk0 + i, j))
    vec = pl.BlockSpec((1, HG_KW), lambda i: (0, 0))
    sblk = pl.BlockSpec((bb, HG_HEADS, HG_DK, HG_DV), lambda i: (i, 0, 0, 0))
    blocks = 4 * _nbytes((tr, HG_KW), F32) + 2 * _nbytes((bb, HG_HEADS, HG_DK, HG_DV), F32)
    body, xspecs, xops, alias = _shared_output(functools.partial(_hgrn_sample_body, seq=seq), 7, prev)
    return pl.pallas_call(
        body,
        grid=(batch // bb,),
        in_specs=[blk(0), blk(1), blk(2), blk(3), vec, vec, sblk] + xspecs,
        out_specs=[pl.BlockSpec((tr, HG_VW), lambda i: (blk0 + i, 0)), sblk],
        out_shape=[jax.ShapeDtypeStruct((proj.shape[0], MIX0_WIDTH), BF16),
                   jax.ShapeDtypeStruct(state.shape, F32)],
        input_output_aliases=alias,
        compiler_params=_params(("arbitrary",), blocks),
        name="hgrn_sample",
    )(proj, proj, proj, proj, lb.reshape(1, HG_KW), nw.reshape(1, HG_VW), state, *xops)


def _head_expand_matrix():
    r = jnp.arange(2 * V7X_LANES)[:, None] % V7X_LANES
    c = jnp.arange(M_DI)[None, :] // M_HEADDIM
    return (r == c).astype(BF16)


def _expand_heads(coef, e2):
    hi = coef.astype(BF16)
    lo = (coef - hi.astype(F32)).astype(BF16)
    return _dot(jnp.concatenate([hi, lo], axis=1), e2, NN)


def _softplus(x):
    return jnp.maximum(x, 0.0) + jnp.log(1.0 + jnp.exp(-jnp.abs(x)))


def _group_rmsnorm(y, width):
    outs = []
    for g in range(y.shape[1] // width):
        yg = y[:, g * width:(g + 1) * width]
        outs.append(yg * lax.rsqrt(jnp.mean(yg * yg, -1, keepdims=True) + RMS_EPS))
    return jnp.concatenate(outs, axis=1)


def _ssd_prompt_body(z_ref, x_ref, bc_ref, dt_ref, cwx_ref, cwb_ref, cbx_ref, cbb_ref, dtb_ref, alog_ref,
                     dexp_ref, nw_ref, e2_ref, y_ref, sfin_ref, conv_ref, s_scr, cx_scr, cb_scr):
    n = pl.program_id(1)
    c = M_CHUNK
    tail = V7X_SUBLANES

    @pl.when(n == 0)
    def _():
        s_scr[...] = jnp.zeros_like(s_scr)
        cx_scr[...] = jnp.zeros_like(cx_scr)
        cb_scr[...] = jnp.zeros_like(cb_scr)

    def conv(u, carry_scr, w_ref, b_ref):
        ext = jnp.concatenate([carry_scr[...], u], axis=0)
        acc = b_ref[...] + w_ref[M_CONV - 1:M_CONV, :] * u
        for j in range(1, M_CONV):
            acc = acc + w_ref[M_CONV - 1 - j:M_CONV - j, :] * ext[tail - j:tail - j + c, :]
        carry_scr[...] = u[c - tail:, :]
        return _silu(acc)

    ux = x_ref[...]
    ubc = bc_ref[...]
    xc = conv(ux, cx_scr, cwx_ref, cbx_ref)
    bcc = conv(ubc, cb_scr, cwb_ref, cbb_ref)
    ngn = M_GROUPS * M_STATE

    row = lax.broadcasted_iota(jnp.int32, (c, c), 0)
    col = lax.broadcasted_iota(jnp.int32, (c, c), 1)
    causal = row >= col
    tri = causal.astype(F32)
    e2 = e2_ref[...]

    dtp = _softplus(dt_ref[...] + dtb_ref[...])
    da = dtp * (-jnp.exp(alog_ref[...]))
    cum = _dot(tri, da, NN, precision=HIGHEST)
    cum_t = cum.T
    last = cum[c - 1:c, :]
    xdt = xc * _expand_heads(dtp, e2)
    xend = (xdt * _expand_heads(jnp.exp(last - cum), e2)).astype(BF16)
    ecum = _expand_heads(jnp.exp(cum), e2)
    xdt_b = xdt.astype(BF16)
    gw = M_HPG * M_HEADDIM

    ys = []
    for g in range(M_GROUPS):
        bg = bcc[:, g * M_STATE:(g + 1) * M_STATE].astype(BF16)
        cg = bcc[:, ngn + g * M_STATE:ngn + (g + 1) * M_STATE].astype(BF16)
        cb = _dot(cg, bg, NT)
        sg = s_scr[g * M_HPG:(g + 1) * M_HPG].reshape(gw, M_STATE)
        y_inter = _dot(cg, sg.astype(BF16), NT) * ecum[:, g * gw:(g + 1) * gw]
        upd = _dot(xend[:, g * gw:(g + 1) * gw], bg, TN)
        parts = []
        for hh in range(M_HPG):
            h = g * M_HPG + hh
            decay = jnp.exp(jnp.where(causal, cum[:, h:h + 1] - cum_t[h:h + 1, :], -jnp.inf))
            m = (cb * decay).astype(BF16)
            parts.append(_dot(m, xdt_b[:, h * M_HEADDIM:(h + 1) * M_HEADDIM], NN))
            s_scr[h] = jnp.exp(last[:, h:h + 1]) * s_scr[h] + upd[hh * M_HEADDIM:(hh + 1) * M_HEADDIM, :]
        ys.append(jnp.concatenate(parts, axis=1) + y_inter)
    y = jnp.concatenate(ys, axis=1) + dexp_ref[...] * xc
    y = y * _silu(z_ref[...])
    y_ref[...] = (_group_rmsnorm(y, gw) * nw_ref[...]).astype(y_ref.dtype)

    @pl.when(n == pl.num_programs(1) - 1)
    def _():
        sfin_ref[0] = s_scr[...]
        conv_ref[0, :, 0:M_DI] = ux[c - (M_CONV - 1):, :]
        conv_ref[0, :, M_DI:] = ubc[c - (M_CONV - 1):, :]


def _ssd_prompt(proj, dt, conv_w, conv_b, dt_bias, a_log, d_exp, norm_w, e2, *, batch, seq, prev=None):
    c = M_CHUNK
    nchunk = seq // c
    tok = lambda j: pl.BlockSpec((c, M_DI), lambda b, n: (b * nchunk + n, j))
    const = lambda shape, j=0: pl.BlockSpec(shape, lambda b, n: (0, j))
    blocks = (3 * _nbytes((c, M_DI), F32) + _nbytes((c, M_DI), BF16) + _nbytes((2 * V7X_LANES, M_DI), BF16)
              + 2 * _nbytes((M_HEADS, M_HEADDIM, M_STATE), F32) + 12 * _nbytes((c, M_DI), F32))
    body, xspecs, xops, alias = _shared_output(_ssd_prompt_body, 13, prev)
    return pl.pallas_call(
        body,
        grid=(batch, nchunk),
        in_specs=[tok(4), tok(5), tok(6),
                  pl.BlockSpec((c, V7X_LANES), lambda b, n: (b * nchunk + n, 0)),
                  const((M_CONV, M_DI), 0), const((M_CONV, M_DI), 1), const((1, M_DI), 0), const((1, M_DI), 1),
                  const((1, V7X_LANES)), const((1, V7X_LANES)), const((1, M_DI)), const((1, M_DI)),
                  const((2 * V7X_LANES, M_DI))] + xspecs,
        out_specs=[pl.BlockSpec((c, M_DI), lambda b, n: (b * nchunk + n, HG_VW // M_DI)),
                   pl.BlockSpec((1, M_HEADS, M_HEADDIM, M_STATE), lambda b, n: (b, 0, 0, 0)),
                   pl.BlockSpec((1, M_CONV - 1, M_CONV_DIM), lambda b, n: (b, 0, 0))],
        out_shape=[jax.ShapeDtypeStruct((proj.shape[0], MIX0_WIDTH), BF16),
                   jax.ShapeDtypeStruct((batch, M_HEADS, M_HEADDIM, M_STATE), F32),
                   jax.ShapeDtypeStruct((batch, M_CONV - 1, M_CONV_DIM), F32)],
        scratch_shapes=[pltpu.VMEM((M_HEADS, M_HEADDIM, M_STATE), F32),
                        pltpu.VMEM((V7X_SUBLANES, M_DI), F32), pltpu.VMEM((V7X_SUBLANES, M_DI), F32)],
        input_output_aliases=alias,
        compiler_params=_params(("arbitrary", "arbitrary"), blocks),
        name="ssd_prompt",
    )(proj, proj, proj, dt, conv_w, conv_w, conv_b, conv_b, dt_bias, a_log, d_exp, norm_w, e2, *xops)


SSD_SAMPLE_BB = 8


def _shift_rows(a, j):
    return a if j == 0 else pltpu.roll(a, j, 0)


def _ssd_sample_body(z_ref, x_ref, bc_ref, dt_ref, s_ref, cs_ref, cwx_ref, cwb_ref, cbx_ref, cbb_ref, dtb_ref,
                     alog_ref, dexp_ref, nw_ref, e2_ref, y_ref, so_ref, co_ref, *, seq):
    rows = V7X_SUBLANES
    hist = M_CONV - 1
    assert rows == 2 * seq and hist <= seq and hist <= rows - seq
    rvec = lax.broadcasted_iota(jnp.int32, (rows, 1), 0)
    valid = rvec < seq
    lane = lax.broadcasted_iota(jnp.int32, (1, V7X_LANES), 1)
    r16 = lax.broadcasted_iota(jnp.int32, (2 * rows, M_STATE), 0)
    ones_rows = jnp.where((r16 == rows) | (r16 == rows + 1), 1.0, 0.0).astype(BF16)
    e2 = e2_ref[...]
    a_neg = -jnp.exp(alog_ref[...])
    gw = M_HPG * M_HEADDIM
    ngn = M_GROUPS * M_STATE

    def conv(u8, buf, w_ref, b_ref):
        buf8 = jnp.concatenate([buf, jnp.zeros((rows - hist, buf.shape[1]), F32)], axis=0)
        ext = jnp.where(valid, u8, _shift_rows(buf8, rows - hist))
        acc = b_ref[...] + w_ref[hist:hist + 1, :] * ext
        for j in range(1, M_CONV):
            acc = acc + w_ref[hist - j:hist - j + 1, :] * _shift_rows(ext, j)
        new_hist = _shift_rows(ext, rows - (seq - hist))[0:hist, :]
        return _silu(acc), new_hist

    def one(b, u_x, u_bc, z8, dt8):
        xc, nhx = conv(u_x, cs_ref[b, :, 0:M_DI], cwx_ref, cbx_ref)
        bcc, nhb = conv(u_bc, cs_ref[b, :, M_DI:], cwb_ref, cbb_ref)
        co_ref[b, :, 0:M_DI] = nhx
        co_ref[b, :, M_DI:] = nhb
        dtp = jnp.where(valid, _softplus(dt8 + dtb_ref[...]), 0.0)
        cum = dtp * a_neg
        k = 1
        while k < seq:
            cum = cum + jnp.where(rvec >= k, _shift_rows(cum, k), 0.0)
            k *= 2
        last = cum[seq - 1:seq, :]
        bmat = bcc[:, :ngn]
        cmat = bcc[:, ngn:]
        coefs = [dtp, jnp.exp(cum), jnp.exp(last - cum), jnp.broadcast_to(jnp.exp(last), (rows, V7X_LANES))]
        for j in range(seq):
            prod = cmat * _shift_rows(bmat, j)
            cbh = jnp.zeros((rows, V7X_LANES), F32)
            for g in range(M_GROUPS):
                cbg = jnp.sum(prod[:, g * M_STATE:(g + 1) * M_STATE], axis=-1, keepdims=True)
                cbh = jnp.where((lane // M_HPG) == g, cbg, cbh)
            coefs.append(cbh * jnp.exp(cum - _shift_rows(cum, j)))
        wide = _expand_heads(jnp.concatenate(coefs, axis=0), e2)
        part = lambda n: wide[n * rows:(n + 1) * rows, :]
        xdt = xc * part(0)
        ecum = part(1)
        xs = xdt * part(2)
        dec = part(3)[0:1, :]
        y = dexp_ref[...] * xc
        for j in range(seq):
            y = y + part(4 + j) * _shift_rows(xdt, j)
        dec_hi = dec.astype(BF16).astype(F32)
        lhs = jnp.concatenate([xs, dec_hi, dec - dec_hi, jnp.zeros((rows - 2, M_DI), F32)], axis=0).astype(BF16)
        parts = []
        for g in range(M_GROUPS):
            bg = bmat[:, g * M_STATE:(g + 1) * M_STATE].astype(BF16)
            cg = cmat[:, g * M_STATE:(g + 1) * M_STATE].astype(BF16)
            s0 = s_ref[b, g * M_HPG:(g + 1) * M_HPG].reshape(gw, M_STATE)
            parts.append(_dot(cg, s0.astype(BF16), NT) * ecum[:, g * gw:(g + 1) * gw])
            rhs = jnp.concatenate([jnp.concatenate([bg, jnp.zeros((rows, M_STATE), BF16)], axis=0), ones_rows], axis=1)
            both = _dot(lhs[:, g * gw:(g + 1) * gw], rhs, TN)
            s_new = both[:, M_STATE:] * s0 + both[:, :M_STATE]
            so_ref[b, g * M_HPG:(g + 1) * M_HPG] = s_new.reshape(M_HPG, M_HEADDIM, M_STATE)
        y = (y + jnp.concatenate(parts, axis=1)) * _silu(z8)
        return _group_rmsnorm(y, gw) * nw_ref[...]

    def pair(p, carry):
        r0 = pl.multiple_of(p * rows, rows)
        tiles = [ref[pl.ds(r0, rows), :] for ref in (x_ref, bc_ref, z_ref, dt_ref)]
        ys = []
        for j in range(2):
            ys.append(one(2 * p + j, *[_shift_rows(t, j * (rows - seq)) for t in tiles]))
        y_ref[pl.ds(r0, rows), :] = jnp.where(valid, ys[0], _shift_rows(ys[1], seq)).astype(y_ref.dtype)
        return carry

    lax.fori_loop(0, SSD_SAMPLE_BB // 2, pair, 0)


def _ssd_sample(proj, row0, dt, state, conv_state, conv_w, conv_b, dt_bias, a_log, d_exp, norm_w, e2, *,
                batch, seq, prev=None):
    bb = SSD_SAMPLE_BB
    tr = bb * seq
    tok = lambda j: pl.BlockSpec((tr, M_DI), lambda i: (row0 // tr + i, j))
    const = lambda shape, j=0: pl.BlockSpec(shape, lambda i: (0, j))
    sblk = pl.BlockSpec((bb, M_HEADS, M_HEADDIM, M_STATE), lambda i: (i, 0, 0, 0))
    cblk = pl.BlockSpec((bb, M_CONV - 1, M_CONV_DIM), lambda i: (i, 0, 0))
    blocks = (3 * _nbytes((tr, M_DI), F32) + 2 * _nbytes((bb, M_HEADS, M_HEADDIM, M_STATE), F32)
              + 2 * _nbytes((bb, V7X_SUBLANES, M_CONV_DIM), F32) + _nbytes((2 * V7X_LANES, M_DI), BF16))
    body, xspecs, xops, alias = _shared_output(functools.partial(_ssd_sample_body, seq=seq), 15, prev)
    return pl.pallas_call(
        body,
        grid=(batch // bb,),
        in_specs=[tok(4), tok(5), tok(6),
                  pl.BlockSpec((tr, V7X_LANES), lambda i: (row0 // tr + i, 0)),
                  sblk, cblk,
                  const((M_CONV, M_DI), 0), const((M_CONV, M_DI), 1), const((1, M_DI), 0), const((1, M_DI), 1),
                  const((1, V7X_LANES)), const((1, V7X_LANES)), const((1, M_DI)), const((1, M_DI)),
                  const((2 * V7X_LANES, M_DI))] + xspecs,
        out_specs=[pl.BlockSpec((tr, M_DI), lambda i: (row0 // tr + i, HG_VW // M_DI)), sblk, cblk],
        out_shape=[jax.ShapeDtypeStruct((proj.shape[0], MIX0_WIDTH), BF16),
                   jax.ShapeDtypeStruct(state.shape, F32),
                   jax.ShapeDtypeStruct(conv_state.shape, F32)],
        input_output_aliases=alias,
        compiler_params=_params(("arbitrary",), blocks),
        name="ssd_sample",
    )(proj, proj, proj, dt, state, conv_state, conv_w, conv_w, conv_b, conv_b, dt_bias, a_log, d_exp, norm_w, e2, *xops)


S5_TILES = 8
S5_TILE_IN = D_MODEL // S5_TILES
S5_TILE_ST = S5_STATE // S5_TILES


def _s5_discretize(a_re, a_im, log_dt, b_re, b_im, c_re, c_im):
    lam_re = jnp.minimum(a_re, -1e-4)
    lam_im = a_im
    dt = jnp.exp(log_dt)[:, None]
    mag = jnp.exp(lam_re * dt)
    ab_re = mag * jnp.cos(lam_im * dt)
    ab_im = mag * jnp.sin(lam_im * dt)
    den = lam_re * lam_re + lam_im * lam_im
    zr = ((ab_re - 1.0) * lam_re + ab_im * lam_im) / den
    zi = (ab_im * lam_re - (ab_re - 1.0) * lam_im) / den
    bb_re = zr[..., None] * b_re - zi[..., None] * b_im
    bb_im = zr[..., None] * b_im + zi[..., None] * b_re
    gpt = S5_GROUPS // S5_TILES
    eye = jnp.eye(gpt, dtype=F32)

    def pack_b(bb):
        return jnp.einsum('kgph,gm->kghmp', bb.reshape(S5_TILES, gpt, S5_P, S5_GROUP), eye).reshape(
            S5_TILES, S5_TILE_IN, S5_TILE_ST).astype(BF16)

    def pack_c(c):
        return jnp.einsum('kghp,gm->kgpmh', c.reshape(S5_TILES, gpt, S5_GROUP, S5_P), eye).reshape(
            S5_TILES, S5_TILE_ST, S5_TILE_IN).astype(BF16)

    slab = lambda a: _s5_slab_layout(a.reshape(S5_STATE))
    return slab(ab_re), slab(ab_im), pack_b(bb_re), pack_b(bb_im), pack_c(c_re), pack_c(c_im)


def _s5_bu_body(x_ref, wr_ref, wi_ref, or_ref, oi_ref):
    x = x_ref[...]
    or_ref[...] = _dot(x, wr_ref[0], NN)
    oi_ref[...] = _dot(x, wi_ref[0], NN)


def _s5_bu(xb, wb_re, wb_im, *, tm):
    t = xb.shape[0]
    wspec = pl.BlockSpec((1, S5_TILE_IN, S5_TILE_ST), lambda k, i: (k, 0, 0))
    ospec = pl.BlockSpec((tm, S5_TILE_ST), lambda k, i: (i, k))
    blocks = _nbytes((tm, S5_TILE_IN), BF16) + 2 * _nbytes((S5_TILE_IN, S5_TILE_ST), BF16) + 2 * _nbytes((tm, S5_TILE_ST), F32)
    return pl.pallas_call(
        _s5_bu_body,
        grid=(S5_TILES, t // tm),
        in_specs=[pl.BlockSpec((tm, S5_TILE_IN), lambda k, i: (i, k)), wspec, wspec],
        out_specs=[ospec, ospec],
        out_shape=[jax.ShapeDtypeStruct((t, S5_STATE), F32)] * 2,
        compiler_params=_params(("arbitrary", "arbitrary"), blocks),
        name="s5_bu",
    )(xb, wb_re, wb_im)


def _s5_scan_body(br_ref, bi_ref, h0r_ref, h0i_ref, ar_ref, ai_ref, *rest, bb, tc):
    hr_ref, hi_ref, fr_ref, fi_ref, cr_scr, ci_scr = rest[-6:]
    n = pl.program_id(1)

    @pl.when(n == 0)
    def _():
        cr_scr[...] = h0r_ref[...]
        ci_scr[...] = h0i_ref[...]

    ar = ar_ref[...]
    ai = ai_ref[...]

    def seq_body(b, carry):
        def step(t, h):
            hr, hi = h
            r = b * tc + t
            nr = ar * hr - ai * hi + br_ref[r]
            ni = ar * hi + ai * hr + bi_ref[r]
            hr_ref[r] = nr
            hi_ref[r] = ni
            return nr, ni

        hr, hi = lax.fori_loop(0, tc, step, (cr_scr[b], ci_scr[b]))
        cr_scr[b] = hr
        ci_scr[b] = hi
        return carry

    lax.fori_loop(0, bb, seq_body, 0)

    @pl.when(n == pl.num_programs(1) - 1)
    def _():
        fr_ref[...] = cr_scr[...]
        fi_ref[...] = ci_scr[...]


def _s5_scan(bu_re, bu_im, h0_re, h0_im, ab_re, ab_im, *, row0, batch, seq, bb, tc, prev=None, name):
    t, sub, w = bu_re.shape
    assert bb == 1 or tc == seq
    rb = bb * tc
    blk = pl.BlockSpec((rb, sub, w), lambda i, n: ((row0 + i * bb * seq) // rb + n, 0, 0))
    sblk = pl.BlockSpec((bb, sub, w), lambda i, n: (i, 0, 0))
    ablk = pl.BlockSpec((sub, w), lambda i, n: (0, 0))
    blocks = 4 * _nbytes((rb, sub, w), F32) + 6 * _nbytes((bb, sub, w), F32)
    keep = [] if prev is None else list(prev)
    n_in = 6
    return pl.pallas_call(
        functools.partial(_s5_scan_body, bb=bb, tc=tc),
        grid=(batch // bb, seq // tc),
        in_specs=[blk, blk, sblk, sblk, ablk, ablk] + [pl.BlockSpec(memory_space=pl.ANY)] * len(keep),
        out_specs=[blk, blk, sblk, sblk],
        out_shape=[jax.ShapeDtypeStruct(bu_re.shape, F32)] * 2 + [jax.ShapeDtypeStruct(h0_re.shape, F32)] * 2,
        scratch_shapes=[pltpu.VMEM((bb, sub, w), F32)] * 2,
        input_output_aliases={n_in + k: k for k in range(len(keep))},
        compiler_params=_params(("arbitrary", "arbitrary"), blocks),
        name=name,
    )(bu_re, bu_im, h0_re, h0_im, ab_re, ab_im, *keep)


def _s5_out_body(hr_ref, hi_ref, x_ref, d_ref, wr_ref, wi_ref, o_ref):
    y = _dot(hr_ref[...].astype(BF16), wr_ref[0], NN) - _dot(hi_ref[...].astype(BF16), wi_ref[0], NN)
    o_ref[...] = jax.nn.gelu(y + d_ref[...] * x_ref[...]).astype(o_ref.dtype)


def _s5_out(h_re, h_im, x, d_skip, wc_re, wc_im, *, tm):
    t = x.shape[0]
    hspec = pl.BlockSpec((tm, S5_TILE_ST), lambda k, i: (i, k))
    wspec = pl.BlockSpec((1, S5_TILE_ST, S5_TILE_IN), lambda k, i: (k, 0, 0))
    xspec = pl.BlockSpec((tm, S5_TILE_IN), lambda k, i: (i, k))
    blocks = (2 * _nbytes((tm, S5_TILE_ST), F32) + 2 * _nbytes((S5_TILE_ST, S5_TILE_IN), BF16)
              + _nbytes((tm, S5_TILE_IN), F32) + _nbytes((tm, S5_TILE_IN), BF16))
    return pl.pallas_call(
        _s5_out_body,
        grid=(S5_TILES, t // tm),
        in_specs=[hspec, hspec, xspec, pl.BlockSpec((1, S5_TILE_IN), lambda k, i: (0, k)), wspec, wspec],
        out_specs=xspec,
        out_shape=jax.ShapeDtypeStruct((t, D_MODEL), BF16),
        compiler_params=_params(("arbitrary", "arbitrary"), blocks),
        name="s5_out",
    )(h_re, h_im, x, d_skip.reshape(1, D_MODEL), wc_re, wc_im)


S5_PROMPT_ROWS = 256
S5_SAMPLE_ROWS = 128
S5_NBLK = S5_TILE_ST // V7X_LANES


def _s5_pitch(rows):
    return rows + (V7X_SUBLANES if (rows // V7X_SUBLANES) % 2 == 0 else 0)


def _s5_slab_layout(a):
    lead = a.shape[:-1]
    return jnp.swapaxes(a.reshape(lead + (S5_TILES, S5_NBLK, V7X_LANES)), -3, -2)


def _s5_from_slab_layout(a):
    lead = a.shape[:-3]
    return jnp.swapaxes(a, -3, -2).reshape(lead + (S5_STATE,))


def _s5_fused_body(xb_ref, x_ref, h0r_ref, h0i_ref, ar_ref, ai_ref, wbr_ref, wbi_ref, wcr_ref, wci_ref, d_ref, *rest,
                   n_seq, steps):
    act_ref, fr_ref, fi_ref, bur, bui, cr, ci = rest[-7:]
    hrs, his = bur, bui
    n = pl.program_id(1)
    rows = n_seq * steps
    S5_PITCH = _s5_pitch(rows)
    for s in range(S5_TILES):
        u = xb_ref[:, s * S5_TILE_IN:(s + 1) * S5_TILE_IN]
        br = _dot(u, wbr_ref[s], NN)
        bi = _dot(u, wbi_ref[s], NN)
        for j in range(S5_NBLK):
            bur[j, s * S5_PITCH:s * S5_PITCH + rows, :] = br[:, j * V7X_LANES:(j + 1) * V7X_LANES]
            bui[j, s * S5_PITCH:s * S5_PITCH + rows, :] = bi[:, j * V7X_LANES:(j + 1) * V7X_LANES]

    @pl.when(n == 0)
    def _():
        cr[...] = h0r_ref[...]
        ci[...] = h0i_ref[...]

    ar = [ar_ref[j] for j in range(S5_NBLK)]
    ai = [ai_ref[j] for j in range(S5_NBLK)]

    def seq_body(q, carry):
        def step(t, h):
            hr, hi = h
            idx = pl.ds(q * steps + t, S5_TILES, stride=S5_PITCH)
            nr, ni = [], []
            for j in range(S5_NBLK):
                nr.append(ar[j] * hr[j] - ai[j] * hi[j] + bur[j, idx, :])
                ni.append(ar[j] * hi[j] + ai[j] * hr[j] + bui[j, idx, :])
                hrs[j, idx, :] = nr[j]
                his[j, idx, :] = ni[j]
            return tuple(nr), tuple(ni)

        h0 = (tuple(cr[q, j] for j in range(S5_NBLK)), tuple(ci[q, j] for j in range(S5_NBLK)))
        hr, hi = lax.fori_loop(0, steps, step, h0, unroll=min(steps, 4))
        for j in range(S5_NBLK):
            cr[q, j] = hr[j]
            ci[q, j] = hi[j]
        return carry

    lax.fori_loop(0, n_seq, seq_body, 0)

    for s in range(S5_TILES):
        seg = slice(s * S5_PITCH, s * S5_PITCH + rows)
        hre = jnp.concatenate([hrs[j, seg, :] for j in range(S5_NBLK)], axis=1).astype(BF16)
        him = jnp.concatenate([his[j, seg, :] for j in range(S5_NBLK)], axis=1).astype(BF16)
        y = _dot(hre, wcr_ref[s], NN) - _dot(him, wci_ref[s], NN)
        cs = slice(s * S5_TILE_IN, (s + 1) * S5_TILE_IN)
        act_ref[:, cs] = jax.nn.gelu(y + d_ref[:, cs] * x_ref[:, cs]).astype(act_ref.dtype)

    @pl.when(n == pl.num_programs(1) - 1)
    def _():
        fr_ref[...] = cr[...]
        fi_ref[...] = ci[...]


def _s5_fused(xb, x, h0_re, h0_im, ab_re, ab_im, wb_re, wb_im, wc_re, wc_im, d_skip, *, row0, batch, seq, rows, n_seq,
              prev=None, name):
    t, d = x.shape
    steps = rows // n_seq
    assert n_seq == 1 or steps == seq
    S5_PITCH = _s5_pitch(rows)
    tok = lambda width: pl.BlockSpec((rows, width), lambda i, n: ((row0 + i * n_seq * seq) // rows + n, 0))
    st = pl.BlockSpec((n_seq, S5_NBLK, S5_TILES, V7X_LANES), lambda i, n: (i, 0, 0, 0))
    const = lambda shape: pl.BlockSpec(shape, lambda i, n: (0,) * len(shape), pipeline_mode=pl.Buffered(1))
    slab = pltpu.VMEM((S5_NBLK, S5_TILES * S5_PITCH, V7X_LANES), F32)
    stv = pltpu.VMEM((n_seq, S5_NBLK, S5_TILES, V7X_LANES), F32)
    blocks = (2 * (_nbytes((rows, d), BF16) * 2 + _nbytes((rows, d), F32))
              + 2 * 6 * _nbytes((n_seq, S5_STATE), F32) + 4 * _nbytes(wb_re.shape, BF16)
              + 2 * _nbytes((S5_NBLK, S5_TILES * S5_PITCH, V7X_LANES), F32))
    keep = [] if prev is None else [prev]
    n_in = 11
    return pl.pallas_call(
        functools.partial(_s5_fused_body, n_seq=n_seq, steps=steps),
        grid=(batch // n_seq, seq // steps),
        in_specs=[tok(d), tok(d), st, st, const(ab_re.shape), const(ab_im.shape), const(wb_re.shape),
                  const(wb_im.shape), const(wc_re.shape), const(wc_im.shape), const((1, d))]
                 + [pl.BlockSpec(memory_space=pl.ANY)] * len(keep),
        out_specs=[tok(d), st, st],
        out_shape=[jax.ShapeDtypeStruct((t, d), BF16), jax.ShapeDtypeStruct(h0_re.shape, F32),
                   jax.ShapeDtypeStruct(h0_im.shape, F32)],
        scratch_shapes=[slab, slab, stv, stv],
        input_output_aliases={n_in + k: k for k in range(len(keep))},
        compiler_params=pltpu.CompilerParams(dimension_semantics=("arbitrary", "arbitrary"),
                                             vmem_limit_bytes=int(min(blocks + COMPILER_SCRATCH_BYTES, V7X_SCOPED_VMEM_CAP))),
        name=name,
    )(xb, x, h0_re, h0_im, ab_re, ab_im, wb_re, wb_im, wc_re, wc_im, d_skip.reshape(1, d), *keep)


def _glu_body(a_ref, wa_ref, wb_ref, ba_ref, bb_ref, o_ref, wa_scr, wb_scr):
    @pl.when(pl.program_id(1) == 0)
    def _():
        wa_scr[...] = wa_ref[0].astype(BF16)
        wb_scr[...] = wb_ref[0].astype(BF16)

    a = a_ref[...]
    o_ref[...] = (_dot(a, wa_scr[...], NN) + ba_ref[...]) * jax.nn.sigmoid(_dot(a, wb_scr[...], NN) + bb_ref[...])


def _glu(a, wa, wb, ba, bb, layer, *, tm, tn):
    t, k = a.shape
    n = wa.shape[2]
    wspec = pl.BlockSpec((1, k, tn), lambda j, i: (layer, 0, j))
    bspec = pl.BlockSpec((1, tn), lambda j, i: (layer, j))
    blocks = _nbytes((tm, k), BF16) + 2 * _nbytes((k, tn), F32) + _nbytes((tm, tn), F32) + 2 * _nbytes((k, tn), BF16)
    return pl.pallas_call(
        _glu_body,
        grid=(n // tn, t // tm),
        in_specs=[pl.BlockSpec((tm, k), lambda j, i: (i, 0)), wspec, wspec, bspec, bspec],
        out_specs=pl.BlockSpec((tm, tn), lambda j, i: (i, j)),
        out_shape=jax.ShapeDtypeStruct((t, n), F32),
        scratch_shapes=[pltpu.VMEM((k, tn), BF16)] * 2,
        compiler_params=_params(("arbitrary", "arbitrary"), blocks),
        name="glu",
    )(a, wa, wb, ba, bb)


def _router_body(x_ref, w_ref, b_ref, e_ref, g_ref):
    logits = _dot(w_ref[...], x_ref[...], NT, precision=HIGHEST) + b_ref[...]
    rows = [logits[e:e + 1, :] for e in range(N_EXPERTS)]
    m = functools.reduce(jnp.maximum, rows)
    ex = [jnp.exp(r - m) for r in rows]
    z = functools.reduce(jnp.add, ex)
    p = [v / z for v in ex]

    def top2_sum(a, b, c, d):
        hi1, lo1, hi2, lo2 = jnp.maximum(a, b), jnp.minimum(a, b), jnp.maximum(c, d), jnp.minimum(c, d)
        return jnp.maximum(hi1, hi2) + jnp.maximum(jnp.minimum(hi1, hi2), jnp.maximum(lo1, lo2))

    assert EXP_PER_GROUP == 4
    best = top2_sum(*p[0:EXP_PER_GROUP])
    gsel = jnp.zeros_like(best, dtype=jnp.int32)
    for g in range(1, N_EXP_GROUPS):
        s = top2_sum(*p[g * EXP_PER_GROUP:(g + 1) * EXP_PER_GROUP])
        better = s > best
        gsel = jnp.where(better, g, gsel)
        best = jnp.where(better, s, best)
    inner = []
    for i in range(EXP_PER_GROUP):
        v = p[i]
        for g in range(1, N_EXP_GROUPS):
            v = jnp.where(gsel == g, p[g * EXP_PER_GROUP + i], v)
        inner.append(v)

    def first_argmax(vals, skip=None):
        bv = bi = None
        for i, v in enumerate(vals):
            v = v if skip is None else jnp.where(skip == i, -jnp.inf, v)
            if bv is None:
                bv, bi = v, jnp.zeros_like(gsel)
            else:
                better = v > bv
                bi = jnp.where(better, i, bi)
                bv = jnp.where(better, v, bv)
        return bv, bi

    p1, i1 = first_argmax(inner)
    p2, i2 = first_argmax(inner, skip=i1)
    tot = p1 + p2
    e1 = gsel * EXP_PER_GROUP + i1
    e2 = gsel * EXP_PER_GROUP + i2
    e_ref[0:1, :] = e1
    e_ref[1:2, :] = e2
    g_ref[0:1, :] = p1 / tot
    g_ref[1:2, :] = p2 / tot


def _router(x, w_router, b_router, *, tm):
    t, d = x.shape
    blocks = _nbytes((tm, d), F32) + _nbytes((N_EXPERTS, d), F32) + 2 * _nbytes((V7X_SUBLANES, tm), F32)
    out = pl.BlockSpec((TOP_K, tm), lambda i: (0, i))
    return pl.pallas_call(
        _router_body,
        grid=(t // tm,),
        in_specs=[pl.BlockSpec((tm, d), lambda i: (i, 0)), pl.BlockSpec((N_EXPERTS, d), lambda i: (0, 0)),
                  pl.BlockSpec((N_EXPERTS, 1), lambda i: (0, 0))],
        out_specs=[out, out],
        out_shape=[jax.ShapeDtypeStruct((TOP_K, t), jnp.int32), jax.ShapeDtypeStruct((TOP_K, t), F32)],
        compiler_params=_params(("arbitrary",), blocks),
        name="router",
    )(x, w_router.T, b_router.reshape(N_EXPERTS, 1))


MOE_TILE = 256
COMBINE_TILE = 256


def _moe_plan(eidx):
    k, t = eidx.shape
    ns = k * t
    n_tiles = ns // MOE_TILE + N_EXPERTS
    e_flat = eidx.reshape(ns)
    order = jnp.argsort(e_flat, stable=True).astype(jnp.int32)
    rank = jnp.argsort(order).astype(jnp.int32)
    experts = jnp.arange(N_EXPERTS, dtype=jnp.int32)[None, :]
    slot_is = e_flat[:, None] == experts
    counts = jnp.sum(slot_is.astype(jnp.int32), axis=0)
    start = jnp.cumsum(counts) - counts
    tiles = (counts + MOE_TILE - 1) // MOE_TILE
    tile_end = jnp.cumsum(tiles)
    tile_start = tile_end - tiles
    tile_ids = jnp.arange(n_tiles, dtype=jnp.int32)
    tile_expert = jnp.minimum(jnp.sum((tile_ids[:, None] >= tile_end[None, :]).astype(jnp.int32), axis=1),
                              N_EXPERTS - 1)
    pick = lambda onehot, table: jnp.sum(jnp.where(onehot, table[None, :], 0), axis=1)
    tile_is = tile_expert[:, None] == experts
    tile_first = pick(tile_is, start) + (tile_ids - pick(tile_is, tile_start)) * MOE_TILE
    position_of_slot = pick(slot_is, tile_start) * MOE_TILE + rank - pick(slot_is, start)
    return (tile_expert.astype(jnp.int32), tile_first.astype(jnp.int32), tile_end[-1:].astype(jnp.int32),
            (order % t).astype(jnp.int32), position_of_slot.astype(jnp.int32))


def _moe_expert_body(te_ref, first_ref, nt_ref, tok_ref, x_hbm, wg_ref, wu_ref, wd_ref, y_ref, xbuf, sem, wg_scr,
                     wu_scr, wd_scr):
    i = pl.program_id(0)
    nt = nt_ref[0]
    tm = MOE_TILE
    last_slot = tok_ref.shape[0] - 1

    def gather_copy(tok, slot, r):
        return pltpu.make_async_copy(x_hbm.at[pl.ds(tok, 1)], xbuf.at[slot, pl.ds(r, 1)], sem.at[slot])

    def start_gather(tile, slot):
        first = first_ref[tile]

        def body(r, c):
            gather_copy(tok_ref[jnp.minimum(first + r, last_slot)], slot, r).start()
            return c

        lax.fori_loop(0, tm, body, 0, unroll=8)

    @pl.when(i == 0)
    def _():
        start_gather(0, 0)

    @pl.when(i + 1 < nt)
    def _():
        start_gather(i + 1, (i + 1) % 2)

    @pl.when(i < nt)
    def _():
        slot = i % 2
        pltpu.make_async_copy(x_hbm.at[pl.ds(0, tm)], xbuf.at[slot], sem.at[slot]).wait()

        @pl.when((i == 0) | (te_ref[i] != te_ref[jnp.maximum(i - 1, 0)]))
        def _():
            wg_scr[...] = wg_ref[0, 0].astype(BF16)
            wu_scr[...] = wu_ref[0, 0].astype(BF16)
            wd_scr[...] = wd_ref[0, 0].astype(BF16)

        x = xbuf[slot].astype(BF16)
        act = (_silu(_dot(x, wg_scr[...], NN)) * _dot(x, wu_scr[...], NN)).astype(BF16)
        y_ref[...] = _dot(act, wd_scr[...], NN)

    @pl.when(i >= nt)
    def _():
        y_ref[...] = jnp.zeros_like(y_ref)


def _moe_experts(x, plan, w_gate, w_up, w_down, layer):
    tile_expert, tile_first, n_valid, sorted_token, _ = plan
    t, d = x.shape
    f = w_gate.shape[3]
    n_tiles = tile_expert.shape[0]
    wspec = lambda a, b: pl.BlockSpec((1, 1, a, b), lambda i, te, *_: (layer, te[i], 0, 0))
    blocks = (2 * 3 * _nbytes((d, f), F32) + 3 * _nbytes((d, f), BF16) + 2 * _nbytes((MOE_TILE, d), F32)
              + 2 * _nbytes((MOE_TILE, d), F32) + 4 * _nbytes((MOE_TILE, f), F32))
    return pl.pallas_call(
        _moe_expert_body,
        grid_spec=pltpu.PrefetchScalarGridSpec(
            num_scalar_prefetch=4,
            grid=(n_tiles,),
            in_specs=[pl.BlockSpec(memory_space=pl.ANY), wspec(d, f), wspec(d, f), wspec(f, d)],
            out_specs=pl.BlockSpec((MOE_TILE, d), lambda i, *_: (i, 0)),
            scratch_shapes=[pltpu.VMEM((2, MOE_TILE, d), F32), pltpu.SemaphoreType.DMA((2,)),
                            pltpu.VMEM((d, f), BF16), pltpu.VMEM((d, f), BF16), pltpu.VMEM((f, d), BF16)]),
        out_shape=jax.ShapeDtypeStruct((n_tiles * MOE_TILE, d), F32),
        compiler_params=pltpu.CompilerParams(dimension_semantics=("arbitrary",),
                                             vmem_limit_bytes=int(min(blocks + COMPILER_SCRATCH_BYTES, V7X_SCOPED_VMEM_CAP))),
        name="moe_experts",
    )(tile_expert, tile_first, n_valid, sorted_token, x, w_gate, w_up, w_down)


def _moe_combine_body(pos_ref, x_ref, w_ref, g_ref, b_ref, y_hbm, o_ref, ob_ref, ybuf, sem, *, n_tok, split_blocks):
    i = pl.program_id(0)
    tc = COMBINE_TILE

    def gather_copy(p, slot, k, r):
        return pltpu.make_async_copy(y_hbm.at[pl.ds(p, 1)], ybuf.at[slot, k, pl.ds(r, 1)], sem.at[slot])

    def start_gather(tile, slot):
        def body(r, c):
            for k in range(TOP_K):
                gather_copy(pos_ref[k * n_tok + tile * tc + r], slot, k, r).start()
            return c

        lax.fori_loop(0, tc, body, 0, unroll=4)

    @pl.when(i == 0)
    def _():
        start_gather(0, 0)

    @pl.when(i + 1 < pl.num_programs(0))
    def _():
        start_gather(i + 1, (i + 1) % 2)

    slot = i % 2
    for k in range(TOP_K):
        pltpu.make_async_copy(y_hbm.at[pl.ds(0, tc)], ybuf.at[slot, k], sem.at[slot]).wait()
    z = ALPHA * x_ref[...]
    for k in range(TOP_K):
        z = z + w_ref[:, k:k + 1] * ybuf[slot, k]
    mu = jnp.mean(z, -1, keepdims=True)
    zc = z - mu
    var = jnp.mean(zc * zc, -1, keepdims=True)
    y = zc * lax.rsqrt(var + LN_EPS) * g_ref[...] + b_ref[...]
    if split_blocks is None:
        o_ref[...] = y
        ob_ref[...] = y.astype(BF16)
    else:
        @pl.when(i < split_blocks)
        def _():
            o_ref[...] = y

        @pl.when(i >= split_blocks)
        def _():
            ob_ref[...] = y


def _moe_combine_layernorm(x, y_slots, plan, gate_w, g, b, *, name, split=None):
    position_of_slot = plan[4]
    t, d = x.shape
    tc = COMBINE_TILE
    row = pl.BlockSpec((tc, d), lambda i, pos: (i, 0))
    vec = pl.BlockSpec((1, d), lambda i, pos: (0, 0))
    blocks = 2 * (3 * _nbytes((tc, d), F32) + _nbytes((tc, d), BF16)) + 2 * TOP_K * _nbytes((tc, d), F32)
    if split is None:
        sb = None
        out_specs = [row, row]
        out_shape = [jax.ShapeDtypeStruct((t, d), F32), jax.ShapeDtypeStruct((t, d), BF16)]
    else:
        sb = split // tc
        out_specs = [pl.BlockSpec((tc, d), lambda i, pos: (jnp.minimum(i, sb - 1), 0)),
                     pl.BlockSpec((tc, d), lambda i, pos: (jnp.maximum(i - sb, 0), 0))]
        out_shape = [jax.ShapeDtypeStruct((split, d), F32), jax.ShapeDtypeStruct((t - split, d), F32)]
    return pl.pallas_call(
        functools.partial(_moe_combine_body, n_tok=t, split_blocks=sb),
        grid_spec=pltpu.PrefetchScalarGridSpec(
            num_scalar_prefetch=1,
            grid=(t // tc,),
            in_specs=[row, pl.BlockSpec((tc, TOP_K), lambda i, pos: (i, 0)), vec, vec,
                      pl.BlockSpec(memory_space=pl.ANY)],
            out_specs=out_specs,
            scratch_shapes=[pltpu.VMEM((2, TOP_K, tc, d), F32), pltpu.SemaphoreType.DMA((2,))]),
        out_shape=out_shape,
        compiler_params=pltpu.CompilerParams(dimension_semantics=("arbitrary",),
                                             vmem_limit_bytes=int(min(blocks + COMPILER_SCRATCH_BYTES, V7X_SCOPED_VMEM_CAP))),
        name=name,
    )(position_of_slot, x, gate_w, g.reshape(1, d), b.reshape(1, d), y_slots)


def _moe_dense_body(x_ref, g_ref, wg_ref, wu_ref, wd_ref, o_ref):
    e = pl.program_id(1)

    @pl.when(e == 0)
    def _():
        o_ref[...] = jnp.zeros_like(o_ref)

    x = x_ref[...]
    h = _dot(x, wg_ref[0], NN)
    u = _dot(x, wu_ref[0], NN)
    lane = lax.broadcasted_iota(jnp.int32, (1, N_EXPERTS), 1)
    gate = jnp.sum(jnp.where(lane == e, g_ref[...], 0.0), axis=1, keepdims=True)
    act = (_silu(h) * u * gate).astype(BF16)
    o_ref[...] += _dot(act, wd_ref[0], NN)


def _moe_dense(xb, gates, wg, wu, wd, *, tm):
    t, d = xb.shape
    f = wg.shape[2]
    blocks = (_nbytes((tm, d), BF16) + _nbytes((tm, V7X_LANES), F32) + 3 * _nbytes((d, f), BF16)
              + _nbytes((tm, d), F32) + 4 * _nbytes((tm, f), F32))
    return pl.pallas_call(
        _moe_dense_body,
        grid=(t // tm, N_EXPERTS),
        in_specs=[pl.BlockSpec((tm, d), lambda i, e: (i, 0)), pl.BlockSpec((tm, N_EXPERTS), lambda i, e: (i, 0)),
                  pl.BlockSpec((1, d, f), lambda i, e: (e, 0, 0)), pl.BlockSpec((1, d, f), lambda i, e: (e, 0, 0)),
                  pl.BlockSpec((1, f, d), lambda i, e: (e, 0, 0))],
        out_specs=pl.BlockSpec((tm, d), lambda i, e: (i, 0)),
        out_shape=jax.ShapeDtypeStruct((t, d), F32),
        compiler_params=_params(("arbitrary", "arbitrary"), blocks),
        name="moe_dense",
    )(xb, gates, wg, wu, wd)


TOKEN_TILE = 1088
ROUTER_TILE = 512
LN_TILE = 256


def _moe_layernorm(x, layer, w_router, b_router, w_gate, w_up, w_down, g, b, split=None):
    eidx, gate_w = _router(x, w_router, b_router, tm=ROUTER_TILE)
    plan = _moe_plan(eidx)
    y_slots = _moe_experts(x, plan, w_gate, w_up, w_down, layer)
    return _moe_combine_layernorm(x, y_slots, plan, gate_w.T, g, b, name=f"moe_combine_ln_{layer}", split=split)


def kernel(x_prompt, x_sample, state_hgrn, state_ssm, state_conv, state_s5_re, state_s5_im, w_in0, hg_lb_logits, hg_norm_w, conv_w, conv_b, dt_bias, a_log, m_d, m_norm_w, w_out0, s5_a_re, s5_a_im, s5_log_dt, s5_b_re, s5_b_im, s5_c_re, s5_c_im, s5_d, glu_w_a, glu_b_a, glu_w_b, glu_b_b, w_router, b_router, w_gate, w_up, w_down, ln1_g, ln1_b, ln2_g, ln2_b):
    bp, lp, d = x_prompt.shape
    bs, ls, _ = x_sample.shape
    tp, ts = bp * lp, bs * ls
    tm = TOKEN_TILE
    x0 = jnp.concatenate([x_prompt.reshape(tp, d), x_sample.reshape(ts, d)], axis=0)
    x0b = x0.astype(BF16)
    lower_bounds = jnp.cumsum(jax.nn.softmax(hg_lb_logits.astype(F32), axis=0), axis=0)
    pad_lanes = lambda v: jnp.pad(v, (0, V7X_LANES - v.shape[0])).reshape(1, V7X_LANES)

    proj = _matmul_w32(x0b, jnp.swapaxes(w_in0, 1, 2), 0, tm=tm, tn=1024, n_cols=IN0_MAIN, transposed=True,
                       name="in_proj")
    w_dt = jnp.pad(w_in0[0, :, IN0_MAIN:], ((0, 0), (0, V7X_LANES - M_HEADS))).astype(BF16)
    dt = _matmul(x0b, w_dt, tm=tm, tn=V7X_LANES, name="dt_proj")
    lb0 = lower_bounds[0]
    e2 = _head_expand_matrix()
    ssd_consts = (conv_w[0], conv_b[0].reshape(1, -1), pad_lanes(dt_bias[0]), pad_lanes(a_log[0]),
                  jnp.repeat(m_d[0], M_HEADDIM).reshape(1, M_DI), m_norm_w[0].reshape(1, M_DI), e2)
    mixed, hg_p = _hgrn_prompt(proj, lb0, hg_norm_w[0], batch=bp, seq=lp)
    mixed, hg_s = _hgrn_sample(proj, tp, state_hgrn[0], lb0, hg_norm_w[0], batch=bs, seq=ls, prev=mixed)
    mixed, ssm_p, conv_p = _ssd_prompt(proj, dt, *ssd_consts, batch=bp, seq=lp, prev=mixed)
    mixed, ssm_s, conv_s = _ssd_sample(proj, tp, dt, state_ssm[0], state_conv[0], *ssd_consts, batch=bs, seq=ls,
                                       prev=mixed)
    mix = _matmul_w32(mixed, w_out0, 0, tm=tm, tn=512, name="out_proj")
    x1, x1b = _res_layernorm(x0, mix, ln1_g[0], ln1_b[0], tm=LN_TILE, name="ln1_0")
    x2, x2b = _moe_layernorm(x1, 0, w_router, b_router, w_gate, w_up, w_down, ln2_g[0], ln2_b[0])

    ab_re, ab_im, wb_re, wb_im, wc_re, wc_im = _s5_discretize(
        s5_a_re[0], s5_a_im[0], s5_log_dt[0], s5_b_re[0], s5_b_im[0], s5_c_re[0], s5_c_im[0])
    s5_consts = (ab_re, ab_im, wb_re, wb_im, wc_re, wc_im, s5_d[0])
    zeros = jnp.zeros((bp, S5_NBLK, S5_TILES, V7X_LANES), F32)
    act, s5r_p, s5i_p = _s5_fused(x2b, x2, zeros, zeros, *s5_consts, row0=0, batch=bp, seq=lp,
                                  rows=S5_PROMPT_ROWS, n_seq=1, name="s5_prompt")
    act, s5r_s, s5i_s = _s5_fused(x2b, x2, _s5_slab_layout(state_s5_re[0].reshape(bs, S5_STATE)),
                                  _s5_slab_layout(state_s5_im[0].reshape(bs, S5_STATE)), *s5_consts,
                                  row0=tp, batch=bs, seq=ls, rows=S5_SAMPLE_ROWS, n_seq=S5_SAMPLE_ROWS // ls,
                                  prev=act, name="s5_sample")
    mix1 = _glu(act, glu_w_a, glu_w_b, glu_b_a, glu_b_b, 0, tm=tm, tn=512)
    x3, x3b = _res_layernorm(x2, mix1, ln1_g[1], ln1_b[1], tm=LN_TILE, name="ln1_1")
    y_p, y_s = _moe_layernorm(x3, 1, w_router, b_router, w_gate, w_up, w_down, ln2_g[1], ln2_b[1], split=tp)

    s5_state = lambda a, b: _s5_from_slab_layout(a).reshape(1, b, S5_GROUPS, S5_P)
    return (y_p.reshape(bp, lp, d), y_s.reshape(bs, ls, d),
            hg_p[None], hg_s[None], ssm_p[None], ssm_s[None], conv_p[None], conv_s[None],
            s5_state(s5r_p, bp), s5_state(s5r_s, bs), s5_state(s5i_p, bp), s5_state(s5i_s, bs))
```

```python
import functools
import math

import jax
import jax.numpy as jnp
from jax import lax
from jax.experimental import pallas as pl
from jax.experimental.pallas import tpu as pltpu

F32 = jnp.float32
BF16 = jnp.bfloat16
HIGHEST = lax.Precision.HIGHEST

D_MODEL = 2048
DEPTH = 2
HG_HEADS = 16
HG_DK = 128
HG_DV = 128
HG_KW = HG_HEADS * HG_DK
HG_VW = HG_HEADS * HG_DV
HG_CHUNK = 64
M_DI = 2048
M_HEADDIM = 64
M_HEADS = 32
M_GROUPS = 8
M_HPG = 4
M_STATE = 128
M_CONV = 4
M_CONV_DIM = M_DI + 2 * M_GROUPS * M_STATE
M_CHUNK = 128
IN0_MAIN = 2 * HG_KW + 2 * HG_VW + M_DI + M_CONV_DIM
S5_GROUP = 16
S5_GROUPS = 128
S5_P = 64
S5_STATE = S5_GROUPS * S5_P
N_EXPERTS = 16
N_EXP_GROUPS = 4
EXP_PER_GROUP = 4
TOP_K = 2
D_EXPERT = 512
ALPHA = (2 * DEPTH) ** 0.25
LN_EPS = 1e-5
RMS_EPS = 1e-6

V7X_LANES = 128
V7X_SUBLANES = 8
V7X_VMEM_BYTES = 64 * 1024 * 1024
V7X_SCOPED_VMEM_CAP = 60000 * 1024
COMPILER_SCRATCH_BYTES = 16 * 1024 * 1024


def _params(semantics, block_bytes):
    limit = min(2 * block_bytes + COMPILER_SCRATCH_BYTES, V7X_SCOPED_VMEM_CAP)
    return pltpu.CompilerParams(dimension_semantics=semantics, vmem_limit_bytes=int(limit))


def _nbytes(shape, dtype):
    return math.prod(shape) * jnp.dtype(dtype).itemsize


def _silu(x):
    return x * jax.nn.sigmoid(x)


def _dot(a, b, dims, precision=None):
    return lax.dot_general(a, b, (dims, ((), ())), precision=precision, preferred_element_type=F32)


NN = ((1,), (0,))
NT = ((1,), (1,))
TN = ((0,), (0,))

MIX0_WIDTH = HG_VW + M_DI


def _shared_output(body, n_in, prev):
    if prev is None:
        return body, [], [], {}

    def with_prev(*refs):
        return body(*refs[:n_in], *refs[n_in + 1:])

    return with_prev, [pl.BlockSpec(memory_space=pl.ANY)], [prev], {n_in: 0}


def _mm_body(x_ref, w_ref, o_ref):
    o_ref[...] = _dot(x_ref[...], w_ref[...], NN).astype(o_ref.dtype)


def _matmul(x, w, *, tm, tn, n_cols=None, out_dtype=F32, name):
    m, k = x.shape
    n = n_cols if n_cols is not None else w.shape[1]
    blocks = _nbytes((tm, k), x.dtype) + _nbytes((k, tn), w.dtype) + _nbytes((tm, tn), out_dtype)
    return pl.pallas_call(
        _mm_body,
        grid=(n // tn, m // tm),
        in_specs=[pl.BlockSpec((tm, k), lambda j, i: (i, 0)), pl.BlockSpec((k, tn), lambda j, i: (0, j))],
        out_specs=pl.BlockSpec((tm, tn), lambda j, i: (i, j)),
        out_shape=jax.ShapeDtypeStruct((m, n), out_dtype),
        compiler_params=_params(("arbitrary", "arbitrary"), blocks),
        name=name,
    )(x, w)


def _mm_w32_body(x_ref, w_ref, o_ref, wb_scr):
    @pl.when(pl.program_id(1) == 0)
    def _():
        wb_scr[...] = w_ref[0].astype(BF16)

    o_ref[...] = _dot(x_ref[...], wb_scr[...], NN).astype(o_ref.dtype)


def _mm_w32t_body(x_ref, w_ref, o_ref, wb_scr):
    @pl.when(pl.program_id(1) == 0)
    def _():
        wb_scr[...] = w_ref[0].astype(BF16)

    o_ref[...] = _dot(x_ref[...], wb_scr[...], NT).astype(o_ref.dtype)


def _matmul_w32(x, w, layer, *, tm, tn, n_cols=None, transposed=False, name):
    m, k = x.shape
    n = n_cols if n_cols is not None else w.shape[1 if transposed else 2]
    blocks = _nbytes((tm, k), BF16) + _nbytes((k, tn), F32) + _nbytes((tm, tn), F32) + _nbytes((k, tn), BF16)
    if transposed:
        body, wblock, wspec = _mm_w32t_body, (tn, k), pl.BlockSpec((1, tn, k), lambda j, i: (layer, j, 0))
    else:
        body, wblock, wspec = _mm_w32_body, (k, tn), pl.BlockSpec((1, k, tn), lambda j, i: (layer, 0, j))
    return pl.pallas_call(
        body,
        grid=(n // tn, m // tm),
        in_specs=[pl.BlockSpec((tm, k), lambda j, i: (i, 0)), wspec],
        out_specs=pl.BlockSpec((tm, tn), lambda j, i: (i, j)),
        out_shape=jax.ShapeDtypeStruct((m, n), F32),
        scratch_shapes=[pltpu.VMEM(wblock, BF16)],
        compiler_params=_params(("arbitrary", "arbitrary"), blocks),
        name=name,
    )(x, w)


def _ln_body(x_ref, mix_ref, g_ref, b_ref, o_ref, ob_ref):
    z = ALPHA * x_ref[...] + mix_ref[...]
    mu = jnp.mean(z, -1, keepdims=True)
    zc = z - mu
    var = jnp.mean(zc * zc, -1, keepdims=True)
    y = zc * lax.rsqrt(var + LN_EPS) * g_ref[...] + b_ref[...]
    o_ref[...] = y
    ob_ref[...] = y.astype(BF16)


def _res_layernorm(x, mix, g, b, *, tm, name, row0=0, nrows=None):
    d = x.shape[1]
    nrows = x.shape[0] if nrows is None else nrows
    src = pl.BlockSpec((tm, d), lambda i: (row0 // tm + i, 0))
    dst = pl.BlockSpec((tm, d), lambda i: (i, 0))
    vec = pl.BlockSpec((1, d), lambda i: (0, 0))
    blocks = 3 * _nbytes((tm, d), F32) + _nbytes((tm, d), BF16)
    return pl.pallas_call(
        _ln_body,
        grid=(nrows // tm,),
        in_specs=[src, src, vec, vec],
        out_specs=[dst, dst],
        out_shape=[jax.ShapeDtypeStruct((nrows, d), F32), jax.ShapeDtypeStruct((nrows, d), BF16)],
        compiler_params=_params(("arbitrary",), blocks),
        name=name,
    )(x, mix, g.reshape(1, d), b.reshape(1, d))


def _hgrn_gates(q, f, lb):
    fg = lb + (1.0 - lb) * jax.nn.sigmoid(f)
    return _silu(q), 1.0 - fg, jnp.log(fg)


def _hgrn_out(o, g, nw):
    o = o * lax.rsqrt(jnp.mean(o * o, -1, keepdims=True) + RMS_EPS)
    return o * nw * _silu(g)


def _hgrn_prompt_body(q_ref, f_ref, v_ref, g_ref, lb_ref, nw_ref, o_ref, sfin_ref, st_scr):
    n = pl.program_id(1)
    c = HG_CHUNK

    @pl.when(n == 0)
    def _():
        st_scr[...] = jnp.zeros_like(st_scr)

    row = lax.broadcasted_iota(jnp.int32, (c, c), 0)
    col = lax.broadcasted_iota(jnp.int32, (c, c), 1)
    causal = row >= col
    tri = causal.astype(F32)
    q, k, logf = _hgrn_gates(q_ref[...], f_ref[...], lb_ref[...])
    v = v_ref[...].astype(BF16)
    cum = _dot(tri, logf, NN, precision=HIGHEST)
    mid = cum[c // 2 - 1:c // 2, :]
    last = cum[c - 1:c, :]
    qm = (q * jnp.exp(cum - mid)).astype(BF16)
    km = (k * jnp.exp(mid - cum)).astype(BF16)
    qi = (q * jnp.exp(cum)).astype(BF16)
    kl = (k * jnp.exp(last - cum)).astype(BF16)
    a_last = jnp.exp(last)
    gate = nw_ref[...] * _silu(g_ref[...])
    heads = [slice(h * HG_DK, (h + 1) * HG_DK) for h in range(HG_HEADS)]
    scores = [jnp.where(causal, _dot(qm[:, sl], km[:, sl], NT), 0.0).astype(BF16) for sl in heads]
    sts = [st_scr[h] for h in range(HG_HEADS)]
    outs = [_dot(qi[:, sl], sts[h].astype(BF16), NT) + _dot(scores[h], v[:, sl], NN) for h, sl in enumerate(heads)]
    for h, sl in enumerate(heads):
        st_scr[h] = a_last[:, sl] * sts[h] + _dot(v[:, sl], kl[:, sl], TN)
    for h, sl in enumerate(heads):
        o = outs[h]
        o = o * lax.rsqrt(jnp.mean(o * o, -1, keepdims=True) + RMS_EPS)
        o_ref[:, sl] = (o * gate[:, sl]).astype(o_ref.dtype)

    @pl.when(n == pl.num_programs(1) - 1)
    def _():
        sfin_ref[0] = st_scr[...]


def _hgrn_prompt(proj, lb, nw, *, batch, seq, prev=None):
    c = HG_CHUNK
    nchunk = seq // c
    blk = lambda j: pl.BlockSpec((c, HG_KW), lambda b, n: (b * nchunk + n, j))
    vec = pl.BlockSpec((1, HG_KW), lambda b, n: (0, 0))
    blocks = 4 * _nbytes((c, HG_KW), F32) + _nbytes((c, HG_VW), BF16) + 2 * _nbytes((HG_HEADS, HG_DV, HG_DK), F32)
    body, xspecs, xops, alias = _shared_output(_hgrn_prompt_body, 6, prev)
    o, st = pl.pallas_call(
        body,
        grid=(batch, nchunk),
        in_specs=[blk(0), blk(1), blk(2), blk(3), vec, vec] + xspecs,
        out_specs=[pl.BlockSpec((c, HG_VW), lambda b, n: (b * nchunk + n, 0)),
                   pl.BlockSpec((1, HG_HEADS, HG_DV, HG_DK), lambda b, n: (b, 0, 0, 0))],
        out_shape=[jax.ShapeDtypeStruct((proj.shape[0], MIX0_WIDTH), BF16),
                   jax.ShapeDtypeStruct((batch, HG_HEADS, HG_DV, HG_DK), F32)],
        scratch_shapes=[pltpu.VMEM((HG_HEADS, HG_DV, HG_DK), F32)],
        input_output_aliases=alias,
        compiler_params=_params(("arbitrary", "arbitrary"), blocks),
        name="hgrn_prompt",
    )(proj, proj, proj, proj, lb.reshape(1, HG_KW), nw.reshape(1, HG_VW), *xops)
    return o, jnp.swapaxes(st, -1, -2)


HG_SAMPLE_BB = 8


def _hgrn_sample_body(q_ref, f_ref, v_ref, g_ref, lb_ref, nw_ref, s_ref, o_ref, so_ref, *, seq):
    rows = 2 * seq
    assert rows == V7X_SUBLANES
    row = lax.broadcasted_iota(jnp.int32, (rows, rows), 0)
    col = lax.broadcasted_iota(jnp.int32, (rows, rows), 1)
    causal = (row >= col) & ((row // seq) == (col // seq))
    rvec = lax.broadcasted_iota(jnp.int32, (rows, 1), 0)
    r16 = lax.broadcasted_iota(jnp.int32, (2 * rows, HG_DV), 0)
    ones_rows = jnp.where((r16 == rows) | (r16 == rows + 1), 1.0, 0.0).astype(BF16)

    heads = [slice(h * HG_DK, (h + 1) * HG_DK) for h in range(HG_HEADS)]
    in_seq = rvec % seq

    def pair(p, carry):
        r0 = pl.multiple_of(p * rows, rows)
        tile = pl.ds(r0, rows)
        q, k, logf = _hgrn_gates(q_ref[tile, :], f_ref[tile, :], lb_ref[...])
        v = v_ref[tile, :].astype(BF16)
        cum = logf
        step = 1
        while step < seq:
            cum = cum + jnp.where(in_seq >= step, _shift_rows(cum, step), 0.0)
            step *= 2
        ecum = jnp.exp(cum)
        qi = (q * ecum).astype(BF16)
        km = (k / ecum).astype(BF16)
        gate = nw_ref[...] * _silu(g_ref[tile, :])
        v16 = jnp.concatenate([v, jnp.zeros_like(v)], axis=0)
        lhs = []
        for j in range(2):
            mine = (rvec // seq) == j
            last = cum[(j + 1) * seq - 1:(j + 1) * seq, :]
            a = jnp.exp(last)
            a_hi = a.astype(BF16).astype(F32)
            kl = jnp.where(mine, k * jnp.exp(last - cum), 0.0)
            lhs.append(jnp.concatenate([kl, a_hi, a - a_hi, jnp.zeros((rows - 2, HG_KW), F32)], axis=0).astype(BF16))
        scores = [jnp.where(causal, _dot(qi[:, sl], km[:, sl], NT), 0.0).astype(BF16) for sl in heads]
        outs = [_dot(scores[h], v[:, sl], NN) for h, sl in enumerate(heads)]
        for j in range(2):
            b = 2 * p + j
            mine = (rvec // seq) == j
            s0 = [s_ref[b, h] for h in range(HG_HEADS)]
            inter = [_dot(qi[:, sl], s0[h].astype(BF16), NN) for h, sl in enumerate(heads)]
            both = [_dot(lhs[j][:, sl], jnp.concatenate([v16[:, sl], ones_rows], axis=1), TN)
                    for sl in heads]
            for h in range(HG_HEADS):
                outs[h] = outs[h] + jnp.where(mine, inter[h], 0.0)
                so_ref[b, h] = both[h][:, HG_DV:] * s0[h] + both[h][:, :HG_DV]
        for h, sl in enumerate(heads):
            o = outs[h]
            o = o * lax.rsqrt(jnp.mean(o * o, -1, keepdims=True) + RMS_EPS)
            o_ref[tile, sl] = (o * gate[:, sl]).astype(o_ref.dtype)
        return carry

    lax.fori_loop(0, HG_SAMPLE_BB // 2, pair, 0)


def _hgrn_sample(proj, row0, state, lb, nw, *, batch, seq, prev=None):
    bb = HG_SAMPLE_BB
    tr = bb * seq
    blk0 = row0 // tr
    blk = lambda j: pl.BlockSpec((tr, HG_KW), lambda i: (blk0 + i, j))
    vec = pl.BlockSpec((1, HG_KW), lambda i: (0, 0))
    sblk = pl.BlockSpec((bb, HG_HEADS, HG_DK, HG_DV), lambda i: (i, 0, 0, 0))
    blocks = 4 * _nbytes((tr, HG_KW), F32) + 2 * _nbytes((bb, HG_HEADS, HG_DK, HG_DV), F32)
    body, xspecs, xops, alias = _shared_output(functools.partial(_hgrn_sample_body, seq=seq), 7, prev)
    return pl.pallas_call(
        body,
        grid=(batch // bb,),
        in_specs=[blk(0), blk(1), blk(2), blk(3), vec, vec, sblk] + xspecs,
        out_specs=[pl.BlockSpec((tr, HG_VW), lambda i: (blk0 + i, 0)), sblk],
        out_shape=[jax.ShapeDtypeStruct((proj.shape[0], MIX0_WIDTH), BF16),
                   jax.ShapeDtypeStruct(state.shape, F32)],
        input_output_aliases=alias,
        compiler_params=_params(("arbitrary",), blocks),
        name="hgrn_sample",
    )(proj, proj, proj, proj, lb.reshape(1, HG_KW), nw.reshape(1, HG_VW), state, *xops)


def _head_expand_matrix():
    r = jnp.arange(2 * V7X_LANES)[:, None] % V7X_LANES
    c = jnp.arange(M_DI)[None, :] // M_HEADDIM
    return (r == c).astype(BF16)


def _expand_heads(coef, e2):
    hi = coef.astype(BF16)
    lo = (coef - hi.astype(F32)).astype(BF16)
    return _dot(jnp.concatenate([hi, lo], axis=1), e2, NN)


def _softplus(x):
    return jnp.maximum(x, 0.0) + jnp.log(1.0 + jnp.exp(-jnp.abs(x)))


def _group_rmsnorm(y, width):
    outs = []
    for g in range(y.shape[1] // width):
        yg = y[:, g * width:(g + 1) * width]
        outs.append(yg * lax.rsqrt(jnp.mean(yg * yg, -1, keepdims=True) + RMS_EPS))
    return jnp.concatenate(outs, axis=1)


def _ssd_prompt_body(z_ref, x_ref, bc_ref, dt_ref, cwx_ref, cwb_ref, cbx_ref, cbb_ref, dtb_ref, alog_ref,
                     dexp_ref, nw_ref, e2_ref, y_ref, sfin_ref, conv_ref, s_scr, cx_scr, cb_scr):
    n = pl.program_id(1)
    c = M_CHUNK
    tail = V7X_SUBLANES

    @pl.when(n == 0)
    def _():
        s_scr[...] = jnp.zeros_like(s_scr)
        cx_scr[...] = jnp.zeros_like(cx_scr)
        cb_scr[...] = jnp.zeros_like(cb_scr)

    def conv(u, carry_scr, w_ref, b_ref):
        ext = jnp.concatenate([carry_scr[...], u], axis=0)
        acc = b_ref[...] + w_ref[M_CONV - 1:M_CONV, :] * u
        for j in range(1, M_CONV):
            acc = acc + w_ref[M_CONV - 1 - j:M_CONV - j, :] * pltpu.roll(ext, j, 0)[tail:, :]
        carry_scr[...] = u[c - tail:, :]
        return _silu(acc)

    ux = x_ref[...]
    ubc = bc_ref[...]
    xc = conv(ux, cx_scr, cwx_ref, cbx_ref)
    bcc = conv(ubc, cb_scr, cwb_ref, cbb_ref)
    ngn = M_GROUPS * M_STATE

    row = lax.broadcasted_iota(jnp.int32, (c, c), 0)
    col = lax.broadcasted_iota(jnp.int32, (c, c), 1)
    causal = row >= col
    tri = causal.astype(F32)
    e2 = e2_ref[...]

    dtp = _softplus(dt_ref[...] + dtb_ref[...])
    da = dtp * (-jnp.exp(alog_ref[...]))
    cum = _dot(tri, da, NN, precision=HIGHEST)
    cum_t = cum.T
    last = cum[c - 1:c, :]
    xdt = xc * _expand_heads(dtp, e2)
    xend = (xdt * _expand_heads(jnp.exp(last - cum), e2)).astype(BF16)
    ecum = _expand_heads(jnp.exp(cum), e2)
    xdt_b = xdt.astype(BF16)
    gw = M_HPG * M_HEADDIM

    ys = []
    for g in range(M_GROUPS):
        bg = bcc[:, g * M_STATE:(g + 1) * M_STATE].astype(BF16)
        cg = bcc[:, ngn + g * M_STATE:ngn + (g + 1) * M_STATE].astype(BF16)
        cb = _dot(cg, bg, NT)
        sg = s_scr[g * M_HPG:(g + 1) * M_HPG].reshape(gw, M_STATE)
        y_inter = _dot(cg, sg.astype(BF16), NT) * ecum[:, g * gw:(g + 1) * gw]
        upd = _dot(xend[:, g * gw:(g + 1) * gw], bg, TN)
        parts = []
        for hh in range(M_HPG):
            h = g * M_HPG + hh
            decay = jnp.exp(jnp.where(causal, cum[:, h:h + 1] - cum_t[h:h + 1, :], -jnp.inf))
            m = (cb * decay).astype(BF16)
            parts.append(_dot(m, xdt_b[:, h * M_HEADDIM:(h + 1) * M_HEADDIM], NN))
            s_scr[h] = jnp.exp(last[:, h:h + 1]) * s_scr[h] + upd[hh * M_HEADDIM:(hh + 1) * M_HEADDIM, :]
        ys.append(jnp.concatenate(parts, axis=1) + y_inter)
    y = jnp.concatenate(ys, axis=1) + dexp_ref[...] * xc
    y = y * _silu(z_ref[...])
    y_ref[...] = (_group_rmsnorm(y, gw) * nw_ref[...]).astype(y_ref.dtype)

    @pl.when(n == pl.num_programs(1) - 1)
    def _():
        sfin_ref[0] = s_scr[...]
        conv_ref[0, :, 0:M_DI] = ux[c - (M_CONV - 1):, :]
        conv_ref[0, :, M_DI:] = ubc[c - (M_CONV - 1):, :]


def _ssd_prompt(proj, dt, conv_w, conv_b, dt_bias, a_log, d_exp, norm_w, e2, *, batch, seq, prev=None):
    c = M_CHUNK
    nchunk = seq // c
    tok = lambda j: pl.BlockSpec((c, M_DI), lambda b, n: (b * nchunk + n, j))
    const = lambda shape, j=0: pl.BlockSpec(shape, lambda b, n: (0, j))
    blocks = (3 * _nbytes((c, M_DI), F32) + _nbytes((c, M_DI), BF16) + _nbytes((2 * V7X_LANES, M_DI), BF16)
              + 2 * _nbytes((M_HEADS, M_HEADDIM, M_STATE), F32) + 12 * _nbytes((c, M_DI), F32))
    body, xspecs, xops, alias = _shared_output(_ssd_prompt_body, 13, prev)
    return pl.pallas_call(
        body,
        grid=(batch, nchunk),
        in_specs=[tok(4), tok(5), tok(6),
                  pl.BlockSpec((c, V7X_LANES), lambda b, n: (b * nchunk + n, 0)),
                  const((M_CONV, M_DI), 0), const((M_CONV, M_DI), 1), const((1, M_DI), 0), const((1, M_DI), 1),
                  const((1, V7X_LANES)), const((1, V7X_LANES)), const((1, M_DI)), const((1, M_DI)),
                  const((2 * V7X_LANES, M_DI))] + xspecs,
        out_specs=[pl.BlockSpec((c, M_DI), lambda b, n: (b * nchunk + n, HG_VW // M_DI)),
                   pl.BlockSpec((1, M_HEADS, M_HEADDIM, M_STATE), lambda b, n: (b, 0, 0, 0)),
                   pl.BlockSpec((1, M_CONV - 1, M_CONV_DIM), lambda b, n: (b, 0, 0))],
        out_shape=[jax.ShapeDtypeStruct((proj.shape[0], MIX0_WIDTH), BF16),
                   jax.ShapeDtypeStruct((batch, M_HEADS, M_HEADDIM, M_STATE), F32),
                   jax.ShapeDtypeStruct((batch, M_CONV - 1, M_CONV_DIM), F32)],
        scratch_shapes=[pltpu.VMEM((M_HEADS, M_HEADDIM, M_STATE), F32),
                        pltpu.VMEM((V7X_SUBLANES, M_DI), F32), pltpu.VMEM((V7X_SUBLANES, M_DI), F32)],
        input_output_aliases=alias,
        compiler_params=_params(("arbitrary", "arbitrary"), blocks),
        name="ssd_prompt",
    )(proj, proj, proj, dt, conv_w, conv_w, conv_b, conv_b, dt_bias, a_log, d_exp, norm_w, e2, *xops)


SSD_SAMPLE_BB = 8


def _shift_rows(a, j):
    return a if j == 0 else pltpu.roll(a, j, 0)


def _ssd_sample_body(z_ref, x_ref, bc_ref, dt_ref, s_ref, cs_ref, cwx_ref, cwb_ref, cbx_ref, cbb_ref, dtb_ref,
                     alog_ref, dexp_ref, nw_ref, e2_ref, y_ref, so_ref, co_ref, *, seq):
    rows = V7X_SUBLANES
    hist = M_CONV - 1
    assert rows == 2 * seq and hist <= seq and hist <= rows - seq
    rvec = lax.broadcasted_iota(jnp.int32, (rows, 1), 0)
    valid = rvec < seq
    lane = lax.broadcasted_iota(jnp.int32, (1, V7X_LANES), 1)
    r16 = lax.broadcasted_iota(jnp.int32, (2 * rows, M_STATE), 0)
    ones_rows = jnp.where((r16 == rows) | (r16 == rows + 1), 1.0, 0.0).astype(BF16)
    e2 = e2_ref[...]
    a_neg = -jnp.exp(alog_ref[...])
    gw = M_HPG * M_HEADDIM
    ngn = M_GROUPS * M_STATE

    def conv(u8, buf, w_ref, b_ref):
        buf8 = jnp.concatenate([buf, jnp.zeros((rows - hist, buf.shape[1]), F32)], axis=0)
        ext = jnp.where(valid, u8, _shift_rows(buf8, rows - hist))
        acc = b_ref[...] + w_ref[hist:hist + 1, :] * ext
        for j in range(1, M_CONV):
            acc = acc + w_ref[hist - j:hist - j + 1, :] * _shift_rows(ext, j)
        new_hist = _shift_rows(ext, rows - (seq - hist))[0:hist, :]
        return _silu(acc), new_hist

    def one(b, u_x, u_bc, z8, dt8):
        xc, nhx = conv(u_x, cs_ref[b, :, 0:M_DI], cwx_ref, cbx_ref)
        bcc, nhb = conv(u_bc, cs_ref[b, :, M_DI:], cwb_ref, cbb_ref)
        co_ref[b, :, 0:M_DI] = nhx
        co_ref[b, :, M_DI:] = nhb
        dtp = jnp.where(valid, _softplus(dt8 + dtb_ref[...]), 0.0)
        cum = dtp * a_neg
        k = 1
        while k < seq:
            cum = cum + jnp.where(rvec >= k, _shift_rows(cum, k), 0.0)
            k *= 2
        last = cum[seq - 1:seq, :]
        bmat = bcc[:, :ngn]
        cmat = bcc[:, ngn:]
        coefs = [dtp, jnp.exp(cum), jnp.exp(last - cum), jnp.broadcast_to(jnp.exp(last), (rows, V7X_LANES))]
        for j in range(seq):
            prod = cmat * _shift_rows(bmat, j)
            cbh = jnp.zeros((rows, V7X_LANES), F32)
            for g in range(M_GROUPS):
                cbg = jnp.sum(prod[:, g * M_STATE:(g + 1) * M_STATE], axis=-1, keepdims=True)
                cbh = jnp.where((lane // M_HPG) == g, cbg, cbh)
            coefs.append(cbh * jnp.exp(cum - _shift_rows(cum, j)))
        wide = _expand_heads(jnp.concatenate(coefs, axis=0), e2)
        part = lambda n: wide[n * rows:(n + 1) * rows, :]
        xdt = xc * part(0)
        ecum = part(1)
        xs = xdt * part(2)
        dec = part(3)[0:1, :]
        y = dexp_ref[...] * xc
        for j in range(seq):
            y = y + part(4 + j) * _shift_rows(xdt, j)
        dec_hi = dec.astype(BF16).astype(F32)
        lhs = jnp.concatenate([xs, dec_hi, dec - dec_hi, jnp.zeros((rows - 2, M_DI), F32)], axis=0).astype(BF16)
        parts = []
        for g in range(M_GROUPS):
            bg = bmat[:, g * M_STATE:(g + 1) * M_STATE].astype(BF16)
            cg = cmat[:, g * M_STATE:(g + 1) * M_STATE].astype(BF16)
            s0 = s_ref[b, g * M_HPG:(g + 1) * M_HPG].reshape(gw, M_STATE)
            parts.append(_dot(cg, s0.astype(BF16), NT) * ecum[:, g * gw:(g + 1) * gw])
            rhs = jnp.concatenate([jnp.concatenate([bg, jnp.zeros((rows, M_STATE), BF16)], axis=0), ones_rows], axis=1)
            both = _dot(lhs[:, g * gw:(g + 1) * gw], rhs, TN)
            s_new = both[:, M_STATE:] * s0 + both[:, :M_STATE]
            so_ref[b, g * M_HPG:(g + 1) * M_HPG] = s_new.reshape(M_HPG, M_HEADDIM, M_STATE)
        y = (y + jnp.concatenate(parts, axis=1)) * _silu(z8)
        return _group_rmsnorm(y, gw) * nw_ref[...]

    def pair(p, carry):
        r0 = pl.multiple_of(p * rows, rows)
        tiles = [ref[pl.ds(r0, rows), :] for ref in (x_ref, bc_ref, z_ref, dt_ref)]
        ys = []
        for j in range(2):
            ys.append(one(2 * p + j, *[_shift_rows(t, j * (rows - seq)) for t in tiles]))
        y_ref[pl.ds(r0, rows), :] = jnp.where(valid, ys[0], _shift_rows(ys[1], seq)).astype(y_ref.dtype)
        return carry

    lax.fori_loop(0, SSD_SAMPLE_BB // 2, pair, 0)


def _ssd_sample(proj, row0, dt, state, conv_state, conv_w, conv_b, dt_bias, a_log, d_exp, norm_w, e2, *,
                batch, seq, prev=None):
    bb = SSD_SAMPLE_BB
    tr = bb * seq
    tok = lambda j: pl.BlockSpec((tr, M_DI), lambda i: (row0 // tr + i, j))
    const = lambda shape, j=0: pl.BlockSpec(shape, lambda i: (0, j))
    sblk = pl.BlockSpec((bb, M_HEADS, M_HEADDIM, M_STATE), lambda i: (i, 0, 0, 0))
    cblk = pl.BlockSpec((bb, M_CONV - 1, M_CONV_DIM), lambda i: (i, 0, 0))
    blocks = (3 * _nbytes((tr, M_DI), F32) + 2 * _nbytes((bb, M_HEADS, M_HEADDIM, M_STATE), F32)
              + 2 * _nbytes((bb, V7X_SUBLANES, M_CONV_DIM), F32) + _nbytes((2 * V7X_LANES, M_DI), BF16))
    body, xspecs, xops, alias = _shared_output(functools.partial(_ssd_sample_body, seq=seq), 15, prev)
    return pl.pallas_call(
        body,
        grid=(batch // bb,),
        in_specs=[tok(4), tok(5), tok(6),
                  pl.BlockSpec((tr, V7X_LANES), lambda i: (row0 // tr + i, 0)),
                  sblk, cblk,
                  const((M_CONV, M_DI), 0), const((M_CONV, M_DI), 1), const((1, M_DI), 0), const((1, M_DI), 1),
                  const((1, V7X_LANES)), const((1, V7X_LANES)), const((1, M_DI)), const((1, M_DI)),
                  const((2 * V7X_LANES, M_DI))] + xspecs,
        out_specs=[pl.BlockSpec((tr, M_DI), lambda i: (row0 // tr + i, HG_VW // M_DI)), sblk, cblk],
        out_shape=[jax.ShapeDtypeStruct((proj.shape[0], MIX0_WIDTH), BF16),
                   jax.ShapeDtypeStruct(state.shape, F32),
                   jax.ShapeDtypeStruct(conv_state.shape, F32)],
        input_output_aliases=alias,
        compiler_params=_params(("arbitrary",), blocks),
        name="ssd_sample",
    )(proj, proj, proj, dt, state, conv_state, conv_w, conv_w, conv_b, conv_b, dt_bias, a_log, d_exp, norm_w, e2, *xops)


S5_TILES = 8
S5_TILE_IN = D_MODEL // S5_TILES
S5_TILE_ST = S5_STATE // S5_TILES


def _s5_discretize(a_re, a_im, log_dt, b_re, b_im, c_re, c_im):
    lam_re = jnp.minimum(a_re, -1e-4)
    lam_im = a_im
    dt = jnp.exp(log_dt)[:, None]
    mag = jnp.exp(lam_re * dt)
    ab_re = mag * jnp.cos(lam_im * dt)
    ab_im = mag * jnp.sin(lam_im * dt)
    den = lam_re * lam_re + lam_im * lam_im
    zr = ((ab_re - 1.0) * lam_re + ab_im * lam_im) / den
    zi = (ab_im * lam_re - (ab_re - 1.0) * lam_im) / den
    bb_re = zr[..., None] * b_re - zi[..., None] * b_im
    bb_im = zr[..., None] * b_im + zi[..., None] * b_re
    gpt = S5_GROUPS // S5_TILES
    eye = jnp.eye(gpt, dtype=F32)

    def pack_b(bb):
        return jnp.einsum('kgph,gm->kghmp', bb.reshape(S5_TILES, gpt, S5_P, S5_GROUP), eye).reshape(
            S5_TILES, S5_TILE_IN, S5_TILE_ST).astype(BF16)

    def pack_c(c):
        return jnp.einsum('kghp,gm->kgpmh', c.reshape(S5_TILES, gpt, S5_GROUP, S5_P), eye).reshape(
            S5_TILES, S5_TILE_ST, S5_TILE_IN).astype(BF16)

    slab = lambda a: _s5_slab_layout(a.reshape(S5_STATE))
    return slab(ab_re), slab(ab_im), pack_b(bb_re), pack_b(bb_im), pack_c(c_re), pack_c(c_im)


def _s5_bu_body(x_ref, wr_ref, wi_ref, or_ref, oi_ref):
    x = x_ref[...]
    or_ref[...] = _dot(x, wr_ref[0], NN)
    oi_ref[...] = _dot(x, wi_ref[0], NN)


def _s5_bu(xb, wb_re, wb_im, *, tm):
    t = xb.shape[0]
    wspec = pl.BlockSpec((1, S5_TILE_IN, S5_TILE_ST), lambda k, i: (k, 0, 0))
    ospec = pl.BlockSpec((tm, S5_TILE_ST), lambda k, i: (i, k))
    blocks = _nbytes((tm, S5_TILE_IN), BF16) + 2 * _nbytes((S5_TILE_IN, S5_TILE_ST), BF16) + 2 * _nbytes((tm, S5_TILE_ST), F32)
    return pl.pallas_call(
        _s5_bu_body,
        grid=(S5_TILES, t // tm),
        in_specs=[pl.BlockSpec((tm, S5_TILE_IN), lambda k, i: (i, k)), wspec, wspec],
        out_specs=[ospec, ospec],
        out_shape=[jax.ShapeDtypeStruct((t, S5_STATE), F32)] * 2,
        compiler_params=_params(("arbitrary", "arbitrary"), blocks),
        name="s5_bu",
    )(xb, wb_re, wb_im)


def _s5_scan_body(br_ref, bi_ref, h0r_ref, h0i_ref, ar_ref, ai_ref, *rest, bb, tc):
    hr_ref, hi_ref, fr_ref, fi_ref, cr_scr, ci_scr = rest[-6:]
    n = pl.program_id(1)

    @pl.when(n == 0)
    def _():
        cr_scr[...] = h0r_ref[...]
        ci_scr[...] = h0i_ref[...]

    ar = ar_ref[...]
    ai = ai_ref[...]

    def seq_body(b, carry):
        def step(t, h):
            hr, hi = h
            r = b * tc + t
            nr = ar * hr - ai * hi + br_ref[r]
            ni = ar * hi + ai * hr + bi_ref[r]
            hr_ref[r] = nr
            hi_ref[r] = ni
            return nr, ni

        hr, hi = lax.fori_loop(0, tc, step, (cr_scr[b], ci_scr[b]))
        cr_scr[b] = hr
        ci_scr[b] = hi
        return carry

    lax.fori_loop(0, bb, seq_body, 0)

    @pl.when(n == pl.num_programs(1) - 1)
    def _():
        fr_ref[...] = cr_scr[...]
        fi_ref[...] = ci_scr[...]


def _s5_scan(bu_re, bu_im, h0_re, h0_im, ab_re, ab_im, *, row0, batch, seq, bb, tc, prev=None, name):
    t, sub, w = bu_re.shape
    assert bb == 1 or tc == seq
    rb = bb * tc
    blk = pl.BlockSpec((rb, sub, w), lambda i, n: ((row0 + i * bb * seq) // rb + n, 0, 0))
    sblk = pl.BlockSpec((bb, sub, w), lambda i, n: (i, 0, 0))
    ablk = pl.BlockSpec((sub, w), lambda i, n: (0, 0))
    blocks = 4 * _nbytes((rb, sub, w), F32) + 6 * _nbytes((bb, sub, w), F32)
    keep = [] if prev is None else list(prev)
    n_in = 6
    return pl.pallas_call(
        functools.partial(_s5_scan_body, bb=bb, tc=tc),
        grid=(batch // bb, seq // tc),
        in_specs=[blk, blk, sblk, sblk, ablk, ablk] + [pl.BlockSpec(memory_space=pl.ANY)] * len(keep),
        out_specs=[blk, blk, sblk, sblk],
        out_shape=[jax.ShapeDtypeStruct(bu_re.shape, F32)] * 2 + [jax.ShapeDtypeStruct(h0_re.shape, F32)] * 2,
        scratch_shapes=[pltpu.VMEM((bb, sub, w), F32)] * 2,
        input_output_aliases={n_in + k: k for k in range(len(keep))},
        compiler_params=_params(("arbitrary", "arbitrary"), blocks),
        name=name,
    )(bu_re, bu_im, h0_re, h0_im, ab_re, ab_im, *keep)


def _s5_out_body(hr_ref, hi_ref, x_ref, d_ref, wr_ref, wi_ref, o_ref):
    y = _dot(hr_ref[...].astype(BF16), wr_ref[0], NN) - _dot(hi_ref[...].astype(BF16), wi_ref[0], NN)
    o_ref[...] = jax.nn.gelu(y + d_ref[...] * x_ref[...]).astype(o_ref.dtype)


def _s5_out(h_re, h_im, x, d_skip, wc_re, wc_im, *, tm):
    t = x.shape[0]
    hspec = pl.BlockSpec((tm, S5_TILE_ST), lambda k, i: (i, k))
    wspec = pl.BlockSpec((1, S5_TILE_ST, S5_TILE_IN), lambda k, i: (k, 0, 0))
    xspec = pl.BlockSpec((tm, S5_TILE_IN), lambda k, i: (i, k))
    blocks = (2 * _nbytes((tm, S5_TILE_ST), F32) + 2 * _nbytes((S5_TILE_ST, S5_TILE_IN), BF16)
              + _nbytes((tm, S5_TILE_IN), F32) + _nbytes((tm, S5_TILE_IN), BF16))
    return pl.pallas_call(
        _s5_out_body,
        grid=(S5_TILES, t // tm),
        in_specs=[hspec, hspec, xspec, pl.BlockSpec((1, S5_TILE_IN), lambda k, i: (0, k)), wspec, wspec],
        out_specs=xspec,
        out_shape=jax.ShapeDtypeStruct((t, D_MODEL), BF16),
        compiler_params=_params(("arbitrary", "arbitrary"), blocks),
        name="s5_out",
    )(h_re, h_im, x, d_skip.reshape(1, D_MODEL), wc_re, wc_im)


S5_PROMPT_ROWS = 256
S5_SAMPLE_ROWS = 128
S5_NBLK = S5_TILE_ST // V7X_LANES


def _s5_pitch(rows):
    return rows + (V7X_SUBLANES if (rows // V7X_SUBLANES) % 2 == 0 else 0)


def _s5_slab_layout(a):
    lead = a.shape[:-1]
    return jnp.swapaxes(a.reshape(lead + (S5_TILES, S5_NBLK, V7X_LANES)), -3, -2)


def _s5_from_slab_layout(a):
    lead = a.shape[:-3]
    return jnp.swapaxes(a, -3, -2).reshape(lead + (S5_STATE,))


def _s5_fused_body(xb_ref, x_ref, h0r_ref, h0i_ref, ar_ref, ai_ref, wbr_ref, wbi_ref, wcr_ref, wci_ref, d_ref, *rest,
                   n_seq, steps):
    act_ref, fr_ref, fi_ref, bur, bui, cr, ci = rest[-7:]
    hrs, his = bur, bui
    n = pl.program_id(1)
    rows = n_seq * steps
    S5_PITCH = _s5_pitch(rows)
    for s in range(S5_TILES):
        u = xb_ref[:, s * S5_TILE_IN:(s + 1) * S5_TILE_IN]
        br = _dot(u, wbr_ref[s], NN)
        bi = _dot(u, wbi_ref[s], NN)
        for j in range(S5_NBLK):
            bur[j, s * S5_PITCH:s * S5_PITCH + rows, :] = br[:, j * V7X_LANES:(j + 1) * V7X_LANES]
            bui[j, s * S5_PITCH:s * S5_PITCH + rows, :] = bi[:, j * V7X_LANES:(j + 1) * V7X_LANES]

    @pl.when(n == 0)
    def _():
        cr[...] = h0r_ref[...]
        ci[...] = h0i_ref[...]

    ar = [ar_ref[j] for j in range(S5_NBLK)]
    ai = [ai_ref[j] for j in range(S5_NBLK)]

    def seq_body(q, carry):
        def step(t, h):
            hr, hi = h
            idx = pl.ds(q * steps + t, S5_TILES, stride=S5_PITCH)
            nr, ni = [], []
            for j in range(S5_NBLK):
                nr.append(ar[j] * hr[j] - ai[j] * hi[j] + bur[j, idx, :])
                ni.append(ar[j] * hi[j] + ai[j] * hr[j] + bui[j, idx, :])
                hrs[j, idx, :] = nr[j]
                his[j, idx, :] = ni[j]
            return tuple(nr), tuple(ni)

        h0 = (tuple(cr[q, j] for j in range(S5_NBLK)), tuple(ci[q, j] for j in range(S5_NBLK)))
        hr, hi = lax.fori_loop(0, steps, step, h0, unroll=min(steps, 4))
        for j in range(S5_NBLK):
            cr[q, j] = hr[j]
            ci[q, j] = hi[j]
        return carry

    lax.fori_loop(0, n_seq, seq_body, 0)

    for s in range(S5_TILES):
        seg = slice(s * S5_PITCH, s * S5_PITCH + rows)
        hre = jnp.concatenate([hrs[j, seg, :] for j in range(S5_NBLK)], axis=1).astype(BF16)
        him = jnp.concatenate([his[j, seg, :] for j in range(S5_NBLK)], axis=1).astype(BF16)
        y = _dot(hre, wcr_ref[s], NN) - _dot(him, wci_ref[s], NN)
        cs = slice(s * S5_TILE_IN, (s + 1) * S5_TILE_IN)
        act_ref[:, cs] = jax.nn.gelu(y + d_ref[:, cs] * x_ref[:, cs]).astype(act_ref.dtype)

    @pl.when(n == pl.num_programs(1) - 1)
    def _():
        fr_ref[...] = cr[...]
        fi_ref[...] = ci[...]


def _s5_fused(xb, x, h0_re, h0_im, ab_re, ab_im, wb_re, wb_im, wc_re, wc_im, d_skip, *, row0, batch, seq, rows, n_seq,
              prev=None, name):
    t, d = x.shape
    steps = rows // n_seq
    assert n_seq == 1 or steps == seq
    S5_PITCH = _s5_pitch(rows)
    tok = lambda width: pl.BlockSpec((rows, width), lambda i, n: ((row0 + i * n_seq * seq) // rows + n, 0))
    st = pl.BlockSpec((n_seq, S5_NBLK, S5_TILES, V7X_LANES), lambda i, n: (i, 0, 0, 0))
    const = lambda shape: pl.BlockSpec(shape, lambda i, n: (0,) * len(shape), pipeline_mode=pl.Buffered(1))
    slab = pltpu.VMEM((S5_NBLK, S5_TILES * S5_PITCH, V7X_LANES), F32)
    stv = pltpu.VMEM((n_seq, S5_NBLK, S5_TILES, V7X_LANES), F32)
    blocks = (2 * (_nbytes((rows, d), BF16) * 2 + _nbytes((rows, d), F32))
              + 2 * 6 * _nbytes((n_seq, S5_STATE), F32) + 4 * _nbytes(wb_re.shape, BF16)
              + 2 * _nbytes((S5_NBLK, S5_TILES * S5_PITCH, V7X_LANES), F32))
    keep = [] if prev is None else [prev]
    n_in = 11
    return pl.pallas_call(
        functools.partial(_s5_fused_body, n_seq=n_seq, steps=steps),
        grid=(batch // n_seq, seq // steps),
        in_specs=[tok(d), tok(d), st, st, const(ab_re.shape), const(ab_im.shape), const(wb_re.shape),
                  const(wb_im.shape), const(wc_re.shape), const(wc_im.shape), const((1, d))]
                 + [pl.BlockSpec(memory_space=pl.ANY)] * len(keep),
        out_specs=[tok(d), st, st],
        out_shape=[jax.ShapeDtypeStruct((t, d), BF16), jax.ShapeDtypeStruct(h0_re.shape, F32),
                   jax.ShapeDtypeStruct(h0_im.shape, F32)],
        scratch_shapes=[slab, slab, stv, stv],
        input_output_aliases={n_in + k: k for k in range(len(keep))},
        compiler_params=pltpu.CompilerParams(dimension_semantics=("arbitrary", "arbitrary"),
                                             vmem_limit_bytes=int(min(blocks + COMPILER_SCRATCH_BYTES, V7X_SCOPED_VMEM_CAP))),
        name=name,
    )(xb, x, h0_re, h0_im, ab_re, ab_im, wb_re, wb_im, wc_re, wc_im, d_skip.reshape(1, d), *keep)


def _glu_body(a_ref, wa_ref, wb_ref, ba_ref, bb_ref, o_ref, wa_scr, wb_scr):
    @pl.when(pl.program_id(1) == 0)
    def _():
        wa_scr[...] = wa_ref[0].astype(BF16)
        wb_scr[...] = wb_ref[0].astype(BF16)

    a = a_ref[...]
    o_ref[...] = (_dot(a, wa_scr[...], NN) + ba_ref[...]) * jax.nn.sigmoid(_dot(a, wb_scr[...], NN) + bb_ref[...])


def _glu(a, wa, wb, ba, bb, layer, *, tm, tn):
    t, k = a.shape
    n = wa.shape[2]
    wspec = pl.BlockSpec((1, k, tn), lambda j, i: (layer, 0, j))
    bspec = pl.BlockSpec((1, tn), lambda j, i: (layer, j))
    blocks = _nbytes((tm, k), BF16) + 2 * _nbytes((k, tn), F32) + _nbytes((tm, tn), F32) + 2 * _nbytes((k, tn), BF16)
    return pl.pallas_call(
        _glu_body,
        grid=(n // tn, t // tm),
        in_specs=[pl.BlockSpec((tm, k), lambda j, i: (i, 0)), wspec, wspec, bspec, bspec],
        out_specs=pl.BlockSpec((tm, tn), lambda j, i: (i, j)),
        out_shape=jax.ShapeDtypeStruct((t, n), F32),
        scratch_shapes=[pltpu.VMEM((k, tn), BF16)] * 2,
        compiler_params=_params(("arbitrary", "arbitrary"), blocks),
        name="glu",
    )(a, wa, wb, ba, bb)


def _router_body(x_ref, w_ref, b_ref, e_ref, g_ref):
    _route(x_ref[...], w_ref, b_ref, e_ref, g_ref)


def _route(x, w_ref, b_ref, e_ref, g_ref):
    logits = _dot(w_ref[...], x, NT, precision=HIGHEST) + b_ref[...]
    rows = [logits[e:e + 1, :] for e in range(N_EXPERTS)]
    m = functools.reduce(jnp.maximum, rows)
    ex = [jnp.exp(r - m) for r in rows]
    z = functools.reduce(jnp.add, ex)
    p = [v / z for v in ex]

    def top2_sum(a, b, c, d):
        hi1, lo1, hi2, lo2 = jnp.maximum(a, b), jnp.minimum(a, b), jnp.maximum(c, d), jnp.minimum(c, d)
        return jnp.maximum(hi1, hi2) + jnp.maximum(jnp.minimum(hi1, hi2), jnp.maximum(lo1, lo2))

    assert EXP_PER_GROUP == 4
    best = top2_sum(*p[0:EXP_PER_GROUP])
    gsel = jnp.zeros_like(best, dtype=jnp.int32)
    for g in range(1, N_EXP_GROUPS):
        s = top2_sum(*p[g * EXP_PER_GROUP:(g + 1) * EXP_PER_GROUP])
        better = s > best
        gsel = jnp.where(better, g, gsel)
        best = jnp.where(better, s, best)
    inner = []
    for i in range(EXP_PER_GROUP):
        v = p[i]
        for g in range(1, N_EXP_GROUPS):
            v = jnp.where(gsel == g, p[g * EXP_PER_GROUP + i], v)
        inner.append(v)

    def first_argmax(vals, skip=None):
        bv = bi = None
        for i, v in enumerate(vals):
            v = v if skip is None else jnp.where(skip == i, -jnp.inf, v)
            if bv is None:
                bv, bi = v, jnp.zeros_like(gsel)
            else:
                better = v > bv
                bi = jnp.where(better, i, bi)
                bv = jnp.where(better, v, bv)
        return bv, bi

    p1, i1 = first_argmax(inner)
    p2, i2 = first_argmax(inner, skip=i1)
    tot = p1 + p2
    e1 = gsel * EXP_PER_GROUP + i1
    e2 = gsel * EXP_PER_GROUP + i2
    e_ref[0:1, :] = e1
    e_ref[1:2, :] = e2
    g_ref[0:1, :] = p1 / tot
    g_ref[1:2, :] = p2 / tot


def _ln_router_body(x_ref, x2_ref, mix_ref, g_ref, b_ref, w_ref, br_ref, o_ref, e_ref, gw_ref, *, first_blocks):
    x = x_ref[...] if first_blocks is None else jnp.where(pl.program_id(0) < first_blocks, x_ref[...], x2_ref[...])
    z = ALPHA * x + mix_ref[...]
    mu = jnp.mean(z, -1, keepdims=True)
    zc = z - mu
    var = jnp.mean(zc * zc, -1, keepdims=True)
    y = zc * lax.rsqrt(var + LN_EPS) * g_ref[...] + b_ref[...]
    o_ref[...] = y
    _route(y, w_ref, br_ref, e_ref, gw_ref)


def _res_layernorm_router(x, mix, g, b, w_router, b_router, *, tm, name, x_tail=None):
    t, d = mix.shape
    row = pl.BlockSpec((tm, d), lambda i: (i, 0))
    vec = pl.BlockSpec((1, d), lambda i: (0, 0))
    sel = pl.BlockSpec((TOP_K, tm), lambda i: (0, i))
    if x_tail is None:
        nb, xspecs, xs = None, [row, vec], [x, g.reshape(1, d)]
    else:
        nb = x.shape[0] // tm
        xspecs = [pl.BlockSpec((tm, d), lambda i: (jnp.minimum(i, nb - 1), 0)),
                  pl.BlockSpec((tm, d), lambda i: (jnp.maximum(i - nb, 0), 0))]
        xs = [x, x_tail]
    blocks = 4 * _nbytes((tm, d), F32) + _nbytes((N_EXPERTS, d), F32) + 2 * _nbytes((V7X_SUBLANES, tm), F32)
    return pl.pallas_call(
        functools.partial(_ln_router_body, first_blocks=nb),
        grid=(t // tm,),
        in_specs=xspecs + [row, vec, vec, pl.BlockSpec((N_EXPERTS, d), lambda i: (0, 0)),
                           pl.BlockSpec((N_EXPERTS, 1), lambda i: (0, 0))],
        out_specs=[row, sel, sel],
        out_shape=[jax.ShapeDtypeStruct((t, d), F32), jax.ShapeDtypeStruct((TOP_K, t), jnp.int32),
                   jax.ShapeDtypeStruct((TOP_K, t), F32)],
        compiler_params=_params(("arbitrary",), blocks),
        name=name,
    )(*xs, mix, g.reshape(1, d), b.reshape(1, d), w_router.T, b_router.reshape(N_EXPERTS, 1))


def _router(x, w_router, b_router, *, tm):
    t, d = x.shape
    blocks = _nbytes((tm, d), F32) + _nbytes((N_EXPERTS, d), F32) + 2 * _nbytes((V7X_SUBLANES, tm), F32)
    out = pl.BlockSpec((TOP_K, tm), lambda i: (0, i))
    return pl.pallas_call(
        _router_body,
        grid=(t // tm,),
        in_specs=[pl.BlockSpec((tm, d), lambda i: (i, 0)), pl.BlockSpec((N_EXPERTS, d), lambda i: (0, 0)),
                  pl.BlockSpec((N_EXPERTS, 1), lambda i: (0, 0))],
        out_specs=[out, out],
        out_shape=[jax.ShapeDtypeStruct((TOP_K, t), jnp.int32), jax.ShapeDtypeStruct((TOP_K, t), F32)],
        compiler_params=_params(("arbitrary",), blocks),
        name="router",
    )(x, w_router.T, b_router.reshape(N_EXPERTS, 1))


MOE_TILE = 256
COMBINE_TILE = 256


def _moe_plan(eidx):
    k, t = eidx.shape
    ns = k * t
    n_tiles = ns // MOE_TILE + N_EXPERTS
    e_flat = eidx.reshape(ns)
    order = jnp.argsort(e_flat, stable=True).astype(jnp.int32)
    rank = jnp.argsort(order).astype(jnp.int32)
    experts = jnp.arange(N_EXPERTS, dtype=jnp.int32)[None, :]
    slot_is = e_flat[:, None] == experts
    counts = jnp.sum(slot_is.astype(jnp.int32), axis=0)
    start = jnp.cumsum(counts) - counts
    tiles = (counts + MOE_TILE - 1) // MOE_TILE
    tile_end = jnp.cumsum(tiles)
    tile_start = tile_end - tiles
    tile_ids = jnp.arange(n_tiles, dtype=jnp.int32)
    tile_expert = jnp.minimum(jnp.sum((tile_ids[:, None] >= tile_end[None, :]).astype(jnp.int32), axis=1),
                              N_EXPERTS - 1)
    pick = lambda onehot, table: jnp.sum(jnp.where(onehot, table[None, :], 0), axis=1)
    tile_is = tile_expert[:, None] == experts
    tile_first = pick(tile_is, start) + (tile_ids - pick(tile_is, tile_start)) * MOE_TILE
    position_of_slot = pick(slot_is, tile_start) * MOE_TILE + rank - pick(slot_is, start)
    return (tile_expert.astype(jnp.int32), tile_first.astype(jnp.int32), tile_end[-1:].astype(jnp.int32),
            (order % t).astype(jnp.int32), position_of_slot.astype(jnp.int32))


def _moe_expert_body(te_ref, first_ref, nt_ref, tok_ref, x_hbm, wg_ref, wu_ref, wd_ref, y_ref, xbuf, sem, wg_scr,
                     wu_scr, wd_scr):
    i = pl.program_id(0)
    nt = nt_ref[0]
    tm = MOE_TILE
    last_slot = tok_ref.shape[0] - 1

    def gather_copy(tok, slot, r):
        return pltpu.make_async_copy(x_hbm.at[pl.ds(tok, 1)], xbuf.at[slot, pl.ds(r, 1)], sem.at[slot])

    def start_gather(tile, slot):
        first = first_ref[tile]

        def body(r, c):
            gather_copy(tok_ref[jnp.minimum(first + r, last_slot)], slot, r).start()
            return c

        lax.fori_loop(0, tm, body, 0, unroll=8)

    @pl.when(i == 0)
    def _():
        start_gather(0, 0)

    @pl.when(i + 1 < nt)
    def _():
        start_gather(i + 1, (i + 1) % 2)

    @pl.when(i < nt)
    def _():
        slot = i % 2
        pltpu.make_async_copy(x_hbm.at[pl.ds(0, tm)], xbuf.at[slot], sem.at[slot]).wait()

        @pl.when((i == 0) | (te_ref[i] != te_ref[jnp.maximum(i - 1, 0)]))
        def _():
            wg_scr[...] = wg_ref[0, 0].astype(BF16)
            wu_scr[...] = wu_ref[0, 0].astype(BF16)
            wd_scr[...] = wd_ref[0, 0].astype(BF16)

        x = xbuf[slot].astype(BF16)
        act = (_silu(_dot(x, wg_scr[...], NN)) * _dot(x, wu_scr[...], NN)).astype(BF16)
        y_ref[...] = _dot(act, wd_scr[...], NN)

    @pl.when(i >= nt)
    def _():
        y_ref[...] = jnp.zeros_like(y_ref)


def _moe_experts(x, plan, w_gate, w_up, w_down, layer):
    tile_expert, tile_first, n_valid, sorted_token, _ = plan
    t, d = x.shape
    f = w_gate.shape[3]
    n_tiles = tile_expert.shape[0]
    wspec = lambda a, b: pl.BlockSpec((1, 1, a, b), lambda i, te, *_: (layer, te[i], 0, 0))
    blocks = (2 * 3 * _nbytes((d, f), F32) + 3 * _nbytes((d, f), BF16) + 2 * _nbytes((MOE_TILE, d), F32)
              + 2 * _nbytes((MOE_TILE, d), F32) + 4 * _nbytes((MOE_TILE, f), F32))
    return pl.pallas_call(
        _moe_expert_body,
        grid_spec=pltpu.PrefetchScalarGridSpec(
            num_scalar_prefetch=4,
            grid=(n_tiles,),
            in_specs=[pl.BlockSpec(memory_space=pl.ANY), wspec(d, f), wspec(d, f), wspec(f, d)],
            out_specs=pl.BlockSpec((MOE_TILE, d), lambda i, *_: (i, 0)),
            scratch_shapes=[pltpu.VMEM((2, MOE_TILE, d), F32), pltpu.SemaphoreType.DMA((2,)),
                            pltpu.VMEM((d, f), BF16), pltpu.VMEM((d, f), BF16), pltpu.VMEM((f, d), BF16)]),
        out_shape=jax.ShapeDtypeStruct((n_tiles * MOE_TILE, d), F32),
        compiler_params=pltpu.CompilerParams(dimension_semantics=("arbitrary",),
                                             vmem_limit_bytes=int(min(blocks + COMPILER_SCRATCH_BYTES, V7X_SCOPED_VMEM_CAP))),
        name="moe_experts",
    )(tile_expert, tile_first, n_valid, sorted_token, x, w_gate, w_up, w_down)


def _moe_combine_body(pos_ref, x_ref, w_ref, g_ref, b_ref, y_hbm, o_ref, ob_ref, ybuf, sem, *, n_tok, split_blocks):
    i = pl.program_id(0)
    tc = COMBINE_TILE

    def gather_copy(p, slot, k, r):
        return pltpu.make_async_copy(y_hbm.at[pl.ds(p, 1)], ybuf.at[slot, k, pl.ds(r, 1)], sem.at[slot])

    def start_gather(tile, slot):
        def body(r, c):
            for k in range(TOP_K):
                gather_copy(pos_ref[k * n_tok + tile * tc + r], slot, k, r).start()
            return c

        lax.fori_loop(0, tc, body, 0, unroll=4)

    @pl.when(i == 0)
    def _():
        start_gather(0, 0)

    @pl.when(i + 1 < pl.num_programs(0))
    def _():
        start_gather(i + 1, (i + 1) % 2)

    slot = i % 2
    for k in range(TOP_K):
        pltpu.make_async_copy(y_hbm.at[pl.ds(0, tc)], ybuf.at[slot, k], sem.at[slot]).wait()
    z = ALPHA * x_ref[...]
    for k in range(TOP_K):
        z = z + w_ref[:, k:k + 1] * ybuf[slot, k]
    mu = jnp.mean(z, -1, keepdims=True)
    zc = z - mu
    var = jnp.mean(zc * zc, -1, keepdims=True)
    y = zc * lax.rsqrt(var + LN_EPS) * g_ref[...] + b_ref[...]
    if split_blocks is None:
        o_ref[...] = y
        ob_ref[...] = y.astype(BF16)
    else:
        @pl.when(i < split_blocks)
        def _():
            o_ref[...] = y

        @pl.when(i >= split_blocks)
        def _():
            ob_ref[...] = y


def _moe_combine_layernorm(x, y_slots, plan, gate_w, g, b, *, name, split=None):
    position_of_slot = plan[4]
    t, d = x.shape
    tc = COMBINE_TILE
    row = pl.BlockSpec((tc, d), lambda i, pos: (i, 0))
    vec = pl.BlockSpec((1, d), lambda i, pos: (0, 0))
    blocks = 2 * (3 * _nbytes((tc, d), F32) + _nbytes((tc, d), BF16)) + 2 * TOP_K * _nbytes((tc, d), F32)
    if split is None:
        sb = None
        out_specs = [row, row]
        out_shape = [jax.ShapeDtypeStruct((t, d), F32), jax.ShapeDtypeStruct((t, d), BF16)]
    else:
        sb = split // tc
        out_specs = [pl.BlockSpec((tc, d), lambda i, pos: (jnp.minimum(i, sb - 1), 0)),
                     pl.BlockSpec((tc, d), lambda i, pos: (jnp.maximum(i - sb, 0), 0))]
        out_shape = [jax.ShapeDtypeStruct((split, d), F32), jax.ShapeDtypeStruct((t - split, d), F32)]
    return pl.pallas_call(
        functools.partial(_moe_combine_body, n_tok=t, split_blocks=sb),
        grid_spec=pltpu.PrefetchScalarGridSpec(
            num_scalar_prefetch=1,
            grid=(t // tc,),
            in_specs=[row, pl.BlockSpec((tc, TOP_K), lambda i, pos: (i, 0)), vec, vec,
                      pl.BlockSpec(memory_space=pl.ANY)],
            out_specs=out_specs,
            scratch_shapes=[pltpu.VMEM((2, TOP_K, tc, d), F32), pltpu.SemaphoreType.DMA((2,))]),
        out_shape=out_shape,
        compiler_params=pltpu.CompilerParams(dimension_semantics=("arbitrary",),
                                             vmem_limit_bytes=int(min(blocks + COMPILER_SCRATCH_BYTES, V7X_SCOPED_VMEM_CAP))),
        name=name,
    )(position_of_slot, x, gate_w, g.reshape(1, d), b.reshape(1, d), y_slots)


def _moe_dense_body(x_ref, g_ref, wg_ref, wu_ref, wd_ref, o_ref):
    e = pl.program_id(1)

    @pl.when(e == 0)
    def _():
        o_ref[...] = jnp.zeros_like(o_ref)

    x = x_ref[...]
    h = _dot(x, wg_ref[0], NN)
    u = _dot(x, wu_ref[0], NN)
    lane = lax.broadcasted_iota(jnp.int32, (1, N_EXPERTS), 1)
    gate = jnp.sum(jnp.where(lane == e, g_ref[...], 0.0), axis=1, keepdims=True)
    act = (_silu(h) * u * gate).astype(BF16)
    o_ref[...] += _dot(act, wd_ref[0], NN)


def _moe_dense(xb, gates, wg, wu, wd, *, tm):
    t, d = xb.shape
    f = wg.shape[2]
    blocks = (_nbytes((tm, d), BF16) + _nbytes((tm, V7X_LANES), F32) + 3 * _nbytes((d, f), BF16)
              + _nbytes((tm, d), F32) + 4 * _nbytes((tm, f), F32))
    return pl.pallas_call(
        _moe_dense_body,
        grid=(t // tm, N_EXPERTS),
        in_specs=[pl.BlockSpec((tm, d), lambda i, e: (i, 0)), pl.BlockSpec((tm, N_EXPERTS), lambda i, e: (i, 0)),
                  pl.BlockSpec((1, d, f), lambda i, e: (e, 0, 0)), pl.BlockSpec((1, d, f), lambda i, e: (e, 0, 0)),
                  pl.BlockSpec((1, f, d), lambda i, e: (e, 0, 0))],
        out_specs=pl.BlockSpec((tm, d), lambda i, e: (i, 0)),
        out_shape=jax.ShapeDtypeStruct((t, d), F32),
        compiler_params=_params(("arbitrary", "arbitrary"), blocks),
        name="moe_dense",
    )(xb, gates, wg, wu, wd)


TOKEN_TILE = 1088
ROUTER_TILE = 512
LN_TILE = 256


def _moe_layernorm(x, eidx, gate_w, layer, w_gate, w_up, w_down, g, b, split=None):
    plan = _moe_plan(eidx)
    y_slots = _moe_experts(x, plan, w_gate, w_up, w_down, layer)
    return _moe_combine_layernorm(x, y_slots, plan, gate_w.T, g, b, name=f"moe_combine_ln_{layer}", split=split)


def kernel(x_prompt, x_sample, state_hgrn, state_ssm, state_conv, state_s5_re, state_s5_im, w_in0, hg_lb_logits, hg_norm_w, conv_w, conv_b, dt_bias, a_log, m_d, m_norm_w, w_out0, s5_a_re, s5_a_im, s5_log_dt, s5_b_re, s5_b_im, s5_c_re, s5_c_im, s5_d, glu_w_a, glu_b_a, glu_w_b, glu_b_b, w_router, b_router, w_gate, w_up, w_down, ln1_g, ln1_b, ln2_g, ln2_b):
    bp, lp, d = x_prompt.shape
    bs, ls, _ = x_sample.shape
    tp, ts = bp * lp, bs * ls
    tm = TOKEN_TILE
    xp2, xs2 = x_prompt.reshape(tp, d), x_sample.reshape(ts, d)
    x0b = jnp.concatenate([xp2.astype(BF16), xs2.astype(BF16)], axis=0)
    lower_bounds = jnp.cumsum(jax.nn.softmax(hg_lb_logits.astype(F32), axis=0), axis=0)
    pad_lanes = lambda v: jnp.pad(v, (0, V7X_LANES - v.shape[0])).reshape(1, V7X_LANES)

    proj = _matmul_w32(x0b, jnp.swapaxes(w_in0, 1, 2), 0, tm=tm, tn=1024, n_cols=IN0_MAIN, transposed=True,
                       name="in_proj")
    w_dt = jnp.pad(w_in0[0, :, IN0_MAIN:], ((0, 0), (0, V7X_LANES - M_HEADS))).astype(BF16)
    dt = _matmul(x0b, w_dt, tm=tm, tn=V7X_LANES, name="dt_proj")
    lb0 = lower_bounds[0]
    e2 = _head_expand_matrix()
    ssd_consts = (conv_w[0], conv_b[0].reshape(1, -1), pad_lanes(dt_bias[0]), pad_lanes(a_log[0]),
                  jnp.repeat(m_d[0], M_HEADDIM).reshape(1, M_DI), m_norm_w[0].reshape(1, M_DI), e2)
    mixed, hg_p = _hgrn_prompt(proj, lb0, hg_norm_w[0], batch=bp, seq=lp)
    mixed, hg_s = _hgrn_sample(proj, tp, state_hgrn[0], lb0, hg_norm_w[0], batch=bs, seq=ls, prev=mixed)
    mixed, ssm_p, conv_p = _ssd_prompt(proj, dt, *ssd_consts, batch=bp, seq=lp, prev=mixed)
    mixed, ssm_s, conv_s = _ssd_sample(proj, tp, dt, state_ssm[0], state_conv[0], *ssd_consts, batch=bs, seq=ls,
                                       prev=mixed)
    mix = _matmul_w32(mixed, w_out0, 0, tm=tm, tn=512, name="out_proj")
    x1, eidx, gate_w = _res_layernorm_router(xp2, mix, ln1_g[0], ln1_b[0], w_router, b_router, tm=LN_TILE,
                                             name="ln1_router_0", x_tail=xs2)
    x2, x2b = _moe_layernorm(x1, eidx, gate_w, 0, w_gate, w_up, w_down, ln2_g[0], ln2_b[0])

    ab_re, ab_im, wb_re, wb_im, wc_re, wc_im = _s5_discretize(
        s5_a_re[0], s5_a_im[0], s5_log_dt[0], s5_b_re[0], s5_b_im[0], s5_c_re[0], s5_c_im[0])
    s5_consts = (ab_re, ab_im, wb_re, wb_im, wc_re, wc_im, s5_d[0])
    zeros = jnp.zeros((bp, S5_NBLK, S5_TILES, V7X_LANES), F32)
    act, s5r_p, s5i_p = _s5_fused(x2b, x2, zeros, zeros, *s5_consts, row0=0, batch=bp, seq=lp,
                                  rows=S5_PROMPT_ROWS, n_seq=1, name="s5_prompt")
    act, s5r_s, s5i_s = _s5_fused(x2b, x2, _s5_slab_layout(state_s5_re[0].reshape(bs, S5_STATE)),
                                  _s5_slab_layout(state_s5_im[0].reshape(bs, S5_STATE)), *s5_consts,
                                  row0=tp, batch=bs, seq=ls, rows=S5_SAMPLE_ROWS, n_seq=S5_SAMPLE_ROWS // ls,
                                  prev=act, name="s5_sample")
    mix1 = _glu(act, glu_w_a, glu_w_b, glu_b_a, glu_b_b, 0, tm=tm, tn=512)
    x3, eidx, gate_w = _res_layernorm_router(x2, mix1, ln1_g[1], ln1_b[1], w_router, b_router, tm=LN_TILE,
                                             name="ln1_router_1")
    y_p, y_s = _moe_layernorm(x3, eidx, gate_w, 1, w_gate, w_up, w_down, ln2_g[1], ln2_b[1], split=tp)

    s5_state = lambda a, b: _s5_from_slab_layout(a).reshape(1, b, S5_GROUPS, S5_P)
    return (y_p.reshape(bp, lp, d), y_s.reshape(bs, ls, d),
            hg_p[None], hg_s[None], ssm_p[None], ssm_s[None], conv_p[None], conv_s[None],
            s5_state(s5r_p, bp), s5_state(s5r_s, bs), s5_state(s5i_p, bp), s5_state(s5i_s, bs))
```

```python
import functools
import math

import jax
import jax.numpy as jnp
from jax import lax
from jax.experimental import pallas as pl
from jax.experimental.pallas import tpu as pltpu

F32 = jnp.float32
BF16 = jnp.bfloat16
HIGHEST = lax.Precision.HIGHEST

D_MODEL = 2048
DEPTH = 2
HG_HEADS = 16
HG_DK = 128
HG_DV = 128
HG_KW = HG_HEADS * HG_DK
HG_VW = HG_HEADS * HG_DV
HG_CHUNK = 64
M_DI = 2048
M_HEADDIM = 64
M_HEADS = 32
M_GROUPS = 8
M_HPG = 4
M_STATE = 128
M_CONV = 4
M_CONV_DIM = M_DI + 2 * M_GROUPS * M_STATE
M_CHUNK = 128
IN0_MAIN = 2 * HG_KW + 2 * HG_VW + M_DI + M_CONV_DIM
S5_GROUP = 16
S5_GROUPS = 128
S5_P = 64
S5_STATE = S5_GROUPS * S5_P
N_EXPERTS = 16
N_EXP_GROUPS = 4
EXP_PER_GROUP = 4
TOP_K = 2
D_EXPERT = 512
ALPHA = (2 * DEPTH) ** 0.25
LN_EPS = 1e-5
RMS_EPS = 1e-6

V7X_LANES = 128
V7X_SUBLANES = 8
V7X_VMEM_BYTES = 64 * 1024 * 1024
V7X_SCOPED_VMEM_CAP = 60000 * 1024
COMPILER_SCRATCH_BYTES = 16 * 1024 * 1024


def _params(semantics, block_bytes):
    limit = min(2 * block_bytes + COMPILER_SCRATCH_BYTES, V7X_SCOPED_VMEM_CAP)
    return pltpu.CompilerParams(dimension_semantics=semantics, vmem_limit_bytes=int(limit))


def _nbytes(shape, dtype):
    return math.prod(shape) * jnp.dtype(dtype).itemsize


def _silu(x):
    return x * jax.nn.sigmoid(x)


def _dot(a, b, dims, precision=None):
    return lax.dot_general(a, b, (dims, ((), ())), precision=precision, preferred_element_type=F32)


NN = ((1,), (0,))
NT = ((1,), (1,))
TN = ((0,), (0,))

MIX0_WIDTH = HG_VW + M_DI


def _shared_output(body, n_in, prev):
    if prev is None:
        return body, [], [], {}

    def with_prev(*refs):
        return body(*refs[:n_in], *refs[n_in + 1:])

    return with_prev, [pl.BlockSpec(memory_space=pl.ANY)], [prev], {n_in: 0}


def _mm_body(x_ref, w_ref, o_ref):
    o_ref[...] = _dot(x_ref[...], w_ref[...], NN).astype(o_ref.dtype)


def _matmul(x, w, *, tm, tn, n_cols=None, out_dtype=F32, name):
    m, k = x.shape
    n = n_cols if n_cols is not None else w.shape[1]
    blocks = _nbytes((tm, k), x.dtype) + _nbytes((k, tn), w.dtype) + _nbytes((tm, tn), out_dtype)
    return pl.pallas_call(
        _mm_body,
        grid=(n // tn, m // tm),
        in_specs=[pl.BlockSpec((tm, k), lambda j, i: (i, 0)), pl.BlockSpec((k, tn), lambda j, i: (0, j))],
        out_specs=pl.BlockSpec((tm, tn), lambda j, i: (i, j)),
        out_shape=jax.ShapeDtypeStruct((m, n), out_dtype),
        compiler_params=_params(("arbitrary", "arbitrary"), blocks),
        name=name,
    )(x, w)


def _mm_w32_body(x_ref, w_ref, o_ref, wb_scr):
    @pl.when(pl.program_id(1) == 0)
    def _():
        wb_scr[...] = w_ref[0].astype(BF16)

    o_ref[...] = _dot(x_ref[...], wb_scr[...], NN).astype(o_ref.dtype)


def _mm_w32t_body(x_ref, w_ref, o_ref, wb_scr):
    @pl.when(pl.program_id(1) == 0)
    def _():
        wb_scr[...] = w_ref[0].astype(BF16)

    o_ref[...] = _dot(x_ref[...], wb_scr[...], NT).astype(o_ref.dtype)


def _matmul_w32(x, w, layer, *, tm, tn, n_cols=None, transposed=False, name):
    m, k = x.shape
    n = n_cols if n_cols is not None else w.shape[1 if transposed else 2]
    blocks = _nbytes((tm, k), BF16) + _nbytes((k, tn), F32) + _nbytes((tm, tn), F32) + _nbytes((k, tn), BF16)
    if transposed:
        body, wblock, wspec = _mm_w32t_body, (tn, k), pl.BlockSpec((1, tn, k), lambda j, i: (layer, j, 0))
    else:
        body, wblock, wspec = _mm_w32_body, (k, tn), pl.BlockSpec((1, k, tn), lambda j, i: (layer, 0, j))
    return pl.pallas_call(
        body,
        grid=(n // tn, m // tm),
        in_specs=[pl.BlockSpec((tm, k), lambda j, i: (i, 0)), wspec],
        out_specs=pl.BlockSpec((tm, tn), lambda j, i: (i, j)),
        out_shape=jax.ShapeDtypeStruct((m, n), F32),
        scratch_shapes=[pltpu.VMEM(wblock, BF16)],
        compiler_params=_params(("arbitrary", "arbitrary"), blocks),
        name=name,
    )(x, w)


def _ln_body(x_ref, mix_ref, g_ref, b_ref, o_ref, ob_ref):
    z = ALPHA * x_ref[...] + mix_ref[...]
    mu = jnp.mean(z, -1, keepdims=True)
    zc = z - mu
    var = jnp.mean(zc * zc, -1, keepdims=True)
    y = zc * lax.rsqrt(var + LN_EPS) * g_ref[...] + b_ref[...]
    o_ref[...] = y
    ob_ref[...] = y.astype(BF16)


def _res_layernorm(x, mix, g, b, *, tm, name, row0=0, nrows=None):
    d = x.shape[1]
    nrows = x.shape[0] if nrows is None else nrows
    src = pl.BlockSpec((tm, d), lambda i: (row0 // tm + i, 0))
    dst = pl.BlockSpec((tm, d), lambda i: (i, 0))
    vec = pl.BlockSpec((1, d), lambda i: (0, 0))
    blocks = 3 * _nbytes((tm, d), F32) + _nbytes((tm, d), BF16)
    return pl.pallas_call(
        _ln_body,
        grid=(nrows // tm,),
        in_specs=[src, src, vec, vec],
        out_specs=[dst, dst],
        out_shape=[jax.ShapeDtypeStruct((nrows, d), F32), jax.ShapeDtypeStruct((nrows, d), BF16)],
        compiler_params=_params(("arbitrary",), blocks),
        name=name,
    )(x, mix, g.reshape(1, d), b.reshape(1, d))


def _hgrn_gates(q, f, lb):
    fg = lb + (1.0 - lb) * jax.nn.sigmoid(f)
    return _silu(q), 1.0 - fg, jnp.log(fg)


def _hgrn_out(o, g, nw):
    o = o * lax.rsqrt(jnp.mean(o * o, -1, keepdims=True) + RMS_EPS)
    return o * nw * _silu(g)


def _hgrn_prompt_body(q_ref, f_ref, v_ref, g_ref, lb_ref, nw_ref, o_ref, sfin_ref, st_scr):
    n = pl.program_id(1)
    c = HG_CHUNK

    @pl.when(n == 0)
    def _():
        st_scr[...] = jnp.zeros_like(st_scr)

    row = lax.broadcasted_iota(jnp.int32, (c, c), 0)
    col = lax.broadcasted_iota(jnp.int32, (c, c), 1)
    causal = row >= col
    tri = causal.astype(F32)
    q, k, logf = _hgrn_gates(q_ref[...], f_ref[...], lb_ref[...])
    v = v_ref[...].astype(BF16)
    cum = _dot(tri, logf, NN, precision=HIGHEST)
    mid = cum[c // 2 - 1:c // 2, :]
    last = cum[c - 1:c, :]
    qm = (q * jnp.exp(cum - mid)).astype(BF16)
    km = (k * jnp.exp(mid - cum)).astype(BF16)
    qi = (q * jnp.exp(cum)).astype(BF16)
    kl = (k * jnp.exp(last - cum)).astype(BF16)
    a_last = jnp.exp(last)
    gate = nw_ref[...] * _silu(g_ref[...])
    heads = [slice(h * HG_DK, (h + 1) * HG_DK) for h in range(HG_HEADS)]
    scores = [jnp.where(causal, _dot(qm[:, sl], km[:, sl], NT), 0.0).astype(BF16) for sl in heads]
    sts = [st_scr[h] for h in range(HG_HEADS)]
    outs = [_dot(qi[:, sl], sts[h].astype(BF16), NT) + _dot(scores[h], v[:, sl], NN) for h, sl in enumerate(heads)]
    for h, sl in enumerate(heads):
        st_scr[h] = a_last[:, sl] * sts[h] + _dot(v[:, sl], kl[:, sl], TN)
    for h, sl in enumerate(heads):
        o = outs[h]
        o = o * lax.rsqrt(jnp.mean(o * o, -1, keepdims=True) + RMS_EPS)
        o_ref[:, sl] = (o * gate[:, sl]).astype(o_ref.dtype)

    @pl.when(n == pl.num_programs(1) - 1)
    def _():
        sfin_ref[0] = st_scr[...]


def _hgrn_prompt(proj, lb, nw, *, batch, seq, prev=None):
    c = HG_CHUNK
    nchunk = seq // c
    blk = lambda j: pl.BlockSpec((c, HG_KW), lambda b, n: (b * nchunk + n, j))
    vec = pl.BlockSpec((1, HG_KW), lambda b, n: (0, 0))
    blocks = 4 * _nbytes((c, HG_KW), F32) + _nbytes((c, HG_VW), BF16) + 2 * _nbytes((HG_HEADS, HG_DV, HG_DK), F32)
    body, xspecs, xops, alias = _shared_output(_hgrn_prompt_body, 6, prev)
    o, st = pl.pallas_call(
        body,
        grid=(batch, nchunk),
        in_specs=[blk(0), blk(1), blk(2), blk(3), vec, vec] + xspecs,
        out_specs=[pl.BlockSpec((c, HG_VW), lambda b, n: (b * nchunk + n, 0)),
                   pl.BlockSpec((1, HG_HEADS, HG_DV, HG_DK), lambda b, n: (b, 0, 0, 0))],
        out_shape=[jax.ShapeDtypeStruct((proj.shape[0], MIX0_WIDTH), BF16),
                   jax.ShapeDtypeStruct((batch, HG_HEADS, HG_DV, HG_DK), F32)],
        scratch_shapes=[pltpu.VMEM((HG_HEADS, HG_DV, HG_DK), F32)],
        input_output_aliases=alias,
        compiler_params=_params(("arbitrary", "arbitrary"), blocks),
        name="hgrn_prompt",
    )(proj, proj, proj, proj, lb.reshape(1, HG_KW), nw.reshape(1, HG_VW), *xops)
    return o, jnp.swapaxes(st, -1, -2)


HG_SAMPLE_BB = 8


def _hgrn_sample_body(q_ref, f_ref, v_ref, g_ref, lb_ref, nw_ref, s_ref, o_ref, so_ref, *, seq):
    rows = 2 * seq
    assert rows == V7X_SUBLANES
    row = lax.broadcasted_iota(jnp.int32, (rows, rows), 0)
    col = lax.broadcasted_iota(jnp.int32, (rows, rows), 1)
    causal = (row >= col) & ((row // seq) == (col // seq))
    rvec = lax.broadcasted_iota(jnp.int32, (rows, 1), 0)
    r16 = lax.broadcasted_iota(jnp.int32, (2 * rows, HG_DV), 0)
    ones_rows = jnp.where((r16 == rows) | (r16 == rows + 1), 1.0, 0.0).astype(BF16)

    heads = [slice(h * HG_DK, (h + 1) * HG_DK) for h in range(HG_HEADS)]
    in_seq = rvec % seq

    def pair(p, carry):
        r0 = pl.multiple_of(p * rows, rows)
        tile = pl.ds(r0, rows)
        q, k, logf = _hgrn_gates(q_ref[tile, :], f_ref[tile, :], lb_ref[...])
        v = v_ref[tile, :].astype(BF16)
        cum = logf
        step = 1
        while step < seq:
            cum = cum + jnp.where(in_seq >= step, _shift_rows(cum, step), 0.0)
            step *= 2
        ecum = jnp.exp(cum)
        qi = (q * ecum).astype(BF16)
        km = (k / ecum).astype(BF16)
        gate = nw_ref[...] * _silu(g_ref[tile, :])
        v16 = jnp.concatenate([v, jnp.zeros_like(v)], axis=0)
        lhs = []
        for j in range(2):
            mine = (rvec // seq) == j
            last = cum[(j + 1) * seq - 1:(j + 1) * seq, :]
            a = jnp.exp(last)
            a_hi = a.astype(BF16).astype(F32)
            kl = jnp.where(mine, k * jnp.exp(last - cum), 0.0)
            lhs.append(jnp.concatenate([kl, a_hi, a - a_hi, jnp.zeros((rows - 2, HG_KW), F32)], axis=0).astype(BF16))
        scores = [jnp.where(causal, _dot(qi[:, sl], km[:, sl], NT), 0.0).astype(BF16) for sl in heads]
        outs = [_dot(scores[h], v[:, sl], NN) for h, sl in enumerate(heads)]
        for j in range(2):
            b = 2 * p + j
            mine = (rvec // seq) == j
            s0 = [s_ref[b, h] for h in range(HG_HEADS)]
            inter = [_dot(qi[:, sl], s0[h].astype(BF16), NN) for h, sl in enumerate(heads)]
            both = [_dot(lhs[j][:, sl], jnp.concatenate([v16[:, sl], ones_rows], axis=1), TN)
                    for sl in heads]
            for h in range(HG_HEADS):
                outs[h] = outs[h] + jnp.where(mine, inter[h], 0.0)
                so_ref[b, h] = both[h][:, HG_DV:] * s0[h] + both[h][:, :HG_DV]
        for h, sl in enumerate(heads):
            o = outs[h]
            o = o * lax.rsqrt(jnp.mean(o * o, -1, keepdims=True) + RMS_EPS)
            o_ref[tile, sl] = (o * gate[:, sl]).astype(o_ref.dtype)
        return carry

    lax.fori_loop(0, HG_SAMPLE_BB // 2, pair, 0)


def _hgrn_sample(proj, row0, state, lb, nw, *, batch, seq, prev=None):
    bb = HG_SAMPLE_BB
    tr = bb * seq
    blk0 = row0 // tr
    blk = lambda j: pl.BlockSpec((tr, HG_KW), lambda i: (blk0 + i, j))
    vec = pl.BlockSpec((1, HG_KW), lambda i: (0, 0))
    sblk = pl.BlockSpec((bb, HG_HEADS, HG_DK, HG_DV), lambda i: (i, 0, 0, 0))
    blocks = 4 * _nbytes((tr, HG_KW), F32) + 2 * _nbytes((bb, HG_HEADS, HG_DK, HG_DV), F32)
    body, xspecs, xops, alias = _shared_output(functools.partial(_hgrn_sample_body, seq=seq), 7, prev)
    return pl.pallas_call(
        body,
        grid=(batch // bb,),
        in_specs=[blk(0), blk(1), blk(2), blk(3), vec, vec, sblk] + xspecs,
        out_specs=[pl.BlockSpec((tr, HG_VW), lambda i: (blk0 + i, 0)), sblk],
        out_shape=[jax.ShapeDtypeStruct((proj.shape[0], MIX0_WIDTH), BF16),
                   jax.ShapeDtypeStruct(state.shape, F32)],
        input_output_aliases=alias,
        compiler_params=_params(("arbitrary",), blocks),
        name="hgrn_sample",
    )(proj, proj, proj, proj, lb.reshape(1, HG_KW), nw.reshape(1, HG_VW), state, *xops)


def _head_expand_matrix():
    r = jnp.arange(2 * V7X_LANES)[:, None] % V7X_LANES
    c = jnp.arange(M_DI)[None, :] // M_HEADDIM
    return (r == c).astype(BF16)


def _expand_heads(coef, e2):
    hi = coef.astype(BF16)
    lo = (coef - hi.astype(F32)).astype(BF16)
    return _dot(jnp.concatenate([hi, lo], axis=1), e2, NN)


def _softplus(x):
    return jnp.maximum(x, 0.0) + jnp.log(1.0 + jnp.exp(-jnp.abs(x)))


def _group_rmsnorm(y, width):
    outs = []
    for g in range(y.shape[1] // width):
        yg = y[:, g * width:(g + 1) * width]
        outs.append(yg * lax.rsqrt(jnp.mean(yg * yg, -1, keepdims=True) + RMS_EPS))
    return jnp.concatenate(outs, axis=1)


def _ssd_prompt_body(z_ref, x_ref, bc_ref, dt_ref, cwx_ref, cwb_ref, cbx_ref, cbb_ref, dtb_ref, alog_ref,
                     dexp_ref, nw_ref, e2_ref, y_ref, sfin_ref, conv_ref, s_scr, cx_scr, cb_scr):
    n = pl.program_id(1)
    c = M_CHUNK
    tail = V7X_SUBLANES

    @pl.when(n == 0)
    def _():
        s_scr[...] = jnp.zeros_like(s_scr)
        cx_scr[...] = jnp.zeros_like(cx_scr)
        cb_scr[...] = jnp.zeros_like(cb_scr)

    def conv(u, carry_scr, w_ref, b_ref):
        ext = jnp.concatenate([carry_scr[...], u], axis=0)
        acc = b_ref[...] + w_ref[M_CONV - 1:M_CONV, :] * u
        for j in range(1, M_CONV):
            acc = acc + w_ref[M_CONV - 1 - j:M_CONV - j, :] * pltpu.roll(ext, j, 0)[tail:, :]
        carry_scr[...] = u[c - tail:, :]
        return _silu(acc)

    ux = x_ref[...]
    ubc = bc_ref[...]
    xc = conv(ux, cx_scr, cwx_ref, cbx_ref)
    bcc = conv(ubc, cb_scr, cwb_ref, cbb_ref)
    ngn = M_GROUPS * M_STATE

    row = lax.broadcasted_iota(jnp.int32, (c, c), 0)
    col = lax.broadcasted_iota(jnp.int32, (c, c), 1)
    causal = row >= col
    tri = causal.astype(F32)
    e2 = e2_ref[...]

    dtp = _softplus(dt_ref[...] + dtb_ref[...])
    da = dtp * (-jnp.exp(alog_ref[...]))
    cum = _dot(tri, da, NN, precision=HIGHEST)
    cum_t = cum.T
    last = cum[c - 1:c, :]
    xdt = xc * _expand_heads(dtp, e2)
    xend = (xdt * _expand_heads(jnp.exp(last - cum), e2)).astype(BF16)
    ecum = _expand_heads(jnp.exp(cum), e2)
    xdt_b = xdt.astype(BF16)
    gw = M_HPG * M_HEADDIM

    ys = []
    for g in range(M_GROUPS):
        bg = bcc[:, g * M_STATE:(g + 1) * M_STATE].astype(BF16)
        cg = bcc[:, ngn + g * M_STATE:ngn + (g + 1) * M_STATE].astype(BF16)
        cb = _dot(cg, bg, NT)
        sg = s_scr[g * M_HPG:(g + 1) * M_HPG].reshape(gw, M_STATE)
        y_inter = _dot(cg, sg.astype(BF16), NT) * ecum[:, g * gw:(g + 1) * gw]
        upd = _dot(xend[:, g * gw:(g + 1) * gw], bg, TN)
        parts = []
        for hh in range(M_HPG):
            h = g * M_HPG + hh
            decay = jnp.exp(jnp.where(causal, cum[:, h:h + 1] - cum_t[h:h + 1, :], -jnp.inf))
            m = (cb * decay).astype(BF16)
            parts.append(_dot(m, xdt_b[:, h * M_HEADDIM:(h + 1) * M_HEADDIM], NN))
            s_scr[h] = jnp.exp(last[:, h:h + 1]) * s_scr[h] + upd[hh * M_HEADDIM:(hh + 1) * M_HEADDIM, :]
        ys.append(jnp.concatenate(parts, axis=1) + y_inter)
    y = jnp.concatenate(ys, axis=1) + dexp_ref[...] * xc
    y = y * _silu(z_ref[...])
    y_ref[...] = (_group_rmsnorm(y, gw) * nw_ref[...]).astype(y_ref.dtype)

    @pl.when(n == pl.num_programs(1) - 1)
    def _():
        sfin_ref[0] = s_scr[...]
        conv_ref[0, :, 0:M_DI] = ux[c - (M_CONV - 1):, :]
        conv_ref[0, :, M_DI:] = ubc[c - (M_CONV - 1):, :]


def _ssd_prompt(proj, dt, conv_w, conv_b, dt_bias, a_log, d_exp, norm_w, e2, *, batch, seq, prev=None):
    c = M_CHUNK
    nchunk = seq // c
    tok = lambda j: pl.BlockSpec((c, M_DI), lambda b, n: (b * nchunk + n, j))
    const = lambda shape, j=0: pl.BlockSpec(shape, lambda b, n: (0, j))
    blocks = (3 * _nbytes((c, M_DI), F32) + _nbytes((c, M_DI), BF16) + _nbytes((2 * V7X_LANES, M_DI), BF16)
              + 2 * _nbytes((M_HEADS, M_HEADDIM, M_STATE), F32) + 12 * _nbytes((c, M_DI), F32))
    body, xspecs, xops, alias = _shared_output(_ssd_prompt_body, 13, prev)
    return pl.pallas_call(
        body,
        grid=(batch, nchunk),
        in_specs=[tok(4), tok(5), tok(6),
                  pl.BlockSpec((c, V7X_LANES), lambda b, n: (b * nchunk + n, 0)),
                  const((M_CONV, M_DI), 0), const((M_CONV, M_DI), 1), const((1, M_DI), 0), const((1, M_DI), 1),
                  const((1, V7X_LANES)), const((1, V7X_LANES)), const((1, M_DI)), const((1, M_DI)),
                  const((2 * V7X_LANES, M_DI))] + xspecs,
        out_specs=[pl.BlockSpec((c, M_DI), lambda b, n: (b * nchunk + n, HG_VW // M_DI)),
                   pl.BlockSpec((1, M_HEADS, M_HEADDIM, M_STATE), lambda b, n: (b, 0, 0, 0)),
                   pl.BlockSpec((1, M_CONV - 1, M_CONV_DIM), lambda b, n: (b, 0, 0))],
        out_shape=[jax.ShapeDtypeStruct((proj.shape[0], MIX0_WIDTH), BF16),
                   jax.ShapeDtypeStruct((batch, M_HEADS, M_HEADDIM, M_STATE), F32),
                   jax.ShapeDtypeStruct((batch, M_CONV - 1, M_CONV_DIM), F32)],
        scratch_shapes=[pltpu.VMEM((M_HEADS, M_HEADDIM, M_STATE), F32),
                        pltpu.VMEM((V7X_SUBLANES, M_DI), F32), pltpu.VMEM((V7X_SUBLANES, M_DI), F32)],
        input_output_aliases=alias,
        compiler_params=_params(("arbitrary", "arbitrary"), blocks),
        name="ssd_prompt",
    )(proj, proj, proj, dt, conv_w, conv_w, conv_b, conv_b, dt_bias, a_log, d_exp, norm_w, e2, *xops)


SSD_SAMPLE_BB = 8


def _shift_rows(a, j):
    return a if j == 0 else pltpu.roll(a, j, 0)


def _ssd_sample_body(z_ref, x_ref, bc_ref, dt_ref, s_ref, cs_ref, cwx_ref, cwb_ref, cbx_ref, cbb_ref, dtb_ref,
                     alog_ref, dexp_ref, nw_ref, e2_ref, y_ref, so_ref, co_ref, *, seq):
    rows = V7X_SUBLANES
    hist = M_CONV - 1
    assert rows == 2 * seq and hist <= seq and hist <= rows - seq
    rvec = lax.broadcasted_iota(jnp.int32, (rows, 1), 0)
    valid = rvec < seq
    lane = lax.broadcasted_iota(jnp.int32, (1, V7X_LANES), 1)
    r16 = lax.broadcasted_iota(jnp.int32, (2 * rows, M_STATE), 0)
    ones_rows = jnp.where((r16 == rows) | (r16 == rows + 1), 1.0, 0.0).astype(BF16)
    e2 = e2_ref[...]
    a_neg = -jnp.exp(alog_ref[...])
    gw = M_HPG * M_HEADDIM
    ngn = M_GROUPS * M_STATE

    def conv(u8, buf, w_ref, b_ref):
        buf8 = jnp.concatenate([buf, jnp.zeros((rows - hist, buf.shape[1]), F32)], axis=0)
        ext = jnp.where(valid, u8, _shift_rows(buf8, rows - hist))
        acc = b_ref[...] + w_ref[hist:hist + 1, :] * ext
        for j in range(1, M_CONV):
            acc = acc + w_ref[hist - j:hist - j + 1, :] * _shift_rows(ext, j)
        new_hist = _shift_rows(ext, rows - (seq - hist))[0:hist, :]
        return _silu(acc), new_hist

    def one(b, u_x, u_bc, z8, dt8):
        xc, nhx = conv(u_x, cs_ref[b, :, 0:M_DI], cwx_ref, cbx_ref)
        bcc, nhb = conv(u_bc, cs_ref[b, :, M_DI:], cwb_ref, cbb_ref)
        co_ref[b, :, 0:M_DI] = nhx
        co_ref[b, :, M_DI:] = nhb
        dtp = jnp.where(valid, _softplus(dt8 + dtb_ref[...]), 0.0)
        cum = dtp * a_neg
        k = 1
        while k < seq:
            cum = cum + jnp.where(rvec >= k, _shift_rows(cum, k), 0.0)
            k *= 2
        last = cum[seq - 1:seq, :]
        bmat = bcc[:, :ngn]
        cmat = bcc[:, ngn:]
        coefs = [dtp, jnp.exp(cum), jnp.exp(last - cum), jnp.broadcast_to(jnp.exp(last), (rows, V7X_LANES))]
        for j in range(seq):
            prod = cmat * _shift_rows(bmat, j)
            cbh = jnp.zeros((rows, V7X_LANES), F32)
            for g in range(M_GROUPS):
                cbg = jnp.sum(prod[:, g * M_STATE:(g + 1) * M_STATE], axis=-1, keepdims=True)
                cbh = jnp.where((lane // M_HPG) == g, cbg, cbh)
            coefs.append(cbh * jnp.exp(cum - _shift_rows(cum, j)))
        wide = _expand_heads(jnp.concatenate(coefs, axis=0), e2)
        part = lambda n: wide[n * rows:(n + 1) * rows, :]
        xdt = xc * part(0)
        ecum = part(1)
        xs = xdt * part(2)
        dec = part(3)[0:1, :]
        y = dexp_ref[...] * xc
        for j in range(seq):
            y = y + part(4 + j) * _shift_rows(xdt, j)
        dec_hi = dec.astype(BF16).astype(F32)
        lhs = jnp.concatenate([xs, dec_hi, dec - dec_hi, jnp.zeros((rows - 2, M_DI), F32)], axis=0).astype(BF16)
        parts = []
        for g in range(M_GROUPS):
            bg = bmat[:, g * M_STATE:(g + 1) * M_STATE].astype(BF16)
            cg = cmat[:, g * M_STATE:(g + 1) * M_STATE].astype(BF16)
            s0 = s_ref[b, g * M_HPG:(g + 1) * M_HPG].reshape(gw, M_STATE)
            parts.append(_dot(cg, s0.astype(BF16), NT) * ecum[:, g * gw:(g + 1) * gw])
            rhs = jnp.concatenate([jnp.concatenate([bg, jnp.zeros((rows, M_STATE), BF16)], axis=0), ones_rows], axis=1)
            both = _dot(lhs[:, g * gw:(g + 1) * gw], rhs, TN)
            s_new = both[:, M_STATE:] * s0 + both[:, :M_STATE]
            so_ref[b, g * M_HPG:(g + 1) * M_HPG] = s_new.reshape(M_HPG, M_HEADDIM, M_STATE)
        y = (y + jnp.concatenate(parts, axis=1)) * _silu(z8)
        return _group_rmsnorm(y, gw) * nw_ref[...]

    def pair(p, carry):
        r0 = pl.multiple_of(p * rows, rows)
        tiles = [ref[pl.ds(r0, rows), :] for ref in (x_ref, bc_ref, z_ref, dt_ref)]
        ys = []
        for j in range(2):
            ys.append(one(2 * p + j, *[_shift_rows(t, j * (rows - seq)) for t in tiles]))
        y_ref[pl.ds(r0, rows), :] = jnp.where(valid, ys[0], _shift_rows(ys[1], seq)).astype(y_ref.dtype)
        return carry

    lax.fori_loop(0, SSD_SAMPLE_BB // 2, pair, 0)


def _ssd_sample(proj, row0, dt, state, conv_state, conv_w, conv_b, dt_bias, a_log, d_exp, norm_w, e2, *,
                batch, seq, prev=None):
    bb = SSD_SAMPLE_BB
    tr = bb * seq
    tok = lambda j: pl.BlockSpec((tr, M_DI), lambda i: (row0 // tr + i, j))
    const = lambda shape, j=0: pl.BlockSpec(shape, lambda i: (0, j))
    sblk = pl.BlockSpec((bb, M_HEADS, M_HEADDIM, M_STATE), lambda i: (i, 0, 0, 0))
    cblk = pl.BlockSpec((bb, M_CONV - 1, M_CONV_DIM), lambda i: (i, 0, 0))
    blocks = (3 * _nbytes((tr, M_DI), F32) + 2 * _nbytes((bb, M_HEADS, M_HEADDIM, M_STATE), F32)
              + 2 * _nbytes((bb, V7X_SUBLANES, M_CONV_DIM), F32) + _nbytes((2 * V7X_LANES, M_DI), BF16))
    body, xspecs, xops, alias = _shared_output(functools.partial(_ssd_sample_body, seq=seq), 15, prev)
    return pl.pallas_call(
        body,
        grid=(batch // bb,),
        in_specs=[tok(4), tok(5), tok(6),
                  pl.BlockSpec((tr, V7X_LANES), lambda i: (row0 // tr + i, 0)),
                  sblk, cblk,
                  const((M_CONV, M_DI), 0), const((M_CONV, M_DI), 1), const((1, M_DI), 0), const((1, M_DI), 1),
                  const((1, V7X_LANES)), const((1, V7X_LANES)), const((1, M_DI)), const((1, M_DI)),
                  const((2 * V7X_LANES, M_DI))] + xspecs,
        out_specs=[pl.BlockSpec((tr, M_DI), lambda i: (row0 // tr + i, HG_VW // M_DI)), sblk, cblk],
        out_shape=[jax.ShapeDtypeStruct((proj.shape[0], MIX0_WIDTH), BF16),
                   jax.ShapeDtypeStruct(state.shape, F32),
                   jax.ShapeDtypeStruct(conv_state.shape, F32)],
        input_output_aliases=alias,
        compiler_params=_params(("arbitrary",), blocks),
        name="ssd_sample",
    )(proj, proj, proj, dt, state, conv_state, conv_w, conv_w, conv_b, conv_b, dt_bias, a_log, d_exp, norm_w, e2, *xops)


S5_TILES = 8
S5_TILE_IN = D_MODEL // S5_TILES
S5_TILE_ST = S5_STATE // S5_TILES


def _s5_discretize(a_re, a_im, log_dt, b_re, b_im, c_re, c_im):
    lam_re = jnp.minimum(a_re, -1e-4)
    lam_im = a_im
    dt = jnp.exp(log_dt)[:, None]
    mag = jnp.exp(lam_re * dt)
    ab_re = mag * jnp.cos(lam_im * dt)
    ab_im = mag * jnp.sin(lam_im * dt)
    den = lam_re * lam_re + lam_im * lam_im
    zr = ((ab_re - 1.0) * lam_re + ab_im * lam_im) / den
    zi = (ab_im * lam_re - (ab_re - 1.0) * lam_im) / den
    bb_re = zr[..., None] * b_re - zi[..., None] * b_im
    bb_im = zr[..., None] * b_im + zi[..., None] * b_re
    gpt = S5_GROUPS // S5_TILES
    eye = jnp.eye(gpt, dtype=F32)

    def pack_b(bb):
        return jnp.einsum('kgph,gm->kghmp', bb.reshape(S5_TILES, gpt, S5_P, S5_GROUP), eye).reshape(
            S5_TILES, S5_TILE_IN, S5_TILE_ST).astype(BF16)

    def pack_c(c):
        return jnp.einsum('kghp,gm->kgpmh', c.reshape(S5_TILES, gpt, S5_GROUP, S5_P), eye).reshape(
            S5_TILES, S5_TILE_ST, S5_TILE_IN).astype(BF16)

    slab = lambda a: _s5_slab_layout(a.reshape(S5_STATE))
    return slab(ab_re), slab(ab_im), pack_b(bb_re), pack_b(bb_im), pack_c(c_re), pack_c(c_im)


def _s5_bu_body(x_ref, wr_ref, wi_ref, or_ref, oi_ref):
    x = x_ref[...]
    or_ref[...] = _dot(x, wr_ref[0], NN)
    oi_ref[...] = _dot(x, wi_ref[0], NN)


def _s5_bu(xb, wb_re, wb_im, *, tm):
    t = xb.shape[0]
    wspec = pl.BlockSpec((1, S5_TILE_IN, S5_TILE_ST), lambda k, i: (k, 0, 0))
    ospec = pl.BlockSpec((tm, S5_TILE_ST), lambda k, i: (i, k))
    blocks = _nbytes((tm, S5_TILE_IN), BF16) + 2 * _nbytes((S5_TILE_IN, S5_TILE_ST), BF16) + 2 * _nbytes((tm, S5_TILE_ST), F32)
    return pl.pallas_call(
        _s5_bu_body,
        grid=(S5_TILES, t // tm),
        in_specs=[pl.BlockSpec((tm, S5_TILE_IN), lambda k, i: (i, k)), wspec, wspec],
        out_specs=[ospec, ospec],
        out_shape=[jax.ShapeDtypeStruct((t, S5_STATE), F32)] * 2,
        compiler_params=_params(("arbitrary", "arbitrary"), blocks),
        name="s5_bu",
    )(xb, wb_re, wb_im)


def _s5_scan_body(br_ref, bi_ref, h0r_ref, h0i_ref, ar_ref, ai_ref, *rest, bb, tc):
    hr_ref, hi_ref, fr_ref, fi_ref, cr_scr, ci_scr = rest[-6:]
    n = pl.program_id(1)

    @pl.when(n == 0)
    def _():
        cr_scr[...] = h0r_ref[...]
        ci_scr[...] = h0i_ref[...]

    ar = ar_ref[...]
    ai = ai_ref[...]

    def seq_body(b, carry):
        def step(t, h):
            hr, hi = h
            r = b * tc + t
            nr = ar * hr - ai * hi + br_ref[r]
            ni = ar * hi + ai * hr + bi_ref[r]
            hr_ref[r] = nr
            hi_ref[r] = ni
            return nr, ni

        hr, hi = lax.fori_loop(0, tc, step, (cr_scr[b], ci_scr[b]))
        cr_scr[b] = hr
        ci_scr[b] = hi
        return carry

    lax.fori_loop(0, bb, seq_body, 0)

    @pl.when(n == pl.num_programs(1) - 1)
    def _():
        fr_ref[...] = cr_scr[...]
        fi_ref[...] = ci_scr[...]


def _s5_scan(bu_re, bu_im, h0_re, h0_im, ab_re, ab_im, *, row0, batch, seq, bb, tc, prev=None, name):
    t, sub, w = bu_re.shape
    assert bb == 1 or tc == seq
    rb = bb * tc
    blk = pl.BlockSpec((rb, sub, w), lambda i, n: ((row0 + i * bb * seq) // rb + n, 0, 0))
    sblk = pl.BlockSpec((bb, sub, w), lambda i, n: (i, 0, 0))
    ablk = pl.BlockSpec((sub, w), lambda i, n: (0, 0))
    blocks = 4 * _nbytes((rb, sub, w), F32) + 6 * _nbytes((bb, sub, w), F32)
    keep = [] if prev is None else list(prev)
    n_in = 6
    return pl.pallas_call(
        functools.partial(_s5_scan_body, bb=bb, tc=tc),
        grid=(batch // bb, seq // tc),
        in_specs=[blk, blk, sblk, sblk, ablk, ablk] + [pl.BlockSpec(memory_space=pl.ANY)] * len(keep),
        out_specs=[blk, blk, sblk, sblk],
        out_shape=[jax.ShapeDtypeStruct(bu_re.shape, F32)] * 2 + [jax.ShapeDtypeStruct(h0_re.shape, F32)] * 2,
        scratch_shapes=[pltpu.VMEM((bb, sub, w), F32)] * 2,
        input_output_aliases={n_in + k: k for k in range(len(keep))},
        compiler_params=_params(("arbitrary", "arbitrary"), blocks),
        name=name,
    )(bu_re, bu_im, h0_re, h0_im, ab_re, ab_im, *keep)


def _s5_out_body(hr_ref, hi_ref, x_ref, d_ref, wr_ref, wi_ref, o_ref):
    y = _dot(hr_ref[...].astype(BF16), wr_ref[0], NN) - _dot(hi_ref[...].astype(BF16), wi_ref[0], NN)
    o_ref[...] = jax.nn.gelu(y + d_ref[...] * x_ref[...]).astype(o_ref.dtype)


def _s5_out(h_re, h_im, x, d_skip, wc_re, wc_im, *, tm):
    t = x.shape[0]
    hspec = pl.BlockSpec((tm, S5_TILE_ST), lambda k, i: (i, k))
    wspec = pl.BlockSpec((1, S5_TILE_ST, S5_TILE_IN), lambda k, i: (k, 0, 0))
    xspec = pl.BlockSpec((tm, S5_TILE_IN), lambda k, i: (i, k))
    blocks = (2 * _nbytes((tm, S5_TILE_ST), F32) + 2 * _nbytes((S5_TILE_ST, S5_TILE_IN), BF16)
              + _nbytes((tm, S5_TILE_IN), F32) + _nbytes((tm, S5_TILE_IN), BF16))
    return pl.pallas_call(
        _s5_out_body,
        grid=(S5_TILES, t // tm),
        in_specs=[hspec, hspec, xspec, pl.BlockSpec((1, S5_TILE_IN), lambda k, i: (0, k)), wspec, wspec],
        out_specs=xspec,
        out_shape=jax.ShapeDtypeStruct((t, D_MODEL), BF16),
        compiler_params=_params(("arbitrary", "arbitrary"), blocks),
        name="s5_out",
    )(h_re, h_im, x, d_skip.reshape(1, D_MODEL), wc_re, wc_im)


S5_PROMPT_ROWS = 256
S5_SAMPLE_ROWS = 128
S5_NBLK = S5_TILE_ST // V7X_LANES


def _s5_pitch(rows):
    assert rows % V7X_SUBLANES == 0
    return rows + V7X_SUBLANES // 2


def _s5_slab_layout(a):
    lead = a.shape[:-1]
    return jnp.swapaxes(a.reshape(lead + (S5_TILES, S5_NBLK, V7X_LANES)), -3, -2)


def _s5_from_slab_layout(a):
    lead = a.shape[:-3]
    return jnp.swapaxes(a, -3, -2).reshape(lead + (S5_STATE,))


def _s5_fused_body(xb_ref, x_ref, h0r_ref, h0i_ref, ar_ref, ai_ref, wbr_ref, wbi_ref, wcr_ref, wci_ref, d_ref, *rest,
                   n_seq, steps):
    act_ref, fr_ref, fi_ref, bur, bui, cr, ci = rest[-7:]
    hrs, his = bur, bui
    n = pl.program_id(1)
    rows = n_seq * steps
    S5_PITCH = _s5_pitch(rows)
    for s in range(S5_TILES):
        u = xb_ref[:, s * S5_TILE_IN:(s + 1) * S5_TILE_IN]
        br = _dot(u, wbr_ref[s], NN)
        bi = _dot(u, wbi_ref[s], NN)
        for j in range(S5_NBLK):
            bur[j, s * S5_PITCH:s * S5_PITCH + rows, :] = br[:, j * V7X_LANES:(j + 1) * V7X_LANES]
            bui[j, s * S5_PITCH:s * S5_PITCH + rows, :] = bi[:, j * V7X_LANES:(j + 1) * V7X_LANES]

    @pl.when(n == 0)
    def _():
        cr[...] = h0r_ref[...]
        ci[...] = h0i_ref[...]

    ar = [ar_ref[j] for j in range(S5_NBLK)]
    ai = [ai_ref[j] for j in range(S5_NBLK)]

    def seq_body(q, carry):
        def step(t, h):
            hr, hi = h
            idx = pl.ds(q * steps + t, S5_TILES, stride=S5_PITCH)
            nr, ni = [], []
            for j in range(S5_NBLK):
                nr.append(ar[j] * hr[j] - ai[j] * hi[j] + bur[j, idx, :])
                ni.append(ar[j] * hi[j] + ai[j] * hr[j] + bui[j, idx, :])
                hrs[j, idx, :] = nr[j]
                his[j, idx, :] = ni[j]
            return tuple(nr), tuple(ni)

        h0 = (tuple(cr[q, j] for j in range(S5_NBLK)), tuple(ci[q, j] for j in range(S5_NBLK)))
        hr, hi = lax.fori_loop(0, steps, step, h0, unroll=min(steps, 4))
        for j in range(S5_NBLK):
            cr[q, j] = hr[j]
            ci[q, j] = hi[j]
        return carry

    lax.fori_loop(0, n_seq, seq_body, 0)

    for s in range(S5_TILES):
        seg = slice(s * S5_PITCH, s * S5_PITCH + rows)
        hre = jnp.concatenate([hrs[j, seg, :] for j in range(S5_NBLK)], axis=1).astype(BF16)
        him = jnp.concatenate([his[j, seg, :] for j in range(S5_NBLK)], axis=1).astype(BF16)
        y = _dot(hre, wcr_ref[s], NN) - _dot(him, wci_ref[s], NN)
        cs = slice(s * S5_TILE_IN, (s + 1) * S5_TILE_IN)
        act_ref[:, cs] = jax.nn.gelu(y + d_ref[:, cs] * x_ref[:, cs]).astype(act_ref.dtype)

    @pl.when(n == pl.num_programs(1) - 1)
    def _():
        fr_ref[...] = cr[...]
        fi_ref[...] = ci[...]


def _s5_fused(xb, x, h0_re, h0_im, ab_re, ab_im, wb_re, wb_im, wc_re, wc_im, d_skip, *, row0, batch, seq, rows, n_seq,
              prev=None, name):
    t, d = x.shape
    steps = rows // n_seq
    assert n_seq == 1 or steps == seq
    S5_PITCH = _s5_pitch(rows)
    tok = lambda width: pl.BlockSpec((rows, width), lambda i, n: ((row0 + i * n_seq * seq) // rows + n, 0))
    st = pl.BlockSpec((n_seq, S5_NBLK, S5_TILES, V7X_LANES), lambda i, n: (i, 0, 0, 0))
    const = lambda shape: pl.BlockSpec(shape, lambda i, n: (0,) * len(shape), pipeline_mode=pl.Buffered(1))
    slab = pltpu.VMEM((S5_NBLK, S5_TILES * S5_PITCH, V7X_LANES), F32)
    stv = pltpu.VMEM((n_seq, S5_NBLK, S5_TILES, V7X_LANES), F32)
    blocks = (2 * (_nbytes((rows, d), BF16) * 2 + _nbytes((rows, d), F32))
              + 2 * 6 * _nbytes((n_seq, S5_STATE), F32) + 4 * _nbytes(wb_re.shape, BF16)
              + 2 * _nbytes((S5_NBLK, S5_TILES * S5_PITCH, V7X_LANES), F32))
    keep = [] if prev is None else [prev]
    n_in = 11
    return pl.pallas_call(
        functools.partial(_s5_fused_body, n_seq=n_seq, steps=steps),
        grid=(batch // n_seq, seq // steps),
        in_specs=[tok(d), tok(d), st, st, const(ab_re.shape), const(ab_im.shape), const(wb_re.shape),
                  const(wb_im.shape), const(wc_re.shape), const(wc_im.shape), const((1, d))]
                 + [pl.BlockSpec(memory_space=pl.ANY)] * len(keep),
        out_specs=[tok(d), st, st],
        out_shape=[jax.ShapeDtypeStruct((t, d), BF16), jax.ShapeDtypeStruct(h0_re.shape, F32),
                   jax.ShapeDtypeStruct(h0_im.shape, F32)],
        scratch_shapes=[slab, slab, stv, stv],
        input_output_aliases={n_in + k: k for k in range(len(keep))},
        compiler_params=pltpu.CompilerParams(dimension_semantics=("arbitrary", "arbitrary"),
                                             vmem_limit_bytes=int(min(blocks + COMPILER_SCRATCH_BYTES, V7X_SCOPED_VMEM_CAP))),
        name=name,
    )(xb, x, h0_re, h0_im, ab_re, ab_im, wb_re, wb_im, wc_re, wc_im, d_skip.reshape(1, d), *keep)


def _glu_body(a_ref, wa_ref, wb_ref, ba_ref, bb_ref, o_ref, wa_scr, wb_scr):
    @pl.when(pl.program_id(1) == 0)
    def _():
        wa_scr[...] = wa_ref[0].astype(BF16)
        wb_scr[...] = wb_ref[0].astype(BF16)

    a = a_ref[...]
    o_ref[...] = (_dot(a, wa_scr[...], NN) + ba_ref[...]) * jax.nn.sigmoid(_dot(a, wb_scr[...], NN) + bb_ref[...])


def _glu(a, wa, wb, ba, bb, layer, *, tm, tn):
    t, k = a.shape
    n = wa.shape[2]
    wspec = pl.BlockSpec((1, k, tn), lambda j, i: (layer, 0, j))
    bspec = pl.BlockSpec((1, tn), lambda j, i: (layer, j))
    blocks = _nbytes((tm, k), BF16) + 2 * _nbytes((k, tn), F32) + _nbytes((tm, tn), F32) + 2 * _nbytes((k, tn), BF16)
    return pl.pallas_call(
        _glu_body,
        grid=(n // tn, t // tm),
        in_specs=[pl.BlockSpec((tm, k), lambda j, i: (i, 0)), wspec, wspec, bspec, bspec],
        out_specs=pl.BlockSpec((tm, tn), lambda j, i: (i, j)),
        out_shape=jax.ShapeDtypeStruct((t, n), F32),
        scratch_shapes=[pltpu.VMEM((k, tn), BF16)] * 2,
        compiler_params=_params(("arbitrary", "arbitrary"), blocks),
        name="glu",
    )(a, wa, wb, ba, bb)


def _router_body(x_ref, w_ref, b_ref, e_ref, g_ref):
    _route(x_ref[...], w_ref, b_ref, e_ref, g_ref)


def _route(x, w_ref, b_ref, e_ref, g_ref):
    logits = _dot(w_ref[...], x, NT, precision=HIGHEST) + b_ref[...]
    rows = [logits[e:e + 1, :] for e in range(N_EXPERTS)]
    m = functools.reduce(jnp.maximum, rows)
    ex = [jnp.exp(r - m) for r in rows]
    z = functools.reduce(jnp.add, ex)
    p = [v / z for v in ex]

    def top2_sum(a, b, c, d):
        hi1, lo1, hi2, lo2 = jnp.maximum(a, b), jnp.minimum(a, b), jnp.maximum(c, d), jnp.minimum(c, d)
        return jnp.maximum(hi1, hi2) + jnp.maximum(jnp.minimum(hi1, hi2), jnp.maximum(lo1, lo2))

    assert EXP_PER_GROUP == 4
    best = top2_sum(*p[0:EXP_PER_GROUP])
    gsel = jnp.zeros_like(best, dtype=jnp.int32)
    for g in range(1, N_EXP_GROUPS):
        s = top2_sum(*p[g * EXP_PER_GROUP:(g + 1) * EXP_PER_GROUP])
        better = s > best
        gsel = jnp.where(better, g, gsel)
        best = jnp.where(better, s, best)
    inner = []
    for i in range(EXP_PER_GROUP):
        v = p[i]
        for g in range(1, N_EXP_GROUPS):
            v = jnp.where(gsel == g, p[g * EXP_PER_GROUP + i], v)
        inner.append(v)

    def first_argmax(vals, skip=None):
        bv = bi = None
        for i, v in enumerate(vals):
            v = v if skip is None else jnp.where(skip == i, -jnp.inf, v)
            if bv is None:
                bv, bi = v, jnp.zeros_like(gsel)
            else:
                better = v > bv
                bi = jnp.where(better, i, bi)
                bv = jnp.where(better, v, bv)
        return bv, bi

    p1, i1 = first_argmax(inner)
    p2, i2 = first_argmax(inner, skip=i1)
    tot = p1 + p2
    e1 = gsel * EXP_PER_GROUP + i1
    e2 = gsel * EXP_PER_GROUP + i2
    e_ref[0:1, :] = e1
    e_ref[1:2, :] = e2
    g_ref[0:1, :] = p1 / tot
    g_ref[1:2, :] = p2 / tot


def _ln_router_body(x_ref, x2_ref, mix_ref, g_ref, b_ref, w_ref, br_ref, o_ref, e_ref, gw_ref, *, first_blocks):
    x = x_ref[...] if first_blocks is None else jnp.where(pl.program_id(0) < first_blocks, x_ref[...], x2_ref[...])
    z = ALPHA * x + mix_ref[...]
    mu = jnp.mean(z, -1, keepdims=True)
    zc = z - mu
    var = jnp.mean(zc * zc, -1, keepdims=True)
    y = zc * lax.rsqrt(var + LN_EPS) * g_ref[...] + b_ref[...]
    o_ref[...] = y
    _route(y, w_ref, br_ref, e_ref, gw_ref)


def _res_layernorm_router(x, mix, g, b, w_router, b_router, *, tm, name, x_tail=None):
    t, d = mix.shape
    row = pl.BlockSpec((tm, d), lambda i: (i, 0))
    vec = pl.BlockSpec((1, d), lambda i: (0, 0))
    sel = pl.BlockSpec((TOP_K, tm), lambda i: (0, i))
    if x_tail is None:
        nb, xspecs, xs = None, [row, vec], [x, g.reshape(1, d)]
    else:
        nb = x.shape[0] // tm
        xspecs = [pl.BlockSpec((tm, d), lambda i: (jnp.minimum(i, nb - 1), 0)),
                  pl.BlockSpec((tm, d), lambda i: (jnp.maximum(i - nb, 0), 0))]
        xs = [x, x_tail]
    blocks = 4 * _nbytes((tm, d), F32) + _nbytes((N_EXPERTS, d), F32) + 2 * _nbytes((V7X_SUBLANES, tm), F32)
    return pl.pallas_call(
        functools.partial(_ln_router_body, first_blocks=nb),
        grid=(t // tm,),
        in_specs=xspecs + [row, vec, vec, pl.BlockSpec((N_EXPERTS, d), lambda i: (0, 0)),
                           pl.BlockSpec((N_EXPERTS, 1), lambda i: (0, 0))],
        out_specs=[row, sel, sel],
        out_shape=[jax.ShapeDtypeStruct((t, d), F32), jax.ShapeDtypeStruct((TOP_K, t), jnp.int32),
                   jax.ShapeDtypeStruct((TOP_K, t), F32)],
        compiler_params=_params(("arbitrary",), blocks),
        name=name,
    )(*xs, mix, g.reshape(1, d), b.reshape(1, d), w_router.T, b_router.reshape(N_EXPERTS, 1))


def _router(x, w_router, b_router, *, tm):
    t, d = x.shape
    blocks = _nbytes((tm, d), F32) + _nbytes((N_EXPERTS, d), F32) + 2 * _nbytes((V7X_SUBLANES, tm), F32)
    out = pl.BlockSpec((TOP_K, tm), lambda i: (0, i))
    return pl.pallas_call(
        _router_body,
        grid=(t // tm,),
        in_specs=[pl.BlockSpec((tm, d), lambda i: (i, 0)), pl.BlockSpec((N_EXPERTS, d), lambda i: (0, 0)),
                  pl.BlockSpec((N_EXPERTS, 1), lambda i: (0, 0))],
        out_specs=[out, out],
        out_shape=[jax.ShapeDtypeStruct((TOP_K, t), jnp.int32), jax.ShapeDtypeStruct((TOP_K, t), F32)],
        compiler_params=_params(("arbitrary",), blocks),
        name="router",
    )(x, w_router.T, b_router.reshape(N_EXPERTS, 1))


MOE_TILE = 256
COMBINE_TILE = 256


def _moe_plan(eidx):
    k, t = eidx.shape
    ns = k * t
    n_tiles = ns // MOE_TILE + N_EXPERTS
    e_flat = eidx.reshape(ns)
    order = jnp.argsort(e_flat, stable=True).astype(jnp.int32)
    rank = jnp.argsort(order).astype(jnp.int32)
    experts = jnp.arange(N_EXPERTS, dtype=jnp.int32)[None, :]
    slot_is = e_flat[:, None] == experts
    counts = jnp.sum(slot_is.astype(jnp.int32), axis=0)
    start = jnp.cumsum(counts) - counts
    tiles = (counts + MOE_TILE - 1) // MOE_TILE
    tile_end = jnp.cumsum(tiles)
    tile_start = tile_end - tiles
    tile_ids = jnp.arange(n_tiles, dtype=jnp.int32)
    tile_expert = jnp.minimum(jnp.sum((tile_ids[:, None] >= tile_end[None, :]).astype(jnp.int32), axis=1),
                              N_EXPERTS - 1)
    pick = lambda onehot, table: jnp.sum(jnp.where(onehot, table[None, :], 0), axis=1)
    tile_is = tile_expert[:, None] == experts
    tile_first = pick(tile_is, start) + (tile_ids - pick(tile_is, tile_start)) * MOE_TILE
    position_of_slot = pick(slot_is, tile_start) * MOE_TILE + rank - pick(slot_is, start)
    sorted_token = jnp.pad(order % t, (0, MOE_TILE))
    return (tile_expert.astype(jnp.int32), tile_first.astype(jnp.int32), tile_end[-1:].astype(jnp.int32),
            sorted_token.astype(jnp.int32), position_of_slot.astype(jnp.int32))


def _moe_expert_body(te_ref, first_ref, nt_ref, tok_ref, x_hbm, wg_ref, wu_ref, wd_ref, y_ref, xbuf, sem, wg_scr,
                     wu_scr, wd_scr):
    i = pl.program_id(0)
    nt = nt_ref[0]
    tm = MOE_TILE
    sub = V7X_SUBLANES

    def gather_copy(tok, slot, r8, j):
        return pltpu.make_async_copy(x_hbm.at[pl.ds(tok, 1)], xbuf.at[slot, r8, pl.ds(j, 1)], sem.at[slot])

    def start_gather(tile, slot):
        first = first_ref[tile]

        def body(r8, c):
            for j in range(sub):
                gather_copy(tok_ref[first + r8 * sub + j], slot, r8, j).start()
            return c

        lax.fori_loop(0, tm // sub, body, 0)

    @pl.when(i == 0)
    def _():
        start_gather(0, 0)

    @pl.when(i + 1 < nt)
    def _():
        start_gather(i + 1, (i + 1) % 2)

    @pl.when(i < nt)
    def _():
        slot = i % 2
        for r8 in range(tm // sub):
            pltpu.make_async_copy(x_hbm.at[pl.ds(0, sub)], xbuf.at[slot, r8], sem.at[slot]).wait()

        @pl.when((i == 0) | (te_ref[i] != te_ref[jnp.maximum(i - 1, 0)]))
        def _():
            wg_scr[...] = wg_ref[0, 0].astype(BF16)
            wu_scr[...] = wu_ref[0, 0].astype(BF16)
            wd_scr[...] = wd_ref[0, 0].astype(BF16)

        x = xbuf[slot].reshape(tm, x_hbm.shape[1]).astype(BF16)
        act = (_silu(_dot(x, wg_scr[...], NN)) * _dot(x, wu_scr[...], NN)).astype(BF16)
        y_ref[...] = _dot(act, wd_scr[...], NN)

    @pl.when(i >= nt)
    def _():
        y_ref[...] = jnp.zeros_like(y_ref)


def _moe_experts(x, plan, w_gate, w_up, w_down, layer):
    tile_expert, tile_first, n_valid, sorted_token, _ = plan
    t, d = x.shape
    f = w_gate.shape[3]
    n_tiles = tile_expert.shape[0]
    wspec = lambda a, b: pl.BlockSpec((1, 1, a, b), lambda i, te, *_: (layer, te[i], 0, 0))
    blocks = (2 * 3 * _nbytes((d, f), F32) + 3 * _nbytes((d, f), BF16) + 2 * _nbytes((MOE_TILE, d), F32)
              + 2 * _nbytes((MOE_TILE, d), F32) + 4 * _nbytes((MOE_TILE, f), F32))
    return pl.pallas_call(
        _moe_expert_body,
        grid_spec=pltpu.PrefetchScalarGridSpec(
            num_scalar_prefetch=4,
            grid=(n_tiles,),
            in_specs=[pl.BlockSpec(memory_space=pl.ANY), wspec(d, f), wspec(d, f), wspec(f, d)],
            out_specs=pl.BlockSpec((MOE_TILE, d), lambda i, *_: (i, 0)),
            scratch_shapes=[pltpu.VMEM((2, MOE_TILE // V7X_SUBLANES, V7X_SUBLANES, d), F32),
                            pltpu.SemaphoreType.DMA((2,)),
                            pltpu.VMEM((d, f), BF16), pltpu.VMEM((d, f), BF16), pltpu.VMEM((f, d), BF16)]),
        out_shape=jax.ShapeDtypeStruct((n_tiles * MOE_TILE, d), F32),
        compiler_params=pltpu.CompilerParams(dimension_semantics=("arbitrary",),
                                             vmem_limit_bytes=int(min(blocks + COMPILER_SCRATCH_BYTES, V7X_SCOPED_VMEM_CAP))),
        name="moe_experts",
    )(tile_expert, tile_first, n_valid, sorted_token, x, w_gate, w_up, w_down)


def _moe_combine_body(pos_ref, x_ref, w_ref, g_ref, b_ref, y_hbm, o_ref, ob_ref, ybuf, sem, *, n_tok, split_blocks):
    i = pl.program_id(0)
    tc = COMBINE_TILE

    sub = V7X_SUBLANES

    def gather_copy(p, slot, k, r8, j):
        return pltpu.make_async_copy(y_hbm.at[pl.ds(p, 1)], ybuf.at[slot, k, r8, pl.ds(j, 1)], sem.at[slot])

    def start_gather(tile, slot):
        def body(r8, c):
            for k in range(TOP_K):
                for j in range(sub):
                    gather_copy(pos_ref[k * n_tok + tile * tc + r8 * sub + j], slot, k, r8, j).start()
            return c

        lax.fori_loop(0, tc // sub, body, 0)

    @pl.when(i == 0)
    def _():
        start_gather(0, 0)

    @pl.when(i + 1 < pl.num_programs(0))
    def _():
        start_gather(i + 1, (i + 1) % 2)

    slot = i % 2
    for k in range(TOP_K):
        for r8 in range(tc // sub):
            pltpu.make_async_copy(y_hbm.at[pl.ds(0, sub)], ybuf.at[slot, k, r8], sem.at[slot]).wait()
    z = ALPHA * x_ref[...]
    for k in range(TOP_K):
        z = z + w_ref[:, k:k + 1] * ybuf[slot, k].reshape(tc, x_ref.shape[1])
    mu = jnp.mean(z, -1, keepdims=True)
    zc = z - mu
    var = jnp.mean(zc * zc, -1, keepdims=True)
    y = zc * lax.rsqrt(var + LN_EPS) * g_ref[...] + b_ref[...]
    if split_blocks is None:
        o_ref[...] = y
        ob_ref[...] = y.astype(BF16)
    else:
        @pl.when(i < split_blocks)
        def _():
            o_ref[...] = y

        @pl.when(i >= split_blocks)
        def _():
            ob_ref[...] = y


def _moe_combine_layernorm(x, y_slots, plan, gate_w, g, b, *, name, split=None):
    position_of_slot = plan[4]
    t, d = x.shape
    tc = COMBINE_TILE
    row = pl.BlockSpec((tc, d), lambda i, pos: (i, 0))
    vec = pl.BlockSpec((1, d), lambda i, pos: (0, 0))
    blocks = 2 * (3 * _nbytes((tc, d), F32) + _nbytes((tc, d), BF16)) + 2 * TOP_K * _nbytes((tc, d), F32)
    if split is None:
        sb = None
        out_specs = [row, row]
        out_shape = [jax.ShapeDtypeStruct((t, d), F32), jax.ShapeDtypeStruct((t, d), BF16)]
    else:
        sb = split // tc
        out_specs = [pl.BlockSpec((tc, d), lambda i, pos: (jnp.minimum(i, sb - 1), 0)),
                     pl.BlockSpec((tc, d), lambda i, pos: (jnp.maximum(i - sb, 0), 0))]
        out_shape = [jax.ShapeDtypeStruct((split, d), F32), jax.ShapeDtypeStruct((t - split, d), F32)]
    return pl.pallas_call(
        functools.partial(_moe_combine_body, n_tok=t, split_blocks=sb),
        grid_spec=pltpu.PrefetchScalarGridSpec(
            num_scalar_prefetch=1,
            grid=(t // tc,),
            in_specs=[row, pl.BlockSpec((tc, TOP_K), lambda i, pos: (i, 0)), vec, vec,
                      pl.BlockSpec(memory_space=pl.ANY)],
            out_specs=out_specs,
            scratch_shapes=[pltpu.VMEM((2, TOP_K, tc // V7X_SUBLANES, V7X_SUBLANES, d), F32),
                            pltpu.SemaphoreType.DMA((2,))]),
        out_shape=out_shape,
        compiler_params=pltpu.CompilerParams(dimension_semantics=("arbitrary",),
                                             vmem_limit_bytes=int(min(blocks + COMPILER_SCRATCH_BYTES, V7X_SCOPED_VMEM_CAP))),
        name=name,
    )(position_of_slot, x, gate_w, g.reshape(1, d), b.reshape(1, d), y_slots)


def _moe_dense_body(x_ref, g_ref, wg_ref, wu_ref, wd_ref, o_ref):
    e = pl.program_id(1)

    @pl.when(e == 0)
    def _():
        o_ref[...] = jnp.zeros_like(o_ref)

    x = x_ref[...]
    h = _dot(x, wg_ref[0], NN)
    u = _dot(x, wu_ref[0], NN)
    lane = lax.broadcasted_iota(jnp.int32, (1, N_EXPERTS), 1)
    gate = jnp.sum(jnp.where(lane == e, g_ref[...], 0.0), axis=1, keepdims=True)
    act = (_silu(h) * u * gate).astype(BF16)
    o_ref[...] += _dot(act, wd_ref[0], NN)


def _moe_dense(xb, gates, wg, wu, wd, *, tm):
    t, d = xb.shape
    f = wg.shape[2]
    blocks = (_nbytes((tm, d), BF16) + _nbytes((tm, V7X_LANES), F32) + 3 * _nbytes((d, f), BF16)
              + _nbytes((tm, d), F32) + 4 * _nbytes((tm, f), F32))
    return pl.pallas_call(
        _moe_dense_body,
        grid=(t // tm, N_EXPERTS),
        in_specs=[pl.BlockSpec((tm, d), lambda i, e: (i, 0)), pl.BlockSpec((tm, N_EXPERTS), lambda i, e: (i, 0)),
                  pl.BlockSpec((1, d, f), lambda i, e: (e, 0, 0)), pl.BlockSpec((1, d, f), lambda i, e: (e, 0, 0)),
                  pl.BlockSpec((1, f, d), lambda i, e: (e, 0, 0))],
        out_specs=pl.BlockSpec((tm, d), lambda i, e: (i, 0)),
        out_shape=jax.ShapeDtypeStruct((t, d), F32),
        compiler_params=_params(("arbitrary", "arbitrary"), blocks),
        name="moe_dense",
    )(xb, gates, wg, wu, wd)


TOKEN_TILE = 1088
IN_PROJ_TM = 544
IN_PROJ_TN = 1792
ROUTER_TILE = 512
LN_TILE = 256


def _moe_layernorm(x, eidx, gate_w, layer, w_gate, w_up, w_down, g, b, split=None):
    plan = _moe_plan(eidx)
    y_slots = _moe_experts(x, plan, w_gate, w_up, w_down, layer)
    return _moe_combine_layernorm(x, y_slots, plan, gate_w.T, g, b, name=f"moe_combine_ln_{layer}", split=split)


def kernel(x_prompt, x_sample, state_hgrn, state_ssm, state_conv, state_s5_re, state_s5_im, w_in0, hg_lb_logits, hg_norm_w, conv_w, conv_b, dt_bias, a_log, m_d, m_norm_w, w_out0, s5_a_re, s5_a_im, s5_log_dt, s5_b_re, s5_b_im, s5_c_re, s5_c_im, s5_d, glu_w_a, glu_b_a, glu_w_b, glu_b_b, w_router, b_router, w_gate, w_up, w_down, ln1_g, ln1_b, ln2_g, ln2_b):
    bp, lp, d = x_prompt.shape
    bs, ls, _ = x_sample.shape
    tp, ts = bp * lp, bs * ls
    tm = TOKEN_TILE
    xp2, xs2 = x_prompt.reshape(tp, d), x_sample.reshape(ts, d)
    x0b = jnp.concatenate([xp2.astype(BF16), xs2.astype(BF16)], axis=0)
    lower_bounds = jnp.cumsum(jax.nn.softmax(hg_lb_logits.astype(F32), axis=0), axis=0)
    pad_lanes = lambda v: jnp.pad(v, (0, V7X_LANES - v.shape[0])).reshape(1, V7X_LANES)

    proj = _matmul_w32(x0b, jnp.swapaxes(w_in0, 1, 2), 0, tm=IN_PROJ_TM, tn=IN_PROJ_TN, n_cols=IN0_MAIN,
                       transposed=True, name="in_proj")
    w_dt = jnp.pad(w_in0[0, :, IN0_MAIN:], ((0, 0), (0, V7X_LANES - M_HEADS))).astype(BF16)
    dt = _matmul(x0b, w_dt, tm=tm, tn=V7X_LANES, name="dt_proj")
    lb0 = lower_bounds[0]
    e2 = _head_expand_matrix()
    ssd_consts = (conv_w[0], conv_b[0].reshape(1, -1), pad_lanes(dt_bias[0]), pad_lanes(a_log[0]),
                  jnp.repeat(m_d[0], M_HEADDIM).reshape(1, M_DI), m_norm_w[0].reshape(1, M_DI), e2)
    mixed, hg_p = _hgrn_prompt(proj, lb0, hg_norm_w[0], batch=bp, seq=lp)
    mixed, hg_s = _hgrn_sample(proj, tp, state_hgrn[0], lb0, hg_norm_w[0], batch=bs, seq=ls, prev=mixed)
    mixed, ssm_p, conv_p = _ssd_prompt(proj, dt, *ssd_consts, batch=bp, seq=lp, prev=mixed)
    mixed, ssm_s, conv_s = _ssd_sample(proj, tp, dt, state_ssm[0], state_conv[0], *ssd_consts, batch=bs, seq=ls,
                                       prev=mixed)
    mix = _matmul_w32(mixed, w_out0, 0, tm=tm, tn=512, name="out_proj")
    x1, eidx, gate_w = _res_layernorm_router(xp2, mix, ln1_g[0], ln1_b[0], w_router, b_router, tm=LN_TILE,
                                             name="ln1_router_0", x_tail=xs2)
    x2, x2b = _moe_layernorm(x1, eidx, gate_w, 0, w_gate, w_up, w_down, ln2_g[0], ln2_b[0])

    ab_re, ab_im, wb_re, wb_im, wc_re, wc_im = _s5_discretize(
        s5_a_re[0], s5_a_im[0], s5_log_dt[0], s5_b_re[0], s5_b_im[0], s5_c_re[0], s5_c_im[0])
    s5_consts = (ab_re, ab_im, wb_re, wb_im, wc_re, wc_im, s5_d[0])
    zeros = jnp.zeros((bp, S5_NBLK, S5_TILES, V7X_LANES), F32)
    act, s5r_p, s5i_p = _s5_fused(x2b, x2, zeros, zeros, *s5_consts, row0=0, batch=bp, seq=lp,
                                  rows=S5_PROMPT_ROWS, n_seq=1, name="s5_prompt")
    act, s5r_s, s5i_s = _s5_fused(x2b, x2, _s5_slab_layout(state_s5_re[0].reshape(bs, S5_STATE)),
                                  _s5_slab_layout(state_s5_im[0].reshape(bs, S5_STATE)), *s5_consts,
                                  row0=tp, batch=bs, seq=ls, rows=S5_SAMPLE_ROWS, n_seq=S5_SAMPLE_ROWS // ls,
                                  prev=act, name="s5_sample")
    mix1 = _glu(act, glu_w_a, glu_w_b, glu_b_a, glu_b_b, 0, tm=tm, tn=512)
    x3, eidx, gate_w = _res_layernorm_router(x2, mix1, ln1_g[1], ln1_b[1], w_router, b_router, tm=LN_TILE,
                                             name="ln1_router_1")
    y_p, y_s = _moe_layernorm(x3, eidx, gate_w, 1, w_gate, w_up, w_down, ln2_g[1], ln2_b[1], split=tp)

    s5_state = lambda a, b: _s5_from_slab_layout(a).reshape(1, b, S5_GROUPS, S5_P)
    return (y_p.reshape(bp, lp, d), y_s.reshape(bs, ls, d),
            hg_p[None], hg_s[None], ssm_p[None], ssm_s[None], conv_p[None], conv_s[None],
            s5_state(s5r_p, bp), s5_state(s5r_s, bs), s5_state(s5i_p, bp), s5_state(s5i_s, bs))
```

```python
import functools
import math

import jax
import jax.numpy as jnp
from jax import lax
from jax.experimental import pallas as pl
from jax.experimental.pallas import tpu as pltpu

F32 = jnp.float32
BF16 = jnp.bfloat16
HIGHEST = lax.Precision.HIGHEST

D_MODEL = 2048
DEPTH = 2
HG_HEADS = 16
HG_DK = 128
HG_DV = 128
HG_KW = HG_HEADS * HG_DK
HG_VW = HG_HEADS * HG_DV
HG_CHUNK = 64
M_DI = 2048
M_HEADDIM = 64
M_HEADS = 32
M_GROUPS = 8
M_HPG = 4
M_STATE = 128
M_CONV = 4
M_CONV_DIM = M_DI + 2 * M_GROUPS * M_STATE
M_CHUNK = 128
IN0_MAIN = 2 * HG_KW + 2 * HG_VW + M_DI + M_CONV_DIM
S5_GROUP = 16
S5_GROUPS = 128
S5_P = 64
S5_STATE = S5_GROUPS * S5_P
N_EXPERTS = 16
N_EXP_GROUPS = 4
EXP_PER_GROUP = 4
TOP_K = 2
D_EXPERT = 512
ALPHA = (2 * DEPTH) ** 0.25
LN_EPS = 1e-5
RMS_EPS = 1e-6

V7X_LANES = 128
V7X_SUBLANES = 8
V7X_VMEM_BYTES = 64 * 1024 * 1024
V7X_SCOPED_VMEM_CAP = 60000 * 1024
COMPILER_SCRATCH_BYTES = 16 * 1024 * 1024


def _params(semantics, block_bytes):
    limit = min(2 * block_bytes + COMPILER_SCRATCH_BYTES, V7X_SCOPED_VMEM_CAP)
    return pltpu.CompilerParams(dimension_semantics=semantics, vmem_limit_bytes=int(limit))


def _nbytes(shape, dtype):
    return math.prod(shape) * jnp.dtype(dtype).itemsize


def _silu(x):
    return x * jax.nn.sigmoid(x)


def _dot(a, b, dims, precision=None):
    return lax.dot_general(a, b, (dims, ((), ())), precision=precision, preferred_element_type=F32)


NN = ((1,), (0,))
NT = ((1,), (1,))
TN = ((0,), (0,))

MIX0_WIDTH = HG_VW + M_DI


def _shared_output(body, n_in, prev):
    if prev is None:
        return body, [], [], {}

    def with_prev(*refs):
        return body(*refs[:n_in], *refs[n_in + 1:])

    return with_prev, [pl.BlockSpec(memory_space=pl.ANY)], [prev], {n_in: 0}


def _mm_body(x_ref, w_ref, o_ref):
    o_ref[...] = _dot(x_ref[...], w_ref[...], NN).astype(o_ref.dtype)


def _matmul(x, w, *, tm, tn, n_cols=None, out_dtype=F32, name):
    m, k = x.shape
    n = n_cols if n_cols is not None else w.shape[1]
    blocks = _nbytes((tm, k), x.dtype) + _nbytes((k, tn), w.dtype) + _nbytes((tm, tn), out_dtype)
    return pl.pallas_call(
        _mm_body,
        grid=(n // tn, m // tm),
        in_specs=[pl.BlockSpec((tm, k), lambda j, i: (i, 0)), pl.BlockSpec((k, tn), lambda j, i: (0, j))],
        out_specs=pl.BlockSpec((tm, tn), lambda j, i: (i, j)),
        out_shape=jax.ShapeDtypeStruct((m, n), out_dtype),
        compiler_params=_params(("arbitrary", "arbitrary"), blocks),
        name=name,
    )(x, w)


def _mm_w32_body(x_ref, w_ref, o_ref, wb_scr):
    @pl.when(pl.program_id(1) == 0)
    def _():
        wb_scr[...] = w_ref[0].astype(BF16)

    o_ref[...] = _dot(x_ref[...], wb_scr[...], NN).astype(o_ref.dtype)


def _mm_w32t_body(x_ref, w_ref, o_ref, wb_scr):
    @pl.when(pl.program_id(1) == 0)
    def _():
        wb_scr[...] = w_ref[0].astype(BF16)

    o_ref[...] = _dot(x_ref[...], wb_scr[...], NT).astype(o_ref.dtype)


def _matmul_w32(x, w, layer, *, tm, tn, n_cols=None, transposed=False, name):
    m, k = x.shape
    n = n_cols if n_cols is not None else w.shape[1 if transposed else 2]
    blocks = _nbytes((tm, k), BF16) + _nbytes((k, tn), F32) + _nbytes((tm, tn), F32) + _nbytes((k, tn), BF16)
    if transposed:
        body, wblock, wspec = _mm_w32t_body, (tn, k), pl.BlockSpec((1, tn, k), lambda j, i: (layer, j, 0))
    else:
        body, wblock, wspec = _mm_w32_body, (k, tn), pl.BlockSpec((1, k, tn), lambda j, i: (layer, 0, j))
    return pl.pallas_call(
        body,
        grid=(n // tn, m // tm),
        in_specs=[pl.BlockSpec((tm, k), lambda j, i: (i, 0)), wspec],
        out_specs=pl.BlockSpec((tm, tn), lambda j, i: (i, j)),
        out_shape=jax.ShapeDtypeStruct((m, n), F32),
        scratch_shapes=[pltpu.VMEM(wblock, BF16)],
        compiler_params=_params(("arbitrary", "arbitrary"), blocks),
        name=name,
    )(x, w)


def _hgrn_gates(q, f, lb):
    fg = lb + (1.0 - lb) * jax.nn.sigmoid(f)
    return _silu(q), 1.0 - fg, jnp.log(fg)


def _hgrn_prompt_body(q_ref, f_ref, v_ref, g_ref, lb_ref, nw_ref, o_ref, sfin_ref, st_scr):
    n = pl.program_id(1)
    c = HG_CHUNK

    @pl.when(n == 0)
    def _():
        st_scr[...] = jnp.zeros_like(st_scr)

    row = lax.broadcasted_iota(jnp.int32, (c, c), 0)
    col = lax.broadcasted_iota(jnp.int32, (c, c), 1)
    causal = row >= col
    tri = causal.astype(F32)
    q, k, logf = _hgrn_gates(q_ref[...], f_ref[...], lb_ref[...])
    v = v_ref[...].astype(BF16)
    cum = _dot(tri, logf, NN, precision=HIGHEST)
    mid = cum[c // 2 - 1:c // 2, :]
    last = cum[c - 1:c, :]
    qm = (q * jnp.exp(cum - mid)).astype(BF16)
    km = (k * jnp.exp(mid - cum)).astype(BF16)
    qi = (q * jnp.exp(cum)).astype(BF16)
    kl = (k * jnp.exp(last - cum)).astype(BF16)
    a_last = jnp.exp(last)
    gate = nw_ref[...] * _silu(g_ref[...])
    heads = [slice(h * HG_DK, (h + 1) * HG_DK) for h in range(HG_HEADS)]
    scores = [jnp.where(causal, _dot(qm[:, sl], km[:, sl], NT), 0.0).astype(BF16) for sl in heads]
    sts = [st_scr[h] for h in range(HG_HEADS)]
    outs = [_dot(qi[:, sl], sts[h].astype(BF16), NT) + _dot(scores[h], v[:, sl], NN) for h, sl in enumerate(heads)]
    for h, sl in enumerate(heads):
        st_scr[h] = a_last[:, sl] * sts[h] + _dot(v[:, sl], kl[:, sl], TN)
    for h, sl in enumerate(heads):
        o = outs[h]
        o = o * lax.rsqrt(jnp.mean(o * o, -1, keepdims=True) + RMS_EPS)
        o_ref[:, sl] = (o * gate[:, sl]).astype(o_ref.dtype)

    @pl.when(n == pl.num_programs(1) - 1)
    def _():
        sfin_ref[0] = st_scr[...]


def _hgrn_prompt(proj, lb, nw, *, batch, seq, prev=None):
    c = HG_CHUNK
    nchunk = seq // c
    blk = lambda j: pl.BlockSpec((c, HG_KW), lambda b, n: (b * nchunk + n, j))
    vec = pl.BlockSpec((1, HG_KW), lambda b, n: (0, 0))
    blocks = 4 * _nbytes((c, HG_KW), F32) + _nbytes((c, HG_VW), BF16) + 2 * _nbytes((HG_HEADS, HG_DV, HG_DK), F32)
    body, xspecs, xops, alias = _shared_output(_hgrn_prompt_body, 6, prev)
    o, st = pl.pallas_call(
        body,
        grid=(batch, nchunk),
        in_specs=[blk(0), blk(1), blk(2), blk(3), vec, vec] + xspecs,
        out_specs=[pl.BlockSpec((c, HG_VW), lambda b, n: (b * nchunk + n, 0)),
                   pl.BlockSpec((1, HG_HEADS, HG_DV, HG_DK), lambda b, n: (b, 0, 0, 0))],
        out_shape=[jax.ShapeDtypeStruct((proj.shape[0], MIX0_WIDTH), BF16),
                   jax.ShapeDtypeStruct((batch, HG_HEADS, HG_DV, HG_DK), F32)],
        scratch_shapes=[pltpu.VMEM((HG_HEADS, HG_DV, HG_DK), F32)],
        input_output_aliases=alias,
        compiler_params=_params(("arbitrary", "arbitrary"), blocks),
        name="hgrn_prompt",
    )(proj, proj, proj, proj, lb.reshape(1, HG_KW), nw.reshape(1, HG_VW), *xops)
    return o, jnp.swapaxes(st, -1, -2)


HG_SAMPLE_BB = 8


def _hgrn_sample_body(q_ref, f_ref, v_ref, g_ref, lb_ref, nw_ref, s_ref, o_ref, so_ref, *, seq):
    rows = 2 * seq
    assert rows == V7X_SUBLANES
    row = lax.broadcasted_iota(jnp.int32, (rows, rows), 0)
    col = lax.broadcasted_iota(jnp.int32, (rows, rows), 1)
    causal = (row >= col) & ((row // seq) == (col // seq))
    rvec = lax.broadcasted_iota(jnp.int32, (rows, 1), 0)
    r16 = lax.broadcasted_iota(jnp.int32, (2 * rows, HG_DV), 0)
    ones_rows = jnp.where((r16 == rows) | (r16 == rows + 1), 1.0, 0.0).astype(BF16)

    heads = [slice(h * HG_DK, (h + 1) * HG_DK) for h in range(HG_HEADS)]
    in_seq = rvec % seq

    def pair(p, carry):
        r0 = pl.multiple_of(p * rows, rows)
        tile = pl.ds(r0, rows)
        q, k, logf = _hgrn_gates(q_ref[tile, :], f_ref[tile, :], lb_ref[...])
        v = v_ref[tile, :].astype(BF16)
        cum = logf
        step = 1
        while step < seq:
            cum = cum + jnp.where(in_seq >= step, _shift_rows(cum, step), 0.0)
            step *= 2
        ecum = jnp.exp(cum)
        qi = (q * ecum).astype(BF16)
        km = (k / ecum).astype(BF16)
        gate = nw_ref[...] * _silu(g_ref[tile, :])
        v16 = jnp.concatenate([v, jnp.zeros_like(v)], axis=0)
        lhs = []
        for j in range(2):
            mine = (rvec // seq) == j
            last = cum[(j + 1) * seq - 1:(j + 1) * seq, :]
            a = jnp.exp(last)
            a_hi = a.astype(BF16).astype(F32)
            kl = jnp.where(mine, k * jnp.exp(last - cum), 0.0)
            lhs.append(jnp.concatenate([kl, a_hi, a - a_hi, jnp.zeros((rows - 2, HG_KW), F32)], axis=0).astype(BF16))
        scores = [jnp.where(causal, _dot(qi[:, sl], km[:, sl], NT), 0.0).astype(BF16) for sl in heads]
        outs = [_dot(scores[h], v[:, sl], NN) for h, sl in enumerate(heads)]
        for j in range(2):
            b = 2 * p + j
            mine = (rvec // seq) == j
            s0 = [s_ref[b, h] for h in range(HG_HEADS)]
            inter = [_dot(qi[:, sl], s0[h].astype(BF16), NN) for h, sl in enumerate(heads)]
            both = [_dot(lhs[j][:, sl], jnp.concatenate([v16[:, sl], ones_rows], axis=1), TN)
                    for sl in heads]
            for h in range(HG_HEADS):
                outs[h] = outs[h] + jnp.where(mine, inter[h], 0.0)
                so_ref[b, h] = both[h][:, HG_DV:] * s0[h] + both[h][:, :HG_DV]
        for h, sl in enumerate(heads):
            o = outs[h]
            o = o * lax.rsqrt(jnp.mean(o * o, -1, keepdims=True) + RMS_EPS)
            o_ref[tile, sl] = (o * gate[:, sl]).astype(o_ref.dtype)
        return carry

    lax.fori_loop(0, HG_SAMPLE_BB // 2, pair, 0)


def _hgrn_sample(proj, row0, state, lb, nw, *, batch, seq, prev=None):
    bb = HG_SAMPLE_BB
    tr = bb * seq
    blk0 = row0 // tr
    blk = lambda j: pl.BlockSpec((tr, HG_KW), lambda i: (blk0 + i, j))
    vec = pl.BlockSpec((1, HG_KW), lambda i: (0, 0))
    sblk = pl.BlockSpec((bb, HG_HEADS, HG_DK, HG_DV), lambda i: (i, 0, 0, 0))
    blocks = 4 * _nbytes((tr, HG_KW), F32) + 2 * _nbytes((bb, HG_HEADS, HG_DK, HG_DV), F32)
    body, xspecs, xops, alias = _shared_output(functools.partial(_hgrn_sample_body, seq=seq), 7, prev)
    return pl.pallas_call(
        body,
        grid=(batch // bb,),
        in_specs=[blk(0), blk(1), blk(2), blk(3), vec, vec, sblk] + xspecs,
        out_specs=[pl.BlockSpec((tr, HG_VW), lambda i: (blk0 + i, 0)), sblk],
        out_shape=[jax.ShapeDtypeStruct((proj.shape[0], MIX0_WIDTH), BF16),
                   jax.ShapeDtypeStruct(state.shape, F32)],
        input_output_aliases=alias,
        compiler_params=_params(("arbitrary",), blocks),
        name="hgrn_sample",
    )(proj, proj, proj, proj, lb.reshape(1, HG_KW), nw.reshape(1, HG_VW), state, *xops)


def _head_expand_matrix():
    r = jnp.arange(2 * V7X_LANES)[:, None] % V7X_LANES
    c = jnp.arange(M_DI)[None, :] // M_HEADDIM
    return (r == c).astype(BF16)


def _expand_heads(coef, e2):
    hi = coef.astype(BF16)
    lo = (coef - hi.astype(F32)).astype(BF16)
    return _dot(jnp.concatenate([hi, lo], axis=1), e2, NN)


def _softplus(x):
    return jnp.maximum(x, 0.0) + jnp.log(1.0 + jnp.exp(-jnp.abs(x)))


def _group_rmsnorm(y, width):
    outs = []
    for g in range(y.shape[1] // width):
        yg = y[:, g * width:(g + 1) * width]
        outs.append(yg * lax.rsqrt(jnp.mean(yg * yg, -1, keepdims=True) + RMS_EPS))
    return jnp.concatenate(outs, axis=1)


def _ssd_prompt_body(z_ref, x_ref, bc_ref, dt_ref, cwx_ref, cwb_ref, cbx_ref, cbb_ref, dtb_ref, alog_ref,
                     dexp_ref, nw_ref, e2_ref, y_ref, sfin_ref, conv_ref, s_scr, cx_scr, cb_scr):
    n = pl.program_id(1)
    c = M_CHUNK
    tail = V7X_SUBLANES

    @pl.when(n == 0)
    def _():
        s_scr[...] = jnp.zeros_like(s_scr)
        cx_scr[...] = jnp.zeros_like(cx_scr)
        cb_scr[...] = jnp.zeros_like(cb_scr)

    def conv(u, carry_scr, w_ref, b_ref):
        ext = jnp.concatenate([carry_scr[...], u], axis=0)
        acc = b_ref[...] + w_ref[M_CONV - 1:M_CONV, :] * u
        for j in range(1, M_CONV):
            acc = acc + w_ref[M_CONV - 1 - j:M_CONV - j, :] * pltpu.roll(ext, j, 0)[tail:, :]
        carry_scr[...] = u[c - tail:, :]
        return _silu(acc)

    ux = x_ref[...]
    ubc = bc_ref[...]
    xc = conv(ux, cx_scr, cwx_ref, cbx_ref)
    bcc = conv(ubc, cb_scr, cwb_ref, cbb_ref)
    ngn = M_GROUPS * M_STATE

    row = lax.broadcasted_iota(jnp.int32, (c, c), 0)
    col = lax.broadcasted_iota(jnp.int32, (c, c), 1)
    causal = row >= col
    tri = causal.astype(F32)
    e2 = e2_ref[...]

    dtp = _softplus(dt_ref[...] + dtb_ref[...])
    da = dtp * (-jnp.exp(alog_ref[...]))
    cum = _dot(tri, da, NN, precision=HIGHEST)
    cum_t = cum.T
    last = cum[c - 1:c, :]
    xdt = xc * _expand_heads(dtp, e2)
    xend = (xdt * _expand_heads(jnp.exp(last - cum), e2)).astype(BF16)
    ecum = _expand_heads(jnp.exp(cum), e2)
    xdt_b = xdt.astype(BF16)
    gw = M_HPG * M_HEADDIM

    ys = []
    for g in range(M_GROUPS):
        bg = bcc[:, g * M_STATE:(g + 1) * M_STATE].astype(BF16)
        cg = bcc[:, ngn + g * M_STATE:ngn + (g + 1) * M_STATE].astype(BF16)
        cb = _dot(cg, bg, NT)
        sg = s_scr[g * M_HPG:(g + 1) * M_HPG].reshape(gw, M_STATE)
        y_inter = _dot(cg, sg.astype(BF16), NT) * ecum[:, g * gw:(g + 1) * gw]
        upd = _dot(xend[:, g * gw:(g + 1) * gw], bg, TN)
        parts = []
        for hh in range(M_HPG):
            h = g * M_HPG + hh
            decay = jnp.exp(jnp.where(causal, cum[:, h:h + 1] - cum_t[h:h + 1, :], -jnp.inf))
            m = (cb * decay).astype(BF16)
            parts.append(_dot(m, xdt_b[:, h * M_HEADDIM:(h + 1) * M_HEADDIM], NN))
            s_scr[h] = jnp.exp(last[:, h:h + 1]) * s_scr[h] + upd[hh * M_HEADDIM:(hh + 1) * M_HEADDIM, :]
        ys.append(jnp.concatenate(parts, axis=1) + y_inter)
    y = jnp.concatenate(ys, axis=1) + dexp_ref[...] * xc
    y = y * _silu(z_ref[...])
    y_ref[...] = (_group_rmsnorm(y, gw) * nw_ref[...]).astype(y_ref.dtype)

    @pl.when(n == pl.num_programs(1) - 1)
    def _():
        sfin_ref[0] = s_scr[...]
        conv_ref[0, :, 0:M_DI] = ux[c - (M_CONV - 1):, :]
        conv_ref[0, :, M_DI:] = ubc[c - (M_CONV - 1):, :]


def _ssd_prompt(proj, dt, conv_w, conv_b, dt_bias, a_log, d_exp, norm_w, e2, *, batch, seq, prev=None):
    c = M_CHUNK
    nchunk = seq // c
    tok = lambda j: pl.BlockSpec((c, M_DI), lambda b, n: (b * nchunk + n, j))
    const = lambda shape, j=0: pl.BlockSpec(shape, lambda b, n: (0, j))
    blocks = (3 * _nbytes((c, M_DI), F32) + _nbytes((c, M_DI), BF16) + _nbytes((2 * V7X_LANES, M_DI), BF16)
              + 2 * _nbytes((M_HEADS, M_HEADDIM, M_STATE), F32) + 12 * _nbytes((c, M_DI), F32))
    body, xspecs, xops, alias = _shared_output(_ssd_prompt_body, 13, prev)
    return pl.pallas_call(
        body,
        grid=(batch, nchunk),
        in_specs=[tok(4), tok(5), tok(6),
                  pl.BlockSpec((c, V7X_LANES), lambda b, n: (b * nchunk + n, 0)),
                  const((M_CONV, M_DI), 0), const((M_CONV, M_DI), 1), const((1, M_DI), 0), const((1, M_DI), 1),
                  const((1, V7X_LANES)), const((1, V7X_LANES)), const((1, M_DI)), const((1, M_DI)),
                  const((2 * V7X_LANES, M_DI))] + xspecs,
        out_specs=[pl.BlockSpec((c, M_DI), lambda b, n: (b * nchunk + n, HG_VW // M_DI)),
                   pl.BlockSpec((1, M_HEADS, M_HEADDIM, M_STATE), lambda b, n: (b, 0, 0, 0)),
                   pl.BlockSpec((1, M_CONV - 1, M_CONV_DIM), lambda b, n: (b, 0, 0))],
        out_shape=[jax.ShapeDtypeStruct((proj.shape[0], MIX0_WIDTH), BF16),
                   jax.ShapeDtypeStruct((batch, M_HEADS, M_HEADDIM, M_STATE), F32),
                   jax.ShapeDtypeStruct((batch, M_CONV - 1, M_CONV_DIM), F32)],
        scratch_shapes=[pltpu.VMEM((M_HEADS, M_HEADDIM, M_STATE), F32),
                        pltpu.VMEM((V7X_SUBLANES, M_DI), F32), pltpu.VMEM((V7X_SUBLANES, M_DI), F32)],
        input_output_aliases=alias,
        compiler_params=_params(("arbitrary", "arbitrary"), blocks),
        name="ssd_prompt",
    )(proj, proj, proj, dt, conv_w, conv_w, conv_b, conv_b, dt_bias, a_log, d_exp, norm_w, e2, *xops)


SSD_SAMPLE_BB = 8


def _shift_rows(a, j):
    return a if j == 0 else pltpu.roll(a, j, 0)


def _ssd_sample_body(z_ref, x_ref, bc_ref, dt_ref, s_ref, cs_ref, cwx_ref, cwb_ref, cbx_ref, cbb_ref, dtb_ref,
                     alog_ref, dexp_ref, nw_ref, e2_ref, y_ref, so_ref, co_ref, *, seq):
    rows = V7X_SUBLANES
    hist = M_CONV - 1
    assert rows == 2 * seq and hist <= seq and hist <= rows - seq
    rvec = lax.broadcasted_iota(jnp.int32, (rows, 1), 0)
    valid = rvec < seq
    lane = lax.broadcasted_iota(jnp.int32, (1, V7X_LANES), 1)
    r16 = lax.broadcasted_iota(jnp.int32, (2 * rows, M_STATE), 0)
    ones_rows = jnp.where((r16 == rows) | (r16 == rows + 1), 1.0, 0.0).astype(BF16)
    e2 = e2_ref[...]
    a_neg = -jnp.exp(alog_ref[...])
    gw = M_HPG * M_HEADDIM
    ngn = M_GROUPS * M_STATE

    def conv(u8, buf, w_ref, b_ref):
        buf8 = jnp.concatenate([buf, jnp.zeros((rows - hist, buf.shape[1]), F32)], axis=0)
        ext = jnp.where(valid, u8, _shift_rows(buf8, rows - hist))
        acc = b_ref[...] + w_ref[hist:hist + 1, :] * ext
        for j in range(1, M_CONV):
            acc = acc + w_ref[hist - j:hist - j + 1, :] * _shift_rows(ext, j)
        new_hist = _shift_rows(ext, rows - (seq - hist))[0:hist, :]
        return _silu(acc), new_hist

    def one(b, u_x, u_bc, z8, dt8):
        xc, nhx = conv(u_x, cs_ref[b, :, 0:M_DI], cwx_ref, cbx_ref)
        bcc, nhb = conv(u_bc, cs_ref[b, :, M_DI:], cwb_ref, cbb_ref)
        co_ref[b, :, 0:M_DI] = nhx
        co_ref[b, :, M_DI:] = nhb
        dtp = jnp.where(valid, _softplus(dt8 + dtb_ref[...]), 0.0)
        cum = dtp * a_neg
        k = 1
        while k < seq:
            cum = cum + jnp.where(rvec >= k, _shift_rows(cum, k), 0.0)
            k *= 2
        last = cum[seq - 1:seq, :]
        bmat = bcc[:, :ngn]
        cmat = bcc[:, ngn:]
        coefs = [dtp, jnp.exp(cum), jnp.exp(last - cum), jnp.broadcast_to(jnp.exp(last), (rows, V7X_LANES))]
        for j in range(seq):
            prod = cmat * _shift_rows(bmat, j)
            cbh = jnp.zeros((rows, V7X_LANES), F32)
            for g in range(M_GROUPS):
                cbg = jnp.sum(prod[:, g * M_STATE:(g + 1) * M_STATE], axis=-1, keepdims=True)
                cbh = jnp.where((lane // M_HPG) == g, cbg, cbh)
            coefs.append(cbh * jnp.exp(cum - _shift_rows(cum, j)))
        wide = _expand_heads(jnp.concatenate(coefs, axis=0), e2)
        part = lambda n: wide[n * rows:(n + 1) * rows, :]
        xdt = xc * part(0)
        ecum = part(1)
        xs = xdt * part(2)
        dec = part(3)[0:1, :]
        y = dexp_ref[...] * xc
        for j in range(seq):
            y = y + part(4 + j) * _shift_rows(xdt, j)
        dec_hi = dec.astype(BF16).astype(F32)
        lhs = jnp.concatenate([xs, dec_hi, dec - dec_hi, jnp.zeros((rows - 2, M_DI), F32)], axis=0).astype(BF16)
        parts = []
        for g in range(M_GROUPS):
            bg = bmat[:, g * M_STATE:(g + 1) * M_STATE].astype(BF16)
            cg = cmat[:, g * M_STATE:(g + 1) * M_STATE].astype(BF16)
            s0 = s_ref[b, g * M_HPG:(g + 1) * M_HPG].reshape(gw, M_STATE)
            parts.append(_dot(cg, s0.astype(BF16), NT) * ecum[:, g * gw:(g + 1) * gw])
            rhs = jnp.concatenate([jnp.concatenate([bg, jnp.zeros((rows, M_STATE), BF16)], axis=0), ones_rows], axis=1)
            both = _dot(lhs[:, g * gw:(g + 1) * gw], rhs, TN)
            s_new = both[:, M_STATE:] * s0 + both[:, :M_STATE]
            so_ref[b, g * M_HPG:(g + 1) * M_HPG] = s_new.reshape(M_HPG, M_HEADDIM, M_STATE)
        y = (y + jnp.concatenate(parts, axis=1)) * _silu(z8)
        return _group_rmsnorm(y, gw) * nw_ref[...]

    def pair(p, carry):
        r0 = pl.multiple_of(p * rows, rows)
        tiles = [ref[pl.ds(r0, rows), :] for ref in (x_ref, bc_ref, z_ref, dt_ref)]
        ys = []
        for j in range(2):
            ys.append(one(2 * p + j, *[_shift_rows(t, j * (rows - seq)) for t in tiles]))
        y_ref[pl.ds(r0, rows), :] = jnp.where(valid, ys[0], _shift_rows(ys[1], seq)).astype(y_ref.dtype)
        return carry

    lax.fori_loop(0, SSD_SAMPLE_BB // 2, pair, 0)


def _ssd_sample(proj, row0, dt, state, conv_state, conv_w, conv_b, dt_bias, a_log, d_exp, norm_w, e2, *,
                batch, seq, prev=None):
    bb = SSD_SAMPLE_BB
    tr = bb * seq
    tok = lambda j: pl.BlockSpec((tr, M_DI), lambda i: (row0 // tr + i, j))
    const = lambda shape, j=0: pl.BlockSpec(shape, lambda i: (0, j))
    sblk = pl.BlockSpec((bb, M_HEADS, M_HEADDIM, M_STATE), lambda i: (i, 0, 0, 0))
    cblk = pl.BlockSpec((bb, M_CONV - 1, M_CONV_DIM), lambda i: (i, 0, 0))
    blocks = (3 * _nbytes((tr, M_DI), F32) + 2 * _nbytes((bb, M_HEADS, M_HEADDIM, M_STATE), F32)
              + 2 * _nbytes((bb, V7X_SUBLANES, M_CONV_DIM), F32) + _nbytes((2 * V7X_LANES, M_DI), BF16))
    body, xspecs, xops, alias = _shared_output(functools.partial(_ssd_sample_body, seq=seq), 15, prev)
    return pl.pallas_call(
        body,
        grid=(batch // bb,),
        in_specs=[tok(4), tok(5), tok(6),
                  pl.BlockSpec((tr, V7X_LANES), lambda i: (row0 // tr + i, 0)),
                  sblk, cblk,
                  const((M_CONV, M_DI), 0), const((M_CONV, M_DI), 1), const((1, M_DI), 0), const((1, M_DI), 1),
                  const((1, V7X_LANES)), const((1, V7X_LANES)), const((1, M_DI)), const((1, M_DI)),
                  const((2 * V7X_LANES, M_DI))] + xspecs,
        out_specs=[pl.BlockSpec((tr, M_DI), lambda i: (row0 // tr + i, HG_VW // M_DI)), sblk, cblk],
        out_shape=[jax.ShapeDtypeStruct((proj.shape[0], MIX0_WIDTH), BF16),
                   jax.ShapeDtypeStruct(state.shape, F32),
                   jax.ShapeDtypeStruct(conv_state.shape, F32)],
        input_output_aliases=alias,
        compiler_params=_params(("arbitrary",), blocks),
        name="ssd_sample",
    )(proj, proj, proj, dt, state, conv_state, conv_w, conv_w, conv_b, conv_b, dt_bias, a_log, d_exp, norm_w, e2, *xops)


S5_TILES = 8
S5_TILE_IN = D_MODEL // S5_TILES
S5_TILE_ST = S5_STATE // S5_TILES


def _s5_discretize(a_re, a_im, log_dt, b_re, b_im, c_re, c_im):
    lam_re = jnp.minimum(a_re, -1e-4)
    lam_im = a_im
    dt = jnp.exp(log_dt)[:, None]
    mag = jnp.exp(lam_re * dt)
    ab_re = mag * jnp.cos(lam_im * dt)
    ab_im = mag * jnp.sin(lam_im * dt)
    den = lam_re * lam_re + lam_im * lam_im
    zr = ((ab_re - 1.0) * lam_re + ab_im * lam_im) / den
    zi = (ab_im * lam_re - (ab_re - 1.0) * lam_im) / den
    bb_re = zr[..., None] * b_re - zi[..., None] * b_im
    bb_im = zr[..., None] * b_im + zi[..., None] * b_re
    gpt = S5_GROUPS // S5_TILES
    eye = jnp.eye(gpt, dtype=F32)

    def pack_b(bb):
        return jnp.einsum('kgph,gm->kghmp', bb.reshape(S5_TILES, gpt, S5_P, S5_GROUP), eye).reshape(
            S5_TILES, S5_TILE_IN, S5_TILE_ST).astype(BF16)

    def pack_c(c):
        return jnp.einsum('kghp,gm->kgpmh', c.reshape(S5_TILES, gpt, S5_GROUP, S5_P), eye).reshape(
            S5_TILES, S5_TILE_ST, S5_TILE_IN).astype(BF16)

    slab = lambda a: _s5_slab_layout(a.reshape(S5_STATE))
    return slab(ab_re), slab(ab_im), pack_b(bb_re), pack_b(bb_im), pack_c(c_re), pack_c(c_im)


S5_PROMPT_ROWS = 256
S5_SAMPLE_ROWS = 128
S5_NBLK = S5_TILE_ST // V7X_LANES


def _s5_pitch(rows):
    assert rows % V7X_SUBLANES == 0
    return rows + V7X_SUBLANES // 2


def _s5_slab_layout(a):
    lead = a.shape[:-1]
    return jnp.swapaxes(a.reshape(lead + (S5_TILES, S5_NBLK, V7X_LANES)), -3, -2)


def _s5_from_slab_layout(a):
    lead = a.shape[:-3]
    return jnp.swapaxes(a, -3, -2).reshape(lead + (S5_STATE,))


def _s5_fused_body(xb_ref, x_ref, h0r_ref, h0i_ref, ar_ref, ai_ref, wbr_ref, wbi_ref, wcr_ref, wci_ref, d_ref, *rest,
                   n_seq, steps):
    act_ref, fr_ref, fi_ref, bur, bui, cr, ci = rest[-7:]
    hrs, his = bur, bui
    n = pl.program_id(1)
    rows = n_seq * steps
    S5_PITCH = _s5_pitch(rows)
    for s in range(S5_TILES):
        u = xb_ref[:, s * S5_TILE_IN:(s + 1) * S5_TILE_IN]
        br = _dot(u, wbr_ref[s], NN)
        bi = _dot(u, wbi_ref[s], NN)
        for j in range(S5_NBLK):
            bur[j, s * S5_PITCH:s * S5_PITCH + rows, :] = br[:, j * V7X_LANES:(j + 1) * V7X_LANES]
            bui[j, s * S5_PITCH:s * S5_PITCH + rows, :] = bi[:, j * V7X_LANES:(j + 1) * V7X_LANES]

    @pl.when(n == 0)
    def _():
        cr[...] = h0r_ref[...]
        ci[...] = h0i_ref[...]

    ar = [ar_ref[j] for j in range(S5_NBLK)]
    ai = [ai_ref[j] for j in range(S5_NBLK)]

    def seq_body(q, carry):
        def step(t, h):
            hr, hi = h
            idx = pl.ds(q * steps + t, S5_TILES, stride=S5_PITCH)
            nr, ni = [], []
            for j in range(S5_NBLK):
                nr.append(ar[j] * hr[j] - ai[j] * hi[j] + bur[j, idx, :])
                ni.append(ar[j] * hi[j] + ai[j] * hr[j] + bui[j, idx, :])
                hrs[j, idx, :] = nr[j]
                his[j, idx, :] = ni[j]
            return tuple(nr), tuple(ni)

        h0 = (tuple(cr[q, j] for j in range(S5_NBLK)), tuple(ci[q, j] for j in range(S5_NBLK)))
        hr, hi = lax.fori_loop(0, steps, step, h0, unroll=min(steps, 4))
        for j in range(S5_NBLK):
            cr[q, j] = hr[j]
            ci[q, j] = hi[j]
        return carry

    lax.fori_loop(0, n_seq, seq_body, 0)

    for s in range(S5_TILES):
        seg = slice(s * S5_PITCH, s * S5_PITCH + rows)
        hre = jnp.concatenate([hrs[j, seg, :] for j in range(S5_NBLK)], axis=1).astype(BF16)
        him = jnp.concatenate([his[j, seg, :] for j in range(S5_NBLK)], axis=1).astype(BF16)
        y = _dot(hre, wcr_ref[s], NN) - _dot(him, wci_ref[s], NN)
        cs = slice(s * S5_TILE_IN, (s + 1) * S5_TILE_IN)
        act_ref[:, cs] = jax.nn.gelu(y + d_ref[:, cs] * x_ref[:, cs]).astype(act_ref.dtype)

    @pl.when(n == pl.num_programs(1) - 1)
    def _():
        fr_ref[...] = cr[...]
        fi_ref[...] = ci[...]


def _s5_fused(xb, x, h0_re, h0_im, ab_re, ab_im, wb_re, wb_im, wc_re, wc_im, d_skip, *, row0, batch, seq, rows, n_seq,
              prev=None, name):
    t, d = x.shape
    steps = rows // n_seq
    assert n_seq == 1 or steps == seq
    S5_PITCH = _s5_pitch(rows)
    tok = lambda width: pl.BlockSpec((rows, width), lambda i, n: ((row0 + i * n_seq * seq) // rows + n, 0))
    st = pl.BlockSpec((n_seq, S5_NBLK, S5_TILES, V7X_LANES), lambda i, n: (i, 0, 0, 0))
    const = lambda shape: pl.BlockSpec(shape, lambda i, n: (0,) * len(shape), pipeline_mode=pl.Buffered(1))
    slab = pltpu.VMEM((S5_NBLK, S5_TILES * S5_PITCH, V7X_LANES), F32)
    stv = pltpu.VMEM((n_seq, S5_NBLK, S5_TILES, V7X_LANES), F32)
    blocks = (2 * (_nbytes((rows, d), BF16) * 2 + _nbytes((rows, d), F32))
              + 2 * 6 * _nbytes((n_seq, S5_STATE), F32) + 4 * _nbytes(wb_re.shape, BF16)
              + 2 * _nbytes((S5_NBLK, S5_TILES * S5_PITCH, V7X_LANES), F32))
    keep = [] if prev is None else [prev]
    n_in = 11
    return pl.pallas_call(
        functools.partial(_s5_fused_body, n_seq=n_seq, steps=steps),
        grid=(batch // n_seq, seq // steps),
        in_specs=[tok(d), tok(d), st, st, const(ab_re.shape), const(ab_im.shape), const(wb_re.shape),
                  const(wb_im.shape), const(wc_re.shape), const(wc_im.shape), const((1, d))]
                 + [pl.BlockSpec(memory_space=pl.ANY)] * len(keep),
        out_specs=[tok(d), st, st],
        out_shape=[jax.ShapeDtypeStruct((t, d), BF16), jax.ShapeDtypeStruct(h0_re.shape, F32),
                   jax.ShapeDtypeStruct(h0_im.shape, F32)],
        scratch_shapes=[slab, slab, stv, stv],
        input_output_aliases={n_in + k: k for k in range(len(keep))},
        compiler_params=pltpu.CompilerParams(dimension_semantics=("arbitrary", "arbitrary"),
                                             vmem_limit_bytes=int(min(blocks + COMPILER_SCRATCH_BYTES, V7X_SCOPED_VMEM_CAP))),
        name=name,
    )(xb, x, h0_re, h0_im, ab_re, ab_im, wb_re, wb_im, wc_re, wc_im, d_skip.reshape(1, d), *keep)


def _glu_body(a_ref, wa_ref, wb_ref, ba_ref, bb_ref, o_ref, wa_scr, wb_scr):
    @pl.when(pl.program_id(1) == 0)
    def _():
        wa_scr[...] = wa_ref[0].astype(BF16)
        wb_scr[...] = wb_ref[0].astype(BF16)

    a = a_ref[...]
    o_ref[...] = (_dot(a, wa_scr[...], NN) + ba_ref[...]) * jax.nn.sigmoid(_dot(a, wb_scr[...], NN) + bb_ref[...])


def _glu(a, wa, wb, ba, bb, layer, *, tm, tn):
    t, k = a.shape
    n = wa.shape[2]
    wspec = pl.BlockSpec((1, k, tn), lambda j, i: (layer, 0, j))
    bspec = pl.BlockSpec((1, tn), lambda j, i: (layer, j))
    blocks = _nbytes((tm, k), BF16) + 2 * _nbytes((k, tn), F32) + _nbytes((tm, tn), F32) + 2 * _nbytes((k, tn), BF16)
    return pl.pallas_call(
        _glu_body,
        grid=(n // tn, t // tm),
        in_specs=[pl.BlockSpec((tm, k), lambda j, i: (i, 0)), wspec, wspec, bspec, bspec],
        out_specs=pl.BlockSpec((tm, tn), lambda j, i: (i, j)),
        out_shape=jax.ShapeDtypeStruct((t, n), F32),
        scratch_shapes=[pltpu.VMEM((k, tn), BF16)] * 2,
        compiler_params=_params(("arbitrary", "arbitrary"), blocks),
        name="glu",
    )(a, wa, wb, ba, bb)


def _route(x, w_ref, b_ref, e_ref, g_ref):
    logits = _dot(w_ref[...], x, NT, precision=HIGHEST) + b_ref[...]
    rows = [logits[e:e + 1, :] for e in range(N_EXPERTS)]
    m = functools.reduce(jnp.maximum, rows)
    ex = [jnp.exp(r - m) for r in rows]
    z = functools.reduce(jnp.add, ex)
    p = [v / z for v in ex]

    def top2_sum(a, b, c, d):
        hi1, lo1, hi2, lo2 = jnp.maximum(a, b), jnp.minimum(a, b), jnp.maximum(c, d), jnp.minimum(c, d)
        return jnp.maximum(hi1, hi2) + jnp.maximum(jnp.minimum(hi1, hi2), jnp.maximum(lo1, lo2))

    assert EXP_PER_GROUP == 4
    best = top2_sum(*p[0:EXP_PER_GROUP])
    gsel = jnp.zeros_like(best, dtype=jnp.int32)
    for g in range(1, N_EXP_GROUPS):
        s = top2_sum(*p[g * EXP_PER_GROUP:(g + 1) * EXP_PER_GROUP])
        better = s > best
        gsel = jnp.where(better, g, gsel)
        best = jnp.where(better, s, best)
    inner = []
    for i in range(EXP_PER_GROUP):
        v = p[i]
        for g in range(1, N_EXP_GROUPS):
            v = jnp.where(gsel == g, p[g * EXP_PER_GROUP + i], v)
        inner.append(v)

    def first_argmax(vals, skip=None):
        bv = bi = None
        for i, v in enumerate(vals):
            v = v if skip is None else jnp.where(skip == i, -jnp.inf, v)
            if bv is None:
                bv, bi = v, jnp.zeros_like(gsel)
            else:
                better = v > bv
                bi = jnp.where(better, i, bi)
                bv = jnp.where(better, v, bv)
        return bv, bi

    p1, i1 = first_argmax(inner)
    p2, i2 = first_argmax(inner, skip=i1)
    tot = p1 + p2
    e1 = gsel * EXP_PER_GROUP + i1
    e2 = gsel * EXP_PER_GROUP + i2
    e_ref[0:1, :] = e1
    e_ref[1:2, :] = e2
    g_ref[0:1, :] = p1 / tot
    g_ref[1:2, :] = p2 / tot


def _ln_router_body(x_ref, x2_ref, mix_ref, g_ref, b_ref, w_ref, br_ref, o_ref, e_ref, gw_ref, *, first_blocks):
    x = x_ref[...] if first_blocks is None else jnp.where(pl.program_id(0) < first_blocks, x_ref[...], x2_ref[...])
    z = ALPHA * x + mix_ref[...]
    mu = jnp.mean(z, -1, keepdims=True)
    zc = z - mu
    var = jnp.mean(zc * zc, -1, keepdims=True)
    y = zc * lax.rsqrt(var + LN_EPS) * g_ref[...] + b_ref[...]
    o_ref[...] = y
    _route(y, w_ref, br_ref, e_ref, gw_ref)


def _res_layernorm_router(x, mix, g, b, w_router, b_router, *, tm, name, x_tail=None):
    t, d = mix.shape
    row = pl.BlockSpec((tm, d), lambda i: (i, 0))
    vec = pl.BlockSpec((1, d), lambda i: (0, 0))
    sel = pl.BlockSpec((TOP_K, tm), lambda i: (0, i))
    if x_tail is None:
        nb, xspecs, xs = None, [row, vec], [x, g.reshape(1, d)]
    else:
        nb = x.shape[0] // tm
        xspecs = [pl.BlockSpec((tm, d), lambda i: (jnp.minimum(i, nb - 1), 0)),
                  pl.BlockSpec((tm, d), lambda i: (jnp.maximum(i - nb, 0), 0))]
        xs = [x, x_tail]
    blocks = 4 * _nbytes((tm, d), F32) + _nbytes((N_EXPERTS, d), F32) + 2 * _nbytes((V7X_SUBLANES, tm), F32)
    return pl.pallas_call(
        functools.partial(_ln_router_body, first_blocks=nb),
        grid=(t // tm,),
        in_specs=xspecs + [row, vec, vec, pl.BlockSpec((N_EXPERTS, d), lambda i: (0, 0)),
                           pl.BlockSpec((N_EXPERTS, 1), lambda i: (0, 0))],
        out_specs=[row, sel, sel],
        out_shape=[jax.ShapeDtypeStruct((t, d), F32), jax.ShapeDtypeStruct((TOP_K, t), jnp.int32),
                   jax.ShapeDtypeStruct((TOP_K, t), F32)],
        compiler_params=_params(("arbitrary",), blocks),
        name=name,
    )(*xs, mix, g.reshape(1, d), b.reshape(1, d), w_router.T, b_router.reshape(N_EXPERTS, 1))


MOE_TILE = 256
COMBINE_TILE = 256


def _moe_plan(eidx):
    k, t = eidx.shape
    ns = k * t
    n_tiles = ns // MOE_TILE + N_EXPERTS
    e_flat = eidx.reshape(ns)
    order = jnp.argsort(e_flat, stable=True).astype(jnp.int32)
    rank = jnp.argsort(order).astype(jnp.int32)
    experts = jnp.arange(N_EXPERTS, dtype=jnp.int32)[None, :]
    slot_is = e_flat[:, None] == experts
    counts = jnp.sum(slot_is.astype(jnp.int32), axis=0)
    start = jnp.cumsum(counts) - counts
    tiles = (counts + MOE_TILE - 1) // MOE_TILE
    tile_end = jnp.cumsum(tiles)
    tile_start = tile_end - tiles
    tile_ids = jnp.arange(n_tiles, dtype=jnp.int32)
    tile_expert = jnp.minimum(jnp.sum((tile_ids[:, None] >= tile_end[None, :]).astype(jnp.int32), axis=1),
                              N_EXPERTS - 1)
    pick = lambda onehot, table: jnp.sum(jnp.where(onehot, table[None, :], 0), axis=1)
    tile_is = tile_expert[:, None] == experts
    tile_first = pick(tile_is, start) + (tile_ids - pick(tile_is, tile_start)) * MOE_TILE
    position_of_slot = pick(slot_is, tile_start) * MOE_TILE + rank - pick(slot_is, start)
    sorted_token = jnp.pad(order % t, (0, MOE_TILE))
    return (tile_expert.astype(jnp.int32), tile_first.astype(jnp.int32), tile_end[-1:].astype(jnp.int32),
            sorted_token.astype(jnp.int32), position_of_slot.astype(jnp.int32))


def _moe_expert_body(te_ref, first_ref, nt_ref, tok_ref, x_hbm, wg_ref, wu_ref, wd_ref, y_ref, xbuf, sem, wg_scr,
                     wu_scr, wd_scr):
    i = pl.program_id(0)
    nt = nt_ref[0]
    tm = MOE_TILE
    sub = V7X_SUBLANES

    def gather_copy(tok, slot, r8, j):
        return pltpu.make_async_copy(x_hbm.at[pl.ds(tok, 1)], xbuf.at[slot, r8, pl.ds(j, 1)], sem.at[slot])

    def start_gather(tile, slot):
        first = first_ref[tile]

        def body(r8, c):
            for j in range(sub):
                gather_copy(tok_ref[first + r8 * sub + j], slot, r8, j).start(priority=j % 2)
            return c

        lax.fori_loop(0, tm // sub, body, 0)

    @pl.when(i == 0)
    def _():
        start_gather(0, 0)

    @pl.when(i + 1 < nt)
    def _():
        start_gather(i + 1, (i + 1) % 2)

    @pl.when(i < nt)
    def _():
        slot = i % 2
        for r8 in range(tm // sub):
            pltpu.make_async_copy(x_hbm.at[pl.ds(0, sub)], xbuf.at[slot, r8], sem.at[slot]).wait()

        @pl.when((i == 0) | (te_ref[i] != te_ref[jnp.maximum(i - 1, 0)]))
        def _():
            wg_scr[...] = wg_ref[0, 0].astype(BF16)
            wu_scr[...] = wu_ref[0, 0].astype(BF16)
            wd_scr[...] = wd_ref[0, 0].astype(BF16)

        x = xbuf[slot].reshape(tm, x_hbm.shape[1]).astype(BF16)
        act = (_silu(_dot(x, wg_scr[...], NN)) * _dot(x, wu_scr[...], NN)).astype(BF16)
        y_ref[...] = _dot(act, wd_scr[...], NN)

    @pl.when(i >= nt)
    def _():
        y_ref[...] = jnp.zeros_like(y_ref)


def _moe_experts(x, plan, w_gate, w_up, w_down, layer):
    tile_expert, tile_first, n_valid, sorted_token, _ = plan
    t, d = x.shape
    f = w_gate.shape[3]
    n_tiles = tile_expert.shape[0]
    wspec = lambda a, b: pl.BlockSpec((1, 1, a, b), lambda i, te, *_: (layer, te[i], 0, 0))
    blocks = (2 * 3 * _nbytes((d, f), F32) + 3 * _nbytes((d, f), BF16) + 2 * _nbytes((MOE_TILE, d), F32)
              + 2 * _nbytes((MOE_TILE, d), F32) + 4 * _nbytes((MOE_TILE, f), F32))
    return pl.pallas_call(
        _moe_expert_body,
        grid_spec=pltpu.PrefetchScalarGridSpec(
            num_scalar_prefetch=4,
            grid=(n_tiles,),
            in_specs=[pl.BlockSpec(memory_space=pl.ANY), wspec(d, f), wspec(d, f), wspec(f, d)],
            out_specs=pl.BlockSpec((MOE_TILE, d), lambda i, *_: (i, 0)),
            scratch_shapes=[pltpu.VMEM((2, MOE_TILE // V7X_SUBLANES, V7X_SUBLANES, d), F32),
                            pltpu.SemaphoreType.DMA((2,)),
                            pltpu.VMEM((d, f), BF16), pltpu.VMEM((d, f), BF16), pltpu.VMEM((f, d), BF16)]),
        out_shape=jax.ShapeDtypeStruct((n_tiles * MOE_TILE, d), F32),
        compiler_params=pltpu.CompilerParams(dimension_semantics=("arbitrary",),
                                             vmem_limit_bytes=int(min(blocks + COMPILER_SCRATCH_BYTES, V7X_SCOPED_VMEM_CAP))),
        name="moe_experts",
    )(tile_expert, tile_first, n_valid, sorted_token, x, w_gate, w_up, w_down)


def _moe_combine_body(pos_ref, x_ref, w_ref, g_ref, b_ref, y_hbm, o_ref, ob_ref, ybuf, sem, *, n_tok, split_blocks):
    i = pl.program_id(0)
    tc = COMBINE_TILE

    sub = V7X_SUBLANES

    def gather_copy(p, slot, k, r8, j):
        return pltpu.make_async_copy(y_hbm.at[pl.ds(p, 1)], ybuf.at[slot, k, r8, pl.ds(j, 1)], sem.at[slot])

    def start_gather(tile, slot):
        def body(r8, c):
            for k in range(TOP_K):
                for j in range(sub):
                    gather_copy(pos_ref[k * n_tok + tile * tc + r8 * sub + j], slot, k, r8, j).start(priority=j % 2)
            return c

        lax.fori_loop(0, tc // sub, body, 0)

    @pl.when(i == 0)
    def _():
        start_gather(0, 0)

    @pl.when(i + 1 < pl.num_programs(0))
    def _():
        start_gather(i + 1, (i + 1) % 2)

    slot = i % 2
    for k in range(TOP_K):
        for r8 in range(tc // sub):
            pltpu.make_async_copy(y_hbm.at[pl.ds(0, sub)], ybuf.at[slot, k, r8], sem.at[slot]).wait()
    z = ALPHA * x_ref[...]
    for k in range(TOP_K):
        z = z + w_ref[:, k:k + 1] * ybuf[slot, k].reshape(tc, x_ref.shape[1])
    mu = jnp.mean(z, -1, keepdims=True)
    zc = z - mu
    var = jnp.mean(zc * zc, -1, keepdims=True)
    y = zc * lax.rsqrt(var + LN_EPS) * g_ref[...] + b_ref[...]
    if split_blocks is None:
        o_ref[...] = y
        ob_ref[...] = y.astype(BF16)
    else:
        @pl.when(i < split_blocks)
        def _():
            o_ref[...] = y

        @pl.when(i >= split_blocks)
        def _():
            ob_ref[...] = y


def _moe_combine_layernorm(x, y_slots, plan, gate_w, g, b, *, name, split=None):
    position_of_slot = plan[4]
    t, d = x.shape
    tc = COMBINE_TILE
    row = pl.BlockSpec((tc, d), lambda i, pos: (i, 0))
    vec = pl.BlockSpec((1, d), lambda i, pos: (0, 0))
    blocks = 2 * (3 * _nbytes((tc, d), F32) + _nbytes((tc, d), BF16)) + 2 * TOP_K * _nbytes((tc, d), F32)
    if split is None:
        sb = None
        out_specs = [row, row]
        out_shape = [jax.ShapeDtypeStruct((t, d), F32), jax.ShapeDtypeStruct((t, d), BF16)]
    else:
        sb = split // tc
        out_specs = [pl.BlockSpec((tc, d), lambda i, pos: (jnp.minimum(i, sb - 1), 0)),
                     pl.BlockSpec((tc, d), lambda i, pos: (jnp.maximum(i - sb, 0), 0))]
        out_shape = [jax.ShapeDtypeStruct((split, d), F32), jax.ShapeDtypeStruct((t - split, d), F32)]
    return pl.pallas_call(
        functools.partial(_moe_combine_body, n_tok=t, split_blocks=sb),
        grid_spec=pltpu.PrefetchScalarGridSpec(
            num_scalar_prefetch=1,
            grid=(t // tc,),
            in_specs=[row, pl.BlockSpec((tc, TOP_K), lambda i, pos: (i, 0)), vec, vec,
                      pl.BlockSpec(memory_space=pl.ANY)],
            out_specs=out_specs,
            scratch_shapes=[pltpu.VMEM((2, TOP_K, tc // V7X_SUBLANES, V7X_SUBLANES, d), F32),
                            pltpu.SemaphoreType.DMA((2,))]),
        out_shape=out_shape,
        compiler_params=pltpu.CompilerParams(dimension_semantics=("arbitrary",),
                                             vmem_limit_bytes=int(min(blocks + COMPILER_SCRATCH_BYTES, V7X_SCOPED_VMEM_CAP))),
        name=name,
    )(position_of_slot, x, gate_w, g.reshape(1, d), b.reshape(1, d), y_slots)


TOKEN_TILE = 1088
IN_PROJ_TN = 1024
LN_TILE = 256


def _moe_layernorm(x, eidx, gate_w, layer, w_gate, w_up, w_down, g, b, split=None):
    plan = _moe_plan(eidx)
    y_slots = _moe_experts(x, plan, w_gate, w_up, w_down, layer)
    return _moe_combine_layernorm(x, y_slots, plan, gate_w.T, g, b, name=f"moe_combine_ln_{layer}", split=split)


def kernel(x_prompt, x_sample, state_hgrn, state_ssm, state_conv, state_s5_re, state_s5_im, w_in0, hg_lb_logits, hg_norm_w, conv_w, conv_b, dt_bias, a_log, m_d, m_norm_w, w_out0, s5_a_re, s5_a_im, s5_log_dt, s5_b_re, s5_b_im, s5_c_re, s5_c_im, s5_d, glu_w_a, glu_b_a, glu_w_b, glu_b_b, w_router, b_router, w_gate, w_up, w_down, ln1_g, ln1_b, ln2_g, ln2_b):
    bp, lp, d = x_prompt.shape
    bs, ls, _ = x_sample.shape
    tp, ts = bp * lp, bs * ls
    tm = TOKEN_TILE
    xp2, xs2 = x_prompt.reshape(tp, d), x_sample.reshape(ts, d)
    x0b = jnp.concatenate([xp2.astype(BF16), xs2.astype(BF16)], axis=0)
    lower_bounds = jnp.cumsum(jax.nn.softmax(hg_lb_logits.astype(F32), axis=0), axis=0)
    pad_lanes = lambda v: jnp.pad(v, (0, V7X_LANES - v.shape[0])).reshape(1, V7X_LANES)

    proj = _matmul_w32(x0b, jnp.swapaxes(w_in0, 1, 2), 0, tm=tm, tn=IN_PROJ_TN, n_cols=IN0_MAIN,
                       transposed=True, name="in_proj")
    w_dt = jnp.pad(w_in0[0, :, IN0_MAIN:], ((0, 0), (0, V7X_LANES - M_HEADS))).astype(BF16)
    dt = _matmul(x0b, w_dt, tm=tm, tn=V7X_LANES, name="dt_proj")
    lb0 = lower_bounds[0]
    e2 = _head_expand_matrix()
    ssd_consts = (conv_w[0], conv_b[0].reshape(1, -1), pad_lanes(dt_bias[0]), pad_lanes(a_log[0]),
                  jnp.repeat(m_d[0], M_HEADDIM).reshape(1, M_DI), m_norm_w[0].reshape(1, M_DI), e2)
    mixed, hg_p = _hgrn_prompt(proj, lb0, hg_norm_w[0], batch=bp, seq=lp)
    mixed, hg_s = _hgrn_sample(proj, tp, state_hgrn[0], lb0, hg_norm_w[0], batch=bs, seq=ls, prev=mixed)
    mixed, ssm_p, conv_p = _ssd_prompt(proj, dt, *ssd_consts, batch=bp, seq=lp, prev=mixed)
    mixed, ssm_s, conv_s = _ssd_sample(proj, tp, dt, state_ssm[0], state_conv[0], *ssd_consts, batch=bs, seq=ls,
                                       prev=mixed)
    mix = _matmul_w32(mixed, w_out0, 0, tm=tm, tn=512, name="out_proj")
    x1, eidx, gate_w = _res_layernorm_router(xp2, mix, ln1_g[0], ln1_b[0], w_router, b_router, tm=LN_TILE,
                                             name="ln1_router_0", x_tail=xs2)
    x2, x2b = _moe_layernorm(x1, eidx, gate_w, 0, w_gate, w_up, w_down, ln2_g[0], ln2_b[0])

    ab_re, ab_im, wb_re, wb_im, wc_re, wc_im = _s5_discretize(
        s5_a_re[0], s5_a_im[0], s5_log_dt[0], s5_b_re[0], s5_b_im[0], s5_c_re[0], s5_c_im[0])
    s5_consts = (ab_re, ab_im, wb_re, wb_im, wc_re, wc_im, s5_d[0])
    zeros = jnp.zeros((bp, S5_NBLK, S5_TILES, V7X_LANES), F32)
    act, s5r_p, s5i_p = _s5_fused(x2b, x2, zeros, zeros, *s5_consts, row0=0, batch=bp, seq=lp,
                                  rows=S5_PROMPT_ROWS, n_seq=1, name="s5_prompt")
    act, s5r_s, s5i_s = _s5_fused(x2b, x2, _s5_slab_layout(state_s5_re[0].reshape(bs, S5_STATE)),
                                  _s5_slab_layout(state_s5_im[0].reshape(bs, S5_STATE)), *s5_consts,
                                  row0=tp, batch=bs, seq=ls, rows=S5_SAMPLE_ROWS, n_seq=S5_SAMPLE_ROWS // ls,
                                  prev=act, name="s5_sample")
    mix1 = _glu(act, glu_w_a, glu_w_b, glu_b_a, glu_b_b, 0, tm=tm, tn=512)
    x3, eidx, gate_w = _res_layernorm_router(x2, mix1, ln1_g[1], ln1_b[1], w_router, b_router, tm=LN_TILE,
                                             name="ln1_router_1")
    y_p, y_s = _moe_layernorm(x3, eidx, gate_w, 1, w_gate, w_up, w_down, ln2_g[1], ln2_b[1], split=tp)

    s5_state = lambda a, b: _s5_from_slab_layout(a).reshape(1, b, S5_GROUPS, S5_P)
    return (y_p.reshape(bp, lp, d), y_s.reshape(bs, ls, d),
            hg_p[None], hg_s[None], ssm_p[None], ssm_s[None], conv_p[None], conv_s[None],
            s5_state(s5r_p, bp), s5_state(s5r_s, bs), s5_state(s5i_p, bp), s5_state(s5i_s, bs))
```

```python
import functools
import math

import jax
import jax.numpy as jnp
from jax import lax
from jax.experimental import pallas as pl
from jax.experimental.pallas import tpu as pltpu

F32 = jnp.float32
BF16 = jnp.bfloat16
HIGHEST = lax.Precision.HIGHEST

D_MODEL = 2048
DEPTH = 2
HG_HEADS = 16
HG_DK = 128
HG_DV = 128
HG_KW = HG_HEADS * HG_DK
HG_VW = HG_HEADS * HG_DV
HG_CHUNK = 64
M_DI = 2048
M_HEADDIM = 64
M_HEADS = 32
M_GROUPS = 8
M_HPG = 4
M_STATE = 128
M_CONV = 4
M_CONV_DIM = M_DI + 2 * M_GROUPS * M_STATE
M_CHUNK = 128
IN0_MAIN = 2 * HG_KW + 2 * HG_VW + M_DI + M_CONV_DIM
S5_GROUP = 16
S5_GROUPS = 128
S5_P = 64
S5_STATE = S5_GROUPS * S5_P
N_EXPERTS = 16
N_EXP_GROUPS = 4
EXP_PER_GROUP = 4
TOP_K = 2
D_EXPERT = 512
ALPHA = (2 * DEPTH) ** 0.25
LN_EPS = 1e-5
RMS_EPS = 1e-6

V7X_LANES = 128
V7X_SUBLANES = 8
V7X_VMEM_BYTES = 64 * 1024 * 1024
V7X_SCOPED_VMEM_CAP = 60000 * 1024
COMPILER_SCRATCH_BYTES = 16 * 1024 * 1024


def _params(semantics, block_bytes):
    limit = min(2 * block_bytes + COMPILER_SCRATCH_BYTES, V7X_SCOPED_VMEM_CAP)
    return pltpu.CompilerParams(dimension_semantics=semantics, vmem_limit_bytes=int(limit))


def _nbytes(shape, dtype):
    return math.prod(shape) * jnp.dtype(dtype).itemsize


def _silu(x):
    return x * jax.nn.sigmoid(x)


def _dot(a, b, dims, precision=None):
    return lax.dot_general(a, b, (dims, ((), ())), precision=precision, preferred_element_type=F32)


NN = ((1,), (0,))
NT = ((1,), (1,))
TN = ((0,), (0,))

MIX0_WIDTH = HG_VW + M_DI


def _shared_output(body, n_in, prev):
    if prev is None:
        return body, [], [], {}

    def with_prev(*refs):
        return body(*refs[:n_in], *refs[n_in + 1:])

    return with_prev, [pl.BlockSpec(memory_space=pl.ANY)], [prev], {n_in: 0}


def _mm_body(x_ref, w_ref, o_ref):
    o_ref[...] = _dot(x_ref[...], w_ref[...], NN).astype(o_ref.dtype)


def _matmul(x, w, *, tm, tn, n_cols=None, out_dtype=F32, name):
    m, k = x.shape
    n = n_cols if n_cols is not None else w.shape[1]
    blocks = _nbytes((tm, k), x.dtype) + _nbytes((k, tn), w.dtype) + _nbytes((tm, tn), out_dtype)
    return pl.pallas_call(
        _mm_body,
        grid=(n // tn, m // tm),
        in_specs=[pl.BlockSpec((tm, k), lambda j, i: (i, 0)), pl.BlockSpec((k, tn), lambda j, i: (0, j))],
        out_specs=pl.BlockSpec((tm, tn), lambda j, i: (i, j)),
        out_shape=jax.ShapeDtypeStruct((m, n), out_dtype),
        compiler_params=_params(("arbitrary", "arbitrary"), blocks),
        name=name,
    )(x, w)


def _mm_w32_body(x_ref, w_ref, o_ref, wb_scr):
    @pl.when(pl.program_id(1) == 0)
    def _():
        wb_scr[...] = w_ref[0].astype(BF16)

    o_ref[...] = _dot(x_ref[...], wb_scr[...], NN).astype(o_ref.dtype)


def _mm_w32t_body(x_ref, w_ref, o_ref, wb_scr):
    @pl.when(pl.program_id(1) == 0)
    def _():
        wb_scr[...] = w_ref[0].astype(BF16)

    o_ref[...] = _dot(x_ref[...], wb_scr[...], NT).astype(o_ref.dtype)


def _matmul_w32(x, w, layer, *, tm, tn, n_cols=None, transposed=False, name):
    m, k = x.shape
    n = n_cols if n_cols is not None else w.shape[1 if transposed else 2]
    blocks = _nbytes((tm, k), BF16) + _nbytes((k, tn), F32) + _nbytes((tm, tn), F32) + _nbytes((k, tn), BF16)
    if transposed:
        body, wblock, wspec = _mm_w32t_body, (tn, k), pl.BlockSpec((1, tn, k), lambda j, i: (layer, j, 0))
    else:
        body, wblock, wspec = _mm_w32_body, (k, tn), pl.BlockSpec((1, k, tn), lambda j, i: (layer, 0, j))
    return pl.pallas_call(
        body,
        grid=(n // tn, m // tm),
        in_specs=[pl.BlockSpec((tm, k), lambda j, i: (i, 0)), wspec],
        out_specs=pl.BlockSpec((tm, tn), lambda j, i: (i, j)),
        out_shape=jax.ShapeDtypeStruct((m, n), F32),
        scratch_shapes=[pltpu.VMEM(wblock, BF16)],
        compiler_params=_params(("arbitrary", "arbitrary"), blocks),
        name=name,
    )(x, w)


def _hgrn_gates(q, f, lb):
    fg = lb + (1.0 - lb) * jax.nn.sigmoid(f)
    return _silu(q), 1.0 - fg, jnp.log(fg)


HG_SAFE_EXPONENT = 80.0


def _hgrn_exact_intra(q, k, cum, v_ref, heads, intra_scr, q_scr, k_scr, cum_scr):
    c = cum.shape[0]
    q_scr[...] = q
    k_scr[...] = k
    cum_scr[...] = cum
    unsafe = jnp.max(jnp.abs(cum - cum[c // 2 - 1:c // 2, :])) > HG_SAFE_EXPONENT

    @pl.when(unsafe)
    def _():
        srow = lax.broadcasted_iota(jnp.int32, (c, 1), 0)
        sub = V7X_SUBLANES

        def row_tile(t8, carry):
            base = pl.multiple_of(t8 * sub, sub)
            cum_t = cum_scr[pl.ds(base, sub), :]
            q_t = q_scr[pl.ds(base, sub), :]
            rows_out = [[] for _ in heads]
            for j in range(sub):
                decay = jnp.exp(jnp.minimum(cum_t[j:j + 1, :] - cum_scr[...], 0.0))
                w = jnp.where(srow <= base + j, q_t[j:j + 1, :] * k_scr[...] * decay, 0.0)
                for h, sl in enumerate(heads):
                    coef = jnp.sum(w[:, sl], axis=-1, keepdims=True)
                    rows_out[h].append(jnp.sum(coef * v_ref[:, sl], axis=0, keepdims=True))
            for h, sl in enumerate(heads):
                intra_scr[pl.ds(base, sub), sl] = jnp.concatenate(rows_out[h], axis=0)
            return carry

        lax.fori_loop(0, c // sub, row_tile, 0)

    return unsafe


def _hgrn_prompt_body(q_ref, f_ref, v_ref, g_ref, lb_ref, nw_ref, o_ref, sfin_ref, st_scr, *guard_scr, guarded):
    n = pl.program_id(1)
    c = HG_CHUNK

    @pl.when(n == 0)
    def _():
        st_scr[...] = jnp.zeros_like(st_scr)
        if guarded:
            guard_scr[0][...] = jnp.zeros_like(guard_scr[0])

    row = lax.broadcasted_iota(jnp.int32, (c, c), 0)
    col = lax.broadcasted_iota(jnp.int32, (c, c), 1)
    causal = row >= col
    tri = causal.astype(F32)
    heads = [slice(h * HG_DK, (h + 1) * HG_DK) for h in range(HG_HEADS)]
    q, k, logf = _hgrn_gates(q_ref[...], f_ref[...], lb_ref[...])
    cum = _dot(tri, logf, NN, precision=HIGHEST)
    if guarded:
        unsafe = _hgrn_exact_intra(q, k, cum, v_ref, heads, *guard_scr)
        q, k, cum = (r[...] for r in guard_scr[1:])
    v = v_ref[...].astype(BF16)
    mid = cum[c // 2 - 1:c // 2, :]
    last = cum[c - 1:c, :]
    qm = (q * jnp.exp(cum - mid)).astype(BF16)
    km = (k * jnp.exp(mid - cum)).astype(BF16)
    qi = (q * jnp.exp(cum)).astype(BF16)
    kl = (k * jnp.exp(last - cum)).astype(BF16)
    a_last = jnp.exp(last)
    gate = nw_ref[...] * _silu(g_ref[...])
    scores = [jnp.where(causal, _dot(qm[:, sl], km[:, sl], NT), 0.0).astype(BF16) for sl in heads]
    sts = [st_scr[h] for h in range(HG_HEADS)]
    inter = [_dot(qi[:, sl], sts[h].astype(BF16), NT) for h, sl in enumerate(heads)]
    intra = [_dot(scores[h], v[:, sl], NN) for h, sl in enumerate(heads)]
    for h, sl in enumerate(heads):
        st_scr[h] = a_last[:, sl] * sts[h] + _dot(v[:, sl], kl[:, sl], TN)
    for h, sl in enumerate(heads):
        o = inter[h] + (jnp.where(unsafe, guard_scr[0][:, sl], intra[h]) if guarded else intra[h])
        o = o * lax.rsqrt(jnp.mean(o * o, -1, keepdims=True) + RMS_EPS)
        o_ref[:, sl] = (o * gate[:, sl]).astype(o_ref.dtype)

    @pl.when(n == pl.num_programs(1) - 1)
    def _():
        sfin_ref[0] = st_scr[...]


def _hgrn_lower_bound_is_safe(lb):
    return jnp.min(lb) >= math.exp(-HG_SAFE_EXPONENT / (HG_CHUNK // 2))


def _hgrn_prompt(proj, lb, nw, *, batch, seq, guarded, prev=None):
    c = HG_CHUNK
    nchunk = seq // c
    blk = lambda j: pl.BlockSpec((c, HG_KW), lambda b, n: (b * nchunk + n, j))
    vec = pl.BlockSpec((1, HG_KW), lambda b, n: (0, 0))
    blocks = 4 * _nbytes((c, HG_KW), F32) + _nbytes((c, HG_VW), BF16) + 2 * _nbytes((HG_HEADS, HG_DV, HG_DK), F32)
    guard_scratch = [pltpu.VMEM((c, HG_KW), F32)] * 4 if guarded else []
    body, xspecs, xops, alias = _shared_output(functools.partial(_hgrn_prompt_body, guarded=guarded), 6, prev)
    o, st = pl.pallas_call(
        body,
        grid=(batch, nchunk),
        in_specs=[blk(0), blk(1), blk(2), blk(3), vec, vec] + xspecs,
        out_specs=[pl.BlockSpec((c, HG_VW), lambda b, n: (b * nchunk + n, 0)),
                   pl.BlockSpec((1, HG_HEADS, HG_DV, HG_DK), lambda b, n: (b, 0, 0, 0))],
        out_shape=[jax.ShapeDtypeStruct((proj.shape[0], MIX0_WIDTH), BF16),
                   jax.ShapeDtypeStruct((batch, HG_HEADS, HG_DV, HG_DK), F32)],
        scratch_shapes=[pltpu.VMEM((HG_HEADS, HG_DV, HG_DK), F32)] + guard_scratch,
        input_output_aliases=alias,
        compiler_params=_params(("arbitrary", "arbitrary"), blocks),
        name="hgrn_prompt_guarded" if guarded else "hgrn_prompt",
    )(proj, proj, proj, proj, lb.reshape(1, HG_KW), nw.reshape(1, HG_VW), *xops)
    return o, jnp.swapaxes(st, -1, -2)


HG_SAMPLE_BB = 8


def _hgrn_sample_body(q_ref, f_ref, v_ref, g_ref, lb_ref, nw_ref, s_ref, o_ref, so_ref, *, seq):
    rows = 2 * seq
    assert rows == V7X_SUBLANES
    row = lax.broadcasted_iota(jnp.int32, (rows, rows), 0)
    col = lax.broadcasted_iota(jnp.int32, (rows, rows), 1)
    causal = (row >= col) & ((row // seq) == (col // seq))
    rvec = lax.broadcasted_iota(jnp.int32, (rows, 1), 0)
    r16 = lax.broadcasted_iota(jnp.int32, (2 * rows, HG_DV), 0)
    ones_rows = jnp.where((r16 == rows) | (r16 == rows + 1), 1.0, 0.0).astype(BF16)

    heads = [slice(h * HG_DK, (h + 1) * HG_DK) for h in range(HG_HEADS)]
    in_seq = rvec % seq

    def pair(p, carry):
        r0 = pl.multiple_of(p * rows, rows)
        tile = pl.ds(r0, rows)
        q, k, logf = _hgrn_gates(q_ref[tile, :], f_ref[tile, :], lb_ref[...])
        v = v_ref[tile, :].astype(BF16)
        cum = logf
        step = 1
        while step < seq:
            cum = cum + jnp.where(in_seq >= step, _shift_rows(cum, step), 0.0)
            step *= 2
        ecum = jnp.exp(cum)
        qi = (q * ecum).astype(BF16)
        km = (k / ecum).astype(BF16)
        gate = nw_ref[...] * _silu(g_ref[tile, :])
        v16 = jnp.concatenate([v, jnp.zeros_like(v)], axis=0)
        lhs = []
        for j in range(2):
            mine = (rvec // seq) == j
            last = cum[(j + 1) * seq - 1:(j + 1) * seq, :]
            a = jnp.exp(last)
            a_hi = a.astype(BF16).astype(F32)
            kl = jnp.where(mine, k * jnp.exp(last - cum), 0.0)
            lhs.append(jnp.concatenate([kl, a_hi, a - a_hi, jnp.zeros((rows - 2, HG_KW), F32)], axis=0).astype(BF16))
        scores = [jnp.where(causal, _dot(qi[:, sl], km[:, sl], NT), 0.0).astype(BF16) for sl in heads]
        outs = [_dot(scores[h], v[:, sl], NN) for h, sl in enumerate(heads)]
        for j in range(2):
            b = 2 * p + j
            mine = (rvec // seq) == j
            s0 = [s_ref[b, h] for h in range(HG_HEADS)]
            inter = [_dot(qi[:, sl], s0[h].astype(BF16), NN) for h, sl in enumerate(heads)]
            both = [_dot(lhs[j][:, sl], jnp.concatenate([v16[:, sl], ones_rows], axis=1), TN)
                    for sl in heads]
            for h in range(HG_HEADS):
                outs[h] = outs[h] + jnp.where(mine, inter[h], 0.0)
                so_ref[b, h] = both[h][:, HG_DV:] * s0[h] + both[h][:, :HG_DV]
        for h, sl in enumerate(heads):
            o = outs[h]
            o = o * lax.rsqrt(jnp.mean(o * o, -1, keepdims=True) + RMS_EPS)
            o_ref[tile, sl] = (o * gate[:, sl]).astype(o_ref.dtype)
        return carry

    lax.fori_loop(0, HG_SAMPLE_BB // 2, pair, 0)


def _hgrn_sample(proj, row0, state, lb, nw, *, batch, seq, prev=None):
    bb = HG_SAMPLE_BB
    tr = bb * seq
    blk0 = row0 // tr
    blk = lambda j: pl.BlockSpec((tr, HG_KW), lambda i: (blk0 + i, j))
    vec = pl.BlockSpec((1, HG_KW), lambda i: (0, 0))
    sblk = pl.BlockSpec((bb, HG_HEADS, HG_DK, HG_DV), lambda i: (i, 0, 0, 0))
    blocks = 4 * _nbytes((tr, HG_KW), F32) + 2 * _nbytes((bb, HG_HEADS, HG_DK, HG_DV), F32)
    body, xspecs, xops, alias = _shared_output(functools.partial(_hgrn_sample_body, seq=seq), 7, prev)
    return pl.pallas_call(
        body,
        grid=(batch // bb,),
        in_specs=[blk(0), blk(1), blk(2), blk(3), vec, vec, sblk] + xspecs,
        out_specs=[pl.BlockSpec((tr, HG_VW), lambda i: (blk0 + i, 0)), sblk],
        out_shape=[jax.ShapeDtypeStruct((proj.shape[0], MIX0_WIDTH), BF16),
                   jax.ShapeDtypeStruct(state.shape, F32)],
        input_output_aliases=alias,
        compiler_params=_params(("arbitrary",), blocks),
        name="hgrn_sample",
    )(proj, proj, proj, proj, lb.reshape(1, HG_KW), nw.reshape(1, HG_VW), state, *xops)


def _head_expand_matrix():
    r = jnp.arange(2 * V7X_LANES)[:, None] % V7X_LANES
    c = jnp.arange(M_DI)[None, :] // M_HEADDIM
    return (r == c).astype(BF16)


def _expand_heads(coef, e2):
    hi = coef.astype(BF16)
    lo = (coef - hi.astype(F32)).astype(BF16)
    return _dot(jnp.concatenate([hi, lo], axis=1), e2, NN)


def _softplus(x):
    return jnp.maximum(x, 0.0) + jnp.log(1.0 + jnp.exp(-jnp.abs(x)))


def _group_rmsnorm(y, width):
    outs = []
    for g in range(y.shape[1] // width):
        yg = y[:, g * width:(g + 1) * width]
        outs.append(yg * lax.rsqrt(jnp.mean(yg * yg, -1, keepdims=True) + RMS_EPS))
    return jnp.concatenate(outs, axis=1)


def _ssd_prompt_body(z_ref, x_ref, bc_ref, dt_ref, cwx_ref, cwb_ref, cbx_ref, cbb_ref, dtb_ref, alog_ref,
                     dexp_ref, nw_ref, e2_ref, y_ref, sfin_ref, conv_ref, s_scr, cx_scr, cb_scr):
    n = pl.program_id(1)
    c = M_CHUNK
    tail = V7X_SUBLANES

    @pl.when(n == 0)
    def _():
        s_scr[...] = jnp.zeros_like(s_scr)
        cx_scr[...] = jnp.zeros_like(cx_scr)
        cb_scr[...] = jnp.zeros_like(cb_scr)

    def conv(u, carry_scr, w_ref, b_ref):
        ext = jnp.concatenate([carry_scr[...], u], axis=0)
        acc = b_ref[...] + w_ref[M_CONV - 1:M_CONV, :] * u
        for j in range(1, M_CONV):
            acc = acc + w_ref[M_CONV - 1 - j:M_CONV - j, :] * pltpu.roll(ext, j, 0)[tail:, :]
        carry_scr[...] = u[c - tail:, :]
        return _silu(acc)

    ux = x_ref[...]
    ubc = bc_ref[...]
    xc = conv(ux, cx_scr, cwx_ref, cbx_ref)
    bcc = conv(ubc, cb_scr, cwb_ref, cbb_ref)
    ngn = M_GROUPS * M_STATE

    row = lax.broadcasted_iota(jnp.int32, (c, c), 0)
    col = lax.broadcasted_iota(jnp.int32, (c, c), 1)
    causal = row >= col
    tri = causal.astype(F32)
    e2 = e2_ref[...]

    dtp = _softplus(dt_ref[...] + dtb_ref[...])
    da = dtp * (-jnp.exp(alog_ref[...]))
    cum = _dot(tri, da, NN, precision=HIGHEST)
    cum_t = cum.T
    last = cum[c - 1:c, :]
    xdt = xc * _expand_heads(dtp, e2)
    xend = (xdt * _expand_heads(jnp.exp(last - cum), e2)).astype(BF16)
    ecum = _expand_heads(jnp.exp(cum), e2)
    xdt_b = xdt.astype(BF16)
    gw = M_HPG * M_HEADDIM

    ys = []
    for g in range(M_GROUPS):
        bg = bcc[:, g * M_STATE:(g + 1) * M_STATE].astype(BF16)
        cg = bcc[:, ngn + g * M_STATE:ngn + (g + 1) * M_STATE].astype(BF16)
        cb = _dot(cg, bg, NT)
        sg = s_scr[g * M_HPG:(g + 1) * M_HPG].reshape(gw, M_STATE)
        y_inter = _dot(cg, sg.astype(BF16), NT) * ecum[:, g * gw:(g + 1) * gw]
        upd = _dot(xend[:, g * gw:(g + 1) * gw], bg, TN)
        parts = []
        for hh in range(M_HPG):
            h = g * M_HPG + hh
            decay = jnp.exp(jnp.where(causal, cum[:, h:h + 1] - cum_t[h:h + 1, :], -jnp.inf))
            m = (cb * decay).astype(BF16)
            parts.append(_dot(m, xdt_b[:, h * M_HEADDIM:(h + 1) * M_HEADDIM], NN))
            s_scr[h] = jnp.exp(last[:, h:h + 1]) * s_scr[h] + upd[hh * M_HEADDIM:(hh + 1) * M_HEADDIM, :]
        ys.append(jnp.concatenate(parts, axis=1) + y_inter)
    y = jnp.concatenate(ys, axis=1) + dexp_ref[...] * xc
    y = y * _silu(z_ref[...])
    y_ref[...] = (_group_rmsnorm(y, gw) * nw_ref[...]).astype(y_ref.dtype)

    @pl.when(n == pl.num_programs(1) - 1)
    def _():
        sfin_ref[0] = s_scr[...]
        conv_ref[0, :, 0:M_DI] = ux[c - (M_CONV - 1):, :]
        conv_ref[0, :, M_DI:] = ubc[c - (M_CONV - 1):, :]


def _ssd_prompt(proj, dt, conv_w, conv_b, dt_bias, a_log, d_exp, norm_w, e2, *, batch, seq, prev=None):
    c = M_CHUNK
    nchunk = seq // c
    tok = lambda j: pl.BlockSpec((c, M_DI), lambda b, n: (b * nchunk + n, j))
    const = lambda shape, j=0: pl.BlockSpec(shape, lambda b, n: (0, j))
    blocks = (3 * _nbytes((c, M_DI), F32) + _nbytes((c, M_DI), BF16) + _nbytes((2 * V7X_LANES, M_DI), BF16)
              + 2 * _nbytes((M_HEADS, M_HEADDIM, M_STATE), F32) + 12 * _nbytes((c, M_DI), F32))
    body, xspecs, xops, alias = _shared_output(_ssd_prompt_body, 13, prev)
    return pl.pallas_call(
        body,
        grid=(batch, nchunk),
        in_specs=[tok(4), tok(5), tok(6),
                  pl.BlockSpec((c, V7X_LANES), lambda b, n: (b * nchunk + n, 0)),
                  const((M_CONV, M_DI), 0), const((M_CONV, M_DI), 1), const((1, M_DI), 0), const((1, M_DI), 1),
                  const((1, V7X_LANES)), const((1, V7X_LANES)), const((1, M_DI)), const((1, M_DI)),
                  const((2 * V7X_LANES, M_DI))] + xspecs,
        out_specs=[pl.BlockSpec((c, M_DI), lambda b, n: (b * nchunk + n, HG_VW // M_DI)),
                   pl.BlockSpec((1, M_HEADS, M_HEADDIM, M_STATE), lambda b, n: (b, 0, 0, 0)),
                   pl.BlockSpec((1, M_CONV - 1, M_CONV_DIM), lambda b, n: (b, 0, 0))],
        out_shape=[jax.ShapeDtypeStruct((proj.shape[0], MIX0_WIDTH), BF16),
                   jax.ShapeDtypeStruct((batch, M_HEADS, M_HEADDIM, M_STATE), F32),
                   jax.ShapeDtypeStruct((batch, M_CONV - 1, M_CONV_DIM), F32)],
        scratch_shapes=[pltpu.VMEM((M_HEADS, M_HEADDIM, M_STATE), F32),
                        pltpu.VMEM((V7X_SUBLANES, M_DI), F32), pltpu.VMEM((V7X_SUBLANES, M_DI), F32)],
        input_output_aliases=alias,
        compiler_params=_params(("arbitrary", "arbitrary"), blocks),
        name="ssd_prompt",
    )(proj, proj, proj, dt, conv_w, conv_w, conv_b, conv_b, dt_bias, a_log, d_exp, norm_w, e2, *xops)


SSD_SAMPLE_BB = 8


def _shift_rows(a, j):
    return a if j == 0 else pltpu.roll(a, j, 0)


def _ssd_sample_body(z_ref, x_ref, bc_ref, dt_ref, s_ref, cs_ref, cwx_ref, cwb_ref, cbx_ref, cbb_ref, dtb_ref,
                     alog_ref, dexp_ref, nw_ref, e2_ref, y_ref, so_ref, co_ref, *, seq):
    rows = V7X_SUBLANES
    hist = M_CONV - 1
    assert rows == 2 * seq and hist <= seq and hist <= rows - seq
    rvec = lax.broadcasted_iota(jnp.int32, (rows, 1), 0)
    valid = rvec < seq
    lane = lax.broadcasted_iota(jnp.int32, (1, V7X_LANES), 1)
    r16 = lax.broadcasted_iota(jnp.int32, (2 * rows, M_STATE), 0)
    ones_rows = jnp.where((r16 == rows) | (r16 == rows + 1), 1.0, 0.0).astype(BF16)
    e2 = e2_ref[...]
    a_neg = -jnp.exp(alog_ref[...])
    gw = M_HPG * M_HEADDIM
    ngn = M_GROUPS * M_STATE

    def conv(u8, buf, w_ref, b_ref):
        buf8 = jnp.concatenate([buf, jnp.zeros((rows - hist, buf.shape[1]), F32)], axis=0)
        ext = jnp.where(valid, u8, _shift_rows(buf8, rows - hist))
        acc = b_ref[...] + w_ref[hist:hist + 1, :] * ext
        for j in range(1, M_CONV):
            acc = acc + w_ref[hist - j:hist - j + 1, :] * _shift_rows(ext, j)
        new_hist = _shift_rows(ext, rows - (seq - hist))[0:hist, :]
        return _silu(acc), new_hist

    def one(b, u_x, u_bc, z8, dt8):
        xc, nhx = conv(u_x, cs_ref[b, :, 0:M_DI], cwx_ref, cbx_ref)
        bcc, nhb = conv(u_bc, cs_ref[b, :, M_DI:], cwb_ref, cbb_ref)
        co_ref[b, :, 0:M_DI] = nhx
        co_ref[b, :, M_DI:] = nhb
        dtp = jnp.where(valid, _softplus(dt8 + dtb_ref[...]), 0.0)
        cum = dtp * a_neg
        k = 1
        while k < seq:
            cum = cum + jnp.where(rvec >= k, _shift_rows(cum, k), 0.0)
            k *= 2
        last = cum[seq - 1:seq, :]
        bmat = bcc[:, :ngn]
        cmat = bcc[:, ngn:]
        coefs = [dtp, jnp.exp(cum), jnp.exp(last - cum), jnp.broadcast_to(jnp.exp(last), (rows, V7X_LANES))]
        for j in range(seq):
            prod = cmat * _shift_rows(bmat, j)
            cbh = jnp.zeros((rows, V7X_LANES), F32)
            for g in range(M_GROUPS):
                cbg = jnp.sum(prod[:, g * M_STATE:(g + 1) * M_STATE], axis=-1, keepdims=True)
                cbh = jnp.where((lane // M_HPG) == g, cbg, cbh)
            coefs.append(cbh * jnp.exp(cum - _shift_rows(cum, j)))
        wide = _expand_heads(jnp.concatenate(coefs, axis=0), e2)
        part = lambda n: wide[n * rows:(n + 1) * rows, :]
        xdt = xc * part(0)
        ecum = part(1)
        xs = xdt * part(2)
        dec = part(3)[0:1, :]
        y = dexp_ref[...] * xc
        for j in range(seq):
            y = y + part(4 + j) * _shift_rows(xdt, j)
        dec_hi = dec.astype(BF16).astype(F32)
        lhs = jnp.concatenate([xs, dec_hi, dec - dec_hi, jnp.zeros((rows - 2, M_DI), F32)], axis=0).astype(BF16)
        parts = []
        for g in range(M_GROUPS):
            bg = bmat[:, g * M_STATE:(g + 1) * M_STATE].astype(BF16)
            cg = cmat[:, g * M_STATE:(g + 1) * M_STATE].astype(BF16)
            s0 = s_ref[b, g * M_HPG:(g + 1) * M_HPG].reshape(gw, M_STATE)
            parts.append(_dot(cg, s0.astype(BF16), NT) * ecum[:, g * gw:(g + 1) * gw])
            rhs = jnp.concatenate([jnp.concatenate([bg, jnp.zeros((rows, M_STATE), BF16)], axis=0), ones_rows], axis=1)
            both = _dot(lhs[:, g * gw:(g + 1) * gw], rhs, TN)
            s_new = both[:, M_STATE:] * s0 + both[:, :M_STATE]
            so_ref[b, g * M_HPG:(g + 1) * M_HPG] = s_new.reshape(M_HPG, M_HEADDIM, M_STATE)
        y = (y + jnp.concatenate(parts, axis=1)) * _silu(z8)
        return _group_rmsnorm(y, gw) * nw_ref[...]

    def pair(p, carry):
        r0 = pl.multiple_of(p * rows, rows)
        tiles = [ref[pl.ds(r0, rows), :] for ref in (x_ref, bc_ref, z_ref, dt_ref)]
        ys = []
        for j in range(2):
            ys.append(one(2 * p + j, *[_shift_rows(t, j * (rows - seq)) for t in tiles]))
        y_ref[pl.ds(r0, rows), :] = jnp.where(valid, ys[0], _shift_rows(ys[1], seq)).astype(y_ref.dtype)
        return carry

    lax.fori_loop(0, SSD_SAMPLE_BB // 2, pair, 0)


def _ssd_sample(proj, row0, dt, state, conv_state, conv_w, conv_b, dt_bias, a_log, d_exp, norm_w, e2, *,
                batch, seq, prev=None):
    bb = SSD_SAMPLE_BB
    tr = bb * seq
    tok = lambda j: pl.BlockSpec((tr, M_DI), lambda i: (row0 // tr + i, j))
    const = lambda shape, j=0: pl.BlockSpec(shape, lambda i: (0, j))
    sblk = pl.BlockSpec((bb, M_HEADS, M_HEADDIM, M_STATE), lambda i: (i, 0, 0, 0))
    cblk = pl.BlockSpec((bb, M_CONV - 1, M_CONV_DIM), lambda i: (i, 0, 0))
    blocks = (3 * _nbytes((tr, M_DI), F32) + 2 * _nbytes((bb, M_HEADS, M_HEADDIM, M_STATE), F32)
              + 2 * _nbytes((bb, V7X_SUBLANES, M_CONV_DIM), F32) + _nbytes((2 * V7X_LANES, M_DI), BF16))
    body, xspecs, xops, alias = _shared_output(functools.partial(_ssd_sample_body, seq=seq), 15, prev)
    return pl.pallas_call(
        body,
        grid=(batch // bb,),
        in_specs=[tok(4), tok(5), tok(6),
                  pl.BlockSpec((tr, V7X_LANES), lambda i: (row0 // tr + i, 0)),
                  sblk, cblk,
                  const((M_CONV, M_DI), 0), const((M_CONV, M_DI), 1), const((1, M_DI), 0), const((1, M_DI), 1),
                  const((1, V7X_LANES)), const((1, V7X_LANES)), const((1, M_DI)), const((1, M_DI)),
                  const((2 * V7X_LANES, M_DI))] + xspecs,
        out_specs=[pl.BlockSpec((tr, M_DI), lambda i: (row0 // tr + i, HG_VW // M_DI)), sblk, cblk],
        out_shape=[jax.ShapeDtypeStruct((proj.shape[0], MIX0_WIDTH), BF16),
                   jax.ShapeDtypeStruct(state.shape, F32),
                   jax.ShapeDtypeStruct(conv_state.shape, F32)],
        input_output_aliases=alias,
        compiler_params=_params(("arbitrary",), blocks),
        name="ssd_sample",
    )(proj, proj, proj, dt, state, conv_state, conv_w, conv_w, conv_b, conv_b, dt_bias, a_log, d_exp, norm_w, e2, *xops)


S5_TILES = 8
S5_TILE_IN = D_MODEL // S5_TILES
S5_TILE_ST = S5_STATE // S5_TILES


def _s5_discretize(a_re, a_im, log_dt, b_re, b_im, c_re, c_im):
    lam_re = jnp.minimum(a_re, -1e-4)
    lam_im = a_im
    dt = jnp.exp(log_dt)[:, None]
    mag = jnp.exp(lam_re * dt)
    ab_re = mag * jnp.cos(lam_im * dt)
    ab_im = mag * jnp.sin(lam_im * dt)
    den = lam_re * lam_re + lam_im * lam_im
    zr = ((ab_re - 1.0) * lam_re + ab_im * lam_im) / den
    zi = (ab_im * lam_re - (ab_re - 1.0) * lam_im) / den
    bb_re = zr[..., None] * b_re - zi[..., None] * b_im
    bb_im = zr[..., None] * b_im + zi[..., None] * b_re
    gpt = S5_GROUPS // S5_TILES
    eye = jnp.eye(gpt, dtype=F32)

    def pack_b(bb):
        return jnp.einsum('kgph,gm->kghmp', bb.reshape(S5_TILES, gpt, S5_P, S5_GROUP), eye).reshape(
            S5_TILES, S5_TILE_IN, S5_TILE_ST).astype(BF16)

    def pack_c(c):
        return jnp.einsum('kghp,gm->kgpmh', c.reshape(S5_TILES, gpt, S5_GROUP, S5_P), eye).reshape(
            S5_TILES, S5_TILE_ST, S5_TILE_IN).astype(BF16)

    slab = lambda a: _s5_slab_layout(a.reshape(S5_STATE))
    return slab(ab_re), slab(ab_im), pack_b(bb_re), pack_b(bb_im), pack_c(c_re), pack_c(c_im)


S5_PROMPT_ROWS = 256
S5_SAMPLE_ROWS = 128
S5_NBLK = S5_TILE_ST // V7X_LANES


def _s5_pitch(rows):
    assert rows % V7X_SUBLANES == 0
    return rows + V7X_SUBLANES // 2


def _s5_slab_layout(a):
    lead = a.shape[:-1]
    return jnp.swapaxes(a.reshape(lead + (S5_TILES, S5_NBLK, V7X_LANES)), -3, -2)


def _s5_from_slab_layout(a):
    lead = a.shape[:-3]
    return jnp.swapaxes(a, -3, -2).reshape(lead + (S5_STATE,))


def _s5_fused_body(xb_ref, x_ref, h0r_ref, h0i_ref, ar_ref, ai_ref, wbr_ref, wbi_ref, wcr_ref, wci_ref, d_ref, *rest,
                   n_seq, steps):
    act_ref, fr_ref, fi_ref, bur, bui, cr, ci = rest[-7:]
    hrs, his = bur, bui
    n = pl.program_id(1)
    rows = n_seq * steps
    S5_PITCH = _s5_pitch(rows)
    for s in range(S5_TILES):
        u = xb_ref[:, s * S5_TILE_IN:(s + 1) * S5_TILE_IN]
        br = _dot(u, wbr_ref[s], NN)
        bi = _dot(u, wbi_ref[s], NN)
        for j in range(S5_NBLK):
            bur[j, s * S5_PITCH:s * S5_PITCH + rows, :] = br[:, j * V7X_LANES:(j + 1) * V7X_LANES]
            bui[j, s * S5_PITCH:s * S5_PITCH + rows, :] = bi[:, j * V7X_LANES:(j + 1) * V7X_LANES]

    @pl.when(n == 0)
    def _():
        cr[...] = h0r_ref[...]
        ci[...] = h0i_ref[...]

    ar = [ar_ref[j] for j in range(S5_NBLK)]
    ai = [ai_ref[j] for j in range(S5_NBLK)]

    def seq_body(q, carry):
        def step(t, h):
            hr, hi = h
            idx = pl.ds(q * steps + t, S5_TILES, stride=S5_PITCH)
            nr, ni = [], []
            for j in range(S5_NBLK):
                nr.append(ar[j] * hr[j] - ai[j] * hi[j] + bur[j, idx, :])
                ni.append(ar[j] * hi[j] + ai[j] * hr[j] + bui[j, idx, :])
                hrs[j, idx, :] = nr[j]
                his[j, idx, :] = ni[j]
            return tuple(nr), tuple(ni)

        h0 = (tuple(cr[q, j] for j in range(S5_NBLK)), tuple(ci[q, j] for j in range(S5_NBLK)))
        hr, hi = lax.fori_loop(0, steps, step, h0, unroll=min(steps, 4))
        for j in range(S5_NBLK):
            cr[q, j] = hr[j]
            ci[q, j] = hi[j]
        return carry

    lax.fori_loop(0, n_seq, seq_body, 0)

    for s in range(S5_TILES):
        seg = slice(s * S5_PITCH, s * S5_PITCH + rows)
        hre = jnp.concatenate([hrs[j, seg, :] for j in range(S5_NBLK)], axis=1).astype(BF16)
        him = jnp.concatenate([his[j, seg, :] for j in range(S5_NBLK)], axis=1).astype(BF16)
        y = _dot(hre, wcr_ref[s], NN) - _dot(him, wci_ref[s], NN)
        cs = slice(s * S5_TILE_IN, (s + 1) * S5_TILE_IN)
        act_ref[:, cs] = jax.nn.gelu(y + d_ref[:, cs] * x_ref[:, cs]).astype(act_ref.dtype)

    @pl.when(n == pl.num_programs(1) - 1)
    def _():
        fr_ref[...] = cr[...]
        fi_ref[...] = ci[...]


def _s5_fused(xb, x, h0_re, h0_im, ab_re, ab_im, wb_re, wb_im, wc_re, wc_im, d_skip, *, row0, batch, seq, rows, n_seq,
              prev=None, name):
    t, d = x.shape
    steps = rows // n_seq
    assert n_seq == 1 or steps == seq
    S5_PITCH = _s5_pitch(rows)
    tok = lambda width: pl.BlockSpec((rows, width), lambda i, n: ((row0 + i * n_seq * seq) // rows + n, 0))
    st = pl.BlockSpec((n_seq, S5_NBLK, S5_TILES, V7X_LANES), lambda i, n: (i, 0, 0, 0))
    const = lambda shape: pl.BlockSpec(shape, lambda i, n: (0,) * len(shape), pipeline_mode=pl.Buffered(1))
    slab = pltpu.VMEM((S5_NBLK, S5_TILES * S5_PITCH, V7X_LANES), F32)
    stv = pltpu.VMEM((n_seq, S5_NBLK, S5_TILES, V7X_LANES), F32)
    blocks = (2 * (_nbytes((rows, d), BF16) * 2 + _nbytes((rows, d), F32))
              + 2 * 6 * _nbytes((n_seq, S5_STATE), F32) + 4 * _nbytes(wb_re.shape, BF16)
              + 2 * _nbytes((S5_NBLK, S5_TILES * S5_PITCH, V7X_LANES), F32))
    keep = [] if prev is None else [prev]
    n_in = 11
    return pl.pallas_call(
        functools.partial(_s5_fused_body, n_seq=n_seq, steps=steps),
        grid=(batch // n_seq, seq // steps),
        in_specs=[tok(d), tok(d), st, st, const(ab_re.shape), const(ab_im.shape), const(wb_re.shape),
                  const(wb_im.shape), const(wc_re.shape), const(wc_im.shape), const((1, d))]
                 + [pl.BlockSpec(memory_space=pl.ANY)] * len(keep),
        out_specs=[tok(d), st, st],
        out_shape=[jax.ShapeDtypeStruct((t, d), BF16), jax.ShapeDtypeStruct(h0_re.shape, F32),
                   jax.ShapeDtypeStruct(h0_im.shape, F32)],
        scratch_shapes=[slab, slab, stv, stv],
        input_output_aliases={n_in + k: k for k in range(len(keep))},
        compiler_params=pltpu.CompilerParams(dimension_semantics=("arbitrary", "arbitrary"),
                                             vmem_limit_bytes=int(min(blocks + COMPILER_SCRATCH_BYTES, V7X_SCOPED_VMEM_CAP))),
        name=name,
    )(xb, x, h0_re, h0_im, ab_re, ab_im, wb_re, wb_im, wc_re, wc_im, d_skip.reshape(1, d), *keep)


def _glu_body(a_ref, wa_ref, wb_ref, ba_ref, bb_ref, o_ref, wa_scr, wb_scr):
    @pl.when(pl.program_id(1) == 0)
    def _():
        wa_scr[...] = wa_ref[0].astype(BF16)
        wb_scr[...] = wb_ref[0].astype(BF16)

    a = a_ref[...]
    o_ref[...] = (_dot(a, wa_scr[...], NN) + ba_ref[...]) * jax.nn.sigmoid(_dot(a, wb_scr[...], NN) + bb_ref[...])


def _glu(a, wa, wb, ba, bb, layer, *, tm, tn):
    t, k = a.shape
    n = wa.shape[2]
    wspec = pl.BlockSpec((1, k, tn), lambda j, i: (layer, 0, j))
    bspec = pl.BlockSpec((1, tn), lambda j, i: (layer, j))
    blocks = _nbytes((tm, k), BF16) + 2 * _nbytes((k, tn), F32) + _nbytes((tm, tn), F32) + 2 * _nbytes((k, tn), BF16)
    return pl.pallas_call(
        _glu_body,
        grid=(n // tn, t // tm),
        in_specs=[pl.BlockSpec((tm, k), lambda j, i: (i, 0)), wspec, wspec, bspec, bspec],
        out_specs=pl.BlockSpec((tm, tn), lambda j, i: (i, j)),
        out_shape=jax.ShapeDtypeStruct((t, n), F32),
        scratch_shapes=[pltpu.VMEM((k, tn), BF16)] * 2,
        compiler_params=_params(("arbitrary", "arbitrary"), blocks),
        name="glu",
    )(a, wa, wb, ba, bb)


def _route(x, w_ref, b_ref, e_ref, g_ref):
    w = w_ref[...]
    w_hi = w.astype(BF16)
    w_lo = (w - w_hi.astype(F32)).astype(BF16)
    x_hi = x.astype(BF16)
    x_lo = (x - x_hi.astype(F32)).astype(BF16)
    logits = _dot(w_hi, x_hi, NT) + _dot(w_hi, x_lo, NT) + _dot(w_lo, x_hi, NT) + b_ref[...]
    rows = [logits[e:e + 1, :] for e in range(N_EXPERTS)]
    m = functools.reduce(jnp.maximum, rows)
    ex = [jnp.exp(r - m) for r in rows]
    z = functools.reduce(jnp.add, ex)
    p = [v / z for v in ex]

    def top2_sum(a, b, c, d):
        hi1, lo1, hi2, lo2 = jnp.maximum(a, b), jnp.minimum(a, b), jnp.maximum(c, d), jnp.minimum(c, d)
        return jnp.maximum(hi1, hi2) + jnp.maximum(jnp.minimum(hi1, hi2), jnp.maximum(lo1, lo2))

    assert EXP_PER_GROUP == 4
    best = top2_sum(*p[0:EXP_PER_GROUP])
    gsel = jnp.zeros_like(best, dtype=jnp.int32)
    for g in range(1, N_EXP_GROUPS):
        s = top2_sum(*p[g * EXP_PER_GROUP:(g + 1) * EXP_PER_GROUP])
        better = s > best
        gsel = jnp.where(better, g, gsel)
        best = jnp.where(better, s, best)
    inner = []
    for i in range(EXP_PER_GROUP):
        v = p[i]
        for g in range(1, N_EXP_GROUPS):
            v = jnp.where(gsel == g, p[g * EXP_PER_GROUP + i], v)
        inner.append(v)

    def first_argmax(vals, skip=None):
        bv = bi = None
        for i, v in enumerate(vals):
            v = v if skip is None else jnp.where(skip == i, -jnp.inf, v)
            if bv is None:
                bv, bi = v, jnp.zeros_like(gsel)
            else:
                better = v > bv
                bi = jnp.where(better, i, bi)
                bv = jnp.where(better, v, bv)
        return bv, bi

    p1, i1 = first_argmax(inner)
    p2, i2 = first_argmax(inner, skip=i1)
    tot = p1 + p2
    e1 = gsel * EXP_PER_GROUP + i1
    e2 = gsel * EXP_PER_GROUP + i2
    e_ref[0:1, :] = e1
    e_ref[1:2, :] = e2
    g_ref[0:1, :] = p1 / tot
    g_ref[1:2, :] = p2 / tot


def _ln_router_body(x_ref, x2_ref, mix_ref, g_ref, b_ref, w_ref, br_ref, o_ref, e_ref, gw_ref, *, first_blocks):
    x = x_ref[...] if first_blocks is None else jnp.where(pl.program_id(0) < first_blocks, x_ref[...], x2_ref[...])
    z = ALPHA * x + mix_ref[...]
    mu = jnp.mean(z, -1, keepdims=True)
    zc = z - mu
    var = jnp.mean(zc * zc, -1, keepdims=True)
    y = zc * lax.rsqrt(var + LN_EPS) * g_ref[...] + b_ref[...]
    o_ref[...] = y
    _route(y, w_ref, br_ref, e_ref, gw_ref)


def _res_layernorm_router(x, mix, g, b, w_router, b_router, *, tm, name, x_tail=None):
    t, d = mix.shape
    row = pl.BlockSpec((tm, d), lambda i: (i, 0))
    vec = pl.BlockSpec((1, d), lambda i: (0, 0))
    sel = pl.BlockSpec((TOP_K, tm), lambda i: (0, i))
    if x_tail is None:
        nb, xspecs, xs = None, [row, vec], [x, g.reshape(1, d)]
    else:
        nb = x.shape[0] // tm
        xspecs = [pl.BlockSpec((tm, d), lambda i: (jnp.minimum(i, nb - 1), 0)),
                  pl.BlockSpec((tm, d), lambda i: (jnp.maximum(i - nb, 0), 0))]
        xs = [x, x_tail]
    blocks = 4 * _nbytes((tm, d), F32) + _nbytes((N_EXPERTS, d), F32) + 2 * _nbytes((V7X_SUBLANES, tm), F32)
    return pl.pallas_call(
        functools.partial(_ln_router_body, first_blocks=nb),
        grid=(t // tm,),
        in_specs=xspecs + [row, vec, vec, pl.BlockSpec((N_EXPERTS, d), lambda i: (0, 0)),
                           pl.BlockSpec((N_EXPERTS, 1), lambda i: (0, 0))],
        out_specs=[row, sel, sel],
        out_shape=[jax.ShapeDtypeStruct((t, d), F32), jax.ShapeDtypeStruct((TOP_K, t), jnp.int32),
                   jax.ShapeDtypeStruct((TOP_K, t), F32)],
        compiler_params=_params(("arbitrary",), blocks),
        name=name,
    )(*xs, mix, g.reshape(1, d), b.reshape(1, d), w_router.T, b_router.reshape(N_EXPERTS, 1))


MOE_TILE = 256
COMBINE_TILE = 256


def _moe_plan(eidx):
    k, t = eidx.shape
    ns = k * t
    n_tiles = ns // MOE_TILE + N_EXPERTS
    e_flat = eidx.reshape(ns)
    order = jnp.argsort(e_flat, stable=True).astype(jnp.int32)
    rank = jnp.argsort(order).astype(jnp.int32)
    experts = jnp.arange(N_EXPERTS, dtype=jnp.int32)[None, :]
    slot_is = e_flat[:, None] == experts
    counts = jnp.sum(slot_is.astype(jnp.int32), axis=0)
    start = jnp.cumsum(counts) - counts
    tiles = (counts + MOE_TILE - 1) // MOE_TILE
    tile_end = jnp.cumsum(tiles)
    tile_start = tile_end - tiles
    tile_ids = jnp.arange(n_tiles, dtype=jnp.int32)
    tile_expert = jnp.minimum(jnp.sum((tile_ids[:, None] >= tile_end[None, :]).astype(jnp.int32), axis=1),
                              N_EXPERTS - 1)
    pick = lambda onehot, table: jnp.sum(jnp.where(onehot, table[None, :], 0), axis=1)
    tile_is = tile_expert[:, None] == experts
    tile_first = pick(tile_is, start) + (tile_ids - pick(tile_is, tile_start)) * MOE_TILE
    position_of_slot = pick(slot_is, tile_start) * MOE_TILE + rank - pick(slot_is, start)
    sorted_token = jnp.pad(order % t, (0, MOE_TILE))
    return (tile_expert.astype(jnp.int32), tile_first.astype(jnp.int32), tile_end[-1:].astype(jnp.int32),
            sorted_token.astype(jnp.int32), position_of_slot.astype(jnp.int32))


def _moe_expert_body(te_ref, first_ref, nt_ref, tok_ref, x_hbm, wg_ref, wu_ref, wd_ref, y_ref, xbuf, sem, wg_scr,
                     wu_scr, wd_scr):
    i = pl.program_id(0)
    nt = nt_ref[0]
    tm = MOE_TILE
    sub = V7X_SUBLANES

    def gather_copy(tok, slot, r8, j):
        return pltpu.make_async_copy(x_hbm.at[pl.ds(tok, 1)], xbuf.at[slot, r8, pl.ds(j, 1)], sem.at[slot])

    def start_gather(tile, slot):
        first = first_ref[tile]

        def body(r8, c):
            for j in range(sub):
                gather_copy(tok_ref[first + r8 * sub + j], slot, r8, j).start(priority=j % 2)
            return c

        lax.fori_loop(0, tm // sub, body, 0)

    @pl.when(i == 0)
    def _():
        start_gather(0, 0)

    @pl.when(i + 1 < nt)
    def _():
        start_gather(i + 1, (i + 1) % 2)

    @pl.when(i < nt)
    def _():
        slot = i % 2
        for r8 in range(tm // sub):
            pltpu.make_async_copy(x_hbm.at[pl.ds(0, sub)], xbuf.at[slot, r8], sem.at[slot]).wait()

        @pl.when((i == 0) | (te_ref[i] != te_ref[jnp.maximum(i - 1, 0)]))
        def _():
            wg_scr[...] = wg_ref[0, 0].astype(BF16)
            wu_scr[...] = wu_ref[0, 0].astype(BF16)
            wd_scr[...] = wd_ref[0, 0].astype(BF16)

        x = xbuf[slot].reshape(tm, x_hbm.shape[1]).astype(BF16)
        act = (_silu(_dot(x, wg_scr[...], NN)) * _dot(x, wu_scr[...], NN)).astype(BF16)
        y_ref[...] = _dot(act, wd_scr[...], NN)

    @pl.when(i >= nt)
    def _():
        y_ref[...] = jnp.zeros_like(y_ref)


def _moe_experts(x, plan, w_gate, w_up, w_down, layer):
    tile_expert, tile_first, n_valid, sorted_token, _ = plan
    t, d = x.shape
    f = w_gate.shape[3]
    n_tiles = tile_expert.shape[0]
    wspec = lambda a, b: pl.BlockSpec((1, 1, a, b), lambda i, te, *_: (layer, te[i], 0, 0))
    blocks = (2 * 3 * _nbytes((d, f), F32) + 3 * _nbytes((d, f), BF16) + 2 * _nbytes((MOE_TILE, d), F32)
              + 2 * _nbytes((MOE_TILE, d), F32) + 4 * _nbytes((MOE_TILE, f), F32))
    return pl.pallas_call(
        _moe_expert_body,
        grid_spec=pltpu.PrefetchScalarGridSpec(
            num_scalar_prefetch=4,
            grid=(n_tiles,),
            in_specs=[pl.BlockSpec(memory_space=pl.ANY), wspec(d, f), wspec(d, f), wspec(f, d)],
            out_specs=pl.BlockSpec((MOE_TILE, d), lambda i, *_: (i, 0)),
            scratch_shapes=[pltpu.VMEM((2, MOE_TILE // V7X_SUBLANES, V7X_SUBLANES, d), F32),
                            pltpu.SemaphoreType.DMA((2,)),
                            pltpu.VMEM((d, f), BF16), pltpu.VMEM((d, f), BF16), pltpu.VMEM((f, d), BF16)]),
        out_shape=jax.ShapeDtypeStruct((n_tiles * MOE_TILE, d), F32),
        compiler_params=pltpu.CompilerParams(dimension_semantics=("arbitrary",),
                                             vmem_limit_bytes=int(min(blocks + COMPILER_SCRATCH_BYTES, V7X_SCOPED_VMEM_CAP))),
        name="moe_experts",
    )(tile_expert, tile_first, n_valid, sorted_token, x, w_gate, w_up, w_down)


def _moe_combine_body(pos_ref, x_ref, w_ref, g_ref, b_ref, y_hbm, o_ref, ob_ref, ybuf, sem, *, n_tok, split_blocks):
    i = pl.program_id(0)
    tc = COMBINE_TILE

    sub = V7X_SUBLANES

    def gather_copy(p, slot, k, r8, j):
        return pltpu.make_async_copy(y_hbm.at[pl.ds(p, 1)], ybuf.at[slot, k, r8, pl.ds(j, 1)], sem.at[slot])

    def start_gather(tile, slot):
        def body(r8, c):
            for k in range(TOP_K):
                for j in range(sub):
                    gather_copy(pos_ref[k * n_tok + tile * tc + r8 * sub + j], slot, k, r8, j).start(priority=j % 2)
            return c

        lax.fori_loop(0, tc // sub, body, 0)

    @pl.when(i == 0)
    def _():
        start_gather(0, 0)

    @pl.when(i + 1 < pl.num_programs(0))
    def _():
        start_gather(i + 1, (i + 1) % 2)

    slot = i % 2
    for k in range(TOP_K):
        for r8 in range(tc // sub):
            pltpu.make_async_copy(y_hbm.at[pl.ds(0, sub)], ybuf.at[slot, k, r8], sem.at[slot]).wait()
    z = ALPHA * x_ref[...]
    for k in range(TOP_K):
        z = z + w_ref[:, k:k + 1] * ybuf[slot, k].reshape(tc, x_ref.shape[1])
    mu = jnp.mean(z, -1, keepdims=True)
    zc = z - mu
    var = jnp.mean(zc * zc, -1, keepdims=True)
    y = zc * lax.rsqrt(var + LN_EPS) * g_ref[...] + b_ref[...]
    if split_blocks is None:
        o_ref[...] = y
        ob_ref[...] = y.astype(BF16)
    else:
        @pl.when(i < split_blocks)
        def _():
            o_ref[...] = y

        @pl.when(i >= split_blocks)
        def _():
            ob_ref[...] = y


def _moe_combine_layernorm(x, y_slots, plan, gate_w, g, b, *, name, split=None):
    position_of_slot = plan[4]
    t, d = x.shape
    tc = COMBINE_TILE
    row = pl.BlockSpec((tc, d), lambda i, pos: (i, 0))
    vec = pl.BlockSpec((1, d), lambda i, pos: (0, 0))
    blocks = 2 * (3 * _nbytes((tc, d), F32) + _nbytes((tc, d), BF16)) + 2 * TOP_K * _nbytes((tc, d), F32)
    if split is None:
        sb = None
        out_specs = [row, row]
        out_shape = [jax.ShapeDtypeStruct((t, d), F32), jax.ShapeDtypeStruct((t, d), BF16)]
    else:
        sb = split // tc
        out_specs = [pl.BlockSpec((tc, d), lambda i, pos: (jnp.minimum(i, sb - 1), 0)),
                     pl.BlockSpec((tc, d), lambda i, pos: (jnp.maximum(i - sb, 0), 0))]
        out_shape = [jax.ShapeDtypeStruct((split, d), F32), jax.ShapeDtypeStruct((t - split, d), F32)]
    return pl.pallas_call(
        functools.partial(_moe_combine_body, n_tok=t, split_blocks=sb),
        grid_spec=pltpu.PrefetchScalarGridSpec(
            num_scalar_prefetch=1,
            grid=(t // tc,),
            in_specs=[row, pl.BlockSpec((tc, TOP_K), lambda i, pos: (i, 0)), vec, vec,
                      pl.BlockSpec(memory_space=pl.ANY)],
            out_specs=out_specs,
            scratch_shapes=[pltpu.VMEM((2, TOP_K, tc // V7X_SUBLANES, V7X_SUBLANES, d), F32),
                            pltpu.SemaphoreType.DMA((2,))]),
        out_shape=out_shape,
        compiler_params=pltpu.CompilerParams(dimension_semantics=("arbitrary",),
                                             vmem_limit_bytes=int(min(blocks + COMPILER_SCRATCH_BYTES, V7X_SCOPED_VMEM_CAP))),
        name=name,
    )(position_of_slot, x, gate_w, g.reshape(1, d), b.reshape(1, d), y_slots)


TOKEN_TILE = 1088
IN_PROJ_TN = 1024
LN_TILE = 256


def _moe_layernorm(x, eidx, gate_w, layer, w_gate, w_up, w_down, g, b, split=None):
    plan = _moe_plan(eidx)
    y_slots = _moe_experts(x, plan, w_gate, w_up, w_down, layer)
    return _moe_combine_layernorm(x, y_slots, plan, gate_w.T, g, b, name=f"moe_combine_ln_{layer}", split=split)


def kernel(x_prompt, x_sample, state_hgrn, state_ssm, state_conv, state_s5_re, state_s5_im, w_in0, hg_lb_logits, hg_norm_w, conv_w, conv_b, dt_bias, a_log, m_d, m_norm_w, w_out0, s5_a_re, s5_a_im, s5_log_dt, s5_b_re, s5_b_im, s5_c_re, s5_c_im, s5_d, glu_w_a, glu_b_a, glu_w_b, glu_b_b, w_router, b_router, w_gate, w_up, w_down, ln1_g, ln1_b, ln2_g, ln2_b):
    bp, lp, d = x_prompt.shape
    bs, ls, _ = x_sample.shape
    tp, ts = bp * lp, bs * ls
    tm = TOKEN_TILE
    xp2, xs2 = x_prompt.reshape(tp, d), x_sample.reshape(ts, d)
    x0b = jnp.concatenate([xp2.astype(BF16), xs2.astype(BF16)], axis=0)
    lower_bounds = jnp.cumsum(jax.nn.softmax(hg_lb_logits.astype(F32), axis=0), axis=0)
    pad_lanes = lambda v: jnp.pad(v, (0, V7X_LANES - v.shape[0])).reshape(1, V7X_LANES)

    proj = _matmul_w32(x0b, jnp.swapaxes(w_in0, 1, 2), 0, tm=tm, tn=IN_PROJ_TN, n_cols=IN0_MAIN,
                       transposed=True, name="in_proj")
    w_dt = jnp.pad(w_in0[0, :, IN0_MAIN:], ((0, 0), (0, V7X_LANES - M_HEADS))).astype(BF16)
    dt = _matmul(x0b, w_dt, tm=tm, tn=V7X_LANES, name="dt_proj")
    lb0 = lower_bounds[0]
    e2 = _head_expand_matrix()
    ssd_consts = (conv_w[0], conv_b[0].reshape(1, -1), pad_lanes(dt_bias[0]), pad_lanes(a_log[0]),
                  jnp.repeat(m_d[0], M_HEADDIM).reshape(1, M_DI), m_norm_w[0].reshape(1, M_DI), e2)
    hgrn_prompt = functools.partial(_hgrn_prompt, proj, lb0, hg_norm_w[0], batch=bp, seq=lp)
    mixed, hg_p = lax.cond(_hgrn_lower_bound_is_safe(lb0), functools.partial(hgrn_prompt, guarded=False),
                           functools.partial(hgrn_prompt, guarded=True))
    mixed, hg_s = _hgrn_sample(proj, tp, state_hgrn[0], lb0, hg_norm_w[0], batch=bs, seq=ls, prev=mixed)
    mixed, ssm_p, conv_p = _ssd_prompt(proj, dt, *ssd_consts, batch=bp, seq=lp, prev=mixed)
    mixed, ssm_s, conv_s = _ssd_sample(proj, tp, dt, state_ssm[0], state_conv[0], *ssd_consts, batch=bs, seq=ls,
                                       prev=mixed)
    mix = _matmul_w32(mixed, w_out0, 0, tm=tm, tn=512, name="out_proj")
    x1, eidx, gate_w = _res_layernorm_router(xp2, mix, ln1_g[0], ln1_b[0], w_router, b_router, tm=LN_TILE,
                                             name="ln1_router_0", x_tail=xs2)
    x2, x2b = _moe_layernorm(x1, eidx, gate_w, 0, w_gate, w_up, w_down, ln2_g[0], ln2_b[0])

    ab_re, ab_im, wb_re, wb_im, wc_re, wc_im = _s5_discretize(
        s5_a_re[0], s5_a_im[0], s5_log_dt[0], s5_b_re[0], s5_b_im[0], s5_c_re[0], s5_c_im[0])
    s5_consts = (ab_re, ab_im, wb_re, wb_im, wc_re, wc_im, s5_d[0])
    zeros = jnp.zeros((bp, S5_NBLK, S5_TILES, V7X_LANES), F32)
    act, s5r_p, s5i_p = _s5_fused(x2b, x2, zeros, zeros, *s5_consts, row0=0, batch=bp, seq=lp,
                                  rows=S5_PROMPT_ROWS, n_seq=1, name="s5_prompt")
    act, s5r_s, s5i_s = _s5_fused(x2b, x2, _s5_slab_layout(state_s5_re[0].reshape(bs, S5_STATE)),
                                  _s5_slab_layout(state_s5_im[0].reshape(bs, S5_STATE)), *s5_consts,
                                  row0=tp, batch=bs, seq=ls, rows=S5_SAMPLE_ROWS, n_seq=S5_SAMPLE_ROWS // ls,
                                  prev=act, name="s5_sample")
    mix1 = _glu(act, glu_w_a, glu_w_b, glu_b_a, glu_b_b, 0, tm=tm, tn=512)
    x3, eidx, gate_w = _res_layernorm_router(x2, mix1, ln1_g[1], ln1_b[1], w_router, b_router, tm=LN_TILE,
                                             name="ln1_router_1")
    y_p, y_s = _moe_layernorm(x3, eidx, gate_w, 1, w_gate, w_up, w_down, ln2_g[1], ln2_b[1], split=tp)

    s5_state = lambda a, b: _s5_from_slab_layout(a).reshape(1, b, S5_GROUPS, S5_P)
    return (y_p.reshape(bp, lp, d), y_s.reshape(bs, ls, d),
            hg_p[None], hg_s[None], ssm_p[None], ssm_s[None], conv_p[None], conv_s[None],
            s5_state(s5r_p, bp), s5_state(s5r_s, bs), s5_state(s5i_p, bp), s5_state(s5i_s, bs))
```

```python
import functools
import math

import jax
import jax.numpy as jnp
from jax import lax
from jax.experimental import pallas as pl
from jax.experimental.pallas import tpu as pltpu

F32 = jnp.float32
BF16 = jnp.bfloat16
HIGHEST = lax.Precision.HIGHEST

D_MODEL = 2048
DEPTH = 2
HG_HEADS = 16
HG_DK = 128
HG_DV = 128
HG_KW = HG_HEADS * HG_DK
HG_VW = HG_HEADS * HG_DV
HG_CHUNK = 64
M_DI = 2048
M_HEADDIM = 64
M_HEADS = 32
M_GROUPS = 8
M_HPG = 4
M_STATE = 128
M_CONV = 4
M_CONV_DIM = M_DI + 2 * M_GROUPS * M_STATE
M_CHUNK = 128
IN0_MAIN = 2 * HG_KW + 2 * HG_VW + M_DI + M_CONV_DIM
S5_GROUP = 16
S5_GROUPS = 128
S5_P = 64
S5_STATE = S5_GROUPS * S5_P
N_EXPERTS = 16
N_EXP_GROUPS = 4
EXP_PER_GROUP = 4
TOP_K = 2
D_EXPERT = 512
ALPHA = (2 * DEPTH) ** 0.25
LN_EPS = 1e-5
RMS_EPS = 1e-6

V7X_LANES = 128
V7X_SUBLANES = 8
V7X_VMEM_BYTES = 64 * 1024 * 1024
V7X_SCOPED_VMEM_CAP = 60000 * 1024
COMPILER_SCRATCH_BYTES = 16 * 1024 * 1024


def _params(semantics, block_bytes):
    limit = min(2 * block_bytes + COMPILER_SCRATCH_BYTES, V7X_SCOPED_VMEM_CAP)
    return pltpu.CompilerParams(dimension_semantics=semantics, vmem_limit_bytes=int(limit))


def _nbytes(shape, dtype):
    return math.prod(shape) * jnp.dtype(dtype).itemsize


def _silu(x):
    return x * jax.nn.sigmoid(x)


def _dot(a, b, dims, precision=None):
    return lax.dot_general(a, b, (dims, ((), ())), precision=precision, preferred_element_type=F32)


NN = ((1,), (0,))
NT = ((1,), (1,))
TN = ((0,), (0,))

MIX0_WIDTH = HG_VW + M_DI


def _shared_output(body, n_in, prev):
    if prev is None:
        return body, [], [], {}

    def with_prev(*refs):
        return body(*refs[:n_in], *refs[n_in + 1:])

    return with_prev, [pl.BlockSpec(memory_space=pl.ANY)], [prev], {n_in: 0}


def _join_bf16_body(a_ref, b_ref, o_ref, *, first_blocks):
    o_ref[...] = jnp.where(pl.program_id(0) < first_blocks, a_ref[...], b_ref[...]).astype(o_ref.dtype)


def _join_bf16(a, b, *, tm):
    d = a.shape[1]
    na, nb = a.shape[0] // tm, b.shape[0] // tm
    blocks = 2 * _nbytes((tm, d), F32) + _nbytes((tm, d), BF16)
    return pl.pallas_call(
        functools.partial(_join_bf16_body, first_blocks=na),
        grid=(na + nb,),
        in_specs=[pl.BlockSpec((tm, d), lambda i: (jnp.minimum(i, na - 1), 0)),
                  pl.BlockSpec((tm, d), lambda i: (jnp.maximum(i - na, 0), 0))],
        out_specs=pl.BlockSpec((tm, d), lambda i: (i, 0)),
        out_shape=jax.ShapeDtypeStruct((a.shape[0] + b.shape[0], d), BF16),
        compiler_params=_params(("arbitrary",), blocks),
        name="join_bf16",
    )(a, b)


def _mm_body(x_ref, w_ref, o_ref):
    o_ref[...] = _dot(x_ref[...], w_ref[...], NN).astype(o_ref.dtype)


def _matmul(x, w, *, tm, tn, n_cols=None, out_dtype=F32, name):
    m, k = x.shape
    n = n_cols if n_cols is not None else w.shape[1]
    blocks = _nbytes((tm, k), x.dtype) + _nbytes((k, tn), w.dtype) + _nbytes((tm, tn), out_dtype)
    return pl.pallas_call(
        _mm_body,
        grid=(n // tn, m // tm),
        in_specs=[pl.BlockSpec((tm, k), lambda j, i: (i, 0)), pl.BlockSpec((k, tn), lambda j, i: (0, j))],
        out_specs=pl.BlockSpec((tm, tn), lambda j, i: (i, j)),
        out_shape=jax.ShapeDtypeStruct((m, n), out_dtype),
        compiler_params=_params(("arbitrary", "arbitrary"), blocks),
        name=name,
    )(x, w)


def _mm_w32_body(x_ref, w_ref, o_ref, wb_scr):
    @pl.when(pl.program_id(1) == 0)
    def _():
        wb_scr[...] = w_ref[0].astype(BF16)

    o_ref[...] = _dot(x_ref[...], wb_scr[...], NN).astype(o_ref.dtype)


def _mm_w32t_body(x_ref, w_ref, o_ref, wb_scr):
    @pl.when(pl.program_id(1) == 0)
    def _():
        wb_scr[...] = w_ref[0].astype(BF16)

    o_ref[...] = _dot(x_ref[...], wb_scr[...], NT).astype(o_ref.dtype)


def _matmul_w32(x, w, layer, *, tm, tn, n_cols=None, transposed=False, name):
    m, k = x.shape
    n = n_cols if n_cols is not None else w.shape[1 if transposed else 2]
    blocks = _nbytes((tm, k), BF16) + _nbytes((k, tn), F32) + _nbytes((tm, tn), F32) + _nbytes((k, tn), BF16)
    if transposed:
        body, wblock, wspec = _mm_w32t_body, (tn, k), pl.BlockSpec((1, tn, k), lambda j, i: (layer, j, 0))
    else:
        body, wblock, wspec = _mm_w32_body, (k, tn), pl.BlockSpec((1, k, tn), lambda j, i: (layer, 0, j))
    return pl.pallas_call(
        body,
        grid=(n // tn, m // tm),
        in_specs=[pl.BlockSpec((tm, k), lambda j, i: (i, 0)), wspec],
        out_specs=pl.BlockSpec((tm, tn), lambda j, i: (i, j)),
        out_shape=jax.ShapeDtypeStruct((m, n), F32),
        scratch_shapes=[pltpu.VMEM(wblock, BF16)],
        compiler_params=_params(("arbitrary", "arbitrary"), blocks),
        name=name,
    )(x, w)


def _hgrn_gates(q, f, lb):
    fg = lb + (1.0 - lb) * jax.nn.sigmoid(f)
    return _silu(q), 1.0 - fg, jnp.log(fg)


HG_SAFE_EXPONENT = 80.0


def _hgrn_exact_intra(q, k, cum, v_ref, heads, intra_scr, q_scr, k_scr, cum_scr):
    c = cum.shape[0]
    q_scr[...] = q
    k_scr[...] = k
    cum_scr[...] = cum
    unsafe = jnp.max(jnp.abs(cum - cum[c // 2 - 1:c // 2, :])) > HG_SAFE_EXPONENT

    @pl.when(unsafe)
    def _():
        srow = lax.broadcasted_iota(jnp.int32, (c, 1), 0)
        sub = V7X_SUBLANES

        def row_tile(t8, carry):
            base = pl.multiple_of(t8 * sub, sub)
            cum_t = cum_scr[pl.ds(base, sub), :]
            q_t = q_scr[pl.ds(base, sub), :]
            rows_out = [[] for _ in heads]
            for j in range(sub):
                decay = jnp.exp(jnp.minimum(cum_t[j:j + 1, :] - cum_scr[...], 0.0))
                w = jnp.where(srow <= base + j, q_t[j:j + 1, :] * k_scr[...] * decay, 0.0)
                for h, sl in enumerate(heads):
                    coef = jnp.sum(w[:, sl], axis=-1, keepdims=True)
                    rows_out[h].append(jnp.sum(coef * v_ref[:, sl], axis=0, keepdims=True))
            for h, sl in enumerate(heads):
                intra_scr[pl.ds(base, sub), sl] = jnp.concatenate(rows_out[h], axis=0)
            return carry

        lax.fori_loop(0, c // sub, row_tile, 0)

    return unsafe


def _hgrn_prompt_body(q_ref, f_ref, v_ref, g_ref, lb_ref, nw_ref, o_ref, sfin_ref, st_scr, *guard_scr, guarded):
    n = pl.program_id(1)
    c = HG_CHUNK

    @pl.when(n == 0)
    def _():
        st_scr[...] = jnp.zeros_like(st_scr)
        if guarded:
            guard_scr[0][...] = jnp.zeros_like(guard_scr[0])

    row = lax.broadcasted_iota(jnp.int32, (c, c), 0)
    col = lax.broadcasted_iota(jnp.int32, (c, c), 1)
    causal = row >= col
    tri = causal.astype(F32)
    heads = [slice(h * HG_DK, (h + 1) * HG_DK) for h in range(HG_HEADS)]
    q, k, logf = _hgrn_gates(q_ref[...], f_ref[...], lb_ref[...])
    cum = _dot(tri, logf, NN, precision=HIGHEST)
    if guarded:
        unsafe = _hgrn_exact_intra(q, k, cum, v_ref, heads, *guard_scr)
        q, k, cum = (r[...] for r in guard_scr[1:])
    v = v_ref[...].astype(BF16)
    mid = cum[c // 2 - 1:c // 2, :]
    last = cum[c - 1:c, :]
    qm = (q * jnp.exp(cum - mid)).astype(BF16)
    km = (k * jnp.exp(mid - cum)).astype(BF16)
    qi = (q * jnp.exp(cum)).astype(BF16)
    kl = (k * jnp.exp(last - cum)).astype(BF16)
    a_last = jnp.exp(last)
    gate = nw_ref[...] * _silu(g_ref[...])
    scores = [jnp.where(causal, _dot(qm[:, sl], km[:, sl], NT), 0.0).astype(BF16) for sl in heads]
    sts = [st_scr[h] for h in range(HG_HEADS)]
    inter = [_dot(qi[:, sl], sts[h].astype(BF16), NT) for h, sl in enumerate(heads)]
    intra = [_dot(scores[h], v[:, sl], NN) for h, sl in enumerate(heads)]
    for h, sl in enumerate(heads):
        st_scr[h] = a_last[:, sl] * sts[h] + _dot(v[:, sl], kl[:, sl], TN)
    for h, sl in enumerate(heads):
        o = inter[h] + (jnp.where(unsafe, guard_scr[0][:, sl], intra[h]) if guarded else intra[h])
        o = o * lax.rsqrt(jnp.mean(o * o, -1, keepdims=True) + RMS_EPS)
        o_ref[:, sl] = (o * gate[:, sl]).astype(o_ref.dtype)

    @pl.when(n == pl.num_programs(1) - 1)
    def _():
        sfin_ref[0] = st_scr[...]


def _hgrn_lower_bound_is_safe(lb):
    return jnp.min(lb) >= math.exp(-HG_SAFE_EXPONENT / (HG_CHUNK // 2))


def _hgrn_prompt(proj, lb, nw, *, batch, seq, guarded, prev=None):
    c = HG_CHUNK
    nchunk = seq // c
    blk = lambda j: pl.BlockSpec((c, HG_KW), lambda b, n: (b * nchunk + n, j))
    vec = pl.BlockSpec((1, HG_KW), lambda b, n: (0, 0))
    blocks = 4 * _nbytes((c, HG_KW), F32) + _nbytes((c, HG_VW), BF16) + 2 * _nbytes((HG_HEADS, HG_DV, HG_DK), F32)
    guard_scratch = [pltpu.VMEM((c, HG_KW), F32)] * 4 if guarded else []
    body, xspecs, xops, alias = _shared_output(functools.partial(_hgrn_prompt_body, guarded=guarded), 6, prev)
    o, st = pl.pallas_call(
        body,
        grid=(batch, nchunk),
        in_specs=[blk(0), blk(1), blk(2), blk(3), vec, vec] + xspecs,
        out_specs=[pl.BlockSpec((c, HG_VW), lambda b, n: (b * nchunk + n, 0)),
                   pl.BlockSpec((1, HG_HEADS, HG_DV, HG_DK), lambda b, n: (b, 0, 0, 0))],
        out_shape=[jax.ShapeDtypeStruct((proj.shape[0], MIX0_WIDTH), BF16),
                   jax.ShapeDtypeStruct((batch, HG_HEADS, HG_DV, HG_DK), F32)],
        scratch_shapes=[pltpu.VMEM((HG_HEADS, HG_DV, HG_DK), F32)] + guard_scratch,
        input_output_aliases=alias,
        compiler_params=_params(("arbitrary", "arbitrary"), blocks),
        name="hgrn_prompt_guarded" if guarded else "hgrn_prompt",
    )(proj, proj, proj, proj, lb.reshape(1, HG_KW), nw.reshape(1, HG_VW), *xops)
    return o, jnp.swapaxes(st, -1, -2)


HG_SAMPLE_BB = 8


def _hgrn_sample_body(q_ref, f_ref, v_ref, g_ref, lb_ref, nw_ref, s_ref, o_ref, so_ref, *, seq):
    rows = 2 * seq
    assert rows == V7X_SUBLANES
    row = lax.broadcasted_iota(jnp.int32, (rows, rows), 0)
    col = lax.broadcasted_iota(jnp.int32, (rows, rows), 1)
    causal = (row >= col) & ((row // seq) == (col // seq))
    rvec = lax.broadcasted_iota(jnp.int32, (rows, 1), 0)
    r16 = lax.broadcasted_iota(jnp.int32, (2 * rows, HG_DV), 0)
    ones_rows = jnp.where((r16 == rows) | (r16 == rows + 1), 1.0, 0.0).astype(BF16)

    heads = [slice(h * HG_DK, (h + 1) * HG_DK) for h in range(HG_HEADS)]
    in_seq = rvec % seq

    def pair(p, carry):
        r0 = pl.multiple_of(p * rows, rows)
        tile = pl.ds(r0, rows)
        q, k, logf = _hgrn_gates(q_ref[tile, :], f_ref[tile, :], lb_ref[...])
        v = v_ref[tile, :].astype(BF16)
        cum = logf
        step = 1
        while step < seq:
            cum = cum + jnp.where(in_seq >= step, _shift_rows(cum, step), 0.0)
            step *= 2
        ecum = jnp.exp(cum)
        qi = (q * ecum).astype(BF16)
        km = (k / ecum).astype(BF16)
        gate = nw_ref[...] * _silu(g_ref[tile, :])
        v16 = jnp.concatenate([v, jnp.zeros_like(v)], axis=0)
        lhs = []
        for j in range(2):
            mine = (rvec // seq) == j
            last = cum[(j + 1) * seq - 1:(j + 1) * seq, :]
            a = jnp.exp(last)
            a_hi = a.astype(BF16).astype(F32)
            kl = jnp.where(mine, k * jnp.exp(last - cum), 0.0)
            lhs.append(jnp.concatenate([kl, a_hi, a - a_hi, jnp.zeros((rows - 2, HG_KW), F32)], axis=0).astype(BF16))
        scores = [jnp.where(causal, _dot(qi[:, sl], km[:, sl], NT), 0.0).astype(BF16) for sl in heads]
        outs = [_dot(scores[h], v[:, sl], NN) for h, sl in enumerate(heads)]
        for j in range(2):
            b = 2 * p + j
            mine = (rvec // seq) == j
            s0 = [s_ref[b, h] for h in range(HG_HEADS)]
            inter = [_dot(qi[:, sl], s0[h].astype(BF16), NN) for h, sl in enumerate(heads)]
            both = [_dot(lhs[j][:, sl], jnp.concatenate([v16[:, sl], ones_rows], axis=1), TN)
                    for sl in heads]
            for h in range(HG_HEADS):
                outs[h] = outs[h] + jnp.where(mine, inter[h], 0.0)
                so_ref[b, h] = both[h][:, HG_DV:] * s0[h] + both[h][:, :HG_DV]
        for h, sl in enumerate(heads):
            o = outs[h]
            o = o * lax.rsqrt(jnp.mean(o * o, -1, keepdims=True) + RMS_EPS)
            o_ref[tile, sl] = (o * gate[:, sl]).astype(o_ref.dtype)
        return carry

    lax.fori_loop(0, HG_SAMPLE_BB // 2, pair, 0)


def _hgrn_sample(proj, row0, state, lb, nw, *, batch, seq, prev=None):
    bb = HG_SAMPLE_BB
    tr = bb * seq
    blk0 = row0 // tr
    blk = lambda j: pl.BlockSpec((tr, HG_KW), lambda i: (blk0 + i, j))
    vec = pl.BlockSpec((1, HG_KW), lambda i: (0, 0))
    sblk = pl.BlockSpec((bb, HG_HEADS, HG_DK, HG_DV), lambda i: (i, 0, 0, 0))
    blocks = 4 * _nbytes((tr, HG_KW), F32) + 2 * _nbytes((bb, HG_HEADS, HG_DK, HG_DV), F32)
    body, xspecs, xops, alias = _shared_output(functools.partial(_hgrn_sample_body, seq=seq), 7, prev)
    return pl.pallas_call(
        body,
        grid=(batch // bb,),
        in_specs=[blk(0), blk(1), blk(2), blk(3), vec, vec, sblk] + xspecs,
        out_specs=[pl.BlockSpec((tr, HG_VW), lambda i: (blk0 + i, 0)), sblk],
        out_shape=[jax.ShapeDtypeStruct((proj.shape[0], MIX0_WIDTH), BF16),
                   jax.ShapeDtypeStruct(state.shape, F32)],
        input_output_aliases=alias,
        compiler_params=_params(("arbitrary",), blocks),
        name="hgrn_sample",
    )(proj, proj, proj, proj, lb.reshape(1, HG_KW), nw.reshape(1, HG_VW), state, *xops)


def _head_expand_matrix():
    r = jnp.arange(2 * V7X_LANES)[:, None] % V7X_LANES
    c = jnp.arange(M_DI)[None, :] // M_HEADDIM
    return (r == c).astype(BF16)


def _expand_heads(coef, e2):
    hi = coef.astype(BF16)
    lo = (coef - hi.astype(F32)).astype(BF16)
    return _dot(jnp.concatenate([hi, lo], axis=1), e2, NN)


def _softplus(x):
    return jnp.maximum(x, 0.0) + jnp.log(1.0 + jnp.exp(-jnp.abs(x)))


def _group_rmsnorm(y, width):
    outs = []
    for g in range(y.shape[1] // width):
        yg = y[:, g * width:(g + 1) * width]
        outs.append(yg * lax.rsqrt(jnp.mean(yg * yg, -1, keepdims=True) + RMS_EPS))
    return jnp.concatenate(outs, axis=1)


def _ssd_prompt_body(z_ref, x_ref, bc_ref, dt_ref, cwx_ref, cwb_ref, cbx_ref, cbb_ref, dtb_ref, alog_ref,
                     dexp_ref, nw_ref, e2_ref, y_ref, sfin_ref, conv_ref, s_scr, cx_scr, cb_scr):
    n = pl.program_id(1)
    c = M_CHUNK
    tail = V7X_SUBLANES

    @pl.when(n == 0)
    def _():
        s_scr[...] = jnp.zeros_like(s_scr)
        cx_scr[...] = jnp.zeros_like(cx_scr)
        cb_scr[...] = jnp.zeros_like(cb_scr)

    def conv(u, carry_scr, w_ref, b_ref):
        ext = jnp.concatenate([carry_scr[...], u], axis=0)
        acc = b_ref[...] + w_ref[M_CONV - 1:M_CONV, :] * u
        for j in range(1, M_CONV):
            acc = acc + w_ref[M_CONV - 1 - j:M_CONV - j, :] * pltpu.roll(ext, j, 0)[tail:, :]
        carry_scr[...] = u[c - tail:, :]
        return _silu(acc)

    ux = x_ref[...]
    ubc = bc_ref[...]
    xc = conv(ux, cx_scr, cwx_ref, cbx_ref)
    bcc = conv(ubc, cb_scr, cwb_ref, cbb_ref)
    ngn = M_GROUPS * M_STATE

    row = lax.broadcasted_iota(jnp.int32, (c, c), 0)
    col = lax.broadcasted_iota(jnp.int32, (c, c), 1)
    causal = row >= col
    tri = causal.astype(F32)
    e2 = e2_ref[...]

    dtp = _softplus(dt_ref[...] + dtb_ref[...])
    da = dtp * (-jnp.exp(alog_ref[...]))
    cum = _dot(tri, da, NN, precision=HIGHEST)
    cum_t = cum.T
    last = cum[c - 1:c, :]
    xdt = xc * _expand_heads(dtp, e2)
    xend = (xdt * _expand_heads(jnp.exp(last - cum), e2)).astype(BF16)
    ecum = _expand_heads(jnp.exp(cum), e2)
    xdt_b = xdt.astype(BF16)
    gw = M_HPG * M_HEADDIM

    ys = []
    for g in range(M_GROUPS):
        bg = bcc[:, g * M_STATE:(g + 1) * M_STATE].astype(BF16)
        cg = bcc[:, ngn + g * M_STATE:ngn + (g + 1) * M_STATE].astype(BF16)
        cb = _dot(cg, bg, NT)
        sg = s_scr[g * M_HPG:(g + 1) * M_HPG].reshape(gw, M_STATE)
        y_inter = _dot(cg, sg.astype(BF16), NT) * ecum[:, g * gw:(g + 1) * gw]
        upd = _dot(xend[:, g * gw:(g + 1) * gw], bg, TN)
        parts = []
        for hh in range(M_HPG):
            h = g * M_HPG + hh
            decay = jnp.exp(jnp.where(causal, cum[:, h:h + 1] - cum_t[h:h + 1, :], -jnp.inf))
            m = (cb * decay).astype(BF16)
            parts.append(_dot(m, xdt_b[:, h * M_HEADDIM:(h + 1) * M_HEADDIM], NN))
            s_scr[h] = jnp.exp(last[:, h:h + 1]) * s_scr[h] + upd[hh * M_HEADDIM:(hh + 1) * M_HEADDIM, :]
        ys.append(jnp.concatenate(parts, axis=1) + y_inter)
    y = jnp.concatenate(ys, axis=1) + dexp_ref[...] * xc
    y = y * _silu(z_ref[...])
    y_ref[...] = (_group_rmsnorm(y, gw) * nw_ref[...]).astype(y_ref.dtype)

    @pl.when(n == pl.num_programs(1) - 1)
    def _():
        sfin_ref[0] = s_scr[...]
        conv_ref[0, :, 0:M_DI] = ux[c - (M_CONV - 1):, :]
        conv_ref[0, :, M_DI:] = ubc[c - (M_CONV - 1):, :]


def _ssd_prompt(proj, dt, conv_w, conv_b, dt_bias, a_log, d_exp, norm_w, e2, *, batch, seq, prev=None):
    c = M_CHUNK
    nchunk = seq // c
    tok = lambda j: pl.BlockSpec((c, M_DI), lambda b, n: (b * nchunk + n, j))
    const = lambda shape, j=0: pl.BlockSpec(shape, lambda b, n: (0, j))
    blocks = (3 * _nbytes((c, M_DI), F32) + _nbytes((c, M_DI), BF16) + _nbytes((2 * V7X_LANES, M_DI), BF16)
              + 2 * _nbytes((M_HEADS, M_HEADDIM, M_STATE), F32) + 12 * _nbytes((c, M_DI), F32))
    body, xspecs, xops, alias = _shared_output(_ssd_prompt_body, 13, prev)
    return pl.pallas_call(
        body,
        grid=(batch, nchunk),
        in_specs=[tok(4), tok(5), tok(6),
                  pl.BlockSpec((c, V7X_LANES), lambda b, n: (b * nchunk + n, 0)),
                  const((M_CONV, M_DI), 0), const((M_CONV, M_DI), 1), const((1, M_DI), 0), const((1, M_DI), 1),
                  const((1, V7X_LANES)), const((1, V7X_LANES)), const((1, M_DI)), const((1, M_DI)),
                  const((2 * V7X_LANES, M_DI))] + xspecs,
        out_specs=[pl.BlockSpec((c, M_DI), lambda b, n: (b * nchunk + n, HG_VW // M_DI)),
                   pl.BlockSpec((1, M_HEADS, M_HEADDIM, M_STATE), lambda b, n: (b, 0, 0, 0)),
                   pl.BlockSpec((1, M_CONV - 1, M_CONV_DIM), lambda b, n: (b, 0, 0))],
        out_shape=[jax.ShapeDtypeStruct((proj.shape[0], MIX0_WIDTH), BF16),
                   jax.ShapeDtypeStruct((batch, M_HEADS, M_HEADDIM, M_STATE), F32),
                   jax.ShapeDtypeStruct((batch, M_CONV - 1, M_CONV_DIM), F32)],
        scratch_shapes=[pltpu.VMEM((M_HEADS, M_HEADDIM, M_STATE), F32),
                        pltpu.VMEM((V7X_SUBLANES, M_DI), F32), pltpu.VMEM((V7X_SUBLANES, M_DI), F32)],
        input_output_aliases=alias,
        compiler_params=_params(("arbitrary", "arbitrary"), blocks),
        name="ssd_prompt",
    )(proj, proj, proj, dt, conv_w, conv_w, conv_b, conv_b, dt_bias, a_log, d_exp, norm_w, e2, *xops)


SSD_SAMPLE_BB = 8


def _shift_rows(a, j):
    return a if j == 0 else pltpu.roll(a, j, 0)


def _ssd_sample_body(z_ref, x_ref, bc_ref, dt_ref, s_ref, cs_ref, cwx_ref, cwb_ref, cbx_ref, cbb_ref, dtb_ref,
                     alog_ref, dexp_ref, nw_ref, e2_ref, y_ref, so_ref, co_ref, *, seq):
    rows = V7X_SUBLANES
    hist = M_CONV - 1
    assert rows == 2 * seq and hist <= seq and hist <= rows - seq
    rvec = lax.broadcasted_iota(jnp.int32, (rows, 1), 0)
    valid = rvec < seq
    lane = lax.broadcasted_iota(jnp.int32, (1, V7X_LANES), 1)
    r16 = lax.broadcasted_iota(jnp.int32, (2 * rows, M_STATE), 0)
    ones_rows = jnp.where((r16 == rows) | (r16 == rows + 1), 1.0, 0.0).astype(BF16)
    e2 = e2_ref[...]
    a_neg = -jnp.exp(alog_ref[...])
    gw = M_HPG * M_HEADDIM
    ngn = M_GROUPS * M_STATE

    def conv(u8, buf, w_ref, b_ref):
        buf8 = jnp.concatenate([buf, jnp.zeros((rows - hist, buf.shape[1]), F32)], axis=0)
        ext = jnp.where(valid, u8, _shift_rows(buf8, rows - hist))
        acc = b_ref[...] + w_ref[hist:hist + 1, :] * ext
        for j in range(1, M_CONV):
            acc = acc + w_ref[hist - j:hist - j + 1, :] * _shift_rows(ext, j)
        new_hist = _shift_rows(ext, rows - (seq - hist))[0:hist, :]
        return _silu(acc), new_hist

    def one(b, u_x, u_bc, z8, dt8):
        xc, nhx = conv(u_x, cs_ref[b, :, 0:M_DI], cwx_ref, cbx_ref)
        bcc, nhb = conv(u_bc, cs_ref[b, :, M_DI:], cwb_ref, cbb_ref)
        co_ref[b, :, 0:M_DI] = nhx
        co_ref[b, :, M_DI:] = nhb
        dtp = jnp.where(valid, _softplus(dt8 + dtb_ref[...]), 0.0)
        cum = dtp * a_neg
        k = 1
        while k < seq:
            cum = cum + jnp.where(rvec >= k, _shift_rows(cum, k), 0.0)
            k *= 2
        last = cum[seq - 1:seq, :]
        bmat = bcc[:, :ngn]
        cmat = bcc[:, ngn:]
        coefs = [dtp, jnp.exp(cum), jnp.exp(last - cum), jnp.broadcast_to(jnp.exp(last), (rows, V7X_LANES))]
        for j in range(seq):
            prod = cmat * _shift_rows(bmat, j)
            cbh = jnp.zeros((rows, V7X_LANES), F32)
            for g in range(M_GROUPS):
                cbg = jnp.sum(prod[:, g * M_STATE:(g + 1) * M_STATE], axis=-1, keepdims=True)
                cbh = jnp.where((lane // M_HPG) == g, cbg, cbh)
            coefs.append(cbh * jnp.exp(cum - _shift_rows(cum, j)))
        wide = _expand_heads(jnp.concatenate(coefs, axis=0), e2)
        part = lambda n: wide[n * rows:(n + 1) * rows, :]
        xdt = xc * part(0)
        ecum = part(1)
        xs = xdt * part(2)
        dec = part(3)[0:1, :]
        y = dexp_ref[...] * xc
        for j in range(seq):
            y = y + part(4 + j) * _shift_rows(xdt, j)
        dec_hi = dec.astype(BF16).astype(F32)
        lhs = jnp.concatenate([xs, dec_hi, dec - dec_hi, jnp.zeros((rows - 2, M_DI), F32)], axis=0).astype(BF16)
        parts = []
        for g in range(M_GROUPS):
            bg = bmat[:, g * M_STATE:(g + 1) * M_STATE].astype(BF16)
            cg = cmat[:, g * M_STATE:(g + 1) * M_STATE].astype(BF16)
            s0 = s_ref[b, g * M_HPG:(g + 1) * M_HPG].reshape(gw, M_STATE)
            parts.append(_dot(cg, s0.astype(BF16), NT) * ecum[:, g * gw:(g + 1) * gw])
            rhs = jnp.concatenate([jnp.concatenate([bg, jnp.zeros((rows, M_STATE), BF16)], axis=0), ones_rows], axis=1)
            both = _dot(lhs[:, g * gw:(g + 1) * gw], rhs, TN)
            s_new = both[:, M_STATE:] * s0 + both[:, :M_STATE]
            so_ref[b, g * M_HPG:(g + 1) * M_HPG] = s_new.reshape(M_HPG, M_HEADDIM, M_STATE)
        y = (y + jnp.concatenate(parts, axis=1)) * _silu(z8)
        return _group_rmsnorm(y, gw) * nw_ref[...]

    def pair(p, carry):
        r0 = pl.multiple_of(p * rows, rows)
        tiles = [ref[pl.ds(r0, rows), :] for ref in (x_ref, bc_ref, z_ref, dt_ref)]
        ys = []
        for j in range(2):
            ys.append(one(2 * p + j, *[_shift_rows(t, j * (rows - seq)) for t in tiles]))
        y_ref[pl.ds(r0, rows), :] = jnp.where(valid, ys[0], _shift_rows(ys[1], seq)).astype(y_ref.dtype)
        return carry

    lax.fori_loop(0, SSD_SAMPLE_BB // 2, pair, 0)


def _ssd_sample(proj, row0, dt, state, conv_state, conv_w, conv_b, dt_bias, a_log, d_exp, norm_w, e2, *,
                batch, seq, prev=None):
    bb = SSD_SAMPLE_BB
    tr = bb * seq
    tok = lambda j: pl.BlockSpec((tr, M_DI), lambda i: (row0 // tr + i, j))
    const = lambda shape, j=0: pl.BlockSpec(shape, lambda i: (0, j))
    sblk = pl.BlockSpec((bb, M_HEADS, M_HEADDIM, M_STATE), lambda i: (i, 0, 0, 0))
    cblk = pl.BlockSpec((bb, M_CONV - 1, M_CONV_DIM), lambda i: (i, 0, 0))
    blocks = (3 * _nbytes((tr, M_DI), F32) + 2 * _nbytes((bb, M_HEADS, M_HEADDIM, M_STATE), F32)
              + 2 * _nbytes((bb, V7X_SUBLANES, M_CONV_DIM), F32) + _nbytes((2 * V7X_LANES, M_DI), BF16))
    body, xspecs, xops, alias = _shared_output(functools.partial(_ssd_sample_body, seq=seq), 15, prev)
    return pl.pallas_call(
        body,
        grid=(batch // bb,),
        in_specs=[tok(4), tok(5), tok(6),
                  pl.BlockSpec((tr, V7X_LANES), lambda i: (row0 // tr + i, 0)),
                  sblk, cblk,
                  const((M_CONV, M_DI), 0), const((M_CONV, M_DI), 1), const((1, M_DI), 0), const((1, M_DI), 1),
                  const((1, V7X_LANES)), const((1, V7X_LANES)), const((1, M_DI)), const((1, M_DI)),
                  const((2 * V7X_LANES, M_DI))] + xspecs,
        out_specs=[pl.BlockSpec((tr, M_DI), lambda i: (row0 // tr + i, HG_VW // M_DI)), sblk, cblk],
        out_shape=[jax.ShapeDtypeStruct((proj.shape[0], MIX0_WIDTH), BF16),
                   jax.ShapeDtypeStruct(state.shape, F32),
                   jax.ShapeDtypeStruct(conv_state.shape, F32)],
        input_output_aliases=alias,
        compiler_params=_params(("arbitrary",), blocks),
        name="ssd_sample",
    )(proj, proj, proj, dt, state, conv_state, conv_w, conv_w, conv_b, conv_b, dt_bias, a_log, d_exp, norm_w, e2, *xops)


S5_TILES = 8
S5_TILE_IN = D_MODEL // S5_TILES
S5_TILE_ST = S5_STATE // S5_TILES


def _s5_discretize(a_re, a_im, log_dt, b_re, b_im, c_re, c_im):
    lam_re = jnp.minimum(a_re, -1e-4)
    lam_im = a_im
    dt = jnp.exp(log_dt)[:, None]
    mag = jnp.exp(lam_re * dt)
    ab_re = mag * jnp.cos(lam_im * dt)
    ab_im = mag * jnp.sin(lam_im * dt)
    den = lam_re * lam_re + lam_im * lam_im
    zr = ((ab_re - 1.0) * lam_re + ab_im * lam_im) / den
    zi = (ab_im * lam_re - (ab_re - 1.0) * lam_im) / den
    bb_re = zr[..., None] * b_re - zi[..., None] * b_im
    bb_im = zr[..., None] * b_im + zi[..., None] * b_re
    gpt = S5_GROUPS // S5_TILES

    def block_diag(blocks, rows, cols):
        rep = jnp.tile(jnp.eye(cols, dtype=F32), (1, gpt))
        wide = jnp.einsum('krc,cn->krn', blocks, rep, precision=HIGHEST)
        own = (jnp.arange(gpt * rows)[:, None] // rows) == (jnp.arange(gpt * cols)[None, :] // cols)
        return jnp.where(own[None], wide, 0.0).astype(BF16)

    def pack_b(bb):
        return block_diag(jnp.swapaxes(bb, 1, 2).reshape(S5_TILES, S5_TILE_IN, S5_P), S5_GROUP, S5_P)

    def pack_c(c):
        return block_diag(jnp.swapaxes(c, 1, 2).reshape(S5_TILES, S5_TILE_ST, S5_GROUP), S5_P, S5_GROUP)

    slab = lambda a: _s5_slab_layout(a.reshape(S5_STATE))
    return slab(ab_re), slab(ab_im), pack_b(bb_re), pack_b(bb_im), pack_c(c_re), pack_c(c_im)


S5_PROMPT_ROWS = 256
S5_SAMPLE_ROWS = 128
S5_NBLK = S5_TILE_ST // V7X_LANES


def _s5_pitch(rows):
    assert rows % V7X_SUBLANES == 0
    return rows + V7X_SUBLANES // 2


def _s5_slab_layout(a):
    lead = a.shape[:-1]
    return jnp.swapaxes(a.reshape(lead + (S5_TILES, S5_NBLK, V7X_LANES)), -3, -2)


def _s5_from_slab_layout(a):
    lead = a.shape[:-3]
    return jnp.swapaxes(a, -3, -2).reshape(lead + (S5_STATE,))


def _s5_fused_body(xb_ref, x_ref, h0r_ref, h0i_ref, ar_ref, ai_ref, wbr_ref, wbi_ref, wcr_ref, wci_ref, d_ref, *rest,
                   n_seq, steps):
    act_ref, fr_ref, fi_ref, bur, bui, cr, ci = rest[-7:]
    hrs, his = bur, bui
    n = pl.program_id(1)
    rows = n_seq * steps
    S5_PITCH = _s5_pitch(rows)
    for s in range(S5_TILES):
        u = xb_ref[:, s * S5_TILE_IN:(s + 1) * S5_TILE_IN]
        br = _dot(u, wbr_ref[s], NN)
        bi = _dot(u, wbi_ref[s], NN)
        for j in range(S5_NBLK):
            bur[j, s * S5_PITCH:s * S5_PITCH + rows, :] = br[:, j * V7X_LANES:(j + 1) * V7X_LANES]
            bui[j, s * S5_PITCH:s * S5_PITCH + rows, :] = bi[:, j * V7X_LANES:(j + 1) * V7X_LANES]

    @pl.when(n == 0)
    def _():
        cr[...] = h0r_ref[...]
        ci[...] = h0i_ref[...]

    ar = [ar_ref[j] for j in range(S5_NBLK)]
    ai = [ai_ref[j] for j in range(S5_NBLK)]

    def seq_body(q, carry):
        def step(t, h):
            hr, hi = h
            idx = pl.ds(q * steps + t, S5_TILES, stride=S5_PITCH)
            nr, ni = [], []
            for j in range(S5_NBLK):
                nr.append(ar[j] * hr[j] - ai[j] * hi[j] + bur[j, idx, :])
                ni.append(ar[j] * hi[j] + ai[j] * hr[j] + bui[j, idx, :])
                hrs[j, idx, :] = nr[j]
                his[j, idx, :] = ni[j]
            return tuple(nr), tuple(ni)

        h0 = (tuple(cr[q, j] for j in range(S5_NBLK)), tuple(ci[q, j] for j in range(S5_NBLK)))
        hr, hi = lax.fori_loop(0, steps, step, h0, unroll=min(steps, 4))
        for j in range(S5_NBLK):
            cr[q, j] = hr[j]
            ci[q, j] = hi[j]
        return carry

    lax.fori_loop(0, n_seq, seq_body, 0)

    for s in range(S5_TILES):
        seg = slice(s * S5_PITCH, s * S5_PITCH + rows)
        hre = jnp.concatenate([hrs[j, seg, :] for j in range(S5_NBLK)], axis=1).astype(BF16)
        him = jnp.concatenate([his[j, seg, :] for j in range(S5_NBLK)], axis=1).astype(BF16)
        y = _dot(hre, wcr_ref[s], NN) - _dot(him, wci_ref[s], NN)
        cs = slice(s * S5_TILE_IN, (s + 1) * S5_TILE_IN)
        act_ref[:, cs] = jax.nn.gelu(y + d_ref[:, cs] * x_ref[:, cs]).astype(act_ref.dtype)

    @pl.when(n == pl.num_programs(1) - 1)
    def _():
        fr_ref[...] = cr[...]
        fi_ref[...] = ci[...]


def _s5_fused(xb, x, h0_re, h0_im, ab_re, ab_im, wb_re, wb_im, wc_re, wc_im, d_skip, *, row0, batch, seq, rows, n_seq,
              prev=None, name):
    t, d = x.shape
    steps = rows // n_seq
    assert n_seq == 1 or steps == seq
    S5_PITCH = _s5_pitch(rows)
    tok = lambda width: pl.BlockSpec((rows, width), lambda i, n: ((row0 + i * n_seq * seq) // rows + n, 0))
    st = pl.BlockSpec((n_seq, S5_NBLK, S5_TILES, V7X_LANES), lambda i, n: (i, 0, 0, 0))
    const = lambda shape: pl.BlockSpec(shape, lambda i, n: (0,) * len(shape), pipeline_mode=pl.Buffered(1))
    slab = pltpu.VMEM((S5_NBLK, S5_TILES * S5_PITCH, V7X_LANES), F32)
    stv = pltpu.VMEM((n_seq, S5_NBLK, S5_TILES, V7X_LANES), F32)
    blocks = (2 * (_nbytes((rows, d), BF16) * 2 + _nbytes((rows, d), F32))
              + 2 * 6 * _nbytes((n_seq, S5_STATE), F32) + 4 * _nbytes(wb_re.shape, BF16)
              + 2 * _nbytes((S5_NBLK, S5_TILES * S5_PITCH, V7X_LANES), F32))
    keep = [] if prev is None else [prev]
    n_in = 11
    return pl.pallas_call(
        functools.partial(_s5_fused_body, n_seq=n_seq, steps=steps),
        grid=(batch // n_seq, seq // steps),
        in_specs=[tok(d), tok(d), st, st, const(ab_re.shape), const(ab_im.shape), const(wb_re.shape),
                  const(wb_im.shape), const(wc_re.shape), const(wc_im.shape), const((1, d))]
                 + [pl.BlockSpec(memory_space=pl.ANY)] * len(keep),
        out_specs=[tok(d), st, st],
        out_shape=[jax.ShapeDtypeStruct((t, d), BF16), jax.ShapeDtypeStruct(h0_re.shape, F32),
                   jax.ShapeDtypeStruct(h0_im.shape, F32)],
        scratch_shapes=[slab, slab, stv, stv],
        input_output_aliases={n_in + k: k for k in range(len(keep))},
        compiler_params=pltpu.CompilerParams(dimension_semantics=("arbitrary", "arbitrary"),
                                             vmem_limit_bytes=int(min(blocks + COMPILER_SCRATCH_BYTES, V7X_SCOPED_VMEM_CAP))),
        name=name,
    )(xb, x, h0_re, h0_im, ab_re, ab_im, wb_re, wb_im, wc_re, wc_im, d_skip.reshape(1, d), *keep)


def _glu_body(a_ref, wa_ref, wb_ref, ba_ref, bb_ref, o_ref, wa_scr, wb_scr):
    @pl.when(pl.program_id(1) == 0)
    def _():
        wa_scr[...] = wa_ref[0].astype(BF16)
        wb_scr[...] = wb_ref[0].astype(BF16)

    a = a_ref[...]
    o_ref[...] = (_dot(a, wa_scr[...], NN) + ba_ref[...]) * jax.nn.sigmoid(_dot(a, wb_scr[...], NN) + bb_ref[...])


def _glu(a, wa, wb, ba, bb, layer, *, tm, tn):
    t, k = a.shape
    n = wa.shape[2]
    wspec = pl.BlockSpec((1, k, tn), lambda j, i: (layer, 0, j))
    bspec = pl.BlockSpec((1, tn), lambda j, i: (layer, j))
    blocks = _nbytes((tm, k), BF16) + 2 * _nbytes((k, tn), F32) + _nbytes((tm, tn), F32) + 2 * _nbytes((k, tn), BF16)
    return pl.pallas_call(
        _glu_body,
        grid=(n // tn, t // tm),
        in_specs=[pl.BlockSpec((tm, k), lambda j, i: (i, 0)), wspec, wspec, bspec, bspec],
        out_specs=pl.BlockSpec((tm, tn), lambda j, i: (i, j)),
        out_shape=jax.ShapeDtypeStruct((t, n), F32),
        scratch_shapes=[pltpu.VMEM((k, tn), BF16)] * 2,
        compiler_params=_params(("arbitrary", "arbitrary"), blocks),
        name="glu",
    )(a, wa, wb, ba, bb)


def _route(x, w_ref, b_ref, e_ref, g_ref):
    w = w_ref[...]
    w_hi = w.astype(BF16)
    w_lo = (w - w_hi.astype(F32)).astype(BF16)
    x_hi = x.astype(BF16)
    x_lo = (x - x_hi.astype(F32)).astype(BF16)
    logits = _dot(w_hi, x_hi, NT) + _dot(w_hi, x_lo, NT) + _dot(w_lo, x_hi, NT) + b_ref[...]
    rows = [logits[e:e + 1, :] for e in range(N_EXPERTS)]
    m = functools.reduce(jnp.maximum, rows)
    ex = [jnp.exp(r - m) for r in rows]
    z = functools.reduce(jnp.add, ex)
    p = [v / z for v in ex]

    def top2_sum(a, b, c, d):
        hi1, lo1, hi2, lo2 = jnp.maximum(a, b), jnp.minimum(a, b), jnp.maximum(c, d), jnp.minimum(c, d)
        return jnp.maximum(hi1, hi2) + jnp.maximum(jnp.minimum(hi1, hi2), jnp.maximum(lo1, lo2))

    assert EXP_PER_GROUP == 4
    best = top2_sum(*p[0:EXP_PER_GROUP])
    gsel = jnp.zeros_like(best, dtype=jnp.int32)
    for g in range(1, N_EXP_GROUPS):
        s = top2_sum(*p[g * EXP_PER_GROUP:(g + 1) * EXP_PER_GROUP])
        better = s > best
        gsel = jnp.where(better, g, gsel)
        best = jnp.where(better, s, best)
    inner = []
    for i in range(EXP_PER_GROUP):
        v = p[i]
        for g in range(1, N_EXP_GROUPS):
            v = jnp.where(gsel == g, p[g * EXP_PER_GROUP + i], v)
        inner.append(v)

    def first_argmax(vals, skip=None):
        bv = bi = None
        for i, v in enumerate(vals):
            v = v if skip is None else jnp.where(skip == i, -jnp.inf, v)
            if bv is None:
                bv, bi = v, jnp.zeros_like(gsel)
            else:
                better = v > bv
                bi = jnp.where(better, i, bi)
                bv = jnp.where(better, v, bv)
        return bv, bi

    p1, i1 = first_argmax(inner)
    p2, i2 = first_argmax(inner, skip=i1)
    tot = p1 + p2
    e1 = gsel * EXP_PER_GROUP + i1
    e2 = gsel * EXP_PER_GROUP + i2
    e_ref[0:1, :] = e1
    e_ref[1:2, :] = e2
    g_ref[0:1, :] = p1 / tot
    g_ref[1:2, :] = p2 / tot


def _ln_router_body(x_ref, x2_ref, mix_ref, g_ref, b_ref, w_ref, br_ref, o_ref, e_ref, gw_ref, *, first_blocks):
    x = x_ref[...] if first_blocks is None else jnp.where(pl.program_id(0) < first_blocks, x_ref[...], x2_ref[...])
    z = ALPHA * x + mix_ref[...]
    mu = jnp.mean(z, -1, keepdims=True)
    zc = z - mu
    var = jnp.mean(zc * zc, -1, keepdims=True)
    y = zc * lax.rsqrt(var + LN_EPS) * g_ref[...] + b_ref[...]
    o_ref[...] = y
    _route(y, w_ref, br_ref, e_ref, gw_ref)


def _res_layernorm_router(x, mix, g, b, w_router, b_router, *, tm, name, x_tail=None):
    t, d = mix.shape
    row = pl.BlockSpec((tm, d), lambda i: (i, 0))
    vec = pl.BlockSpec((1, d), lambda i: (0, 0))
    sel = pl.BlockSpec((TOP_K, tm), lambda i: (0, i))
    if x_tail is None:
        nb, xspecs, xs = None, [row, vec], [x, g.reshape(1, d)]
    else:
        nb = x.shape[0] // tm
        xspecs = [pl.BlockSpec((tm, d), lambda i: (jnp.minimum(i, nb - 1), 0)),
                  pl.BlockSpec((tm, d), lambda i: (jnp.maximum(i - nb, 0), 0))]
        xs = [x, x_tail]
    blocks = 4 * _nbytes((tm, d), F32) + _nbytes((N_EXPERTS, d), F32) + 2 * _nbytes((V7X_SUBLANES, tm), F32)
    return pl.pallas_call(
        functools.partial(_ln_router_body, first_blocks=nb),
        grid=(t // tm,),
        in_specs=xspecs + [row, vec, vec, pl.BlockSpec((N_EXPERTS, d), lambda i: (0, 0)),
                           pl.BlockSpec((N_EXPERTS, 1), lambda i: (0, 0))],
        out_specs=[row, sel, sel],
        out_shape=[jax.ShapeDtypeStruct((t, d), F32), jax.ShapeDtypeStruct((TOP_K, t), jnp.int32),
                   jax.ShapeDtypeStruct((TOP_K, t), F32)],
        compiler_params=_params(("arbitrary",), blocks),
        name=name,
    )(*xs, mix, g.reshape(1, d), b.reshape(1, d), w_router.T, b_router.reshape(N_EXPERTS, 1))


MOE_TILE = 256
COMBINE_TILE = 256


def _moe_plan(eidx):
    k, t = eidx.shape
    ns = k * t
    n_tiles = ns // MOE_TILE + N_EXPERTS
    e_flat = eidx.reshape(ns)
    order = jnp.argsort(e_flat, stable=True).astype(jnp.int32)
    rank = jnp.argsort(order).astype(jnp.int32)
    experts = jnp.arange(N_EXPERTS, dtype=jnp.int32)[None, :]
    slot_is = e_flat[:, None] == experts
    counts = jnp.sum(slot_is.astype(jnp.int32), axis=0)
    start = jnp.cumsum(counts) - counts
    tiles = (counts + MOE_TILE - 1) // MOE_TILE
    tile_end = jnp.cumsum(tiles)
    tile_start = tile_end - tiles
    tile_ids = jnp.arange(n_tiles, dtype=jnp.int32)
    tile_expert = jnp.minimum(jnp.sum((tile_ids[:, None] >= tile_end[None, :]).astype(jnp.int32), axis=1),
                              N_EXPERTS - 1)
    pick = lambda onehot, table: jnp.sum(jnp.where(onehot, table[None, :], 0), axis=1)
    tile_is = tile_expert[:, None] == experts
    tile_first = pick(tile_is, start) + (tile_ids - pick(tile_is, tile_start)) * MOE_TILE
    position_of_slot = pick(slot_is, tile_start) * MOE_TILE + rank - pick(slot_is, start)
    sorted_token = jnp.pad(order % t, (0, MOE_TILE))
    return (tile_expert.astype(jnp.int32), tile_first.astype(jnp.int32), tile_end[-1:].astype(jnp.int32),
            sorted_token.astype(jnp.int32), position_of_slot.astype(jnp.int32))


def _moe_expert_body(te_ref, first_ref, nt_ref, tok_ref, x_hbm, wg_ref, wu_ref, wd_ref, y_ref, xbuf, sem, wg_scr,
                     wu_scr, wd_scr):
    i = pl.program_id(0)
    nt = nt_ref[0]
    tm = MOE_TILE
    sub = V7X_SUBLANES

    def gather_copy(tok, slot, r8, j):
        return pltpu.make_async_copy(x_hbm.at[pl.ds(tok, 1)], xbuf.at[slot, r8, pl.ds(j, 1)], sem.at[slot])

    def start_gather(tile, slot):
        first = first_ref[tile]

        def body(r8, c):
            for j in range(sub):
                gather_copy(tok_ref[first + r8 * sub + j], slot, r8, j).start(priority=j % 2)
            return c

        lax.fori_loop(0, tm // sub, body, 0)

    @pl.when(i == 0)
    def _():
        start_gather(0, 0)

    @pl.when(i + 1 < nt)
    def _():
        start_gather(i + 1, (i + 1) % 2)

    @pl.when(i < nt)
    def _():
        slot = i % 2
        for r8 in range(tm // sub):
            pltpu.make_async_copy(x_hbm.at[pl.ds(0, sub)], xbuf.at[slot, r8], sem.at[slot]).wait()

        @pl.when((i == 0) | (te_ref[i] != te_ref[jnp.maximum(i - 1, 0)]))
        def _():
            wg_scr[...] = wg_ref[0, 0].astype(BF16)
            wu_scr[...] = wu_ref[0, 0].astype(BF16)
            wd_scr[...] = wd_ref[0, 0].astype(BF16)

        x = xbuf[slot].reshape(tm, x_hbm.shape[1]).astype(BF16)
        act = (_silu(_dot(x, wg_scr[...], NN)) * _dot(x, wu_scr[...], NN)).astype(BF16)
        y_ref[...] = _dot(act, wd_scr[...], NN)

    @pl.when(i >= nt)
    def _():
        y_ref[...] = jnp.zeros_like(y_ref)


def _moe_experts(x, plan, w_gate, w_up, w_down, layer):
    tile_expert, tile_first, n_valid, sorted_token, _ = plan
    t, d = x.shape
    f = w_gate.shape[3]
    n_tiles = tile_expert.shape[0]
    wspec = lambda a, b: pl.BlockSpec((1, 1, a, b), lambda i, te, *_: (layer, te[i], 0, 0))
    blocks = (2 * 3 * _nbytes((d, f), F32) + 3 * _nbytes((d, f), BF16) + 2 * _nbytes((MOE_TILE, d), F32)
              + 2 * _nbytes((MOE_TILE, d), F32) + 4 * _nbytes((MOE_TILE, f), F32))
    return pl.pallas_call(
        _moe_expert_body,
        grid_spec=pltpu.PrefetchScalarGridSpec(
            num_scalar_prefetch=4,
            grid=(n_tiles,),
            in_specs=[pl.BlockSpec(memory_space=pl.ANY), wspec(d, f), wspec(d, f), wspec(f, d)],
            out_specs=pl.BlockSpec((MOE_TILE, d), lambda i, *_: (i, 0)),
            scratch_shapes=[pltpu.VMEM((2, MOE_TILE // V7X_SUBLANES, V7X_SUBLANES, d), F32),
                            pltpu.SemaphoreType.DMA((2,)),
                            pltpu.VMEM((d, f), BF16), pltpu.VMEM((d, f), BF16), pltpu.VMEM((f, d), BF16)]),
        out_shape=jax.ShapeDtypeStruct((n_tiles * MOE_TILE, d), F32),
        compiler_params=pltpu.CompilerParams(dimension_semantics=("arbitrary",),
                                             vmem_limit_bytes=int(min(blocks + COMPILER_SCRATCH_BYTES, V7X_SCOPED_VMEM_CAP))),
        name="moe_experts",
    )(tile_expert, tile_first, n_valid, sorted_token, x, w_gate, w_up, w_down)


def _moe_combine_body(pos_ref, x_ref, w_ref, g_ref, b_ref, y_hbm, o_ref, ob_ref, ybuf, sem, *, n_tok, split_blocks):
    i = pl.program_id(0)
    tc = COMBINE_TILE

    sub = V7X_SUBLANES

    def gather_copy(p, slot, k, r8, j):
        return pltpu.make_async_copy(y_hbm.at[pl.ds(p, 1)], ybuf.at[slot, k, r8, pl.ds(j, 1)], sem.at[slot])

    def start_gather(tile, slot):
        def body(r8, c):
            for k in range(TOP_K):
                for j in range(sub):
                    gather_copy(pos_ref[k * n_tok + tile * tc + r8 * sub + j], slot, k, r8, j).start(priority=j % 2)
            return c

        lax.fori_loop(0, tc // sub, body, 0)

    @pl.when(i == 0)
    def _():
        start_gather(0, 0)

    @pl.when(i + 1 < pl.num_programs(0))
    def _():
        start_gather(i + 1, (i + 1) % 2)

    slot = i % 2
    for k in range(TOP_K):
        for r8 in range(tc // sub):
            pltpu.make_async_copy(y_hbm.at[pl.ds(0, sub)], ybuf.at[slot, k, r8], sem.at[slot]).wait()
    z = ALPHA * x_ref[...]
    for k in range(TOP_K):
        z = z + w_ref[:, k:k + 1] * ybuf[slot, k].reshape(tc, x_ref.shape[1])
    mu = jnp.mean(z, -1, keepdims=True)
    zc = z - mu
    var = jnp.mean(zc * zc, -1, keepdims=True)
    y = zc * lax.rsqrt(var + LN_EPS) * g_ref[...] + b_ref[...]
    if split_blocks is None:
        o_ref[...] = y
        ob_ref[...] = y.astype(BF16)
    else:
        @pl.when(i < split_blocks)
        def _():
            o_ref[...] = y

        @pl.when(i >= split_blocks)
        def _():
            ob_ref[...] = y


def _moe_combine_layernorm(x, y_slots, plan, gate_w, g, b, *, name, split=None):
    position_of_slot = plan[4]
    t, d = x.shape
    tc = COMBINE_TILE
    row = pl.BlockSpec((tc, d), lambda i, pos: (i, 0))
    vec = pl.BlockSpec((1, d), lambda i, pos: (0, 0))
    blocks = 2 * (3 * _nbytes((tc, d), F32) + _nbytes((tc, d), BF16)) + 2 * TOP_K * _nbytes((tc, d), F32)
    if split is None:
        sb = None
        out_specs = [row, row]
        out_shape = [jax.ShapeDtypeStruct((t, d), F32), jax.ShapeDtypeStruct((t, d), BF16)]
    else:
        sb = split // tc
        out_specs = [pl.BlockSpec((tc, d), lambda i, pos: (jnp.minimum(i, sb - 1), 0)),
                     pl.BlockSpec((tc, d), lambda i, pos: (jnp.maximum(i - sb, 0), 0))]
        out_shape = [jax.ShapeDtypeStruct((split, d), F32), jax.ShapeDtypeStruct((t - split, d), F32)]
    return pl.pallas_call(
        functools.partial(_moe_combine_body, n_tok=t, split_blocks=sb),
        grid_spec=pltpu.PrefetchScalarGridSpec(
            num_scalar_prefetch=1,
            grid=(t // tc,),
            in_specs=[row, pl.BlockSpec((tc, TOP_K), lambda i, pos: (i, 0)), vec, vec,
                      pl.BlockSpec(memory_space=pl.ANY)],
            out_specs=out_specs,
            scratch_shapes=[pltpu.VMEM((2, TOP_K, tc // V7X_SUBLANES, V7X_SUBLANES, d), F32),
                            pltpu.SemaphoreType.DMA((2,))]),
        out_shape=out_shape,
        compiler_params=pltpu.CompilerParams(dimension_semantics=("arbitrary",),
                                             vmem_limit_bytes=int(min(blocks + COMPILER_SCRATCH_BYTES, V7X_SCOPED_VMEM_CAP))),
        name=name,
    )(position_of_slot, x, gate_w, g.reshape(1, d), b.reshape(1, d), y_slots)


TOKEN_TILE = 1088
IN_PROJ_TN = 1024
LN_TILE = 256


def _moe_layernorm(x, eidx, gate_w, layer, w_gate, w_up, w_down, g, b, split=None):
    plan = _moe_plan(eidx)
    y_slots = _moe_experts(x, plan, w_gate, w_up, w_down, layer)
    return _moe_combine_layernorm(x, y_slots, plan, gate_w.T, g, b, name=f"moe_combine_ln_{layer}", split=split)


def kernel(x_prompt, x_sample, state_hgrn, state_ssm, state_conv, state_s5_re, state_s5_im, w_in0, hg_lb_logits, hg_norm_w, conv_w, conv_b, dt_bias, a_log, m_d, m_norm_w, w_out0, s5_a_re, s5_a_im, s5_log_dt, s5_b_re, s5_b_im, s5_c_re, s5_c_im, s5_d, glu_w_a, glu_b_a, glu_w_b, glu_b_b, w_router, b_router, w_gate, w_up, w_down, ln1_g, ln1_b, ln2_g, ln2_b):
    bp, lp, d = x_prompt.shape
    bs, ls, _ = x_sample.shape
    tp, ts = bp * lp, bs * ls
    tm = TOKEN_TILE
    xp2, xs2 = x_prompt.reshape(tp, d), x_sample.reshape(ts, d)
    x0b = _join_bf16(xp2, xs2, tm=2 * LN_TILE)
    lower_bounds = jnp.cumsum(jax.nn.softmax(hg_lb_logits.astype(F32), axis=0), axis=0)
    pad_lanes = lambda v: jnp.pad(v, (0, V7X_LANES - v.shape[0])).reshape(1, V7X_LANES)

    proj = _matmul_w32(x0b, jnp.swapaxes(w_in0, 1, 2), 0, tm=tm, tn=IN_PROJ_TN, n_cols=IN0_MAIN,
                       transposed=True, name="in_proj")
    w_dt = jnp.pad(w_in0[0, :, IN0_MAIN:], ((0, 0), (0, V7X_LANES - M_HEADS))).astype(BF16)
    dt = _matmul(x0b, w_dt, tm=tm, tn=V7X_LANES, name="dt_proj")
    lb0 = lower_bounds[0]
    e2 = _head_expand_matrix()
    ssd_consts = (conv_w[0], conv_b[0].reshape(1, -1), pad_lanes(dt_bias[0]), pad_lanes(a_log[0]),
                  jnp.repeat(m_d[0], M_HEADDIM).reshape(1, M_DI), m_norm_w[0].reshape(1, M_DI), e2)
    hgrn_prompt = functools.partial(_hgrn_prompt, proj, lb0, hg_norm_w[0], batch=bp, seq=lp)
    mixed, hg_p = lax.cond(_hgrn_lower_bound_is_safe(lb0), functools.partial(hgrn_prompt, guarded=False),
                           functools.partial(hgrn_prompt, guarded=True))
    mixed, hg_s = _hgrn_sample(proj, tp, state_hgrn[0], lb0, hg_norm_w[0], batch=bs, seq=ls, prev=mixed)
    mixed, ssm_p, conv_p = _ssd_prompt(proj, dt, *ssd_consts, batch=bp, seq=lp, prev=mixed)
    mixed, ssm_s, conv_s = _ssd_sample(proj, tp, dt, state_ssm[0], state_conv[0], *ssd_consts, batch=bs, seq=ls,
                                       prev=mixed)
    mix = _matmul_w32(mixed, w_out0, 0, tm=tm, tn=512, name="out_proj")
    x1, eidx, gate_w = _res_layernorm_router(xp2, mix, ln1_g[0], ln1_b[0], w_router, b_router, tm=LN_TILE,
                                             name="ln1_router_0", x_tail=xs2)
    x2, x2b = _moe_layernorm(x1, eidx, gate_w, 0, w_gate, w_up, w_down, ln2_g[0], ln2_b[0])

    ab_re, ab_im, wb_re, wb_im, wc_re, wc_im = _s5_discretize(
        s5_a_re[0], s5_a_im[0], s5_log_dt[0], s5_b_re[0], s5_b_im[0], s5_c_re[0], s5_c_im[0])
    s5_consts = (ab_re, ab_im, wb_re, wb_im, wc_re, wc_im, s5_d[0])
    zeros = jnp.zeros((bp, S5_NBLK, S5_TILES, V7X_LANES), F32)
    act, s5r_p, s5i_p = _s5_fused(x2b, x2, zeros, zeros, *s5_consts, row0=0, batch=bp, seq=lp,
                                  rows=S5_PROMPT_ROWS, n_seq=1, name="s5_prompt")
    act, s5r_s, s5i_s = _s5_fused(x2b, x2, _s5_slab_layout(state_s5_re[0].reshape(bs, S5_STATE)),
                                  _s5_slab_layout(state_s5_im[0].reshape(bs, S5_STATE)), *s5_consts,
                                  row0=tp, batch=bs, seq=ls, rows=S5_SAMPLE_ROWS, n_seq=S5_SAMPLE_ROWS // ls,
                                  prev=act, name="s5_sample")
    mix1 = _glu(act, glu_w_a, glu_w_b, glu_b_a, glu_b_b, 0, tm=tm, tn=512)
    x3, eidx, gate_w = _res_layernorm_router(x2, mix1, ln1_g[1], ln1_b[1], w_router, b_router, tm=LN_TILE,
                                             name="ln1_router_1")
    y_p, y_s = _moe_layernorm(x3, eidx, gate_w, 1, w_gate, w_up, w_down, ln2_g[1], ln2_b[1], split=tp)

    s5_state = lambda a, b: _s5_from_slab_layout(a).reshape(1, b, S5_GROUPS, S5_P)
    return (y_p.reshape(bp, lp, d), y_s.reshape(bs, ls, d),
            hg_p[None], hg_s[None], ssm_p[None], ssm_s[None], conv_p[None], conv_s[None],
            s5_state(s5r_p, bp), s5_state(s5r_s, bs), s5_state(s5i_p, bp), s5_state(s5i_s, bs))
```

```python
import functools
import math

import jax
import jax.numpy as jnp
from jax import lax
from jax.experimental import pallas as pl
from jax.experimental.pallas import tpu as pltpu

F32 = jnp.float32
BF16 = jnp.bfloat16
HIGHEST = lax.Precision.HIGHEST

D_MODEL = 2048
DEPTH = 2
HG_HEADS = 16
HG_DK = 128
HG_DV = 128
HG_KW = HG_HEADS * HG_DK
HG_VW = HG_HEADS * HG_DV
HG_CHUNK = 64
M_DI = 2048
M_HEADDIM = 64
M_HEADS = 32
M_GROUPS = 8
M_HPG = 4
M_STATE = 128
M_CONV = 4
M_CONV_DIM = M_DI + 2 * M_GROUPS * M_STATE
M_CHUNK = 128
IN0_MAIN = 2 * HG_KW + 2 * HG_VW + M_DI + M_CONV_DIM
S5_GROUP = 16
S5_GROUPS = 128
S5_P = 64
S5_STATE = S5_GROUPS * S5_P
N_EXPERTS = 16
N_EXP_GROUPS = 4
EXP_PER_GROUP = 4
TOP_K = 2
D_EXPERT = 512
ALPHA = (2 * DEPTH) ** 0.25
LN_EPS = 1e-5
RMS_EPS = 1e-6

V7X_LANES = 128
V7X_SUBLANES = 8
V7X_SCOPED_VMEM_CAP = 60000 * 1024
COMPILER_SCRATCH_BYTES = 16 * 1024 * 1024


def _params(semantics, block_bytes):
    limit = min(2 * block_bytes + COMPILER_SCRATCH_BYTES, V7X_SCOPED_VMEM_CAP)
    return pltpu.CompilerParams(dimension_semantics=semantics, vmem_limit_bytes=int(limit))


def _nbytes(shape, dtype):
    return math.prod(shape) * jnp.dtype(dtype).itemsize


def _silu(x):
    return x * jax.nn.sigmoid(x)


def _dot(a, b, dims, precision=None):
    return lax.dot_general(a, b, (dims, ((), ())), precision=precision, preferred_element_type=F32)


NN = ((1,), (0,))
NT = ((1,), (1,))
TN = ((0,), (0,))

MIX0_WIDTH = HG_VW + M_DI


def _shared_output(body, n_in, prev):
    if prev is None:
        return body, [], [], {}

    def with_prev(*refs):
        return body(*refs[:n_in], *refs[n_in + 1:])

    return with_prev, [pl.BlockSpec(memory_space=pl.ANY)], [prev], {n_in: 0}


def _join_bf16_body(a_ref, b_ref, o_ref, *, first_blocks):
    o_ref[...] = jnp.where(pl.program_id(0) < first_blocks, a_ref[...], b_ref[...]).astype(o_ref.dtype)


def _join_bf16(a, b, *, tm):
    d = a.shape[1]
    na, nb = a.shape[0] // tm, b.shape[0] // tm
    blocks = 2 * _nbytes((tm, d), F32) + _nbytes((tm, d), BF16)
    return pl.pallas_call(
        functools.partial(_join_bf16_body, first_blocks=na),
        grid=(na + nb,),
        in_specs=[pl.BlockSpec((tm, d), lambda i: (jnp.minimum(i, na - 1), 0)),
                  pl.BlockSpec((tm, d), lambda i: (jnp.maximum(i - na, 0), 0))],
        out_specs=pl.BlockSpec((tm, d), lambda i: (i, 0)),
        out_shape=jax.ShapeDtypeStruct((a.shape[0] + b.shape[0], d), BF16),
        compiler_params=_params(("arbitrary",), blocks),
        name="join_bf16",
    )(a, b)


def _mm_body(x_ref, w_ref, o_ref):
    o_ref[...] = _dot(x_ref[...], w_ref[...], NN).astype(o_ref.dtype)


def _matmul(x, w, *, tm, tn, n_cols=None, out_dtype=F32, name):
    m, k = x.shape
    n = n_cols if n_cols is not None else w.shape[1]
    blocks = _nbytes((tm, k), x.dtype) + _nbytes((k, tn), w.dtype) + _nbytes((tm, tn), out_dtype)
    return pl.pallas_call(
        _mm_body,
        grid=(n // tn, m // tm),
        in_specs=[pl.BlockSpec((tm, k), lambda j, i: (i, 0)), pl.BlockSpec((k, tn), lambda j, i: (0, j))],
        out_specs=pl.BlockSpec((tm, tn), lambda j, i: (i, j)),
        out_shape=jax.ShapeDtypeStruct((m, n), out_dtype),
        compiler_params=_params(("arbitrary", "arbitrary"), blocks),
        name=name,
    )(x, w)


def _mm_w32_body(x_ref, w_ref, o_ref, wb_scr):
    @pl.when(pl.program_id(1) == 0)
    def _():
        wb_scr[...] = w_ref[0].astype(BF16)

    o_ref[...] = _dot(x_ref[...], wb_scr[...], NN).astype(o_ref.dtype)


def _mm_w32t_body(x_ref, w_ref, o_ref, wb_scr):
    @pl.when(pl.program_id(1) == 0)
    def _():
        wb_scr[...] = w_ref[0].astype(BF16)

    o_ref[...] = _dot(x_ref[...], wb_scr[...], NT).astype(o_ref.dtype)


def _matmul_w32(x, w, layer, *, tm, tn, n_cols=None, transposed=False, name):
    m, k = x.shape
    n = n_cols if n_cols is not None else w.shape[1 if transposed else 2]
    blocks = _nbytes((tm, k), BF16) + _nbytes((k, tn), F32) + _nbytes((tm, tn), F32) + _nbytes((k, tn), BF16)
    if transposed:
        body, wblock, wspec = _mm_w32t_body, (tn, k), pl.BlockSpec((1, tn, k), lambda j, i: (layer, j, 0))
    else:
        body, wblock, wspec = _mm_w32_body, (k, tn), pl.BlockSpec((1, k, tn), lambda j, i: (layer, 0, j))
    return pl.pallas_call(
        body,
        grid=(n // tn, m // tm),
        in_specs=[pl.BlockSpec((tm, k), lambda j, i: (i, 0)), wspec],
        out_specs=pl.BlockSpec((tm, tn), lambda j, i: (i, j)),
        out_shape=jax.ShapeDtypeStruct((m, n), F32),
        scratch_shapes=[pltpu.VMEM(wblock, BF16)],
        compiler_params=_params(("arbitrary", "arbitrary"), blocks),
        name=name,
    )(x, w)


def _hgrn_gates(q, f, lb):
    fg = lb + (1.0 - lb) * jax.nn.sigmoid(f)
    return _silu(q), 1.0 - fg, jnp.log(fg)


HG_SAFE_EXPONENT = 80.0


def _hgrn_exact_intra(q, k, cum, v_ref, heads, intra_scr, q_scr, k_scr, cum_scr):
    c = cum.shape[0]
    q_scr[...] = q
    k_scr[...] = k
    cum_scr[...] = cum
    unsafe = jnp.max(jnp.abs(cum - cum[c // 2 - 1:c // 2, :])) > HG_SAFE_EXPONENT

    @pl.when(unsafe)
    def _():
        srow = lax.broadcasted_iota(jnp.int32, (c, 1), 0)
        sub = V7X_SUBLANES

        def row_tile(t8, carry):
            base = pl.multiple_of(t8 * sub, sub)
            cum_t = cum_scr[pl.ds(base, sub), :]
            q_t = q_scr[pl.ds(base, sub), :]
            rows_out = [[] for _ in heads]
            for j in range(sub):
                decay = jnp.exp(jnp.minimum(cum_t[j:j + 1, :] - cum_scr[...], 0.0))
                w = jnp.where(srow <= base + j, q_t[j:j + 1, :] * k_scr[...] * decay, 0.0)
                for h, sl in enumerate(heads):
                    coef = jnp.sum(w[:, sl], axis=-1, keepdims=True)
                    rows_out[h].append(jnp.sum(coef * v_ref[:, sl], axis=0, keepdims=True))
            for h, sl in enumerate(heads):
                intra_scr[pl.ds(base, sub), sl] = jnp.concatenate(rows_out[h], axis=0)
            return carry

        lax.fori_loop(0, c // sub, row_tile, 0)

    return unsafe


def _hgrn_prompt_body(q_ref, f_ref, v_ref, g_ref, lb_ref, nw_ref, o_ref, sfin_ref, st_scr, *guard_scr, guarded):
    n = pl.program_id(1)
    c = HG_CHUNK

    @pl.when(n == 0)
    def _():
        st_scr[...] = jnp.zeros_like(st_scr)
        if guarded:
            guard_scr[0][...] = jnp.zeros_like(guard_scr[0])

    row = lax.broadcasted_iota(jnp.int32, (c, c), 0)
    col = lax.broadcasted_iota(jnp.int32, (c, c), 1)
    causal = row >= col
    tri = causal.astype(F32)
    heads = [slice(h * HG_DK, (h + 1) * HG_DK) for h in range(HG_HEADS)]
    q, k, logf = _hgrn_gates(q_ref[...], f_ref[...], lb_ref[...])
    cum = _dot(tri, logf, NN, precision=HIGHEST)
    if guarded:
        unsafe = _hgrn_exact_intra(q, k, cum, v_ref, heads, *guard_scr)
        q, k, cum = (r[...] for r in guard_scr[1:])
    v = v_ref[...].astype(BF16)
    mid = cum[c // 2 - 1:c // 2, :]
    last = cum[c - 1:c, :]
    qm = (q * jnp.exp(cum - mid)).astype(BF16)
    km = (k * jnp.exp(mid - cum)).astype(BF16)
    qi = (q * jnp.exp(cum)).astype(BF16)
    kl = (k * jnp.exp(last - cum)).astype(BF16)
    a_last = jnp.exp(last)
    gate = nw_ref[...] * _silu(g_ref[...])
    scores = [jnp.where(causal, _dot(qm[:, sl], km[:, sl], NT), 0.0).astype(BF16) for sl in heads]
    sts = [st_scr[h] for h in range(HG_HEADS)]
    inter = [_dot(qi[:, sl], sts[h].astype(BF16), NT) for h, sl in enumerate(heads)]
    intra = [_dot(scores[h], v[:, sl], NN) for h, sl in enumerate(heads)]
    for h, sl in enumerate(heads):
        st_scr[h] = a_last[:, sl] * sts[h] + _dot(v[:, sl], kl[:, sl], TN)
    for h, sl in enumerate(heads):
        o = inter[h] + (jnp.where(unsafe, guard_scr[0][:, sl], intra[h]) if guarded else intra[h])
        o = o * lax.rsqrt(jnp.mean(o * o, -1, keepdims=True) + RMS_EPS)
        o_ref[:, sl] = (o * gate[:, sl]).astype(o_ref.dtype)

    @pl.when(n == pl.num_programs(1) - 1)
    def _():
        sfin_ref[0] = st_scr[...]


def _hgrn_lower_bound_is_safe(lb):
    return jnp.min(lb) >= math.exp(-HG_SAFE_EXPONENT / (HG_CHUNK // 2))


def _hgrn_prompt(proj, lb, nw, *, batch, seq, guarded, prev=None):
    c = HG_CHUNK
    nchunk = seq // c
    blk = lambda j: pl.BlockSpec((c, HG_KW), lambda b, n: (b * nchunk + n, j))
    vec = pl.BlockSpec((1, HG_KW), lambda b, n: (0, 0))
    blocks = 4 * _nbytes((c, HG_KW), F32) + _nbytes((c, HG_VW), BF16) + 2 * _nbytes((HG_HEADS, HG_DV, HG_DK), F32)
    guard_scratch = [pltpu.VMEM((c, HG_KW), F32)] * 4 if guarded else []
    body, xspecs, xops, alias = _shared_output(functools.partial(_hgrn_prompt_body, guarded=guarded), 6, prev)
    o, st = pl.pallas_call(
        body,
        grid=(batch, nchunk),
        in_specs=[blk(0), blk(1), blk(2), blk(3), vec, vec] + xspecs,
        out_specs=[pl.BlockSpec((c, HG_VW), lambda b, n: (b * nchunk + n, 0)),
                   pl.BlockSpec((1, HG_HEADS, HG_DV, HG_DK), lambda b, n: (b, 0, 0, 0))],
        out_shape=[jax.ShapeDtypeStruct((proj.shape[0], MIX0_WIDTH), BF16),
                   jax.ShapeDtypeStruct((batch, HG_HEADS, HG_DV, HG_DK), F32)],
        scratch_shapes=[pltpu.VMEM((HG_HEADS, HG_DV, HG_DK), F32)] + guard_scratch,
        input_output_aliases=alias,
        compiler_params=_params(("arbitrary", "arbitrary"), blocks),
        name="hgrn_prompt_guarded" if guarded else "hgrn_prompt",
    )(proj, proj, proj, proj, lb.reshape(1, HG_KW), nw.reshape(1, HG_VW), *xops)
    return o, jnp.swapaxes(st, -1, -2)


HG_SAMPLE_BB = 8


def _hgrn_sample_body(q_ref, f_ref, v_ref, g_ref, lb_ref, nw_ref, s_ref, o_ref, so_ref, *, seq):
    rows = 2 * seq
    assert rows == V7X_SUBLANES
    row = lax.broadcasted_iota(jnp.int32, (rows, rows), 0)
    col = lax.broadcasted_iota(jnp.int32, (rows, rows), 1)
    causal = (row >= col) & ((row // seq) == (col // seq))
    rvec = lax.broadcasted_iota(jnp.int32, (rows, 1), 0)
    r16 = lax.broadcasted_iota(jnp.int32, (2 * rows, HG_DV), 0)
    ones_rows = jnp.where((r16 == rows) | (r16 == rows + 1), 1.0, 0.0).astype(BF16)

    heads = [slice(h * HG_DK, (h + 1) * HG_DK) for h in range(HG_HEADS)]
    in_seq = rvec % seq

    def pair(p, carry):
        r0 = pl.multiple_of(p * rows, rows)
        tile = pl.ds(r0, rows)
        q, k, logf = _hgrn_gates(q_ref[tile, :], f_ref[tile, :], lb_ref[...])
        v = v_ref[tile, :].astype(BF16)
        cum = logf
        step = 1
        while step < seq:
            cum = cum + jnp.where(in_seq >= step, _shift_rows(cum, step), 0.0)
            step *= 2
        ecum = jnp.exp(cum)
        qi = (q * ecum).astype(BF16)
        km = (k / ecum).astype(BF16)
        gate = nw_ref[...] * _silu(g_ref[tile, :])
        v16 = jnp.concatenate([v, jnp.zeros_like(v)], axis=0)
        lhs = []
        for j in range(2):
            mine = (rvec // seq) == j
            last = cum[(j + 1) * seq - 1:(j + 1) * seq, :]
            a = jnp.exp(last)
            a_hi = a.astype(BF16).astype(F32)
            kl = jnp.where(mine, k * jnp.exp(last - cum), 0.0)
            lhs.append(jnp.concatenate([kl, a_hi, a - a_hi, jnp.zeros((rows - 2, HG_KW), F32)], axis=0).astype(BF16))
        scores = [jnp.where(causal, _dot(qi[:, sl], km[:, sl], NT), 0.0).astype(BF16) for sl in heads]
        outs = [_dot(scores[h], v[:, sl], NN) for h, sl in enumerate(heads)]
        for j in range(2):
            b = 2 * p + j
            mine = (rvec // seq) == j
            s0 = [s_ref[b, h] for h in range(HG_HEADS)]
            inter = [_dot(qi[:, sl], s0[h].astype(BF16), NN) for h, sl in enumerate(heads)]
            both = [_dot(lhs[j][:, sl], jnp.concatenate([v16[:, sl], ones_rows], axis=1), TN)
                    for sl in heads]
            for h in range(HG_HEADS):
                outs[h] = outs[h] + jnp.where(mine, inter[h], 0.0)
                so_ref[b, h] = both[h][:, HG_DV:] * s0[h] + both[h][:, :HG_DV]
        for h, sl in enumerate(heads):
            o = outs[h]
            o = o * lax.rsqrt(jnp.mean(o * o, -1, keepdims=True) + RMS_EPS)
            o_ref[tile, sl] = (o * gate[:, sl]).astype(o_ref.dtype)
        return carry

    lax.fori_loop(0, HG_SAMPLE_BB // 2, pair, 0)


def _hgrn_sample(proj, row0, state, lb, nw, *, batch, seq, prev=None):
    bb = HG_SAMPLE_BB
    tr = bb * seq
    blk0 = row0 // tr
    blk = lambda j: pl.BlockSpec((tr, HG_KW), lambda i: (blk0 + i, j))
    vec = pl.BlockSpec((1, HG_KW), lambda i: (0, 0))
    sblk = pl.BlockSpec((bb, HG_HEADS, HG_DK, HG_DV), lambda i: (i, 0, 0, 0))
    blocks = 4 * _nbytes((tr, HG_KW), F32) + 2 * _nbytes((bb, HG_HEADS, HG_DK, HG_DV), F32)
    body, xspecs, xops, alias = _shared_output(functools.partial(_hgrn_sample_body, seq=seq), 7, prev)
    return pl.pallas_call(
        body,
        grid=(batch // bb,),
        in_specs=[blk(0), blk(1), blk(2), blk(3), vec, vec, sblk] + xspecs,
        out_specs=[pl.BlockSpec((tr, HG_VW), lambda i: (blk0 + i, 0)), sblk],
        out_shape=[jax.ShapeDtypeStruct((proj.shape[0], MIX0_WIDTH), BF16),
                   jax.ShapeDtypeStruct(state.shape, F32)],
        input_output_aliases=alias,
        compiler_params=_params(("arbitrary",), blocks),
        name="hgrn_sample",
    )(proj, proj, proj, proj, lb.reshape(1, HG_KW), nw.reshape(1, HG_VW), state, *xops)


def _head_expand_matrix():
    r = jnp.arange(2 * V7X_LANES)[:, None] % V7X_LANES
    c = jnp.arange(M_DI)[None, :] // M_HEADDIM
    return (r == c).astype(BF16)


def _expand_heads(coef, e2):
    hi = coef.astype(BF16)
    lo = (coef - hi.astype(F32)).astype(BF16)
    return _dot(jnp.concatenate([hi, lo], axis=1), e2, NN)


def _softplus(x):
    return jnp.maximum(x, 0.0) + jnp.log(1.0 + jnp.exp(-jnp.abs(x)))


def _group_rmsnorm(y, width):
    outs = []
    for g in range(y.shape[1] // width):
        yg = y[:, g * width:(g + 1) * width]
        outs.append(yg * lax.rsqrt(jnp.mean(yg * yg, -1, keepdims=True) + RMS_EPS))
    return jnp.concatenate(outs, axis=1)


def _ssd_prompt_body(z_ref, x_ref, bc_ref, dt_ref, cwx_ref, cwb_ref, cbx_ref, cbb_ref, dtb_ref, alog_ref,
                     dexp_ref, nw_ref, e2_ref, y_ref, sfin_ref, conv_ref, s_scr, cx_scr, cb_scr):
    n = pl.program_id(1)
    c = M_CHUNK
    tail = V7X_SUBLANES

    @pl.when(n == 0)
    def _():
        s_scr[...] = jnp.zeros_like(s_scr)
        cx_scr[...] = jnp.zeros_like(cx_scr)
        cb_scr[...] = jnp.zeros_like(cb_scr)

    def conv(u, carry_scr, w_ref, b_ref):
        ext = jnp.concatenate([carry_scr[...], u], axis=0)
        acc = b_ref[...] + w_ref[M_CONV - 1:M_CONV, :] * u
        for j in range(1, M_CONV):
            acc = acc + w_ref[M_CONV - 1 - j:M_CONV - j, :] * pltpu.roll(ext, j, 0)[tail:, :]
        carry_scr[...] = u[c - tail:, :]
        return _silu(acc)

    ux = x_ref[...]
    ubc = bc_ref[...]
    xc = conv(ux, cx_scr, cwx_ref, cbx_ref)
    bcc = conv(ubc, cb_scr, cwb_ref, cbb_ref)
    ngn = M_GROUPS * M_STATE

    row = lax.broadcasted_iota(jnp.int32, (c, c), 0)
    col = lax.broadcasted_iota(jnp.int32, (c, c), 1)
    causal = row >= col
    tri = causal.astype(F32)
    e2 = e2_ref[...]

    dtp = _softplus(dt_ref[...] + dtb_ref[...])
    da = dtp * (-jnp.exp(alog_ref[...]))
    cum = _dot(tri, da, NN, precision=HIGHEST)
    cum_t = cum.T
    last = cum[c - 1:c, :]
    xdt = xc * _expand_heads(dtp, e2)
    xend = (xdt * _expand_heads(jnp.exp(last - cum), e2)).astype(BF16)
    ecum = _expand_heads(jnp.exp(cum), e2)
    xdt_b = xdt.astype(BF16)
    gw = M_HPG * M_HEADDIM

    ys = []
    for g in range(M_GROUPS):
        bg = bcc[:, g * M_STATE:(g + 1) * M_STATE].astype(BF16)
        cg = bcc[:, ngn + g * M_STATE:ngn + (g + 1) * M_STATE].astype(BF16)
        cb = _dot(cg, bg, NT)
        sg = s_scr[g * M_HPG:(g + 1) * M_HPG].reshape(gw, M_STATE)
        y_inter = _dot(cg, sg.astype(BF16), NT) * ecum[:, g * gw:(g + 1) * gw]
        upd = _dot(xend[:, g * gw:(g + 1) * gw], bg, TN)
        parts = []
        for hh in range(M_HPG):
            h = g * M_HPG + hh
            decay = jnp.exp(jnp.where(causal, cum[:, h:h + 1] - cum_t[h:h + 1, :], -jnp.inf))
            m = (cb * decay).astype(BF16)
            parts.append(_dot(m, xdt_b[:, h * M_HEADDIM:(h + 1) * M_HEADDIM], NN))
            s_scr[h] = jnp.exp(last[:, h:h + 1]) * s_scr[h] + upd[hh * M_HEADDIM:(hh + 1) * M_HEADDIM, :]
        ys.append(jnp.concatenate(parts, axis=1) + y_inter)
    y = jnp.concatenate(ys, axis=1) + dexp_ref[...] * xc
    y = y * _silu(z_ref[...])
    y_ref[...] = (_group_rmsnorm(y, gw) * nw_ref[...]).astype(y_ref.dtype)

    @pl.when(n == pl.num_programs(1) - 1)
    def _():
        sfin_ref[0] = s_scr[...]
        conv_ref[0, :, 0:M_DI] = ux[c - (M_CONV - 1):, :]
        conv_ref[0, :, M_DI:] = ubc[c - (M_CONV - 1):, :]


def _ssd_prompt(proj, dt, conv_w, conv_b, dt_bias, a_log, d_exp, norm_w, e2, *, batch, seq, prev=None):
    c = M_CHUNK
    nchunk = seq // c
    tok = lambda j: pl.BlockSpec((c, M_DI), lambda b, n: (b * nchunk + n, j))
    const = lambda shape, j=0: pl.BlockSpec(shape, lambda b, n: (0, j))
    blocks = (3 * _nbytes((c, M_DI), F32) + _nbytes((c, M_DI), BF16) + _nbytes((2 * V7X_LANES, M_DI), BF16)
              + 2 * _nbytes((M_HEADS, M_HEADDIM, M_STATE), F32) + 12 * _nbytes((c, M_DI), F32))
    body, xspecs, xops, alias = _shared_output(_ssd_prompt_body, 13, prev)
    return pl.pallas_call(
        body,
        grid=(batch, nchunk),
        in_specs=[tok(4), tok(5), tok(6),
                  pl.BlockSpec((c, V7X_LANES), lambda b, n: (b * nchunk + n, 0)),
                  const((M_CONV, M_DI), 0), const((M_CONV, M_DI), 1), const((1, M_DI), 0), const((1, M_DI), 1),
                  const((1, V7X_LANES)), const((1, V7X_LANES)), const((1, M_DI)), const((1, M_DI)),
                  const((2 * V7X_LANES, M_DI))] + xspecs,
        out_specs=[pl.BlockSpec((c, M_DI), lambda b, n: (b * nchunk + n, HG_VW // M_DI)),
                   pl.BlockSpec((1, M_HEADS, M_HEADDIM, M_STATE), lambda b, n: (b, 0, 0, 0)),
                   pl.BlockSpec((1, M_CONV - 1, M_CONV_DIM), lambda b, n: (b, 0, 0))],
        out_shape=[jax.ShapeDtypeStruct((proj.shape[0], MIX0_WIDTH), BF16),
                   jax.ShapeDtypeStruct((batch, M_HEADS, M_HEADDIM, M_STATE), F32),
                   jax.ShapeDtypeStruct((batch, M_CONV - 1, M_CONV_DIM), F32)],
        scratch_shapes=[pltpu.VMEM((M_HEADS, M_HEADDIM, M_STATE), F32),
                        pltpu.VMEM((V7X_SUBLANES, M_DI), F32), pltpu.VMEM((V7X_SUBLANES, M_DI), F32)],
        input_output_aliases=alias,
        compiler_params=_params(("arbitrary", "arbitrary"), blocks),
        name="ssd_prompt",
    )(proj, proj, proj, dt, conv_w, conv_w, conv_b, conv_b, dt_bias, a_log, d_exp, norm_w, e2, *xops)


SSD_SAMPLE_BB = 8


def _shift_rows(a, j):
    return a if j == 0 else pltpu.roll(a, j, 0)


def _ssd_sample_body(z_ref, x_ref, bc_ref, dt_ref, s_ref, cs_ref, cwx_ref, cwb_ref, cbx_ref, cbb_ref, dtb_ref,
                     alog_ref, dexp_ref, nw_ref, e2_ref, y_ref, so_ref, co_ref, *, seq):
    rows = V7X_SUBLANES
    hist = M_CONV - 1
    assert rows == 2 * seq and hist <= seq and hist <= rows - seq
    rvec = lax.broadcasted_iota(jnp.int32, (rows, 1), 0)
    valid = rvec < seq
    lane = lax.broadcasted_iota(jnp.int32, (1, V7X_LANES), 1)
    r16 = lax.broadcasted_iota(jnp.int32, (2 * rows, M_STATE), 0)
    ones_rows = jnp.where((r16 == rows) | (r16 == rows + 1), 1.0, 0.0).astype(BF16)
    e2 = e2_ref[...]
    a_neg = -jnp.exp(alog_ref[...])
    gw = M_HPG * M_HEADDIM
    ngn = M_GROUPS * M_STATE

    def conv(u8, buf, w_ref, b_ref):
        buf8 = jnp.concatenate([buf, jnp.zeros((rows - hist, buf.shape[1]), F32)], axis=0)
        ext = jnp.where(valid, u8, _shift_rows(buf8, rows - hist))
        acc = b_ref[...] + w_ref[hist:hist + 1, :] * ext
        for j in range(1, M_CONV):
            acc = acc + w_ref[hist - j:hist - j + 1, :] * _shift_rows(ext, j)
        new_hist = _shift_rows(ext, rows - (seq - hist))[0:hist, :]
        return _silu(acc), new_hist

    def one(b, u_x, u_bc, z8, dt8):
        xc, nhx = conv(u_x, cs_ref[b, :, 0:M_DI], cwx_ref, cbx_ref)
        bcc, nhb = conv(u_bc, cs_ref[b, :, M_DI:], cwb_ref, cbb_ref)
        co_ref[b, :, 0:M_DI] = nhx
        co_ref[b, :, M_DI:] = nhb
        dtp = jnp.where(valid, _softplus(dt8 + dtb_ref[...]), 0.0)
        cum = dtp * a_neg
        k = 1
        while k < seq:
            cum = cum + jnp.where(rvec >= k, _shift_rows(cum, k), 0.0)
            k *= 2
        last = cum[seq - 1:seq, :]
        bmat = bcc[:, :ngn]
        cmat = bcc[:, ngn:]
        coefs = [dtp, jnp.exp(cum), jnp.exp(last - cum), jnp.broadcast_to(jnp.exp(last), (rows, V7X_LANES))]
        for j in range(seq):
            prod = cmat * _shift_rows(bmat, j)
            cbh = jnp.zeros((rows, V7X_LANES), F32)
            for g in range(M_GROUPS):
                cbg = jnp.sum(prod[:, g * M_STATE:(g + 1) * M_STATE], axis=-1, keepdims=True)
                cbh = jnp.where((lane // M_HPG) == g, cbg, cbh)
            coefs.append(cbh * jnp.exp(cum - _shift_rows(cum, j)))
        wide = _expand_heads(jnp.concatenate(coefs, axis=0), e2)
        part = lambda n: wide[n * rows:(n + 1) * rows, :]
        xdt = xc * part(0)
        ecum = part(1)
        xs = xdt * part(2)
        dec = part(3)[0:1, :]
        y = dexp_ref[...] * xc
        for j in range(seq):
            y = y + part(4 + j) * _shift_rows(xdt, j)
        dec_hi = dec.astype(BF16).astype(F32)
        lhs = jnp.concatenate([xs, dec_hi, dec - dec_hi, jnp.zeros((rows - 2, M_DI), F32)], axis=0).astype(BF16)
        parts = []
        for g in range(M_GROUPS):
            bg = bmat[:, g * M_STATE:(g + 1) * M_STATE].astype(BF16)
            cg = cmat[:, g * M_STATE:(g + 1) * M_STATE].astype(BF16)
            s0 = s_ref[b, g * M_HPG:(g + 1) * M_HPG].reshape(gw, M_STATE)
            parts.append(_dot(cg, s0.astype(BF16), NT) * ecum[:, g * gw:(g + 1) * gw])
            rhs = jnp.concatenate([jnp.concatenate([bg, jnp.zeros((rows, M_STATE), BF16)], axis=0), ones_rows], axis=1)
            both = _dot(lhs[:, g * gw:(g + 1) * gw], rhs, TN)
            s_new = both[:, M_STATE:] * s0 + both[:, :M_STATE]
            so_ref[b, g * M_HPG:(g + 1) * M_HPG] = s_new.reshape(M_HPG, M_HEADDIM, M_STATE)
        y = (y + jnp.concatenate(parts, axis=1)) * _silu(z8)
        return _group_rmsnorm(y, gw) * nw_ref[...]

    def pair(p, carry):
        r0 = pl.multiple_of(p * rows, rows)
        tiles = [ref[pl.ds(r0, rows), :] for ref in (x_ref, bc_ref, z_ref, dt_ref)]
        ys = []
        for j in range(2):
            ys.append(one(2 * p + j, *[_shift_rows(t, j * (rows - seq)) for t in tiles]))
        y_ref[pl.ds(r0, rows), :] = jnp.where(valid, ys[0], _shift_rows(ys[1], seq)).astype(y_ref.dtype)
        return carry

    lax.fori_loop(0, SSD_SAMPLE_BB // 2, pair, 0)


def _ssd_sample(proj, row0, dt, state, conv_state, conv_w, conv_b, dt_bias, a_log, d_exp, norm_w, e2, *,
                batch, seq, prev=None):
    bb = SSD_SAMPLE_BB
    tr = bb * seq
    tok = lambda j: pl.BlockSpec((tr, M_DI), lambda i: (row0 // tr + i, j))
    const = lambda shape, j=0: pl.BlockSpec(shape, lambda i: (0, j))
    sblk = pl.BlockSpec((bb, M_HEADS, M_HEADDIM, M_STATE), lambda i: (i, 0, 0, 0))
    cblk = pl.BlockSpec((bb, M_CONV - 1, M_CONV_DIM), lambda i: (i, 0, 0))
    blocks = (3 * _nbytes((tr, M_DI), F32) + 2 * _nbytes((bb, M_HEADS, M_HEADDIM, M_STATE), F32)
              + 2 * _nbytes((bb, V7X_SUBLANES, M_CONV_DIM), F32) + _nbytes((2 * V7X_LANES, M_DI), BF16))
    body, xspecs, xops, alias = _shared_output(functools.partial(_ssd_sample_body, seq=seq), 15, prev)
    return pl.pallas_call(
        body,
        grid=(batch // bb,),
        in_specs=[tok(4), tok(5), tok(6),
                  pl.BlockSpec((tr, V7X_LANES), lambda i: (row0 // tr + i, 0)),
                  sblk, cblk,
                  const((M_CONV, M_DI), 0), const((M_CONV, M_DI), 1), const((1, M_DI), 0), const((1, M_DI), 1),
                  const((1, V7X_LANES)), const((1, V7X_LANES)), const((1, M_DI)), const((1, M_DI)),
                  const((2 * V7X_LANES, M_DI))] + xspecs,
        out_specs=[pl.BlockSpec((tr, M_DI), lambda i: (row0 // tr + i, HG_VW // M_DI)), sblk, cblk],
        out_shape=[jax.ShapeDtypeStruct((proj.shape[0], MIX0_WIDTH), BF16),
                   jax.ShapeDtypeStruct(state.shape, F32),
                   jax.ShapeDtypeStruct(conv_state.shape, F32)],
        input_output_aliases=alias,
        compiler_params=_params(("arbitrary",), blocks),
        name="ssd_sample",
    )(proj, proj, proj, dt, state, conv_state, conv_w, conv_w, conv_b, conv_b, dt_bias, a_log, d_exp, norm_w, e2, *xops)


S5_TILES = 8
S5_TILE_IN = D_MODEL // S5_TILES
S5_TILE_ST = S5_STATE // S5_TILES


def _s5_discretize(a_re, a_im, log_dt, b_re, b_im, c_re, c_im):
    lam_re = jnp.minimum(a_re, -1e-4)
    lam_im = a_im
    dt = jnp.exp(log_dt)[:, None]
    mag = jnp.exp(lam_re * dt)
    ab_re = mag * jnp.cos(lam_im * dt)
    ab_im = mag * jnp.sin(lam_im * dt)
    den = lam_re * lam_re + lam_im * lam_im
    zr = ((ab_re - 1.0) * lam_re + ab_im * lam_im) / den
    zi = (ab_im * lam_re - (ab_re - 1.0) * lam_im) / den
    bb_re = zr[..., None] * b_re - zi[..., None] * b_im
    bb_im = zr[..., None] * b_im + zi[..., None] * b_re
    gpt = S5_GROUPS // S5_TILES

    def block_diag(blocks, rows, cols):
        rep = jnp.tile(jnp.eye(cols, dtype=F32), (1, gpt))
        wide = jnp.einsum('krc,cn->krn', blocks, rep, precision=HIGHEST)
        own = (jnp.arange(gpt * rows)[:, None] // rows) == (jnp.arange(gpt * cols)[None, :] // cols)
        return jnp.where(own[None], wide, 0.0).astype(BF16)

    def pack_b(bb):
        return block_diag(jnp.swapaxes(bb, 1, 2).reshape(S5_TILES, S5_TILE_IN, S5_P), S5_GROUP, S5_P)

    def pack_c(c):
        return block_diag(jnp.swapaxes(c, 1, 2).reshape(S5_TILES, S5_TILE_ST, S5_GROUP), S5_P, S5_GROUP)

    slab = lambda a: _s5_slab_layout(a.reshape(S5_STATE))
    return slab(ab_re), slab(ab_im), pack_b(bb_re), pack_b(bb_im), pack_c(c_re), pack_c(c_im)


S5_PROMPT_ROWS = 256
S5_SAMPLE_ROWS = 128
S5_NBLK = S5_TILE_ST // V7X_LANES


def _s5_pitch(rows):
    assert rows % V7X_SUBLANES == 0
    return rows + V7X_SUBLANES // 2


def _s5_slab_layout(a):
    lead = a.shape[:-1]
    return jnp.swapaxes(a.reshape(lead + (S5_TILES, S5_NBLK, V7X_LANES)), -3, -2)


def _s5_from_slab_layout(a):
    lead = a.shape[:-3]
    return jnp.swapaxes(a, -3, -2).reshape(lead + (S5_STATE,))


def _s5_fused_body(xb_ref, x_ref, h0r_ref, h0i_ref, ar_ref, ai_ref, wbr_ref, wbi_ref, wcr_ref, wci_ref, d_ref, *rest,
                   n_seq, steps):
    act_ref, fr_ref, fi_ref, bur, bui, cr, ci = rest[-7:]
    hrs, his = bur, bui
    n = pl.program_id(1)
    rows = n_seq * steps
    S5_PITCH = _s5_pitch(rows)
    for s in range(S5_TILES):
        u = xb_ref[:, s * S5_TILE_IN:(s + 1) * S5_TILE_IN]
        br = _dot(u, wbr_ref[s], NN)
        bi = _dot(u, wbi_ref[s], NN)
        for j in range(S5_NBLK):
            bur[j, s * S5_PITCH:s * S5_PITCH + rows, :] = br[:, j * V7X_LANES:(j + 1) * V7X_LANES]
            bui[j, s * S5_PITCH:s * S5_PITCH + rows, :] = bi[:, j * V7X_LANES:(j + 1) * V7X_LANES]

    @pl.when(n == 0)
    def _():
        cr[...] = h0r_ref[...]
        ci[...] = h0i_ref[...]

    ar = [ar_ref[j] for j in range(S5_NBLK)]
    ai = [ai_ref[j] for j in range(S5_NBLK)]

    def seq_body(q, carry):
        def step(t, h):
            hr, hi = h
            idx = pl.ds(q * steps + t, S5_TILES, stride=S5_PITCH)
            nr, ni = [], []
            for j in range(S5_NBLK):
                nr.append(ar[j] * hr[j] - ai[j] * hi[j] + bur[j, idx, :])
                ni.append(ar[j] * hi[j] + ai[j] * hr[j] + bui[j, idx, :])
                hrs[j, idx, :] = nr[j]
                his[j, idx, :] = ni[j]
            return tuple(nr), tuple(ni)

        h0 = (tuple(cr[q, j] for j in range(S5_NBLK)), tuple(ci[q, j] for j in range(S5_NBLK)))
        hr, hi = lax.fori_loop(0, steps, step, h0, unroll=min(steps, 4))
        for j in range(S5_NBLK):
            cr[q, j] = hr[j]
            ci[q, j] = hi[j]
        return carry

    lax.fori_loop(0, n_seq, seq_body, 0)

    for s in range(S5_TILES):
        seg = slice(s * S5_PITCH, s * S5_PITCH + rows)
        hre = jnp.concatenate([hrs[j, seg, :] for j in range(S5_NBLK)], axis=1).astype(BF16)
        him = jnp.concatenate([his[j, seg, :] for j in range(S5_NBLK)], axis=1).astype(BF16)
        y = _dot(hre, wcr_ref[s], NN) - _dot(him, wci_ref[s], NN)
        cs = slice(s * S5_TILE_IN, (s + 1) * S5_TILE_IN)
        act_ref[:, cs] = jax.nn.gelu(y + d_ref[:, cs] * x_ref[:, cs]).astype(act_ref.dtype)

    @pl.when(n == pl.num_programs(1) - 1)
    def _():
        fr_ref[...] = cr[...]
        fi_ref[...] = ci[...]


def _s5_fused(xb, x, h0_re, h0_im, ab_re, ab_im, wb_re, wb_im, wc_re, wc_im, d_skip, *, row0, batch, seq, rows, n_seq,
              prev=None, name):
    t, d = x.shape
    steps = rows // n_seq
    assert n_seq == 1 or steps == seq
    S5_PITCH = _s5_pitch(rows)
    tok = lambda width: pl.BlockSpec((rows, width), lambda i, n: ((row0 + i * n_seq * seq) // rows + n, 0))
    st = pl.BlockSpec((n_seq, S5_NBLK, S5_TILES, V7X_LANES), lambda i, n: (i, 0, 0, 0))
    const = lambda shape: pl.BlockSpec(shape, lambda i, n: (0,) * len(shape), pipeline_mode=pl.Buffered(1))
    slab = pltpu.VMEM((S5_NBLK, S5_TILES * S5_PITCH, V7X_LANES), F32)
    stv = pltpu.VMEM((n_seq, S5_NBLK, S5_TILES, V7X_LANES), F32)
    blocks = (2 * (_nbytes((rows, d), BF16) * 2 + _nbytes((rows, d), F32))
              + 2 * 6 * _nbytes((n_seq, S5_STATE), F32) + 4 * _nbytes(wb_re.shape, BF16)
              + 2 * _nbytes((S5_NBLK, S5_TILES * S5_PITCH, V7X_LANES), F32))
    keep = [] if prev is None else [prev]
    n_in = 11
    return pl.pallas_call(
        functools.partial(_s5_fused_body, n_seq=n_seq, steps=steps),
        grid=(batch // n_seq, seq // steps),
        in_specs=[tok(d), tok(d), st, st, const(ab_re.shape), const(ab_im.shape), const(wb_re.shape),
                  const(wb_im.shape), const(wc_re.shape), const(wc_im.shape), const((1, d))]
                 + [pl.BlockSpec(memory_space=pl.ANY)] * len(keep),
        out_specs=[tok(d), st, st],
        out_shape=[jax.ShapeDtypeStruct((t, d), BF16), jax.ShapeDtypeStruct(h0_re.shape, F32),
                   jax.ShapeDtypeStruct(h0_im.shape, F32)],
        scratch_shapes=[slab, slab, stv, stv],
        input_output_aliases={n_in + k: k for k in range(len(keep))},
        compiler_params=pltpu.CompilerParams(dimension_semantics=("arbitrary", "arbitrary"),
                                             vmem_limit_bytes=int(min(blocks + COMPILER_SCRATCH_BYTES, V7X_SCOPED_VMEM_CAP))),
        name=name,
    )(xb, x, h0_re, h0_im, ab_re, ab_im, wb_re, wb_im, wc_re, wc_im, d_skip.reshape(1, d), *keep)


def _glu_body(a_ref, wa_ref, wb_ref, ba_ref, bb_ref, o_ref, wa_scr, wb_scr):
    @pl.when(pl.program_id(1) == 0)
    def _():
        wa_scr[...] = wa_ref[0].astype(BF16)
        wb_scr[...] = wb_ref[0].astype(BF16)

    a = a_ref[...]
    o_ref[...] = (_dot(a, wa_scr[...], NN) + ba_ref[...]) * jax.nn.sigmoid(_dot(a, wb_scr[...], NN) + bb_ref[...])


def _glu(a, wa, wb, ba, bb, layer, *, tm, tn):
    t, k = a.shape
    n = wa.shape[2]
    wspec = pl.BlockSpec((1, k, tn), lambda j, i: (layer, 0, j))
    bspec = pl.BlockSpec((1, tn), lambda j, i: (layer, j))
    blocks = _nbytes((tm, k), BF16) + 2 * _nbytes((k, tn), F32) + _nbytes((tm, tn), F32) + 2 * _nbytes((k, tn), BF16)
    return pl.pallas_call(
        _glu_body,
        grid=(n // tn, t // tm),
        in_specs=[pl.BlockSpec((tm, k), lambda j, i: (i, 0)), wspec, wspec, bspec, bspec],
        out_specs=pl.BlockSpec((tm, tn), lambda j, i: (i, j)),
        out_shape=jax.ShapeDtypeStruct((t, n), F32),
        scratch_shapes=[pltpu.VMEM((k, tn), BF16)] * 2,
        compiler_params=_params(("arbitrary", "arbitrary"), blocks),
        name="glu",
    )(a, wa, wb, ba, bb)


def _route(x, w_ref, b_ref, e_ref, g_ref):
    w = w_ref[...]
    w_hi = w.astype(BF16)
    w_lo = (w - w_hi.astype(F32)).astype(BF16)
    x_hi = x.astype(BF16)
    x_lo = (x - x_hi.astype(F32)).astype(BF16)
    logits = _dot(w_hi, x_hi, NT) + _dot(w_hi, x_lo, NT) + _dot(w_lo, x_hi, NT) + b_ref[...]
    rows = [logits[e:e + 1, :] for e in range(N_EXPERTS)]
    m = functools.reduce(jnp.maximum, rows)
    ex = [jnp.exp(r - m) for r in rows]
    z = functools.reduce(jnp.add, ex)
    p = [v / z for v in ex]

    def top2_sum(a, b, c, d):
        hi1, lo1, hi2, lo2 = jnp.maximum(a, b), jnp.minimum(a, b), jnp.maximum(c, d), jnp.minimum(c, d)
        return jnp.maximum(hi1, hi2) + jnp.maximum(jnp.minimum(hi1, hi2), jnp.maximum(lo1, lo2))

    assert EXP_PER_GROUP == 4
    best = top2_sum(*p[0:EXP_PER_GROUP])
    gsel = jnp.zeros_like(best, dtype=jnp.int32)
    for g in range(1, N_EXP_GROUPS):
        s = top2_sum(*p[g * EXP_PER_GROUP:(g + 1) * EXP_PER_GROUP])
        better = s > best
        gsel = jnp.where(better, g, gsel)
        best = jnp.where(better, s, best)
    inner = []
    for i in range(EXP_PER_GROUP):
        v = p[i]
        for g in range(1, N_EXP_GROUPS):
            v = jnp.where(gsel == g, p[g * EXP_PER_GROUP + i], v)
        inner.append(v)

    def first_argmax(vals, skip=None):
        bv = bi = None
        for i, v in enumerate(vals):
            v = v if skip is None else jnp.where(skip == i, -jnp.inf, v)
            if bv is None:
                bv, bi = v, jnp.zeros_like(gsel)
            else:
                better = v > bv
                bi = jnp.where(better, i, bi)
                bv = jnp.where(better, v, bv)
        return bv, bi

    p1, i1 = first_argmax(inner)
    p2, i2 = first_argmax(inner, skip=i1)
    tot = p1 + p2
    e1 = gsel * EXP_PER_GROUP + i1
    e2 = gsel * EXP_PER_GROUP + i2
    e_ref[0:1, :] = e1
    e_ref[1:2, :] = e2
    g_ref[0:1, :] = p1 / tot
    g_ref[1:2, :] = p2 / tot


def _ln_router_body(x_ref, x2_ref, mix_ref, g_ref, b_ref, w_ref, br_ref, o_ref, e_ref, gw_ref, *, first_blocks):
    x = x_ref[...] if first_blocks is None else jnp.where(pl.program_id(0) < first_blocks, x_ref[...], x2_ref[...])
    z = ALPHA * x + mix_ref[...]
    mu = jnp.mean(z, -1, keepdims=True)
    zc = z - mu
    var = jnp.mean(zc * zc, -1, keepdims=True)
    y = zc * lax.rsqrt(var + LN_EPS) * g_ref[...] + b_ref[...]
    o_ref[...] = y
    _route(y, w_ref, br_ref, e_ref, gw_ref)


def _res_layernorm_router(x, mix, g, b, w_router, b_router, *, tm, name, x_tail=None):
    t, d = mix.shape
    row = pl.BlockSpec((tm, d), lambda i: (i, 0))
    vec = pl.BlockSpec((1, d), lambda i: (0, 0))
    sel = pl.BlockSpec((TOP_K, tm), lambda i: (0, i))
    if x_tail is None:
        nb, xspecs, xs = None, [row, vec], [x, g.reshape(1, d)]
    else:
        nb = x.shape[0] // tm
        xspecs = [pl.BlockSpec((tm, d), lambda i: (jnp.minimum(i, nb - 1), 0)),
                  pl.BlockSpec((tm, d), lambda i: (jnp.maximum(i - nb, 0), 0))]
        xs = [x, x_tail]
    blocks = 4 * _nbytes((tm, d), F32) + _nbytes((N_EXPERTS, d), F32) + 2 * _nbytes((V7X_SUBLANES, tm), F32)
    return pl.pallas_call(
        functools.partial(_ln_router_body, first_blocks=nb),
        grid=(t // tm,),
        in_specs=xspecs + [row, vec, vec, pl.BlockSpec((N_EXPERTS, d), lambda i: (0, 0)),
                           pl.BlockSpec((N_EXPERTS, 1), lambda i: (0, 0))],
        out_specs=[row, sel, sel],
        out_shape=[jax.ShapeDtypeStruct((t, d), F32), jax.ShapeDtypeStruct((TOP_K, t), jnp.int32),
                   jax.ShapeDtypeStruct((TOP_K, t), F32)],
        compiler_params=_params(("arbitrary",), blocks),
        name=name,
    )(*xs, mix, g.reshape(1, d), b.reshape(1, d), w_router.T, b_router.reshape(N_EXPERTS, 1))


MOE_TILE = 256
COMBINE_TILE = 512


def _moe_plan(eidx):
    k, t = eidx.shape
    ns = k * t
    n_tiles = ns // MOE_TILE + N_EXPERTS
    e_flat = eidx.reshape(ns)
    order = jnp.argsort(e_flat, stable=True).astype(jnp.int32)
    rank = jnp.argsort(order).astype(jnp.int32)
    experts = jnp.arange(N_EXPERTS, dtype=jnp.int32)[None, :]
    slot_is = e_flat[:, None] == experts
    counts = jnp.sum(slot_is.astype(jnp.int32), axis=0)
    start = jnp.cumsum(counts) - counts
    tiles = (counts + MOE_TILE - 1) // MOE_TILE
    tile_end = jnp.cumsum(tiles)
    tile_start = tile_end - tiles
    tile_ids = jnp.arange(n_tiles, dtype=jnp.int32)
    tile_expert = jnp.minimum(jnp.sum((tile_ids[:, None] >= tile_end[None, :]).astype(jnp.int32), axis=1),
                              N_EXPERTS - 1)
    pick = lambda onehot, table: jnp.sum(jnp.where(onehot, table[None, :], 0), axis=1)
    tile_is = tile_expert[:, None] == experts
    tile_first = pick(tile_is, start) + (tile_ids - pick(tile_is, tile_start)) * MOE_TILE
    position_of_slot = pick(slot_is, tile_start) * MOE_TILE + rank - pick(slot_is, start)
    sorted_token = jnp.pad(order % t, (0, MOE_TILE))
    return (tile_expert.astype(jnp.int32), tile_first.astype(jnp.int32), tile_end[-1:].astype(jnp.int32),
            sorted_token.astype(jnp.int32), position_of_slot.astype(jnp.int32))


def _moe_expert_body(te_ref, first_ref, nt_ref, tok_ref, x_hbm, wg_ref, wu_ref, wd_ref, y_ref, xbuf, sem, wg_scr,
                     wu_scr, wd_scr):
    i = pl.program_id(0)
    nt = nt_ref[0]
    tm = MOE_TILE
    sub = V7X_SUBLANES

    def gather_copy(tok, slot, r8, j):
        return pltpu.make_async_copy(x_hbm.at[pl.ds(tok, 1)], xbuf.at[slot, r8, pl.ds(j, 1)], sem.at[slot])

    def start_gather(tile, slot):
        first = first_ref[tile]

        def body(r8, c):
            for j in range(sub):
                gather_copy(tok_ref[first + r8 * sub + j], slot, r8, j).start(priority=j % 2)
            return c

        lax.fori_loop(0, tm // sub, body, 0)

    @pl.when(i == 0)
    def _():
        start_gather(0, 0)

    @pl.when(i + 1 < nt)
    def _():
        start_gather(i + 1, (i + 1) % 2)

    @pl.when(i < nt)
    def _():
        slot = i % 2
        for r8 in range(tm // sub):
            pltpu.make_async_copy(x_hbm.at[pl.ds(0, sub)], xbuf.at[slot, r8], sem.at[slot]).wait()

        @pl.when((i == 0) | (te_ref[i] != te_ref[jnp.maximum(i - 1, 0)]))
        def _():
            wg_scr[...] = wg_ref[0, 0].astype(BF16)
            wu_scr[...] = wu_ref[0, 0].astype(BF16)
            wd_scr[...] = wd_ref[0, 0].astype(BF16)

        x = xbuf[slot].reshape(tm, x_hbm.shape[1]).astype(BF16)
        act = (_silu(_dot(x, wg_scr[...], NN)) * _dot(x, wu_scr[...], NN)).astype(BF16)
        y_ref[...] = _dot(act, wd_scr[...], NN)

    @pl.when(i >= nt)
    def _():
        y_ref[...] = jnp.zeros_like(y_ref)


def _moe_experts(x, plan, w_gate, w_up, w_down, layer):
    tile_expert, tile_first, n_valid, sorted_token, _ = plan
    t, d = x.shape
    f = w_gate.shape[3]
    n_tiles = tile_expert.shape[0]
    wspec = lambda a, b: pl.BlockSpec((1, 1, a, b), lambda i, te, *_: (layer, te[i], 0, 0))
    blocks = (2 * 3 * _nbytes((d, f), F32) + 3 * _nbytes((d, f), BF16) + 2 * _nbytes((MOE_TILE, d), F32)
              + 2 * _nbytes((MOE_TILE, d), F32) + 4 * _nbytes((MOE_TILE, f), F32))
    return pl.pallas_call(
        _moe_expert_body,
        grid_spec=pltpu.PrefetchScalarGridSpec(
            num_scalar_prefetch=4,
            grid=(n_tiles,),
            in_specs=[pl.BlockSpec(memory_space=pl.ANY), wspec(d, f), wspec(d, f), wspec(f, d)],
            out_specs=pl.BlockSpec((MOE_TILE, d), lambda i, *_: (i, 0)),
            scratch_shapes=[pltpu.VMEM((2, MOE_TILE // V7X_SUBLANES, V7X_SUBLANES, d), F32),
                            pltpu.SemaphoreType.DMA((2,)),
                            pltpu.VMEM((d, f), BF16), pltpu.VMEM((d, f), BF16), pltpu.VMEM((f, d), BF16)]),
        out_shape=jax.ShapeDtypeStruct((n_tiles * MOE_TILE, d), F32),
        compiler_params=pltpu.CompilerParams(dimension_semantics=("arbitrary",),
                                             vmem_limit_bytes=int(min(blocks + COMPILER_SCRATCH_BYTES, V7X_SCOPED_VMEM_CAP))),
        name="moe_experts",
    )(tile_expert, tile_first, n_valid, sorted_token, x, w_gate, w_up, w_down)


def _moe_combine_body(pos_ref, x_ref, w_ref, g_ref, b_ref, y_hbm, o_ref, ob_ref, ybuf, sem, *, n_tok, split_blocks):
    i = pl.program_id(0)
    tc = COMBINE_TILE

    sub = V7X_SUBLANES

    def gather_copy(p, slot, k, r8, j):
        return pltpu.make_async_copy(y_hbm.at[pl.ds(p, 1)], ybuf.at[slot, k, r8, pl.ds(j, 1)], sem.at[slot])

    def start_gather(tile, slot):
        def body(r8, c):
            for k in range(TOP_K):
                for j in range(sub):
                    gather_copy(pos_ref[k * n_tok + tile * tc + r8 * sub + j], slot, k, r8, j).start(priority=j % 2)
            return c

        lax.fori_loop(0, tc // sub, body, 0)

    @pl.when(i == 0)
    def _():
        start_gather(0, 0)

    @pl.when(i + 1 < pl.num_programs(0))
    def _():
        start_gather(i + 1, (i + 1) % 2)

    slot = i % 2
    for k in range(TOP_K):
        for r8 in range(tc // sub):
            pltpu.make_async_copy(y_hbm.at[pl.ds(0, sub)], ybuf.at[slot, k, r8], sem.at[slot]).wait()
    z = ALPHA * x_ref[...]
    for k in range(TOP_K):
        z = z + w_ref[:, k:k + 1] * ybuf[slot, k].reshape(tc, x_ref.shape[1])
    mu = jnp.mean(z, -1, keepdims=True)
    zc = z - mu
    var = jnp.mean(zc * zc, -1, keepdims=True)
    y = zc * lax.rsqrt(var + LN_EPS) * g_ref[...] + b_ref[...]
    if split_blocks is None:
        o_ref[...] = y
        ob_ref[...] = y.astype(BF16)
    else:
        @pl.when(i < split_blocks)
        def _():
            o_ref[...] = y

        @pl.when(i >= split_blocks)
        def _():
            ob_ref[...] = y


def _moe_combine_layernorm(x, y_slots, plan, gate_w, g, b, *, name, split=None):
    position_of_slot = plan[4]
    t, d = x.shape
    tc = COMBINE_TILE
    row = pl.BlockSpec((tc, d), lambda i, pos: (i, 0))
    vec = pl.BlockSpec((1, d), lambda i, pos: (0, 0))
    blocks = 2 * (3 * _nbytes((tc, d), F32) + _nbytes((tc, d), BF16)) + 2 * TOP_K * _nbytes((tc, d), F32)
    if split is None:
        sb = None
        out_specs = [row, row]
        out_shape = [jax.ShapeDtypeStruct((t, d), F32), jax.ShapeDtypeStruct((t, d), BF16)]
    else:
        sb = split // tc
        out_specs = [pl.BlockSpec((tc, d), lambda i, pos: (jnp.minimum(i, sb - 1), 0)),
                     pl.BlockSpec((tc, d), lambda i, pos: (jnp.maximum(i - sb, 0), 0))]
        out_shape = [jax.ShapeDtypeStruct((split, d), F32), jax.ShapeDtypeStruct((t - split, d), F32)]
    return pl.pallas_call(
        functools.partial(_moe_combine_body, n_tok=t, split_blocks=sb),
        grid_spec=pltpu.PrefetchScalarGridSpec(
            num_scalar_prefetch=1,
            grid=(t // tc,),
            in_specs=[row, pl.BlockSpec((tc, TOP_K), lambda i, pos: (i, 0)), vec, vec,
                      pl.BlockSpec(memory_space=pl.ANY)],
            out_specs=out_specs,
            scratch_shapes=[pltpu.VMEM((2, TOP_K, tc // V7X_SUBLANES, V7X_SUBLANES, d), F32),
                            pltpu.SemaphoreType.DMA((2,))]),
        out_shape=out_shape,
        compiler_params=pltpu.CompilerParams(dimension_semantics=("arbitrary",),
                                             vmem_limit_bytes=int(min(blocks + COMPILER_SCRATCH_BYTES, V7X_SCOPED_VMEM_CAP))),
        name=name,
    )(position_of_slot, x, gate_w, g.reshape(1, d), b.reshape(1, d), y_slots)


TOKEN_TILE = 1088
IN_PROJ_TN = 1024
LN_TILE = 512


def _moe_layernorm(x, eidx, gate_w, layer, w_gate, w_up, w_down, g, b, split=None):
    plan = _moe_plan(eidx)
    y_slots = _moe_experts(x, plan, w_gate, w_up, w_down, layer)
    return _moe_combine_layernorm(x, y_slots, plan, gate_w.T, g, b, name=f"moe_combine_ln_{layer}", split=split)


def kernel(x_prompt, x_sample, state_hgrn, state_ssm, state_conv, state_s5_re, state_s5_im, w_in0, hg_lb_logits, hg_norm_w, conv_w, conv_b, dt_bias, a_log, m_d, m_norm_w, w_out0, s5_a_re, s5_a_im, s5_log_dt, s5_b_re, s5_b_im, s5_c_re, s5_c_im, s5_d, glu_w_a, glu_b_a, glu_w_b, glu_b_b, w_router, b_router, w_gate, w_up, w_down, ln1_g, ln1_b, ln2_g, ln2_b):
    bp, lp, d = x_prompt.shape
    bs, ls, _ = x_sample.shape
    tp, ts = bp * lp, bs * ls
    tm = TOKEN_TILE
    xp2, xs2 = x_prompt.reshape(tp, d), x_sample.reshape(ts, d)
    x0b = _join_bf16(xp2, xs2, tm=LN_TILE)
    lower_bounds = jnp.cumsum(jax.nn.softmax(hg_lb_logits.astype(F32), axis=0), axis=0)
    pad_lanes = lambda v: jnp.pad(v, (0, V7X_LANES - v.shape[0])).reshape(1, V7X_LANES)

    proj = _matmul_w32(x0b, jnp.swapaxes(w_in0, 1, 2), 0, tm=tm, tn=IN_PROJ_TN, n_cols=IN0_MAIN,
                       transposed=True, name="in_proj")
    w_dt = jnp.pad(w_in0[0, :, IN0_MAIN:], ((0, 0), (0, V7X_LANES - M_HEADS))).astype(BF16)
    dt = _matmul(x0b, w_dt, tm=tm, tn=V7X_LANES, name="dt_proj")
    lb0 = lower_bounds[0]
    e2 = _head_expand_matrix()
    ssd_consts = (conv_w[0], conv_b[0].reshape(1, -1), pad_lanes(dt_bias[0]), pad_lanes(a_log[0]),
                  jnp.repeat(m_d[0], M_HEADDIM).reshape(1, M_DI), m_norm_w[0].reshape(1, M_DI), e2)
    hgrn_prompt = functools.partial(_hgrn_prompt, proj, lb0, hg_norm_w[0], batch=bp, seq=lp)
    mixed, hg_p = lax.cond(_hgrn_lower_bound_is_safe(lb0), functools.partial(hgrn_prompt, guarded=False),
                           functools.partial(hgrn_prompt, guarded=True))
    mixed, hg_s = _hgrn_sample(proj, tp, state_hgrn[0], lb0, hg_norm_w[0], batch=bs, seq=ls, prev=mixed)
    mixed, ssm_p, conv_p = _ssd_prompt(proj, dt, *ssd_consts, batch=bp, seq=lp, prev=mixed)
    mixed, ssm_s, conv_s = _ssd_sample(proj, tp, dt, state_ssm[0], state_conv[0], *ssd_consts, batch=bs, seq=ls,
                                       prev=mixed)
    mix = _matmul_w32(mixed, w_out0, 0, tm=tm, tn=512, name="out_proj")
    x1, eidx, gate_w = _res_layernorm_router(xp2, mix, ln1_g[0], ln1_b[0], w_router, b_router, tm=LN_TILE,
                                             name="ln1_router_0", x_tail=xs2)
    x2, x2b = _moe_layernorm(x1, eidx, gate_w, 0, w_gate, w_up, w_down, ln2_g[0], ln2_b[0])

    ab_re, ab_im, wb_re, wb_im, wc_re, wc_im = _s5_discretize(
        s5_a_re[0], s5_a_im[0], s5_log_dt[0], s5_b_re[0], s5_b_im[0], s5_c_re[0], s5_c_im[0])
    s5_consts = (ab_re, ab_im, wb_re, wb_im, wc_re, wc_im, s5_d[0])
    zeros = jnp.zeros((bp, S5_NBLK, S5_TILES, V7X_LANES), F32)
    act, s5r_p, s5i_p = _s5_fused(x2b, x2, zeros, zeros, *s5_consts, row0=0, batch=bp, seq=lp,
                                  rows=S5_PROMPT_ROWS, n_seq=1, name="s5_prompt")
    act, s5r_s, s5i_s = _s5_fused(x2b, x2, _s5_slab_layout(state_s5_re[0].reshape(bs, S5_STATE)),
                                  _s5_slab_layout(state_s5_im[0].reshape(bs, S5_STATE)), *s5_consts,
                                  row0=tp, batch=bs, seq=ls, rows=S5_SAMPLE_ROWS, n_seq=S5_SAMPLE_ROWS // ls,
                                  prev=act, name="s5_sample")
    mix1 = _glu(act, glu_w_a, glu_w_b, glu_b_a, glu_b_b, 0, tm=tm, tn=512)
    x3, eidx, gate_w = _res_layernorm_router(x2, mix1, ln1_g[1], ln1_b[1], w_router, b_router, tm=LN_TILE,
                                             name="ln1_router_1")
    y_p, y_s = _moe_layernorm(x3, eidx, gate_w, 1, w_gate, w_up, w_down, ln2_g[1], ln2_b[1], split=tp)

    s5_state = lambda a, b: _s5_from_slab_layout(a).reshape(1, b, S5_GROUPS, S5_P)
    return (y_p.reshape(bp, lp, d), y_s.reshape(bs, ls, d),
            hg_p[None], hg_s[None], ssm_p[None], ssm_s[None], conv_p[None], conv_s[None],
            s5_state(s5r_p, bp), s5_state(s5r_s, bs), s5_state(s5i_p, bp), s5_state(s5i_s, bs))
```

```python
import functools
import math

import jax
import jax.numpy as jnp
from jax import lax
from jax.experimental import pallas as pl
from jax.experimental.pallas import tpu as pltpu

F32 = jnp.float32
BF16 = jnp.bfloat16
HIGHEST = lax.Precision.HIGHEST

D_MODEL = 2048
DEPTH = 2
HG_HEADS = 16
HG_DK = 128
HG_DV = 128
HG_KW = HG_HEADS * HG_DK
HG_VW = HG_HEADS * HG_DV
HG_CHUNK = 64
M_DI = 2048
M_HEADDIM = 64
M_HEADS = 32
M_GROUPS = 8
M_HPG = 4
M_STATE = 128
M_CONV = 4
M_CONV_DIM = M_DI + 2 * M_GROUPS * M_STATE
M_CHUNK = 128
IN0_MAIN = 2 * HG_KW + 2 * HG_VW + M_DI + M_CONV_DIM
S5_GROUP = 16
S5_GROUPS = 128
S5_P = 64
S5_STATE = S5_GROUPS * S5_P
N_EXPERTS = 16
N_EXP_GROUPS = 4
EXP_PER_GROUP = 4
TOP_K = 2
D_EXPERT = 512
ALPHA = (2 * DEPTH) ** 0.25
LN_EPS = 1e-5
RMS_EPS = 1e-6

V7X_LANES = 128
V7X_SUBLANES = 8
V7X_SCOPED_VMEM_CAP = 60000 * 1024
COMPILER_SCRATCH_BYTES = 16 * 1024 * 1024


def _params(semantics, block_bytes):
    limit = min(2 * block_bytes + COMPILER_SCRATCH_BYTES, V7X_SCOPED_VMEM_CAP)
    return pltpu.CompilerParams(dimension_semantics=semantics, vmem_limit_bytes=int(limit))


def _nbytes(shape, dtype):
    return math.prod(shape) * jnp.dtype(dtype).itemsize


def _silu(x):
    return x * jax.nn.sigmoid(x)


def _dot(a, b, dims, precision=None):
    return lax.dot_general(a, b, (dims, ((), ())), precision=precision, preferred_element_type=F32)


NN = ((1,), (0,))
NT = ((1,), (1,))
TN = ((0,), (0,))

MIX0_WIDTH = HG_VW + M_DI


def _shared_output(body, n_in, prev):
    if prev is None:
        return body, [], [], {}

    def with_prev(*refs):
        return body(*refs[:n_in], *refs[n_in + 1:])

    return with_prev, [pl.BlockSpec(memory_space=pl.ANY)], [prev], {n_in: 0}


def _join_bf16_body(a_ref, b_ref, o_ref, *, first_blocks):
    o_ref[...] = jnp.where(pl.program_id(0) < first_blocks, a_ref[...], b_ref[...]).astype(o_ref.dtype)


def _join_bf16(a, b, *, tm):
    d = a.shape[1]
    na, nb = a.shape[0] // tm, b.shape[0] // tm
    blocks = 2 * _nbytes((tm, d), F32) + _nbytes((tm, d), BF16)
    return pl.pallas_call(
        functools.partial(_join_bf16_body, first_blocks=na),
        grid=(na + nb,),
        in_specs=[pl.BlockSpec((tm, d), lambda i: (jnp.minimum(i, na - 1), 0)),
                  pl.BlockSpec((tm, d), lambda i: (jnp.maximum(i - na, 0), 0))],
        out_specs=pl.BlockSpec((tm, d), lambda i: (i, 0)),
        out_shape=jax.ShapeDtypeStruct((a.shape[0] + b.shape[0], d), BF16),
        compiler_params=_params(("arbitrary",), blocks),
        name="join_bf16",
    )(a, b)


def _mm_body(x_ref, w_ref, o_ref):
    o_ref[...] = _dot(x_ref[...], w_ref[...], NN).astype(o_ref.dtype)


def _matmul(x, w, *, tm, tn, n_cols=None, out_dtype=F32, name):
    m, k = x.shape
    n = n_cols if n_cols is not None else w.shape[1]
    blocks = _nbytes((tm, k), x.dtype) + _nbytes((k, tn), w.dtype) + _nbytes((tm, tn), out_dtype)
    return pl.pallas_call(
        _mm_body,
        grid=(n // tn, m // tm),
        in_specs=[pl.BlockSpec((tm, k), lambda j, i: (i, 0)), pl.BlockSpec((k, tn), lambda j, i: (0, j))],
        out_specs=pl.BlockSpec((tm, tn), lambda j, i: (i, j)),
        out_shape=jax.ShapeDtypeStruct((m, n), out_dtype),
        compiler_params=_params(("arbitrary", "arbitrary"), blocks),
        name=name,
    )(x, w)


def _mm_w32_body(x_ref, w_ref, o_ref, wb_scr):
    @pl.when(pl.program_id(1) == 0)
    def _():
        wb_scr[...] = w_ref[0].astype(BF16)

    o_ref[...] = _dot(x_ref[...], wb_scr[...], NN).astype(o_ref.dtype)


def _mm_w32t_body(x_ref, w_ref, o_ref, wb_scr):
    @pl.when(pl.program_id(1) == 0)
    def _():
        wb_scr[...] = w_ref[0].astype(BF16)

    o_ref[...] = _dot(x_ref[...], wb_scr[...], NT).astype(o_ref.dtype)


def _matmul_w32(x, w, layer, *, tm, tn, n_cols=None, transposed=False, name):
    m, k = x.shape
    n = n_cols if n_cols is not None else w.shape[1 if transposed else 2]
    blocks = _nbytes((tm, k), BF16) + _nbytes((k, tn), F32) + _nbytes((tm, tn), F32) + _nbytes((k, tn), BF16)
    if transposed:
        body, wblock, wspec = _mm_w32t_body, (tn, k), pl.BlockSpec((1, tn, k), lambda j, i: (layer, j, 0))
    else:
        body, wblock, wspec = _mm_w32_body, (k, tn), pl.BlockSpec((1, k, tn), lambda j, i: (layer, 0, j))
    return pl.pallas_call(
        body,
        grid=(n // tn, m // tm),
        in_specs=[pl.BlockSpec((tm, k), lambda j, i: (i, 0)), wspec],
        out_specs=pl.BlockSpec((tm, tn), lambda j, i: (i, j)),
        out_shape=jax.ShapeDtypeStruct((m, n), F32),
        scratch_shapes=[pltpu.VMEM(wblock, BF16)],
        compiler_params=_params(("arbitrary", "arbitrary"), blocks),
        name=name,
    )(x, w)


def _hgrn_gates(q, f, lb):
    fg = lb + (1.0 - lb) * jax.nn.sigmoid(f)
    return _silu(q), 1.0 - fg, jnp.log(fg)


HG_SAFE_EXPONENT = 80.0


def _hgrn_exact_intra(q, k, cum, v_ref, heads, intra_scr, q_scr, k_scr, cum_scr):
    c = cum.shape[0]
    q_scr[...] = q
    k_scr[...] = k
    cum_scr[...] = cum
    unsafe = jnp.max(jnp.abs(cum - cum[c // 2 - 1:c // 2, :])) > HG_SAFE_EXPONENT

    @pl.when(unsafe)
    def _():
        srow = lax.broadcasted_iota(jnp.int32, (c, 1), 0)
        sub = V7X_SUBLANES

        def row_tile(t8, carry):
            base = pl.multiple_of(t8 * sub, sub)
            cum_t = cum_scr[pl.ds(base, sub), :]
            q_t = q_scr[pl.ds(base, sub), :]
            rows_out = [[] for _ in heads]
            for j in range(sub):
                decay = jnp.exp(jnp.minimum(cum_t[j:j + 1, :] - cum_scr[...], 0.0))
                w = jnp.where(srow <= base + j, q_t[j:j + 1, :] * k_scr[...] * decay, 0.0)
                for h, sl in enumerate(heads):
                    coef = jnp.sum(w[:, sl], axis=-1, keepdims=True)
                    rows_out[h].append(jnp.sum(coef * v_ref[:, sl], axis=0, keepdims=True))
            for h, sl in enumerate(heads):
                intra_scr[pl.ds(base, sub), sl] = jnp.concatenate(rows_out[h], axis=0)
            return carry

        lax.fori_loop(0, c // sub, row_tile, 0)

    return unsafe


def _hgrn_prompt_body(q_ref, f_ref, v_ref, g_ref, lb_ref, nw_ref, o_ref, sfin_ref, st_scr, *guard_scr, guarded):
    n = pl.program_id(1)
    c = HG_CHUNK

    @pl.when(n == 0)
    def _():
        st_scr[...] = jnp.zeros_like(st_scr)
        if guarded:
            guard_scr[0][...] = jnp.zeros_like(guard_scr[0])

    row = lax.broadcasted_iota(jnp.int32, (c, c), 0)
    col = lax.broadcasted_iota(jnp.int32, (c, c), 1)
    causal = row >= col
    tri = causal.astype(F32)
    heads = [slice(h * HG_DK, (h + 1) * HG_DK) for h in range(HG_HEADS)]
    q, k, logf = _hgrn_gates(q_ref[...], f_ref[...], lb_ref[...])
    cum = _dot(tri, logf, NN, precision=HIGHEST)
    if guarded:
        unsafe = _hgrn_exact_intra(q, k, cum, v_ref, heads, *guard_scr)
        q, k, cum = (r[...] for r in guard_scr[1:])
    v = v_ref[...].astype(BF16)
    mid = cum[c // 2 - 1:c // 2, :]
    last = cum[c - 1:c, :]
    qm = (q * jnp.exp(cum - mid)).astype(BF16)
    km = (k * jnp.exp(mid - cum)).astype(BF16)
    qi = (q * jnp.exp(cum)).astype(BF16)
    kl = (k * jnp.exp(last - cum)).astype(BF16)
    a_last = jnp.exp(last)
    gate = nw_ref[...] * _silu(g_ref[...])
    scores = [jnp.where(causal, _dot(qm[:, sl], km[:, sl], NT), 0.0).astype(BF16) for sl in heads]
    sts = [st_scr[h] for h in range(HG_HEADS)]
    if guarded:
        outs = [_dot(qi[:, sl], sts[h].astype(BF16), NT)
                + jnp.where(unsafe, guard_scr[0][:, sl], _dot(scores[h], v[:, sl], NN)) for h, sl in enumerate(heads)]
    else:
        outs = [_dot(qi[:, sl], sts[h].astype(BF16), NT) + _dot(scores[h], v[:, sl], NN)
                for h, sl in enumerate(heads)]
    for h, sl in enumerate(heads):
        st_scr[h] = a_last[:, sl] * sts[h] + _dot(v[:, sl], kl[:, sl], TN)
    for h, sl in enumerate(heads):
        o = outs[h]
        o = o * lax.rsqrt(jnp.mean(o * o, -1, keepdims=True) + RMS_EPS)
        o_ref[:, sl] = (o * gate[:, sl]).astype(o_ref.dtype)

    @pl.when(n == pl.num_programs(1) - 1)
    def _():
        sfin_ref[0] = st_scr[...]


def _hgrn_lower_bound_is_safe(lb):
    return jnp.min(lb) >= math.exp(-HG_SAFE_EXPONENT / (HG_CHUNK // 2))


def _hgrn_prompt(proj, lb, nw, *, batch, seq, guarded, prev=None):
    c = HG_CHUNK
    nchunk = seq // c
    blk = lambda j: pl.BlockSpec((c, HG_KW), lambda b, n: (b * nchunk + n, j))
    vec = pl.BlockSpec((1, HG_KW), lambda b, n: (0, 0))
    blocks = 4 * _nbytes((c, HG_KW), F32) + _nbytes((c, HG_VW), BF16) + 2 * _nbytes((HG_HEADS, HG_DV, HG_DK), F32)
    guard_scratch = [pltpu.VMEM((c, HG_KW), F32)] * 4 if guarded else []
    body, xspecs, xops, alias = _shared_output(functools.partial(_hgrn_prompt_body, guarded=guarded), 6, prev)
    o, st = pl.pallas_call(
        body,
        grid=(batch, nchunk),
        in_specs=[blk(0), blk(1), blk(2), blk(3), vec, vec] + xspecs,
        out_specs=[pl.BlockSpec((c, HG_VW), lambda b, n: (b * nchunk + n, 0)),
                   pl.BlockSpec((1, HG_HEADS, HG_DV, HG_DK), lambda b, n: (b, 0, 0, 0))],
        out_shape=[jax.ShapeDtypeStruct((proj.shape[0], MIX0_WIDTH), BF16),
                   jax.ShapeDtypeStruct((batch, HG_HEADS, HG_DV, HG_DK), F32)],
        scratch_shapes=[pltpu.VMEM((HG_HEADS, HG_DV, HG_DK), F32)] + guard_scratch,
        input_output_aliases=alias,
        compiler_params=_params(("arbitrary", "arbitrary"), blocks),
        name="hgrn_prompt_guarded" if guarded else "hgrn_prompt",
    )(proj, proj, proj, proj, lb.reshape(1, HG_KW), nw.reshape(1, HG_VW), *xops)
    return o, jnp.swapaxes(st, -1, -2)


HG_SAMPLE_BB = 8


def _hgrn_sample_body(q_ref, f_ref, v_ref, g_ref, lb_ref, nw_ref, s_ref, o_ref, so_ref, *, seq):
    rows = 2 * seq
    assert rows == V7X_SUBLANES
    row = lax.broadcasted_iota(jnp.int32, (rows, rows), 0)
    col = lax.broadcasted_iota(jnp.int32, (rows, rows), 1)
    causal = (row >= col) & ((row // seq) == (col // seq))
    rvec = lax.broadcasted_iota(jnp.int32, (rows, 1), 0)
    r16 = lax.broadcasted_iota(jnp.int32, (2 * rows, HG_DV), 0)
    ones_rows = jnp.where((r16 == rows) | (r16 == rows + 1), 1.0, 0.0).astype(BF16)

    heads = [slice(h * HG_DK, (h + 1) * HG_DK) for h in range(HG_HEADS)]
    in_seq = rvec % seq

    def pair(p, carry):
        r0 = pl.multiple_of(p * rows, rows)
        tile = pl.ds(r0, rows)
        q, k, logf = _hgrn_gates(q_ref[tile, :], f_ref[tile, :], lb_ref[...])
        v = v_ref[tile, :].astype(BF16)
        cum = logf
        step = 1
        while step < seq:
            cum = cum + jnp.where(in_seq >= step, _shift_rows(cum, step), 0.0)
            step *= 2
        ecum = jnp.exp(cum)
        qi = (q * ecum).astype(BF16)
        km = (k / ecum).astype(BF16)
        gate = nw_ref[...] * _silu(g_ref[tile, :])
        v16 = jnp.concatenate([v, jnp.zeros_like(v)], axis=0)
        lhs = []
        for j in range(2):
            mine = (rvec // seq) == j
            last = cum[(j + 1) * seq - 1:(j + 1) * seq, :]
            a = jnp.exp(last)
            a_hi = a.astype(BF16).astype(F32)
            kl = jnp.where(mine, k * jnp.exp(last - cum), 0.0)
            lhs.append(jnp.concatenate([kl, a_hi, a - a_hi, jnp.zeros((rows - 2, HG_KW), F32)], axis=0).astype(BF16))
        scores = [jnp.where(causal, _dot(qi[:, sl], km[:, sl], NT), 0.0).astype(BF16) for sl in heads]
        outs = [_dot(scores[h], v[:, sl], NN) for h, sl in enumerate(heads)]
        for j in range(2):
            b = 2 * p + j
            mine = (rvec // seq) == j
            s0 = [s_ref[b, h] for h in range(HG_HEADS)]
            inter = [_dot(qi[:, sl], s0[h].astype(BF16), NN) for h, sl in enumerate(heads)]
            both = [_dot(lhs[j][:, sl], jnp.concatenate([v16[:, sl], ones_rows], axis=1), TN)
                    for sl in heads]
            for h in range(HG_HEADS):
                outs[h] = outs[h] + jnp.where(mine, inter[h], 0.0)
                so_ref[b, h] = both[h][:, HG_DV:] * s0[h] + both[h][:, :HG_DV]
        for h, sl in enumerate(heads):
            o = outs[h]
            o = o * lax.rsqrt(jnp.mean(o * o, -1, keepdims=True) + RMS_EPS)
            o_ref[tile, sl] = (o * gate[:, sl]).astype(o_ref.dtype)
        return carry

    lax.fori_loop(0, HG_SAMPLE_BB // 2, pair, 0)


def _hgrn_sample(proj, row0, state, lb, nw, *, batch, seq, prev=None):
    bb = HG_SAMPLE_BB
    tr = bb * seq
    blk0 = row0 // tr
    blk = lambda j: pl.BlockSpec((tr, HG_KW), lambda i: (blk0 + i, j))
    vec = pl.BlockSpec((1, HG_KW), lambda i: (0, 0))
    sblk = pl.BlockSpec((bb, HG_HEADS, HG_DK, HG_DV), lambda i: (i, 0, 0, 0))
    blocks = 4 * _nbytes((tr, HG_KW), F32) + 2 * _nbytes((bb, HG_HEADS, HG_DK, HG_DV), F32)
    body, xspecs, xops, alias = _shared_output(functools.partial(_hgrn_sample_body, seq=seq), 7, prev)
    return pl.pallas_call(
        body,
        grid=(batch // bb,),
        in_specs=[blk(0), blk(1), blk(2), blk(3), vec, vec, sblk] + xspecs,
        out_specs=[pl.BlockSpec((tr, HG_VW), lambda i: (blk0 + i, 0)), sblk],
        out_shape=[jax.ShapeDtypeStruct((proj.shape[0], MIX0_WIDTH), BF16),
                   jax.ShapeDtypeStruct(state.shape, F32)],
        input_output_aliases=alias,
        compiler_params=_params(("arbitrary",), blocks),
        name="hgrn_sample",
    )(proj, proj, proj, proj, lb.reshape(1, HG_KW), nw.reshape(1, HG_VW), state, *xops)


def _head_expand_matrix():
    r = jnp.arange(2 * V7X_LANES)[:, None] % V7X_LANES
    c = jnp.arange(M_DI)[None, :] // M_HEADDIM
    return (r == c).astype(BF16)


def _expand_heads(coef, e2):
    hi = coef.astype(BF16)
    lo = (coef - hi.astype(F32)).astype(BF16)
    return _dot(jnp.concatenate([hi, lo], axis=1), e2, NN)


def _softplus(x):
    return jnp.maximum(x, 0.0) + jnp.log(1.0 + jnp.exp(-jnp.abs(x)))


def _group_rmsnorm(y, width):
    outs = []
    for g in range(y.shape[1] // width):
        yg = y[:, g * width:(g + 1) * width]
        outs.append(yg * lax.rsqrt(jnp.mean(yg * yg, -1, keepdims=True) + RMS_EPS))
    return jnp.concatenate(outs, axis=1)


def _ssd_prompt_body(z_ref, x_ref, bc_ref, dt_ref, cwx_ref, cwb_ref, cbx_ref, cbb_ref, dtb_ref, alog_ref,
                     dexp_ref, nw_ref, e2_ref, y_ref, sfin_ref, conv_ref, s_scr, cx_scr, cb_scr):
    n = pl.program_id(1)
    c = M_CHUNK
    tail = V7X_SUBLANES

    @pl.when(n == 0)
    def _():
        s_scr[...] = jnp.zeros_like(s_scr)
        cx_scr[...] = jnp.zeros_like(cx_scr)
        cb_scr[...] = jnp.zeros_like(cb_scr)

    def conv(u, carry_scr, w_ref, b_ref):
        ext = jnp.concatenate([carry_scr[...], u], axis=0)
        acc = b_ref[...] + w_ref[M_CONV - 1:M_CONV, :] * u
        for j in range(1, M_CONV):
            acc = acc + w_ref[M_CONV - 1 - j:M_CONV - j, :] * pltpu.roll(ext, j, 0)[tail:, :]
        carry_scr[...] = u[c - tail:, :]
        return _silu(acc)

    ux = x_ref[...]
    ubc = bc_ref[...]
    xc = conv(ux, cx_scr, cwx_ref, cbx_ref)
    bcc = conv(ubc, cb_scr, cwb_ref, cbb_ref)
    ngn = M_GROUPS * M_STATE

    row = lax.broadcasted_iota(jnp.int32, (c, c), 0)
    col = lax.broadcasted_iota(jnp.int32, (c, c), 1)
    causal = row >= col
    tri = causal.astype(F32)
    e2 = e2_ref[...]

    dtp = _softplus(dt_ref[...] + dtb_ref[...])
    da = dtp * (-jnp.exp(alog_ref[...]))
    cum = _dot(tri, da, NN, precision=HIGHEST)
    cum_t = cum.T
    last = cum[c - 1:c, :]
    xdt = xc * _expand_heads(dtp, e2)
    xend = (xdt * _expand_heads(jnp.exp(last - cum), e2)).astype(BF16)
    ecum = _expand_heads(jnp.exp(cum), e2)
    xdt_b = xdt.astype(BF16)
    gw = M_HPG * M_HEADDIM

    ys = []
    for g in range(M_GROUPS):
        bg = bcc[:, g * M_STATE:(g + 1) * M_STATE].astype(BF16)
        cg = bcc[:, ngn + g * M_STATE:ngn + (g + 1) * M_STATE].astype(BF16)
        cb = _dot(cg, bg, NT)
        sg = s_scr[g * M_HPG:(g + 1) * M_HPG].reshape(gw, M_STATE)
        y_inter = _dot(cg, sg.astype(BF16), NT) * ecum[:, g * gw:(g + 1) * gw]
        upd = _dot(xend[:, g * gw:(g + 1) * gw], bg, TN)
        parts = []
        for hh in range(M_HPG):
            h = g * M_HPG + hh
            decay = jnp.exp(jnp.where(causal, cum[:, h:h + 1] - cum_t[h:h + 1, :], -jnp.inf))
            m = (cb * decay).astype(BF16)
            parts.append(_dot(m, xdt_b[:, h * M_HEADDIM:(h + 1) * M_HEADDIM], NN))
            s_scr[h] = jnp.exp(last[:, h:h + 1]) * s_scr[h] + upd[hh * M_HEADDIM:(hh + 1) * M_HEADDIM, :]
        ys.append(jnp.concatenate(parts, axis=1) + y_inter)
    y = jnp.concatenate(ys, axis=1) + dexp_ref[...] * xc
    y = y * _silu(z_ref[...])
    y_ref[...] = (_group_rmsnorm(y, gw) * nw_ref[...]).astype(y_ref.dtype)

    @pl.when(n == pl.num_programs(1) - 1)
    def _():
        sfin_ref[0] = s_scr[...]
        conv_ref[0, :, 0:M_DI] = ux[c - (M_CONV - 1):, :]
        conv_ref[0, :, M_DI:] = ubc[c - (M_CONV - 1):, :]


def _ssd_prompt(proj, dt, conv_w, conv_b, dt_bias, a_log, d_exp, norm_w, e2, *, batch, seq, prev=None):
    c = M_CHUNK
    nchunk = seq // c
    tok = lambda j: pl.BlockSpec((c, M_DI), lambda b, n: (b * nchunk + n, j))
    const = lambda shape, j=0: pl.BlockSpec(shape, lambda b, n: (0, j))
    blocks = (3 * _nbytes((c, M_DI), F32) + _nbytes((c, M_DI), BF16) + _nbytes((2 * V7X_LANES, M_DI), BF16)
              + 2 * _nbytes((M_HEADS, M_HEADDIM, M_STATE), F32) + 12 * _nbytes((c, M_DI), F32))
    body, xspecs, xops, alias = _shared_output(_ssd_prompt_body, 13, prev)
    return pl.pallas_call(
        body,
        grid=(batch, nchunk),
        in_specs=[tok(4), tok(5), tok(6),
                  pl.BlockSpec((c, V7X_LANES), lambda b, n: (b * nchunk + n, 0)),
                  const((M_CONV, M_DI), 0), const((M_CONV, M_DI), 1), const((1, M_DI), 0), const((1, M_DI), 1),
                  const((1, V7X_LANES)), const((1, V7X_LANES)), const((1, M_DI)), const((1, M_DI)),
                  const((2 * V7X_LANES, M_DI))] + xspecs,
        out_specs=[pl.BlockSpec((c, M_DI), lambda b, n: (b * nchunk + n, HG_VW // M_DI)),
                   pl.BlockSpec((1, M_HEADS, M_HEADDIM, M_STATE), lambda b, n: (b, 0, 0, 0)),
                   pl.BlockSpec((1, M_CONV - 1, M_CONV_DIM), lambda b, n: (b, 0, 0))],
        out_shape=[jax.ShapeDtypeStruct((proj.shape[0], MIX0_WIDTH), BF16),
                   jax.ShapeDtypeStruct((batch, M_HEADS, M_HEADDIM, M_STATE), F32),
                   jax.ShapeDtypeStruct((batch, M_CONV - 1, M_CONV_DIM), F32)],
        scratch_shapes=[pltpu.VMEM((M_HEADS, M_HEADDIM, M_STATE), F32),
                        pltpu.VMEM((V7X_SUBLANES, M_DI), F32), pltpu.VMEM((V7X_SUBLANES, M_DI), F32)],
        input_output_aliases=alias,
        compiler_params=_params(("arbitrary", "arbitrary"), blocks),
        name="ssd_prompt",
    )(proj, proj, proj, dt, conv_w, conv_w, conv_b, conv_b, dt_bias, a_log, d_exp, norm_w, e2, *xops)


SSD_SAMPLE_BB = 8


def _shift_rows(a, j):
    return a if j == 0 else pltpu.roll(a, j, 0)


def _ssd_sample_body(z_ref, x_ref, bc_ref, dt_ref, s_ref, cs_ref, cwx_ref, cwb_ref, cbx_ref, cbb_ref, dtb_ref,
                     alog_ref, dexp_ref, nw_ref, e2_ref, y_ref, so_ref, co_ref, *, seq):
    rows = V7X_SUBLANES
    hist = M_CONV - 1
    assert rows == 2 * seq and hist <= seq and hist <= rows - seq
    rvec = lax.broadcasted_iota(jnp.int32, (rows, 1), 0)
    valid = rvec < seq
    lane = lax.broadcasted_iota(jnp.int32, (1, V7X_LANES), 1)
    r16 = lax.broadcasted_iota(jnp.int32, (2 * rows, M_STATE), 0)
    ones_rows = jnp.where((r16 == rows) | (r16 == rows + 1), 1.0, 0.0).astype(BF16)
    e2 = e2_ref[...]
    a_neg = -jnp.exp(alog_ref[...])
    gw = M_HPG * M_HEADDIM
    ngn = M_GROUPS * M_STATE

    def conv(u8, buf, w_ref, b_ref):
        buf8 = jnp.concatenate([buf, jnp.zeros((rows - hist, buf.shape[1]), F32)], axis=0)
        ext = jnp.where(valid, u8, _shift_rows(buf8, rows - hist))
        acc = b_ref[...] + w_ref[hist:hist + 1, :] * ext
        for j in range(1, M_CONV):
            acc = acc + w_ref[hist - j:hist - j + 1, :] * _shift_rows(ext, j)
        new_hist = _shift_rows(ext, rows - (seq - hist))[0:hist, :]
        return _silu(acc), new_hist

    def one(b, u_x, u_bc, z8, dt8):
        xc, nhx = conv(u_x, cs_ref[b, :, 0:M_DI], cwx_ref, cbx_ref)
        bcc, nhb = conv(u_bc, cs_ref[b, :, M_DI:], cwb_ref, cbb_ref)
        co_ref[b, :, 0:M_DI] = nhx
        co_ref[b, :, M_DI:] = nhb
        dtp = jnp.where(valid, _softplus(dt8 + dtb_ref[...]), 0.0)
        cum = dtp * a_neg
        k = 1
        while k < seq:
            cum = cum + jnp.where(rvec >= k, _shift_rows(cum, k), 0.0)
            k *= 2
        last = cum[seq - 1:seq, :]
        bmat = bcc[:, :ngn]
        cmat = bcc[:, ngn:]
        coefs = [dtp, jnp.exp(cum), jnp.exp(last - cum), jnp.broadcast_to(jnp.exp(last), (rows, V7X_LANES))]
        for j in range(seq):
            prod = cmat * _shift_rows(bmat, j)
            cbh = jnp.zeros((rows, V7X_LANES), F32)
            for g in range(M_GROUPS):
                cbg = jnp.sum(prod[:, g * M_STATE:(g + 1) * M_STATE], axis=-1, keepdims=True)
                cbh = jnp.where((lane // M_HPG) == g, cbg, cbh)
            coefs.append(cbh * jnp.exp(cum - _shift_rows(cum, j)))
        wide = _expand_heads(jnp.concatenate(coefs, axis=0), e2)
        part = lambda n: wide[n * rows:(n + 1) * rows, :]
        xdt = xc * part(0)
        ecum = part(1)
        xs = xdt * part(2)
        dec = part(3)[0:1, :]
        y = dexp_ref[...] * xc
        for j in range(seq):
            y = y + part(4 + j) * _shift_rows(xdt, j)
        dec_hi = dec.astype(BF16).astype(F32)
        lhs = jnp.concatenate([xs, dec_hi, dec - dec_hi, jnp.zeros((rows - 2, M_DI), F32)], axis=0).astype(BF16)
        parts = []
        for g in range(M_GROUPS):
            bg = bmat[:, g * M_STATE:(g + 1) * M_STATE].astype(BF16)
            cg = cmat[:, g * M_STATE:(g + 1) * M_STATE].astype(BF16)
            s0 = s_ref[b, g * M_HPG:(g + 1) * M_HPG].reshape(gw, M_STATE)
            parts.append(_dot(cg, s0.astype(BF16), NT) * ecum[:, g * gw:(g + 1) * gw])
            rhs = jnp.concatenate([jnp.concatenate([bg, jnp.zeros((rows, M_STATE), BF16)], axis=0), ones_rows], axis=1)
            both = _dot(lhs[:, g * gw:(g + 1) * gw], rhs, TN)
            s_new = both[:, M_STATE:] * s0 + both[:, :M_STATE]
            so_ref[b, g * M_HPG:(g + 1) * M_HPG] = s_new.reshape(M_HPG, M_HEADDIM, M_STATE)
        y = (y + jnp.concatenate(parts, axis=1)) * _silu(z8)
        return _group_rmsnorm(y, gw) * nw_ref[...]

    def pair(p, carry):
        r0 = pl.multiple_of(p * rows, rows)
        tiles = [ref[pl.ds(r0, rows), :] for ref in (x_ref, bc_ref, z_ref, dt_ref)]
        ys = []
        for j in range(2):
            ys.append(one(2 * p + j, *[_shift_rows(t, j * (rows - seq)) for t in tiles]))
        y_ref[pl.ds(r0, rows), :] = jnp.where(valid, ys[0], _shift_rows(ys[1], seq)).astype(y_ref.dtype)
        return carry

    lax.fori_loop(0, SSD_SAMPLE_BB // 2, pair, 0)


def _ssd_sample(proj, row0, dt, state, conv_state, conv_w, conv_b, dt_bias, a_log, d_exp, norm_w, e2, *,
                batch, seq, prev=None):
    bb = SSD_SAMPLE_BB
    tr = bb * seq
    tok = lambda j: pl.BlockSpec((tr, M_DI), lambda i: (row0 // tr + i, j))
    const = lambda shape, j=0: pl.BlockSpec(shape, lambda i: (0, j))
    sblk = pl.BlockSpec((bb, M_HEADS, M_HEADDIM, M_STATE), lambda i: (i, 0, 0, 0))
    cblk = pl.BlockSpec((bb, M_CONV - 1, M_CONV_DIM), lambda i: (i, 0, 0))
    blocks = (3 * _nbytes((tr, M_DI), F32) + 2 * _nbytes((bb, M_HEADS, M_HEADDIM, M_STATE), F32)
              + 2 * _nbytes((bb, V7X_SUBLANES, M_CONV_DIM), F32) + _nbytes((2 * V7X_LANES, M_DI), BF16))
    body, xspecs, xops, alias = _shared_output(functools.partial(_ssd_sample_body, seq=seq), 15, prev)
    return pl.pallas_call(
        body,
        grid=(batch // bb,),
        in_specs=[tok(4), tok(5), tok(6),
                  pl.BlockSpec((tr, V7X_LANES), lambda i: (row0 // tr + i, 0)),
                  sblk, cblk,
                  const((M_CONV, M_DI), 0), const((M_CONV, M_DI), 1), const((1, M_DI), 0), const((1, M_DI), 1),
                  const((1, V7X_LANES)), const((1, V7X_LANES)), const((1, M_DI)), const((1, M_DI)),
                  const((2 * V7X_LANES, M_DI))] + xspecs,
        out_specs=[pl.BlockSpec((tr, M_DI), lambda i: (row0 // tr + i, HG_VW // M_DI)), sblk, cblk],
        out_shape=[jax.ShapeDtypeStruct((proj.shape[0], MIX0_WIDTH), BF16),
                   jax.ShapeDtypeStruct(state.shape, F32),
                   jax.ShapeDtypeStruct(conv_state.shape, F32)],
        input_output_aliases=alias,
        compiler_params=_params(("arbitrary",), blocks),
        name="ssd_sample",
    )(proj, proj, proj, dt, state, conv_state, conv_w, conv_w, conv_b, conv_b, dt_bias, a_log, d_exp, norm_w, e2, *xops)


S5_TILES = 8
S5_TILE_IN = D_MODEL // S5_TILES
S5_TILE_ST = S5_STATE // S5_TILES


def _s5_discretize(a_re, a_im, log_dt, b_re, b_im, c_re, c_im):
    lam_re = jnp.minimum(a_re, -1e-4)
    lam_im = a_im
    dt = jnp.exp(log_dt)[:, None]
    mag = jnp.exp(lam_re * dt)
    ab_re = mag * jnp.cos(lam_im * dt)
    ab_im = mag * jnp.sin(lam_im * dt)
    den = lam_re * lam_re + lam_im * lam_im
    zr = ((ab_re - 1.0) * lam_re + ab_im * lam_im) / den
    zi = (ab_im * lam_re - (ab_re - 1.0) * lam_im) / den
    bb_re = zr[..., None] * b_re - zi[..., None] * b_im
    bb_im = zr[..., None] * b_im + zi[..., None] * b_re
    gpt = S5_GROUPS // S5_TILES

    def block_diag(blocks, rows, cols):
        rep = jnp.tile(jnp.eye(cols, dtype=F32), (1, gpt))
        wide = jnp.einsum('krc,cn->krn', blocks, rep, precision=HIGHEST)
        own = (jnp.arange(gpt * rows)[:, None] // rows) == (jnp.arange(gpt * cols)[None, :] // cols)
        return jnp.where(own[None], wide, 0.0).astype(BF16)

    def pack_b(bb):
        return block_diag(jnp.swapaxes(bb, 1, 2).reshape(S5_TILES, S5_TILE_IN, S5_P), S5_GROUP, S5_P)

    def pack_c(c):
        return block_diag(jnp.swapaxes(c, 1, 2).reshape(S5_TILES, S5_TILE_ST, S5_GROUP), S5_P, S5_GROUP)

    slab = lambda a: _s5_slab_layout(a.reshape(S5_STATE))
    return slab(ab_re), slab(ab_im), pack_b(bb_re), pack_b(bb_im), pack_c(c_re), pack_c(c_im)


S5_PROMPT_ROWS = 256
S5_SAMPLE_ROWS = 128
S5_NBLK = S5_TILE_ST // V7X_LANES


def _s5_pitch(rows):
    assert rows % V7X_SUBLANES == 0
    return rows + V7X_SUBLANES // 2


def _s5_slab_layout(a):
    lead = a.shape[:-1]
    return jnp.swapaxes(a.reshape(lead + (S5_TILES, S5_NBLK, V7X_LANES)), -3, -2)


def _s5_from_slab_layout(a):
    lead = a.shape[:-3]
    return jnp.swapaxes(a, -3, -2).reshape(lead + (S5_STATE,))


def _s5_fused_body(xb_ref, x_ref, h0r_ref, h0i_ref, ar_ref, ai_ref, wbr_ref, wbi_ref, wcr_ref, wci_ref, d_ref, *rest,
                   n_seq, steps):
    act_ref, fr_ref, fi_ref, bur, bui, cr, ci = rest[-7:]
    hrs, his = bur, bui
    n = pl.program_id(1)
    rows = n_seq * steps
    S5_PITCH = _s5_pitch(rows)
    for s in range(S5_TILES):
        u = xb_ref[:, s * S5_TILE_IN:(s + 1) * S5_TILE_IN]
        br = _dot(u, wbr_ref[s], NN)
        bi = _dot(u, wbi_ref[s], NN)
        for j in range(S5_NBLK):
            bur[j, s * S5_PITCH:s * S5_PITCH + rows, :] = br[:, j * V7X_LANES:(j + 1) * V7X_LANES]
            bui[j, s * S5_PITCH:s * S5_PITCH + rows, :] = bi[:, j * V7X_LANES:(j + 1) * V7X_LANES]

    @pl.when(n == 0)
    def _():
        cr[...] = h0r_ref[...]
        ci[...] = h0i_ref[...]

    ar = [ar_ref[j] for j in range(S5_NBLK)]
    ai = [ai_ref[j] for j in range(S5_NBLK)]

    def seq_body(q, carry):
        def step(t, h):
            hr, hi = h
            idx = pl.ds(q * steps + t, S5_TILES, stride=S5_PITCH)
            nr, ni = [], []
            for j in range(S5_NBLK):
                nr.append(ar[j] * hr[j] - ai[j] * hi[j] + bur[j, idx, :])
                ni.append(ar[j] * hi[j] + ai[j] * hr[j] + bui[j, idx, :])
                hrs[j, idx, :] = nr[j]
                his[j, idx, :] = ni[j]
            return tuple(nr), tuple(ni)

        h0 = (tuple(cr[q, j] for j in range(S5_NBLK)), tuple(ci[q, j] for j in range(S5_NBLK)))
        hr, hi = lax.fori_loop(0, steps, step, h0, unroll=min(steps, 4))
        for j in range(S5_NBLK):
            cr[q, j] = hr[j]
            ci[q, j] = hi[j]
        return carry

    lax.fori_loop(0, n_seq, seq_body, 0)

    for s in range(S5_TILES):
        seg = slice(s * S5_PITCH, s * S5_PITCH + rows)
        hre = jnp.concatenate([hrs[j, seg, :] for j in range(S5_NBLK)], axis=1).astype(BF16)
        him = jnp.concatenate([his[j, seg, :] for j in range(S5_NBLK)], axis=1).astype(BF16)
        y = _dot(hre, wcr_ref[s], NN) - _dot(him, wci_ref[s], NN)
        cs = slice(s * S5_TILE_IN, (s + 1) * S5_TILE_IN)
        act_ref[:, cs] = jax.nn.gelu(y + d_ref[:, cs] * x_ref[:, cs]).astype(act_ref.dtype)

    @pl.when(n == pl.num_programs(1) - 1)
    def _():
        fr_ref[...] = cr[...]
        fi_ref[...] = ci[...]


def _s5_fused(xb, x, h0_re, h0_im, ab_re, ab_im, wb_re, wb_im, wc_re, wc_im, d_skip, *, row0, batch, seq, rows, n_seq,
              prev=None, name):
    t, d = x.shape
    steps = rows // n_seq
    assert n_seq == 1 or steps == seq
    S5_PITCH = _s5_pitch(rows)
    tok = lambda width: pl.BlockSpec((rows, width), lambda i, n: ((row0 + i * n_seq * seq) // rows + n, 0))
    st = pl.BlockSpec((n_seq, S5_NBLK, S5_TILES, V7X_LANES), lambda i, n: (i, 0, 0, 0))
    const = lambda shape: pl.BlockSpec(shape, lambda i, n: (0,) * len(shape), pipeline_mode=pl.Buffered(1))
    slab = pltpu.VMEM((S5_NBLK, S5_TILES * S5_PITCH, V7X_LANES), F32)
    stv = pltpu.VMEM((n_seq, S5_NBLK, S5_TILES, V7X_LANES), F32)
    blocks = (2 * (_nbytes((rows, d), BF16) * 2 + _nbytes((rows, d), F32))
              + 2 * 6 * _nbytes((n_seq, S5_STATE), F32) + 4 * _nbytes(wb_re.shape, BF16)
              + 2 * _nbytes((S5_NBLK, S5_TILES * S5_PITCH, V7X_LANES), F32))
    keep = [] if prev is None else [prev]
    n_in = 11
    return pl.pallas_call(
        functools.partial(_s5_fused_body, n_seq=n_seq, steps=steps),
        grid=(batch // n_seq, seq // steps),
        in_specs=[tok(d), tok(d), st, st, const(ab_re.shape), const(ab_im.shape), const(wb_re.shape),
                  const(wb_im.shape), const(wc_re.shape), const(wc_im.shape), const((1, d))]
                 + [pl.BlockSpec(memory_space=pl.ANY)] * len(keep),
        out_specs=[tok(d), st, st],
        out_shape=[jax.ShapeDtypeStruct((t, d), BF16), jax.ShapeDtypeStruct(h0_re.shape, F32),
                   jax.ShapeDtypeStruct(h0_im.shape, F32)],
        scratch_shapes=[slab, slab, stv, stv],
        input_output_aliases={n_in + k: k for k in range(len(keep))},
        compiler_params=pltpu.CompilerParams(dimension_semantics=("arbitrary", "arbitrary"),
                                             vmem_limit_bytes=int(min(blocks + COMPILER_SCRATCH_BYTES, V7X_SCOPED_VMEM_CAP))),
        name=name,
    )(xb, x, h0_re, h0_im, ab_re, ab_im, wb_re, wb_im, wc_re, wc_im, d_skip.reshape(1, d), *keep)


def _glu_body(a_ref, wa_ref, wb_ref, ba_ref, bb_ref, o_ref, wa_scr, wb_scr):
    @pl.when(pl.program_id(1) == 0)
    def _():
        wa_scr[...] = wa_ref[0].astype(BF16)
        wb_scr[...] = wb_ref[0].astype(BF16)

    a = a_ref[...]
    o_ref[...] = (_dot(a, wa_scr[...], NN) + ba_ref[...]) * jax.nn.sigmoid(_dot(a, wb_scr[...], NN) + bb_ref[...])


def _glu(a, wa, wb, ba, bb, layer, *, tm, tn):
    t, k = a.shape
    n = wa.shape[2]
    wspec = pl.BlockSpec((1, k, tn), lambda j, i: (layer, 0, j))
    bspec = pl.BlockSpec((1, tn), lambda j, i: (layer, j))
    blocks = _nbytes((tm, k), BF16) + 2 * _nbytes((k, tn), F32) + _nbytes((tm, tn), F32) + 2 * _nbytes((k, tn), BF16)
    return pl.pallas_call(
        _glu_body,
        grid=(n // tn, t // tm),
        in_specs=[pl.BlockSpec((tm, k), lambda j, i: (i, 0)), wspec, wspec, bspec, bspec],
        out_specs=pl.BlockSpec((tm, tn), lambda j, i: (i, j)),
        out_shape=jax.ShapeDtypeStruct((t, n), F32),
        scratch_shapes=[pltpu.VMEM((k, tn), BF16)] * 2,
        compiler_params=_params(("arbitrary", "arbitrary"), blocks),
        name="glu",
    )(a, wa, wb, ba, bb)


def _route(x, w_ref, b_ref, e_ref, g_ref):
    w = w_ref[...]
    w_hi = w.astype(BF16)
    w_lo = (w - w_hi.astype(F32)).astype(BF16)
    x_hi = x.astype(BF16)
    x_lo = (x - x_hi.astype(F32)).astype(BF16)
    logits = _dot(w_hi, x_hi, NT) + _dot(w_hi, x_lo, NT) + _dot(w_lo, x_hi, NT) + b_ref[...]
    rows = [logits[e:e + 1, :] for e in range(N_EXPERTS)]
    m = functools.reduce(jnp.maximum, rows)
    ex = [jnp.exp(r - m) for r in rows]
    z = functools.reduce(jnp.add, ex)
    p = [v / z for v in ex]

    def top2_sum(a, b, c, d):
        hi1, lo1, hi2, lo2 = jnp.maximum(a, b), jnp.minimum(a, b), jnp.maximum(c, d), jnp.minimum(c, d)
        return jnp.maximum(hi1, hi2) + jnp.maximum(jnp.minimum(hi1, hi2), jnp.maximum(lo1, lo2))

    assert EXP_PER_GROUP == 4
    best = top2_sum(*p[0:EXP_PER_GROUP])
    gsel = jnp.zeros_like(best, dtype=jnp.int32)
    for g in range(1, N_EXP_GROUPS):
        s = top2_sum(*p[g * EXP_PER_GROUP:(g + 1) * EXP_PER_GROUP])
        better = s > best
        gsel = jnp.where(better, g, gsel)
        best = jnp.where(better, s, best)
    inner = []
    for i in range(EXP_PER_GROUP):
        v = p[i]
        for g in range(1, N_EXP_GROUPS):
            v = jnp.where(gsel == g, p[g * EXP_PER_GROUP + i], v)
        inner.append(v)

    def first_argmax(vals, skip=None):
        bv = bi = None
        for i, v in enumerate(vals):
            v = v if skip is None else jnp.where(skip == i, -jnp.inf, v)
            if bv is None:
                bv, bi = v, jnp.zeros_like(gsel)
            else:
                better = v > bv
                bi = jnp.where(better, i, bi)
                bv = jnp.where(better, v, bv)
        return bv, bi

    p1, i1 = first_argmax(inner)
    p2, i2 = first_argmax(inner, skip=i1)
    tot = p1 + p2
    e1 = gsel * EXP_PER_GROUP + i1
    e2 = gsel * EXP_PER_GROUP + i2
    e_ref[0:1, :] = e1
    e_ref[1:2, :] = e2
    g_ref[0:1, :] = p1 / tot
    g_ref[1:2, :] = p2 / tot


def _ln_router_body(x_ref, x2_ref, mix_ref, g_ref, b_ref, w_ref, br_ref, o_ref, e_ref, gw_ref, *, first_blocks):
    x = x_ref[...] if first_blocks is None else jnp.where(pl.program_id(0) < first_blocks, x_ref[...], x2_ref[...])
    z = ALPHA * x + mix_ref[...]
    mu = jnp.mean(z, -1, keepdims=True)
    zc = z - mu
    var = jnp.mean(zc * zc, -1, keepdims=True)
    y = zc * lax.rsqrt(var + LN_EPS) * g_ref[...] + b_ref[...]
    o_ref[...] = y
    _route(y, w_ref, br_ref, e_ref, gw_ref)


def _res_layernorm_router(x, mix, g, b, w_router, b_router, *, tm, name, x_tail=None):
    t, d = mix.shape
    row = pl.BlockSpec((tm, d), lambda i: (i, 0))
    vec = pl.BlockSpec((1, d), lambda i: (0, 0))
    sel = pl.BlockSpec((TOP_K, tm), lambda i: (0, i))
    if x_tail is None:
        nb, xspecs, xs = None, [row, vec], [x, g.reshape(1, d)]
    else:
        nb = x.shape[0] // tm
        xspecs = [pl.BlockSpec((tm, d), lambda i: (jnp.minimum(i, nb - 1), 0)),
                  pl.BlockSpec((tm, d), lambda i: (jnp.maximum(i - nb, 0), 0))]
        xs = [x, x_tail]
    blocks = 4 * _nbytes((tm, d), F32) + _nbytes((N_EXPERTS, d), F32) + 2 * _nbytes((V7X_SUBLANES, tm), F32)
    return pl.pallas_call(
        functools.partial(_ln_router_body, first_blocks=nb),
        grid=(t // tm,),
        in_specs=xspecs + [row, vec, vec, pl.BlockSpec((N_EXPERTS, d), lambda i: (0, 0)),
                           pl.BlockSpec((N_EXPERTS, 1), lambda i: (0, 0))],
        out_specs=[row, sel, sel],
        out_shape=[jax.ShapeDtypeStruct((t, d), F32), jax.ShapeDtypeStruct((TOP_K, t), jnp.int32),
                   jax.ShapeDtypeStruct((TOP_K, t), F32)],
        compiler_params=_params(("arbitrary",), blocks),
        name=name,
    )(*xs, mix, g.reshape(1, d), b.reshape(1, d), w_router.T, b_router.reshape(N_EXPERTS, 1))


MOE_TILE = 256
COMBINE_TILE = 256


def _moe_plan(eidx):
    k, t = eidx.shape
    ns = k * t
    n_tiles = ns // MOE_TILE + N_EXPERTS
    e_flat = eidx.reshape(ns)
    order = jnp.argsort(e_flat, stable=True).astype(jnp.int32)
    rank = jnp.argsort(order).astype(jnp.int32)
    experts = jnp.arange(N_EXPERTS, dtype=jnp.int32)[None, :]
    slot_is = e_flat[:, None] == experts
    counts = jnp.sum(slot_is.astype(jnp.int32), axis=0)
    start = jnp.cumsum(counts) - counts
    tiles = (counts + MOE_TILE - 1) // MOE_TILE
    tile_end = jnp.cumsum(tiles)
    tile_start = tile_end - tiles
    tile_ids = jnp.arange(n_tiles, dtype=jnp.int32)
    tile_expert = jnp.minimum(jnp.sum((tile_ids[:, None] >= tile_end[None, :]).astype(jnp.int32), axis=1),
                              N_EXPERTS - 1)
    pick = lambda onehot, table: jnp.sum(jnp.where(onehot, table[None, :], 0), axis=1)
    tile_is = tile_expert[:, None] == experts
    tile_first = pick(tile_is, start) + (tile_ids - pick(tile_is, tile_start)) * MOE_TILE
    position_of_slot = pick(slot_is, tile_start) * MOE_TILE + rank - pick(slot_is, start)
    sorted_token = jnp.pad(order % t, (0, MOE_TILE))
    return (tile_expert.astype(jnp.int32), tile_first.astype(jnp.int32), tile_end[-1:].astype(jnp.int32),
            sorted_token.astype(jnp.int32), position_of_slot.astype(jnp.int32))


def _moe_expert_body(te_ref, first_ref, nt_ref, tok_ref, x_hbm, wg_ref, wu_ref, wd_ref, y_ref, xbuf, sem, wg_scr,
                     wu_scr, wd_scr):
    i = pl.program_id(0)
    nt = nt_ref[0]
    tm = MOE_TILE
    sub = V7X_SUBLANES

    def gather_copy(tok, slot, r8, j):
        return pltpu.make_async_copy(x_hbm.at[pl.ds(tok, 1)], xbuf.at[slot, r8, pl.ds(j, 1)], sem.at[slot])

    def start_gather(tile, slot):
        first = first_ref[tile]

        def body(r8, c):
            for j in range(sub):
                gather_copy(tok_ref[first + r8 * sub + j], slot, r8, j).start(priority=j % 2)
            return c

        lax.fori_loop(0, tm // sub, body, 0)

    @pl.when(i == 0)
    def _():
        start_gather(0, 0)

    @pl.when(i + 1 < nt)
    def _():
        start_gather(i + 1, (i + 1) % 2)

    @pl.when(i < nt)
    def _():
        slot = i % 2
        for r8 in range(tm // sub):
            pltpu.make_async_copy(x_hbm.at[pl.ds(0, sub)], xbuf.at[slot, r8], sem.at[slot]).wait()

        @pl.when((i == 0) | (te_ref[i] != te_ref[jnp.maximum(i - 1, 0)]))
        def _():
            wg_scr[...] = wg_ref[0, 0].astype(BF16)
            wu_scr[...] = wu_ref[0, 0].astype(BF16)
            wd_scr[...] = wd_ref[0, 0].astype(BF16)

        x = xbuf[slot].reshape(tm, x_hbm.shape[1]).astype(BF16)
        act = (_silu(_dot(x, wg_scr[...], NN)) * _dot(x, wu_scr[...], NN)).astype(BF16)
        y_ref[...] = _dot(act, wd_scr[...], NN)

    @pl.when(i >= nt)
    def _():
        y_ref[...] = jnp.zeros_like(y_ref)


def _moe_experts(x, plan, w_gate, w_up, w_down, layer):
    tile_expert, tile_first, n_valid, sorted_token, _ = plan
    t, d = x.shape
    f = w_gate.shape[3]
    n_tiles = tile_expert.shape[0]
    wspec = lambda a, b: pl.BlockSpec((1, 1, a, b), lambda i, te, *_: (layer, te[i], 0, 0))
    blocks = (2 * 3 * _nbytes((d, f), F32) + 3 * _nbytes((d, f), BF16) + 2 * _nbytes((MOE_TILE, d), F32)
              + 2 * _nbytes((MOE_TILE, d), F32) + 4 * _nbytes((MOE_TILE, f), F32))
    return pl.pallas_call(
        _moe_expert_body,
        grid_spec=pltpu.PrefetchScalarGridSpec(
            num_scalar_prefetch=4,
            grid=(n_tiles,),
            in_specs=[pl.BlockSpec(memory_space=pl.ANY), wspec(d, f), wspec(d, f), wspec(f, d)],
            out_specs=pl.BlockSpec((MOE_TILE, d), lambda i, *_: (i, 0)),
            scratch_shapes=[pltpu.VMEM((2, MOE_TILE // V7X_SUBLANES, V7X_SUBLANES, d), F32),
                            pltpu.SemaphoreType.DMA((2,)),
                            pltpu.VMEM((d, f), BF16), pltpu.VMEM((d, f), BF16), pltpu.VMEM((f, d), BF16)]),
        out_shape=jax.ShapeDtypeStruct((n_tiles * MOE_TILE, d), F32),
        compiler_params=pltpu.CompilerParams(dimension_semantics=("arbitrary",),
                                             vmem_limit_bytes=int(min(blocks + COMPILER_SCRATCH_BYTES, V7X_SCOPED_VMEM_CAP))),
        name="moe_experts",
    )(tile_expert, tile_first, n_valid, sorted_token, x, w_gate, w_up, w_down)


def _moe_combine_body(pos_ref, x_ref, w_ref, g_ref, b_ref, y_hbm, o_ref, ob_ref, ybuf, sem, *, n_tok, split_blocks):
    i = pl.program_id(0)
    tc = COMBINE_TILE

    sub = V7X_SUBLANES

    def gather_copy(p, slot, k, r8, j):
        return pltpu.make_async_copy(y_hbm.at[pl.ds(p, 1)], ybuf.at[slot, k, r8, pl.ds(j, 1)], sem.at[slot])

    def start_gather(tile, slot):
        def body(r8, c):
            for k in range(TOP_K):
                for j in range(sub):
                    gather_copy(pos_ref[k * n_tok + tile * tc + r8 * sub + j], slot, k, r8, j).start(priority=j % 2)
            return c

        lax.fori_loop(0, tc // sub, body, 0)

    @pl.when(i == 0)
    def _():
        start_gather(0, 0)

    @pl.when(i + 1 < pl.num_programs(0))
    def _():
        start_gather(i + 1, (i + 1) % 2)

    slot = i % 2
    for k in range(TOP_K):
        for r8 in range(tc // sub):
            pltpu.make_async_copy(y_hbm.at[pl.ds(0, sub)], ybuf.at[slot, k, r8], sem.at[slot]).wait()
    z = ALPHA * x_ref[...]
    for k in range(TOP_K):
        z = z + w_ref[:, k:k + 1] * ybuf[slot, k].reshape(tc, x_ref.shape[1])
    mu = jnp.mean(z, -1, keepdims=True)
    zc = z - mu
    var = jnp.mean(zc * zc, -1, keepdims=True)
    y = zc * lax.rsqrt(var + LN_EPS) * g_ref[...] + b_ref[...]
    if split_blocks is None:
        o_ref[...] = y
        ob_ref[...] = y.astype(BF16)
    else:
        @pl.when(i < split_blocks)
        def _():
            o_ref[...] = y

        @pl.when(i >= split_blocks)
        def _():
            ob_ref[...] = y


def _moe_combine_layernorm(x, y_slots, plan, gate_w, g, b, *, name, split=None):
    position_of_slot = plan[4]
    t, d = x.shape
    tc = COMBINE_TILE
    row = pl.BlockSpec((tc, d), lambda i, pos: (i, 0))
    vec = pl.BlockSpec((1, d), lambda i, pos: (0, 0))
    blocks = 2 * (3 * _nbytes((tc, d), F32) + _nbytes((tc, d), BF16)) + 2 * TOP_K * _nbytes((tc, d), F32)
    if split is None:
        sb = None
        out_specs = [row, row]
        out_shape = [jax.ShapeDtypeStruct((t, d), F32), jax.ShapeDtypeStruct((t, d), BF16)]
    else:
        sb = split // tc
        out_specs = [pl.BlockSpec((tc, d), lambda i, pos: (jnp.minimum(i, sb - 1), 0)),
                     pl.BlockSpec((tc, d), lambda i, pos: (jnp.maximum(i - sb, 0), 0))]
        out_shape = [jax.ShapeDtypeStruct((split, d), F32), jax.ShapeDtypeStruct((t - split, d), F32)]
    return pl.pallas_call(
        functools.partial(_moe_combine_body, n_tok=t, split_blocks=sb),
        grid_spec=pltpu.PrefetchScalarGridSpec(
            num_scalar_prefetch=1,
            grid=(t // tc,),
            in_specs=[row, pl.BlockSpec((tc, TOP_K), lambda i, pos: (i, 0)), vec, vec,
                      pl.BlockSpec(memory_space=pl.ANY)],
            out_specs=out_specs,
            scratch_shapes=[pltpu.VMEM((2, TOP_K, tc // V7X_SUBLANES, V7X_SUBLANES, d), F32),
                            pltpu.SemaphoreType.DMA((2,))]),
        out_shape=out_shape,
        compiler_params=pltpu.CompilerParams(dimension_semantics=("arbitrary",),
                                             vmem_limit_bytes=int(min(blocks + COMPILER_SCRATCH_BYTES, V7X_SCOPED_VMEM_CAP))),
        name=name,
    )(position_of_slot, x, gate_w, g.reshape(1, d), b.reshape(1, d), y_slots)


TOKEN_TILE = 1088
IN_PROJ_TN = 1024
LN_TILE = 512


def _moe_layernorm(x, eidx, gate_w, layer, w_gate, w_up, w_down, g, b, split=None):
    plan = _moe_plan(eidx)
    y_slots = _moe_experts(x, plan, w_gate, w_up, w_down, layer)
    return _moe_combine_layernorm(x, y_slots, plan, gate_w.T, g, b, name=f"moe_combine_ln_{layer}", split=split)


def kernel(x_prompt, x_sample, state_hgrn, state_ssm, state_conv, state_s5_re, state_s5_im, w_in0, hg_lb_logits, hg_norm_w, conv_w, conv_b, dt_bias, a_log, m_d, m_norm_w, w_out0, s5_a_re, s5_a_im, s5_log_dt, s5_b_re, s5_b_im, s5_c_re, s5_c_im, s5_d, glu_w_a, glu_b_a, glu_w_b, glu_b_b, w_router, b_router, w_gate, w_up, w_down, ln1_g, ln1_b, ln2_g, ln2_b):
    bp, lp, d = x_prompt.shape
    bs, ls, _ = x_sample.shape
    tp, ts = bp * lp, bs * ls
    tm = TOKEN_TILE
    xp2, xs2 = x_prompt.reshape(tp, d), x_sample.reshape(ts, d)
    x0b = _join_bf16(xp2, xs2, tm=LN_TILE)
    lower_bounds = jnp.cumsum(jax.nn.softmax(hg_lb_logits.astype(F32), axis=0), axis=0)
    pad_lanes = lambda v: jnp.pad(v, (0, V7X_LANES - v.shape[0])).reshape(1, V7X_LANES)

    proj = _matmul_w32(x0b, jnp.swapaxes(w_in0, 1, 2), 0, tm=tm, tn=IN_PROJ_TN, n_cols=IN0_MAIN,
                       transposed=True, name="in_proj")
    w_dt = jnp.pad(w_in0[0, :, IN0_MAIN:], ((0, 0), (0, V7X_LANES - M_HEADS))).astype(BF16)
    dt = _matmul(x0b, w_dt, tm=tm, tn=V7X_LANES, name="dt_proj")
    lb0 = lower_bounds[0]
    e2 = _head_expand_matrix()
    ssd_consts = (conv_w[0], conv_b[0].reshape(1, -1), pad_lanes(dt_bias[0]), pad_lanes(a_log[0]),
                  jnp.repeat(m_d[0], M_HEADDIM).reshape(1, M_DI), m_norm_w[0].reshape(1, M_DI), e2)
    hgrn_prompt = functools.partial(_hgrn_prompt, proj, lb0, hg_norm_w[0], batch=bp, seq=lp)
    mixed, hg_p = lax.cond(_hgrn_lower_bound_is_safe(lb0), functools.partial(hgrn_prompt, guarded=False),
                           functools.partial(hgrn_prompt, guarded=True))
    mixed, hg_s = _hgrn_sample(proj, tp, state_hgrn[0], lb0, hg_norm_w[0], batch=bs, seq=ls, prev=mixed)
    mixed, ssm_p, conv_p = _ssd_prompt(proj, dt, *ssd_consts, batch=bp, seq=lp, prev=mixed)
    mixed, ssm_s, conv_s = _ssd_sample(proj, tp, dt, state_ssm[0], state_conv[0], *ssd_consts, batch=bs, seq=ls,
                                       prev=mixed)
    mix = _matmul_w32(mixed, w_out0, 0, tm=tm, tn=512, name="out_proj")
    x1, eidx, gate_w = _res_layernorm_router(xp2, mix, ln1_g[0], ln1_b[0], w_router, b_router, tm=LN_TILE,
                                             name="ln1_router_0", x_tail=xs2)
    x2, x2b = _moe_layernorm(x1, eidx, gate_w, 0, w_gate, w_up, w_down, ln2_g[0], ln2_b[0])

    ab_re, ab_im, wb_re, wb_im, wc_re, wc_im = _s5_discretize(
        s5_a_re[0], s5_a_im[0], s5_log_dt[0], s5_b_re[0], s5_b_im[0], s5_c_re[0], s5_c_im[0])
    s5_consts = (ab_re, ab_im, wb_re, wb_im, wc_re, wc_im, s5_d[0])
    zeros = jnp.zeros((bp, S5_NBLK, S5_TILES, V7X_LANES), F32)
    act, s5r_p, s5i_p = _s5_fused(x2b, x2, zeros, zeros, *s5_consts, row0=0, batch=bp, seq=lp,
                                  rows=S5_PROMPT_ROWS, n_seq=1, name="s5_prompt")
    act, s5r_s, s5i_s = _s5_fused(x2b, x2, _s5_slab_layout(state_s5_re[0].reshape(bs, S5_STATE)),
                                  _s5_slab_layout(state_s5_im[0].reshape(bs, S5_STATE)), *s5_consts,
                                  row0=tp, batch=bs, seq=ls, rows=S5_SAMPLE_ROWS, n_seq=S5_SAMPLE_ROWS // ls,
                                  prev=act, name="s5_sample")
    mix1 = _glu(act, glu_w_a, glu_w_b, glu_b_a, glu_b_b, 0, tm=tm, tn=512)
    x3, eidx, gate_w = _res_layernorm_router(x2, mix1, ln1_g[1], ln1_b[1], w_router, b_router, tm=LN_TILE,
                                             name="ln1_router_1")
    y_p, y_s = _moe_layernorm(x3, eidx, gate_w, 1, w_gate, w_up, w_down, ln2_g[1], ln2_b[1], split=tp)

    s5_state = lambda a, b: _s5_from_slab_layout(a).reshape(1, b, S5_GROUPS, S5_P)
    return (y_p.reshape(bp, lp, d), y_s.reshape(bs, ls, d),
            hg_p[None], hg_s[None], ssm_p[None], ssm_s[None], conv_p[None], conv_s[None],
            s5_state(s5r_p, bp), s5_state(s5r_s, bs), s5_state(s5i_p, bp), s5_state(s5i_s, bs))
```

```python
import functools
import math

import jax
import jax.numpy as jnp
from jax import lax
from jax.experimental import pallas as pl
from jax.experimental.pallas import tpu as pltpu

F32 = jnp.float32
BF16 = jnp.bfloat16
HIGHEST = lax.Precision.HIGHEST

D_MODEL = 2048
DEPTH = 2
HG_HEADS = 16
HG_DK = 128
HG_DV = 128
HG_KW = HG_HEADS * HG_DK
HG_VW = HG_HEADS * HG_DV
HG_CHUNK = 64
M_DI = 2048
M_HEADDIM = 64
M_HEADS = 32
M_GROUPS = 8
M_HPG = 4
M_STATE = 128
M_CONV = 4
M_CONV_DIM = M_DI + 2 * M_GROUPS * M_STATE
M_CHUNK = 128
IN0_MAIN = 2 * HG_KW + 2 * HG_VW + M_DI + M_CONV_DIM
S5_GROUP = 16
S5_GROUPS = 128
S5_P = 64
S5_STATE = S5_GROUPS * S5_P
N_EXPERTS = 16
N_EXP_GROUPS = 4
EXP_PER_GROUP = 4
TOP_K = 2
D_EXPERT = 512
ALPHA = (2 * DEPTH) ** 0.25
LN_EPS = 1e-5
RMS_EPS = 1e-6

V7X_LANES = 128
V7X_SUBLANES = 8
V7X_SCOPED_VMEM_CAP = 60000 * 1024
COMPILER_SCRATCH_BYTES = 16 * 1024 * 1024


def _params(semantics, block_bytes):
    limit = min(2 * block_bytes + COMPILER_SCRATCH_BYTES, V7X_SCOPED_VMEM_CAP)
    return pltpu.CompilerParams(dimension_semantics=semantics, vmem_limit_bytes=int(limit))


def _nbytes(shape, dtype):
    return math.prod(shape) * jnp.dtype(dtype).itemsize


def _silu(x):
    return x * jax.nn.sigmoid(x)


def _dot(a, b, dims, precision=None):
    return lax.dot_general(a, b, (dims, ((), ())), precision=precision, preferred_element_type=F32)


NN = ((1,), (0,))
NT = ((1,), (1,))
TN = ((0,), (0,))

MIX0_WIDTH = HG_VW + M_DI


def _shared_output(body, n_in, prev):
    if prev is None:
        return body, [], [], {}

    def with_prev(*refs):
        return body(*refs[:n_in], *refs[n_in + 1:])

    return with_prev, [pl.BlockSpec(memory_space=pl.ANY)], [prev], {n_in: 0}


def _join_bf16_body(a_ref, b_ref, w_ref, o_ref, p_ref, *, first_blocks):
    x = jnp.where(pl.program_id(0) < first_blocks, a_ref[...], b_ref[...]).astype(o_ref.dtype)
    o_ref[...] = x
    p_ref[...] = _dot(x, w_ref[...], NN)


def _join_bf16(a, b, w, *, tm):
    d = a.shape[1]
    na, nb = a.shape[0] // tm, b.shape[0] // tm
    rows = a.shape[0] + b.shape[0]
    blocks = 2 * _nbytes((tm, d), F32) + _nbytes((tm, d), BF16) + _nbytes(w.shape, BF16) + _nbytes((tm, w.shape[1]), F32)
    return pl.pallas_call(
        functools.partial(_join_bf16_body, first_blocks=na),
        grid=(na + nb,),
        in_specs=[pl.BlockSpec((tm, d), lambda i: (jnp.minimum(i, na - 1), 0)),
                  pl.BlockSpec((tm, d), lambda i: (jnp.maximum(i - na, 0), 0)),
                  pl.BlockSpec(w.shape, lambda i: (0, 0))],
        out_specs=[pl.BlockSpec((tm, d), lambda i: (i, 0)), pl.BlockSpec((tm, w.shape[1]), lambda i: (i, 0))],
        out_shape=[jax.ShapeDtypeStruct((rows, d), BF16), jax.ShapeDtypeStruct((rows, w.shape[1]), F32)],
        compiler_params=_params(("arbitrary",), blocks),
        name="join_bf16",
    )(a, b, w)


def _mm_w32_body(x_ref, w_ref, o_ref, wb_scr):
    @pl.when(pl.program_id(1) == 0)
    def _():
        wb_scr[...] = w_ref[0].astype(BF16)

    o_ref[...] = _dot(x_ref[...], wb_scr[...], NN).astype(o_ref.dtype)


def _mm_w32t_body(x_ref, w_ref, o_ref, wb_scr):
    @pl.when(pl.program_id(1) == 0)
    def _():
        wb_scr[...] = w_ref[0].astype(BF16)

    o_ref[...] = _dot(x_ref[...], wb_scr[...], NT).astype(o_ref.dtype)


def _matmul_w32(x, w, layer, *, tm, tn, n_cols=None, transposed=False, name):
    m, k = x.shape
    n = n_cols if n_cols is not None else w.shape[1 if transposed else 2]
    blocks = _nbytes((tm, k), BF16) + _nbytes((k, tn), F32) + _nbytes((tm, tn), F32) + _nbytes((k, tn), BF16)
    if transposed:
        body, wblock, wspec = _mm_w32t_body, (tn, k), pl.BlockSpec((1, tn, k), lambda j, i: (layer, j, 0))
    else:
        body, wblock, wspec = _mm_w32_body, (k, tn), pl.BlockSpec((1, k, tn), lambda j, i: (layer, 0, j))
    return pl.pallas_call(
        body,
        grid=(n // tn, m // tm),
        in_specs=[pl.BlockSpec((tm, k), lambda j, i: (i, 0)), wspec],
        out_specs=pl.BlockSpec((tm, tn), lambda j, i: (i, j)),
        out_shape=jax.ShapeDtypeStruct((m, n), F32),
        scratch_shapes=[pltpu.VMEM(wblock, BF16)],
        compiler_params=_params(("arbitrary", "arbitrary"), blocks),
        name=name,
    )(x, w)


def _hgrn_gates(q, f, lb):
    fg = lb + (1.0 - lb) * jax.nn.sigmoid(f)
    return _silu(q), 1.0 - fg, jnp.log(fg)


HG_SAFE_EXPONENT = 80.0


def _hgrn_exact_intra(q, k, cum, v_ref, heads, intra_scr, q_scr, k_scr, cum_scr):
    c = cum.shape[0]
    q_scr[...] = q
    k_scr[...] = k
    cum_scr[...] = cum
    unsafe = jnp.max(jnp.abs(cum - cum[c // 2 - 1:c // 2, :])) > HG_SAFE_EXPONENT

    @pl.when(unsafe)
    def _():
        srow = lax.broadcasted_iota(jnp.int32, (c, 1), 0)
        sub = V7X_SUBLANES

        def row_tile(t8, carry):
            base = pl.multiple_of(t8 * sub, sub)
            cum_t = cum_scr[pl.ds(base, sub), :]
            q_t = q_scr[pl.ds(base, sub), :]
            rows_out = [[] for _ in heads]
            for j in range(sub):
                decay = jnp.exp(jnp.minimum(cum_t[j:j + 1, :] - cum_scr[...], 0.0))
                w = jnp.where(srow <= base + j, q_t[j:j + 1, :] * k_scr[...] * decay, 0.0)
                for h, sl in enumerate(heads):
                    coef = jnp.sum(w[:, sl], axis=-1, keepdims=True)
                    rows_out[h].append(jnp.sum(coef * v_ref[:, sl], axis=0, keepdims=True))
            for h, sl in enumerate(heads):
                intra_scr[pl.ds(base, sub), sl] = jnp.concatenate(rows_out[h], axis=0)
            return carry

        lax.fori_loop(0, c // sub, row_tile, 0)

    return unsafe


def _hgrn_prompt_body(q_ref, f_ref, v_ref, g_ref, lb_ref, nw_ref, o_ref, sfin_ref, st_scr, *guard_scr, guarded):
    n = pl.program_id(1)
    c = HG_CHUNK

    @pl.when(n == 0)
    def _():
        st_scr[...] = jnp.zeros_like(st_scr)
        if guarded:
            guard_scr[0][...] = jnp.zeros_like(guard_scr[0])

    row = lax.broadcasted_iota(jnp.int32, (c, c), 0)
    col = lax.broadcasted_iota(jnp.int32, (c, c), 1)
    causal = row >= col
    tri = causal.astype(F32)
    heads = [slice(h * HG_DK, (h + 1) * HG_DK) for h in range(HG_HEADS)]
    q, k, logf = _hgrn_gates(q_ref[...], f_ref[...], lb_ref[...])
    cum = _dot(tri, logf, NN, precision=HIGHEST)
    if guarded:
        unsafe = _hgrn_exact_intra(q, k, cum, v_ref, heads, *guard_scr)
        q, k, cum = (r[...] for r in guard_scr[1:])
    v = v_ref[...].astype(BF16)
    mid = cum[c // 2 - 1:c // 2, :]
    last = cum[c - 1:c, :]
    qm = (q * jnp.exp(cum - mid)).astype(BF16)
    km = (k * jnp.exp(mid - cum)).astype(BF16)
    qi = (q * jnp.exp(cum)).astype(BF16)
    kl = (k * jnp.exp(last - cum)).astype(BF16)
    a_last = jnp.exp(last)
    gate = nw_ref[...] * _silu(g_ref[...])
    scores = [jnp.where(causal, _dot(qm[:, sl], km[:, sl], NT), 0.0).astype(BF16) for sl in heads]
    sts = [st_scr[h] for h in range(HG_HEADS)]
    if guarded:
        outs = [_dot(qi[:, sl], sts[h].astype(BF16), NT)
                + jnp.where(unsafe, guard_scr[0][:, sl], _dot(scores[h], v[:, sl], NN)) for h, sl in enumerate(heads)]
    else:
        outs = [_dot(qi[:, sl], sts[h].astype(BF16), NT) + _dot(scores[h], v[:, sl], NN)
                for h, sl in enumerate(heads)]
    for h, sl in enumerate(heads):
        st_scr[h] = a_last[:, sl] * sts[h] + _dot(v[:, sl], kl[:, sl], TN)
    for h, sl in enumerate(heads):
        o = outs[h]
        o = o * lax.rsqrt(jnp.mean(o * o, -1, keepdims=True) + RMS_EPS)
        o_ref[:, sl] = (o * gate[:, sl]).astype(o_ref.dtype)

    @pl.when(n == pl.num_programs(1) - 1)
    def _():
        sfin_ref[0] = st_scr[...]


def _hgrn_lower_bound_is_safe(lb):
    return jnp.min(lb) >= math.exp(-HG_SAFE_EXPONENT / (HG_CHUNK // 2))


def _hgrn_prompt(proj, lb, nw, *, batch, seq, guarded, prev=None):
    c = HG_CHUNK
    nchunk = seq // c
    blk = lambda j: pl.BlockSpec((c, HG_KW), lambda b, n: (b * nchunk + n, j))
    vec = pl.BlockSpec((1, HG_KW), lambda b, n: (0, 0))
    blocks = 4 * _nbytes((c, HG_KW), F32) + _nbytes((c, HG_VW), BF16) + 2 * _nbytes((HG_HEADS, HG_DV, HG_DK), F32)
    guard_scratch = [pltpu.VMEM((c, HG_KW), F32)] * 4 if guarded else []
    body, xspecs, xops, alias = _shared_output(functools.partial(_hgrn_prompt_body, guarded=guarded), 6, prev)
    o, st = pl.pallas_call(
        body,
        grid=(batch, nchunk),
        in_specs=[blk(0), blk(1), blk(2), blk(3), vec, vec] + xspecs,
        out_specs=[pl.BlockSpec((c, HG_VW), lambda b, n: (b * nchunk + n, 0)),
                   pl.BlockSpec((1, HG_HEADS, HG_DV, HG_DK), lambda b, n: (b, 0, 0, 0))],
        out_shape=[jax.ShapeDtypeStruct((proj.shape[0], MIX0_WIDTH), BF16),
                   jax.ShapeDtypeStruct((batch, HG_HEADS, HG_DV, HG_DK), F32)],
        scratch_shapes=[pltpu.VMEM((HG_HEADS, HG_DV, HG_DK), F32)] + guard_scratch,
        input_output_aliases=alias,
        compiler_params=_params(("arbitrary", "arbitrary"), blocks),
        name="hgrn_prompt_guarded" if guarded else "hgrn_prompt",
    )(proj, proj, proj, proj, lb.reshape(1, HG_KW), nw.reshape(1, HG_VW), *xops)
    return o, jnp.swapaxes(st, -1, -2)


HG_SAMPLE_BB = 8


def _hgrn_sample_body(q_ref, f_ref, v_ref, g_ref, lb_ref, nw_ref, s_ref, o_ref, so_ref, *, seq):
    rows = 2 * seq
    assert rows == V7X_SUBLANES
    row = lax.broadcasted_iota(jnp.int32, (rows, rows), 0)
    col = lax.broadcasted_iota(jnp.int32, (rows, rows), 1)
    causal = (row >= col) & ((row // seq) == (col // seq))
    rvec = lax.broadcasted_iota(jnp.int32, (rows, 1), 0)
    r16 = lax.broadcasted_iota(jnp.int32, (2 * rows, HG_DV), 0)
    ones_rows = jnp.where((r16 == rows) | (r16 == rows + 1), 1.0, 0.0).astype(BF16)

    heads = [slice(h * HG_DK, (h + 1) * HG_DK) for h in range(HG_HEADS)]
    in_seq = rvec % seq

    def pair(p, carry):
        r0 = pl.multiple_of(p * rows, rows)
        tile = pl.ds(r0, rows)
        q, k, logf = _hgrn_gates(q_ref[tile, :], f_ref[tile, :], lb_ref[...])
        v = v_ref[tile, :].astype(BF16)
        cum = logf
        step = 1
        while step < seq:
            cum = cum + jnp.where(in_seq >= step, _shift_rows(cum, step), 0.0)
            step *= 2
        ecum = jnp.exp(cum)
        qi = (q * ecum).astype(BF16)
        km = (k / ecum).astype(BF16)
        gate = nw_ref[...] * _silu(g_ref[tile, :])
        v16 = jnp.concatenate([v, jnp.zeros_like(v)], axis=0)
        lhs = []
        for j in range(2):
            mine = (rvec // seq) == j
            last = cum[(j + 1) * seq - 1:(j + 1) * seq, :]
            a = jnp.exp(last)
            a_hi = a.astype(BF16).astype(F32)
            kl = jnp.where(mine, k * jnp.exp(last - cum), 0.0)
            lhs.append(jnp.concatenate([kl, a_hi, a - a_hi, jnp.zeros((rows - 2, HG_KW), F32)], axis=0).astype(BF16))
        scores = [jnp.where(causal, _dot(qi[:, sl], km[:, sl], NT), 0.0).astype(BF16) for sl in heads]
        outs = [_dot(scores[h], v[:, sl], NN) for h, sl in enumerate(heads)]
        for j in range(2):
            b = 2 * p + j
            mine = (rvec // seq) == j
            s0 = [s_ref[b, h] for h in range(HG_HEADS)]
            inter = [_dot(qi[:, sl], s0[h].astype(BF16), NN) for h, sl in enumerate(heads)]
            both = [_dot(lhs[j][:, sl], jnp.concatenate([v16[:, sl], ones_rows], axis=1), TN)
                    for sl in heads]
            for h in range(HG_HEADS):
                outs[h] = outs[h] + jnp.where(mine, inter[h], 0.0)
                so_ref[b, h] = both[h][:, HG_DV:] * s0[h] + both[h][:, :HG_DV]
        for h, sl in enumerate(heads):
            o = outs[h]
            o = o * lax.rsqrt(jnp.mean(o * o, -1, keepdims=True) + RMS_EPS)
            o_ref[tile, sl] = (o * gate[:, sl]).astype(o_ref.dtype)
        return carry

    lax.fori_loop(0, HG_SAMPLE_BB // 2, pair, 0)


def _hgrn_sample(proj, row0, state, lb, nw, *, batch, seq, prev=None):
    bb = HG_SAMPLE_BB
    tr = bb * seq
    blk0 = row0 // tr
    blk = lambda j: pl.BlockSpec((tr, HG_KW), lambda i: (blk0 + i, j))
    vec = pl.BlockSpec((1, HG_KW), lambda i: (0, 0))
    sblk = pl.BlockSpec((bb, HG_HEADS, HG_DK, HG_DV), lambda i: (i, 0, 0, 0))
    blocks = 4 * _nbytes((tr, HG_KW), F32) + 2 * _nbytes((bb, HG_HEADS, HG_DK, HG_DV), F32)
    body, xspecs, xops, alias = _shared_output(functools.partial(_hgrn_sample_body, seq=seq), 7, prev)
    return pl.pallas_call(
        body,
        grid=(batch // bb,),
        in_specs=[blk(0), blk(1), blk(2), blk(3), vec, vec, sblk] + xspecs,
        out_specs=[pl.BlockSpec((tr, HG_VW), lambda i: (blk0 + i, 0)), sblk],
        out_shape=[jax.ShapeDtypeStruct((proj.shape[0], MIX0_WIDTH), BF16),
                   jax.ShapeDtypeStruct(state.shape, F32)],
        input_output_aliases=alias,
        compiler_params=_params(("arbitrary",), blocks),
        name="hgrn_sample",
    )(proj, proj, proj, proj, lb.reshape(1, HG_KW), nw.reshape(1, HG_VW), state, *xops)


def _head_expand_matrix():
    r = jnp.arange(2 * V7X_LANES)[:, None] % V7X_LANES
    c = jnp.arange(M_DI)[None, :] // M_HEADDIM
    return (r == c).astype(BF16)


def _expand_heads(coef, e2):
    hi = coef.astype(BF16)
    lo = (coef - hi.astype(F32)).astype(BF16)
    return _dot(jnp.concatenate([hi, lo], axis=1), e2, NN)


def _softplus(x):
    return jnp.maximum(x, 0.0) + jnp.log(1.0 + jnp.exp(-jnp.abs(x)))


def _group_rmsnorm(y, width):
    outs = []
    for g in range(y.shape[1] // width):
        yg = y[:, g * width:(g + 1) * width]
        outs.append(yg * lax.rsqrt(jnp.mean(yg * yg, -1, keepdims=True) + RMS_EPS))
    return jnp.concatenate(outs, axis=1)


def _ssd_prompt_body(z_ref, x_ref, bc_ref, dt_ref, cwx_ref, cwb_ref, cbx_ref, cbb_ref, dtb_ref, alog_ref,
                     dexp_ref, nw_ref, e2_ref, y_ref, sfin_ref, conv_ref, s_scr, cx_scr, cb_scr):
    n = pl.program_id(1)
    c = M_CHUNK
    tail = V7X_SUBLANES

    @pl.when(n == 0)
    def _():
        s_scr[...] = jnp.zeros_like(s_scr)
        cx_scr[...] = jnp.zeros_like(cx_scr)
        cb_scr[...] = jnp.zeros_like(cb_scr)

    def conv(u, carry_scr, w_ref, b_ref):
        ext = jnp.concatenate([carry_scr[...], u], axis=0)
        acc = b_ref[...] + w_ref[M_CONV - 1:M_CONV, :] * u
        for j in range(1, M_CONV):
            acc = acc + w_ref[M_CONV - 1 - j:M_CONV - j, :] * pltpu.roll(ext, j, 0)[tail:, :]
        carry_scr[...] = u[c - tail:, :]
        return _silu(acc)

    ux = x_ref[...]
    ubc = bc_ref[...]
    xc = conv(ux, cx_scr, cwx_ref, cbx_ref)
    bcc = conv(ubc, cb_scr, cwb_ref, cbb_ref)
    ngn = M_GROUPS * M_STATE

    row = lax.broadcasted_iota(jnp.int32, (c, c), 0)
    col = lax.broadcasted_iota(jnp.int32, (c, c), 1)
    causal = row >= col
    tri = causal.astype(F32)
    e2 = e2_ref[...]

    dtp = _softplus(dt_ref[...] + dtb_ref[...])
    da = dtp * (-jnp.exp(alog_ref[...]))
    cum = _dot(tri, da, NN, precision=HIGHEST)
    cum_t = cum.T
    last = cum[c - 1:c, :]
    xdt = xc * _expand_heads(dtp, e2)
    xend = (xdt * _expand_heads(jnp.exp(last - cum), e2)).astype(BF16)
    ecum = _expand_heads(jnp.exp(cum), e2)
    xdt_b = xdt.astype(BF16)
    gw = M_HPG * M_HEADDIM

    ys = []
    for g in range(M_GROUPS):
        bg = bcc[:, g * M_STATE:(g + 1) * M_STATE].astype(BF16)
        cg = bcc[:, ngn + g * M_STATE:ngn + (g + 1) * M_STATE].astype(BF16)
        cb = _dot(cg, bg, NT)
        sg = s_scr[g * M_HPG:(g + 1) * M_HPG].reshape(gw, M_STATE)
        y_inter = _dot(cg, sg.astype(BF16), NT) * ecum[:, g * gw:(g + 1) * gw]
        upd = _dot(xend[:, g * gw:(g + 1) * gw], bg, TN)
        parts = []
        for hh in range(M_HPG):
            h = g * M_HPG + hh
            decay = jnp.exp(jnp.where(causal, cum[:, h:h + 1] - cum_t[h:h + 1, :], -jnp.inf))
            m = (cb * decay).astype(BF16)
            parts.append(_dot(m, xdt_b[:, h * M_HEADDIM:(h + 1) * M_HEADDIM], NN))
            s_scr[h] = jnp.exp(last[:, h:h + 1]) * s_scr[h] + upd[hh * M_HEADDIM:(hh + 1) * M_HEADDIM, :]
        ys.append(jnp.concatenate(parts, axis=1) + y_inter)
    y = jnp.concatenate(ys, axis=1) + dexp_ref[...] * xc
    y = y * _silu(z_ref[...])
    y_ref[...] = (_group_rmsnorm(y, gw) * nw_ref[...]).astype(y_ref.dtype)

    @pl.when(n == pl.num_programs(1) - 1)
    def _():
        sfin_ref[0] = s_scr[...]
        conv_ref[0, :, 0:M_DI] = ux[c - (M_CONV - 1):, :]
        conv_ref[0, :, M_DI:] = ubc[c - (M_CONV - 1):, :]


def _ssd_prompt(proj, dt, conv_w, conv_b, dt_bias, a_log, d_exp, norm_w, e2, *, batch, seq, prev=None):
    c = M_CHUNK
    nchunk = seq // c
    tok = lambda j: pl.BlockSpec((c, M_DI), lambda b, n: (b * nchunk + n, j))
    const = lambda shape, j=0: pl.BlockSpec(shape, lambda b, n: (0, j))
    blocks = (3 * _nbytes((c, M_DI), F32) + _nbytes((c, M_DI), BF16) + _nbytes((2 * V7X_LANES, M_DI), BF16)
              + 2 * _nbytes((M_HEADS, M_HEADDIM, M_STATE), F32) + 12 * _nbytes((c, M_DI), F32))
    body, xspecs, xops, alias = _shared_output(_ssd_prompt_body, 13, prev)
    return pl.pallas_call(
        body,
        grid=(batch, nchunk),
        in_specs=[tok(4), tok(5), tok(6),
                  pl.BlockSpec((c, V7X_LANES), lambda b, n: (b * nchunk + n, 0)),
                  const((M_CONV, M_DI), 0), const((M_CONV, M_DI), 1), const((1, M_DI), 0), const((1, M_DI), 1),
                  const((1, V7X_LANES)), const((1, V7X_LANES)), const((1, M_DI)), const((1, M_DI)),
                  const((2 * V7X_LANES, M_DI))] + xspecs,
        out_specs=[pl.BlockSpec((c, M_DI), lambda b, n: (b * nchunk + n, HG_VW // M_DI)),
                   pl.BlockSpec((1, M_HEADS, M_HEADDIM, M_STATE), lambda b, n: (b, 0, 0, 0)),
                   pl.BlockSpec((1, M_CONV - 1, M_CONV_DIM), lambda b, n: (b, 0, 0))],
        out_shape=[jax.ShapeDtypeStruct((proj.shape[0], MIX0_WIDTH), BF16),
                   jax.ShapeDtypeStruct((batch, M_HEADS, M_HEADDIM, M_STATE), F32),
                   jax.ShapeDtypeStruct((batch, M_CONV - 1, M_CONV_DIM), F32)],
        scratch_shapes=[pltpu.VMEM((M_HEADS, M_HEADDIM, M_STATE), F32),
                        pltpu.VMEM((V7X_SUBLANES, M_DI), F32), pltpu.VMEM((V7X_SUBLANES, M_DI), F32)],
        input_output_aliases=alias,
        compiler_params=_params(("arbitrary", "arbitrary"), blocks),
        name="ssd_prompt",
    )(proj, proj, proj, dt, conv_w, conv_w, conv_b, conv_b, dt_bias, a_log, d_exp, norm_w, e2, *xops)


SSD_SAMPLE_BB = 8


def _shift_rows(a, j):
    return a if j == 0 else pltpu.roll(a, j, 0)


def _ssd_sample_body(z_ref, x_ref, bc_ref, dt_ref, s_ref, cs_ref, cwx_ref, cwb_ref, cbx_ref, cbb_ref, dtb_ref,
                     alog_ref, dexp_ref, nw_ref, e2_ref, y_ref, so_ref, co_ref, *, seq):
    rows = V7X_SUBLANES
    hist = M_CONV - 1
    assert rows == 2 * seq and hist <= seq and hist <= rows - seq
    rvec = lax.broadcasted_iota(jnp.int32, (rows, 1), 0)
    valid = rvec < seq
    lane = lax.broadcasted_iota(jnp.int32, (1, V7X_LANES), 1)
    r16 = lax.broadcasted_iota(jnp.int32, (2 * rows, M_STATE), 0)
    ones_rows = jnp.where((r16 == rows) | (r16 == rows + 1), 1.0, 0.0).astype(BF16)
    e2 = e2_ref[...]
    a_neg = -jnp.exp(alog_ref[...])
    gw = M_HPG * M_HEADDIM
    ngn = M_GROUPS * M_STATE

    def conv(u8, buf, w_ref, b_ref):
        buf8 = jnp.concatenate([buf, jnp.zeros((rows - hist, buf.shape[1]), F32)], axis=0)
        ext = jnp.where(valid, u8, _shift_rows(buf8, rows - hist))
        acc = b_ref[...] + w_ref[hist:hist + 1, :] * ext
        for j in range(1, M_CONV):
            acc = acc + w_ref[hist - j:hist - j + 1, :] * _shift_rows(ext, j)
        new_hist = _shift_rows(ext, rows - (seq - hist))[0:hist, :]
        return _silu(acc), new_hist

    def one(b, u_x, u_bc, z8, dt8):
        xc, nhx = conv(u_x, cs_ref[b, :, 0:M_DI], cwx_ref, cbx_ref)
        bcc, nhb = conv(u_bc, cs_ref[b, :, M_DI:], cwb_ref, cbb_ref)
        co_ref[b, :, 0:M_DI] = nhx
        co_ref[b, :, M_DI:] = nhb
        dtp = jnp.where(valid, _softplus(dt8 + dtb_ref[...]), 0.0)
        cum = dtp * a_neg
        k = 1
        while k < seq:
            cum = cum + jnp.where(rvec >= k, _shift_rows(cum, k), 0.0)
            k *= 2
        last = cum[seq - 1:seq, :]
        bmat = bcc[:, :ngn]
        cmat = bcc[:, ngn:]
        coefs = [dtp, jnp.exp(cum), jnp.exp(last - cum), jnp.broadcast_to(jnp.exp(last), (rows, V7X_LANES))]
        for j in range(seq):
            prod = cmat * _shift_rows(bmat, j)
            cbh = jnp.zeros((rows, V7X_LANES), F32)
            for g in range(M_GROUPS):
                cbg = jnp.sum(prod[:, g * M_STATE:(g + 1) * M_STATE], axis=-1, keepdims=True)
                cbh = jnp.where((lane // M_HPG) == g, cbg, cbh)
            coefs.append(cbh * jnp.exp(cum - _shift_rows(cum, j)))
        wide = _expand_heads(jnp.concatenate(coefs, axis=0), e2)
        part = lambda n: wide[n * rows:(n + 1) * rows, :]
        xdt = xc * part(0)
        ecum = part(1)
        xs = xdt * part(2)
        dec = part(3)[0:1, :]
        y = dexp_ref[...] * xc
        for j in range(seq):
            y = y + part(4 + j) * _shift_rows(xdt, j)
        dec_hi = dec.astype(BF16).astype(F32)
        lhs = jnp.concatenate([xs, dec_hi, dec - dec_hi, jnp.zeros((rows - 2, M_DI), F32)], axis=0).astype(BF16)
        parts = []
        for g in range(M_GROUPS):
            bg = bmat[:, g * M_STATE:(g + 1) * M_STATE].astype(BF16)
            cg = cmat[:, g * M_STATE:(g + 1) * M_STATE].astype(BF16)
            s0 = s_ref[b, g * M_HPG:(g + 1) * M_HPG].reshape(gw, M_STATE)
            parts.append(_dot(cg, s0.astype(BF16), NT) * ecum[:, g * gw:(g + 1) * gw])
            rhs = jnp.concatenate([jnp.concatenate([bg, jnp.zeros((rows, M_STATE), BF16)], axis=0), ones_rows], axis=1)
            both = _dot(lhs[:, g * gw:(g + 1) * gw], rhs, TN)
            s_new = both[:, M_STATE:] * s0 + both[:, :M_STATE]
            so_ref[b, g * M_HPG:(g + 1) * M_HPG] = s_new.reshape(M_HPG, M_HEADDIM, M_STATE)
        y = (y + jnp.concatenate(parts, axis=1)) * _silu(z8)
        return _group_rmsnorm(y, gw) * nw_ref[...]

    def pair(p, carry):
        r0 = pl.multiple_of(p * rows, rows)
        tiles = [ref[pl.ds(r0, rows), :] for ref in (x_ref, bc_ref, z_ref, dt_ref)]
        ys = []
        for j in range(2):
            ys.append(one(2 * p + j, *[_shift_rows(t, j * (rows - seq)) for t in tiles]))
        y_ref[pl.ds(r0, rows), :] = jnp.where(valid, ys[0], _shift_rows(ys[1], seq)).astype(y_ref.dtype)
        return carry

    lax.fori_loop(0, SSD_SAMPLE_BB // 2, pair, 0)


def _ssd_sample(proj, row0, dt, state, conv_state, conv_w, conv_b, dt_bias, a_log, d_exp, norm_w, e2, *,
                batch, seq, prev=None):
    bb = SSD_SAMPLE_BB
    tr = bb * seq
    tok = lambda j: pl.BlockSpec((tr, M_DI), lambda i: (row0 // tr + i, j))
    const = lambda shape, j=0: pl.BlockSpec(shape, lambda i: (0, j))
    sblk = pl.BlockSpec((bb, M_HEADS, M_HEADDIM, M_STATE), lambda i: (i, 0, 0, 0))
    cblk = pl.BlockSpec((bb, M_CONV - 1, M_CONV_DIM), lambda i: (i, 0, 0))
    blocks = (3 * _nbytes((tr, M_DI), F32) + 2 * _nbytes((bb, M_HEADS, M_HEADDIM, M_STATE), F32)
              + 2 * _nbytes((bb, V7X_SUBLANES, M_CONV_DIM), F32) + _nbytes((2 * V7X_LANES, M_DI), BF16))
    body, xspecs, xops, alias = _shared_output(functools.partial(_ssd_sample_body, seq=seq), 15, prev)
    return pl.pallas_call(
        body,
        grid=(batch // bb,),
        in_specs=[tok(4), tok(5), tok(6),
                  pl.BlockSpec((tr, V7X_LANES), lambda i: (row0 // tr + i, 0)),
                  sblk, cblk,
                  const((M_CONV, M_DI), 0), const((M_CONV, M_DI), 1), const((1, M_DI), 0), const((1, M_DI), 1),
                  const((1, V7X_LANES)), const((1, V7X_LANES)), const((1, M_DI)), const((1, M_DI)),
                  const((2 * V7X_LANES, M_DI))] + xspecs,
        out_specs=[pl.BlockSpec((tr, M_DI), lambda i: (row0 // tr + i, HG_VW // M_DI)), sblk, cblk],
        out_shape=[jax.ShapeDtypeStruct((proj.shape[0], MIX0_WIDTH), BF16),
                   jax.ShapeDtypeStruct(state.shape, F32),
                   jax.ShapeDtypeStruct(conv_state.shape, F32)],
        input_output_aliases=alias,
        compiler_params=_params(("arbitrary",), blocks),
        name="ssd_sample",
    )(proj, proj, proj, dt, state, conv_state, conv_w, conv_w, conv_b, conv_b, dt_bias, a_log, d_exp, norm_w, e2, *xops)


S5_TILES = 8
S5_TILE_IN = D_MODEL // S5_TILES
S5_TILE_ST = S5_STATE // S5_TILES


def _s5_discretize(a_re, a_im, log_dt, b_re, b_im, c_re, c_im):
    lam_re = jnp.minimum(a_re, -1e-4)
    lam_im = a_im
    dt = jnp.exp(log_dt)[:, None]
    mag = jnp.exp(lam_re * dt)
    ab_re = mag * jnp.cos(lam_im * dt)
    ab_im = mag * jnp.sin(lam_im * dt)
    den = lam_re * lam_re + lam_im * lam_im
    zr = ((ab_re - 1.0) * lam_re + ab_im * lam_im) / den
    zi = (ab_im * lam_re - (ab_re - 1.0) * lam_im) / den
    bb_re = zr[..., None] * b_re - zi[..., None] * b_im
    bb_im = zr[..., None] * b_im + zi[..., None] * b_re
    gpt = S5_GROUPS // S5_TILES

    def block_diag(blocks, rows, cols):
        rep = jnp.tile(jnp.eye(cols, dtype=F32), (1, gpt))
        wide = jnp.einsum('krc,cn->krn', blocks, rep, precision=HIGHEST)
        own = (jnp.arange(gpt * rows)[:, None] // rows) == (jnp.arange(gpt * cols)[None, :] // cols)
        return jnp.where(own[None], wide, 0.0).astype(BF16)

    def pack_b(bb):
        return block_diag(jnp.swapaxes(bb, 1, 2).reshape(S5_TILES, S5_TILE_IN, S5_P), S5_GROUP, S5_P)

    def pack_c(c):
        return block_diag(jnp.swapaxes(c, 1, 2).reshape(S5_TILES, S5_TILE_ST, S5_GROUP), S5_P, S5_GROUP)

    slab = lambda a: _s5_slab_layout(a.reshape(S5_STATE))
    return slab(ab_re), slab(ab_im), pack_b(bb_re), pack_b(bb_im), pack_c(c_re), pack_c(c_im)


S5_PROMPT_ROWS = 256
S5_SAMPLE_ROWS = 128
S5_NBLK = S5_TILE_ST // V7X_LANES


def _s5_pitch(rows):
    assert rows % V7X_SUBLANES == 0
    return rows + V7X_SUBLANES // 2


def _s5_slab_layout(a):
    lead = a.shape[:-1]
    return jnp.swapaxes(a.reshape(lead + (S5_TILES, S5_NBLK, V7X_LANES)), -3, -2)


def _s5_from_slab_layout(a):
    lead = a.shape[:-3]
    return jnp.swapaxes(a, -3, -2).reshape(lead + (S5_STATE,))


def _s5_fused_body(xb_ref, x_ref, h0r_ref, h0i_ref, ar_ref, ai_ref, wbr_ref, wbi_ref, wcr_ref, wci_ref, d_ref, *rest,
                   n_seq, steps):
    act_ref, fr_ref, fi_ref, bur, bui, cr, ci = rest[-7:]
    hrs, his = bur, bui
    n = pl.program_id(1)
    rows = n_seq * steps
    S5_PITCH = _s5_pitch(rows)
    for s in range(S5_TILES):
        u = xb_ref[:, s * S5_TILE_IN:(s + 1) * S5_TILE_IN]
        br = _dot(u, wbr_ref[s], NN)
        bi = _dot(u, wbi_ref[s], NN)
        for j in range(S5_NBLK):
            bur[j, s * S5_PITCH:s * S5_PITCH + rows, :] = br[:, j * V7X_LANES:(j + 1) * V7X_LANES]
            bui[j, s * S5_PITCH:s * S5_PITCH + rows, :] = bi[:, j * V7X_LANES:(j + 1) * V7X_LANES]

    @pl.when(n == 0)
    def _():
        cr[...] = h0r_ref[...]
        ci[...] = h0i_ref[...]

    ar = [ar_ref[j] for j in range(S5_NBLK)]
    ai = [ai_ref[j] for j in range(S5_NBLK)]

    def seq_body(q, carry):
        def step(t, h):
            hr, hi = h
            idx = pl.ds(q * steps + t, S5_TILES, stride=S5_PITCH)
            nr, ni = [], []
            for j in range(S5_NBLK):
                nr.append(ar[j] * hr[j] - ai[j] * hi[j] + bur[j, idx, :])
                ni.append(ar[j] * hi[j] + ai[j] * hr[j] + bui[j, idx, :])
                hrs[j, idx, :] = nr[j]
                his[j, idx, :] = ni[j]
            return tuple(nr), tuple(ni)

        h0 = (tuple(cr[q, j] for j in range(S5_NBLK)), tuple(ci[q, j] for j in range(S5_NBLK)))
        hr, hi = lax.fori_loop(0, steps, step, h0, unroll=min(steps, 4))
        for j in range(S5_NBLK):
            cr[q, j] = hr[j]
            ci[q, j] = hi[j]
        return carry

    lax.fori_loop(0, n_seq, seq_body, 0)

    for s in range(S5_TILES):
        seg = slice(s * S5_PITCH, s * S5_PITCH + rows)
        hre = jnp.concatenate([hrs[j, seg, :] for j in range(S5_NBLK)], axis=1).astype(BF16)
        him = jnp.concatenate([his[j, seg, :] for j in range(S5_NBLK)], axis=1).astype(BF16)
        y = _dot(hre, wcr_ref[s], NN) - _dot(him, wci_ref[s], NN)
        cs = slice(s * S5_TILE_IN, (s + 1) * S5_TILE_IN)
        act_ref[:, cs] = jax.nn.gelu(y + d_ref[:, cs] * x_ref[:, cs]).astype(act_ref.dtype)

    @pl.when(n == pl.num_programs(1) - 1)
    def _():
        fr_ref[...] = cr[...]
        fi_ref[...] = ci[...]


def _s5_fused(xb, x, h0_re, h0_im, ab_re, ab_im, wb_re, wb_im, wc_re, wc_im, d_skip, *, row0, batch, seq, rows, n_seq,
              prev=None, name):
    t, d = x.shape
    steps = rows // n_seq
    assert n_seq == 1 or steps == seq
    S5_PITCH = _s5_pitch(rows)
    tok = lambda width: pl.BlockSpec((rows, width), lambda i, n: ((row0 + i * n_seq * seq) // rows + n, 0))
    st = pl.BlockSpec((n_seq, S5_NBLK, S5_TILES, V7X_LANES), lambda i, n: (i, 0, 0, 0))
    const = lambda shape: pl.BlockSpec(shape, lambda i, n: (0,) * len(shape), pipeline_mode=pl.Buffered(1))
    slab = pltpu.VMEM((S5_NBLK, S5_TILES * S5_PITCH, V7X_LANES), F32)
    stv = pltpu.VMEM((n_seq, S5_NBLK, S5_TILES, V7X_LANES), F32)
    blocks = (2 * (_nbytes((rows, d), BF16) * 2 + _nbytes((rows, d), F32))
              + 2 * 6 * _nbytes((n_seq, S5_STATE), F32) + 4 * _nbytes(wb_re.shape, BF16)
              + 2 * _nbytes((S5_NBLK, S5_TILES * S5_PITCH, V7X_LANES), F32))
    keep = [] if prev is None else [prev]
    n_in = 11
    return pl.pallas_call(
        functools.partial(_s5_fused_body, n_seq=n_seq, steps=steps),
        grid=(batch // n_seq, seq // steps),
        in_specs=[tok(d), tok(d), st, st, const(ab_re.shape), const(ab_im.shape), const(wb_re.shape),
                  const(wb_im.shape), const(wc_re.shape), const(wc_im.shape), const((1, d))]
                 + [pl.BlockSpec(memory_space=pl.ANY)] * len(keep),
        out_specs=[tok(d), st, st],
        out_shape=[jax.ShapeDtypeStruct((t, d), BF16), jax.ShapeDtypeStruct(h0_re.shape, F32),
                   jax.ShapeDtypeStruct(h0_im.shape, F32)],
        scratch_shapes=[slab, slab, stv, stv],
        input_output_aliases={n_in + k: k for k in range(len(keep))},
        compiler_params=pltpu.CompilerParams(dimension_semantics=("arbitrary", "arbitrary"),
                                             vmem_limit_bytes=int(min(blocks + COMPILER_SCRATCH_BYTES, V7X_SCOPED_VMEM_CAP))),
        name=name,
    )(xb, x, h0_re, h0_im, ab_re, ab_im, wb_re, wb_im, wc_re, wc_im, d_skip.reshape(1, d), *keep)


def _glu_body(a_ref, wa_ref, wb_ref, ba_ref, bb_ref, o_ref, wa_scr, wb_scr):
    @pl.when(pl.program_id(1) == 0)
    def _():
        wa_scr[...] = wa_ref[0].astype(BF16)
        wb_scr[...] = wb_ref[0].astype(BF16)

    a = a_ref[...]
    o_ref[...] = (_dot(a, wa_scr[...], NN) + ba_ref[...]) * jax.nn.sigmoid(_dot(a, wb_scr[...], NN) + bb_ref[...])


def _glu(a, wa, wb, ba, bb, layer, *, tm, tn):
    t, k = a.shape
    n = wa.shape[2]
    wspec = pl.BlockSpec((1, k, tn), lambda j, i: (layer, 0, j))
    bspec = pl.BlockSpec((1, tn), lambda j, i: (layer, j))
    blocks = _nbytes((tm, k), BF16) + 2 * _nbytes((k, tn), F32) + _nbytes((tm, tn), F32) + 2 * _nbytes((k, tn), BF16)
    return pl.pallas_call(
        _glu_body,
        grid=(n // tn, t // tm),
        in_specs=[pl.BlockSpec((tm, k), lambda j, i: (i, 0)), wspec, wspec, bspec, bspec],
        out_specs=pl.BlockSpec((tm, tn), lambda j, i: (i, j)),
        out_shape=jax.ShapeDtypeStruct((t, n), F32),
        scratch_shapes=[pltpu.VMEM((k, tn), BF16)] * 2,
        compiler_params=_params(("arbitrary", "arbitrary"), blocks),
        name="glu",
    )(a, wa, wb, ba, bb)


def _route(x, w_ref, b_ref, e_ref, g_ref):
    w = w_ref[...]
    w_hi = w.astype(BF16)
    w_lo = (w - w_hi.astype(F32)).astype(BF16)
    x_hi = x.astype(BF16)
    x_lo = (x - x_hi.astype(F32)).astype(BF16)
    logits = _dot(w_hi, x_hi, NT) + _dot(w_hi, x_lo, NT) + _dot(w_lo, x_hi, NT) + b_ref[...]
    rows = [logits[e:e + 1, :] for e in range(N_EXPERTS)]
    m = functools.reduce(jnp.maximum, rows)
    ex = [jnp.exp(r - m) for r in rows]
    z = functools.reduce(jnp.add, ex)
    p = [v / z for v in ex]

    def top2_sum(a, b, c, d):
        hi1, lo1, hi2, lo2 = jnp.maximum(a, b), jnp.minimum(a, b), jnp.maximum(c, d), jnp.minimum(c, d)
        return jnp.maximum(hi1, hi2) + jnp.maximum(jnp.minimum(hi1, hi2), jnp.maximum(lo1, lo2))

    assert EXP_PER_GROUP == 4
    best = top2_sum(*p[0:EXP_PER_GROUP])
    gsel = jnp.zeros_like(best, dtype=jnp.int32)
    for g in range(1, N_EXP_GROUPS):
        s = top2_sum(*p[g * EXP_PER_GROUP:(g + 1) * EXP_PER_GROUP])
        better = s > best
        gsel = jnp.where(better, g, gsel)
        best = jnp.where(better, s, best)
    inner = []
    for i in range(EXP_PER_GROUP):
        v = p[i]
        for g in range(1, N_EXP_GROUPS):
            v = jnp.where(gsel == g, p[g * EXP_PER_GROUP + i], v)
        inner.append(v)

    def first_argmax(vals, skip=None):
        bv = bi = None
        for i, v in enumerate(vals):
            v = v if skip is None else jnp.where(skip == i, -jnp.inf, v)
            if bv is None:
                bv, bi = v, jnp.zeros_like(gsel)
            else:
                better = v > bv
                bi = jnp.where(better, i, bi)
                bv = jnp.where(better, v, bv)
        return bv, bi

    p1, i1 = first_argmax(inner)
    p2, i2 = first_argmax(inner, skip=i1)
    tot = p1 + p2
    e1 = gsel * EXP_PER_GROUP + i1
    e2 = gsel * EXP_PER_GROUP + i2
    e_ref[0:1, :] = e1
    e_ref[1:2, :] = e2
    g_ref[0:1, :] = p1 / tot
    g_ref[1:2, :] = p2 / tot


def _ln_router_body(x_ref, x2_ref, mix_ref, g_ref, b_ref, w_ref, br_ref, o_ref, e_ref, gw_ref, *, first_blocks):
    x = x_ref[...] if first_blocks is None else jnp.where(pl.program_id(0) < first_blocks, x_ref[...], x2_ref[...])
    z = ALPHA * x + mix_ref[...]
    mu = jnp.mean(z, -1, keepdims=True)
    zc = z - mu
    var = jnp.mean(zc * zc, -1, keepdims=True)
    y = zc * lax.rsqrt(var + LN_EPS) * g_ref[...] + b_ref[...]
    o_ref[...] = y
    _route(y, w_ref, br_ref, e_ref, gw_ref)


def _res_layernorm_router(x, mix, g, b, w_router, b_router, *, tm, name, x_tail=None):
    t, d = mix.shape
    row = pl.BlockSpec((tm, d), lambda i: (i, 0))
    vec = pl.BlockSpec((1, d), lambda i: (0, 0))
    sel = pl.BlockSpec((TOP_K, tm), lambda i: (0, i))
    if x_tail is None:
        nb, xspecs, xs = None, [row, vec], [x, g.reshape(1, d)]
    else:
        nb = x.shape[0] // tm
        xspecs = [pl.BlockSpec((tm, d), lambda i: (jnp.minimum(i, nb - 1), 0)),
                  pl.BlockSpec((tm, d), lambda i: (jnp.maximum(i - nb, 0), 0))]
        xs = [x, x_tail]
    blocks = 4 * _nbytes((tm, d), F32) + _nbytes((N_EXPERTS, d), F32) + 2 * _nbytes((V7X_SUBLANES, tm), F32)
    return pl.pallas_call(
        functools.partial(_ln_router_body, first_blocks=nb),
        grid=(t // tm,),
        in_specs=xspecs + [row, vec, vec, pl.BlockSpec((N_EXPERTS, d), lambda i: (0, 0)),
                           pl.BlockSpec((N_EXPERTS, 1), lambda i: (0, 0))],
        out_specs=[row, sel, sel],
        out_shape=[jax.ShapeDtypeStruct((t, d), F32), jax.ShapeDtypeStruct((TOP_K, t), jnp.int32),
                   jax.ShapeDtypeStruct((TOP_K, t), F32)],
        compiler_params=_params(("arbitrary",), blocks),
        name=name,
    )(*xs, mix, g.reshape(1, d), b.reshape(1, d), w_router.T, b_router.reshape(N_EXPERTS, 1))


MOE_TILE = 256
COMBINE_TILE = 256


def _moe_plan(eidx):
    k, t = eidx.shape
    ns = k * t
    n_tiles = ns // MOE_TILE + N_EXPERTS
    e_flat = eidx.reshape(ns)
    order = jnp.argsort(e_flat, stable=True).astype(jnp.int32)
    rank = jnp.argsort(order).astype(jnp.int32)
    experts = jnp.arange(N_EXPERTS, dtype=jnp.int32)[None, :]
    slot_is = e_flat[:, None] == experts
    counts = jnp.sum(slot_is.astype(jnp.int32), axis=0)
    start = jnp.cumsum(counts) - counts
    tiles = (counts + MOE_TILE - 1) // MOE_TILE
    tile_end = jnp.cumsum(tiles)
    tile_start = tile_end - tiles
    tile_ids = jnp.arange(n_tiles, dtype=jnp.int32)
    tile_expert = jnp.minimum(jnp.sum((tile_ids[:, None] >= tile_end[None, :]).astype(jnp.int32), axis=1),
                              N_EXPERTS - 1)
    pick = lambda onehot, table: jnp.sum(jnp.where(onehot, table[None, :], 0), axis=1)
    tile_is = tile_expert[:, None] == experts
    tile_first = pick(tile_is, start) + (tile_ids - pick(tile_is, tile_start)) * MOE_TILE
    position_of_slot = pick(slot_is, tile_start) * MOE_TILE + rank - pick(slot_is, start)
    sorted_token = jnp.pad(order % t, (0, MOE_TILE))
    return (tile_expert.astype(jnp.int32), tile_first.astype(jnp.int32), tile_end[-1:].astype(jnp.int32),
            sorted_token.astype(jnp.int32), position_of_slot.astype(jnp.int32))


def _moe_expert_body(te_ref, first_ref, nt_ref, tok_ref, x_hbm, wg_ref, wu_ref, wd_ref, y_ref, xbuf, sem, wg_scr,
                     wu_scr, wd_scr):
    i = pl.program_id(0)
    nt = nt_ref[0]
    tm = MOE_TILE
    sub = V7X_SUBLANES

    def gather_copy(tok, slot, r8, j):
        return pltpu.make_async_copy(x_hbm.at[pl.ds(tok, 1)], xbuf.at[slot, r8, pl.ds(j, 1)], sem.at[slot])

    def start_gather(tile, slot):
        first = first_ref[tile]

        def body(r8, c):
            for j in range(sub):
                gather_copy(tok_ref[first + r8 * sub + j], slot, r8, j).start(priority=j % 2)
            return c

        lax.fori_loop(0, tm // sub, body, 0)

    @pl.when(i == 0)
    def _():
        start_gather(0, 0)

    @pl.when(i + 1 < nt)
    def _():
        start_gather(i + 1, (i + 1) % 2)

    @pl.when(i < nt)
    def _():
        slot = i % 2
        for r8 in range(tm // sub):
            pltpu.make_async_copy(x_hbm.at[pl.ds(0, sub)], xbuf.at[slot, r8], sem.at[slot]).wait()

        @pl.when((i == 0) | (te_ref[i] != te_ref[jnp.maximum(i - 1, 0)]))
        def _():
            wg_scr[...] = wg_ref[0, 0].astype(BF16)
            wu_scr[...] = wu_ref[0, 0].astype(BF16)
            wd_scr[...] = wd_ref[0, 0].astype(BF16)

        x = xbuf[slot].reshape(tm, x_hbm.shape[1]).astype(BF16)
        act = (_silu(_dot(x, wg_scr[...], NN)) * _dot(x, wu_scr[...], NN)).astype(BF16)
        y_ref[...] = _dot(act, wd_scr[...], NN)

    @pl.when(i >= nt)
    def _():
        y_ref[...] = jnp.zeros_like(y_ref)


def _moe_experts(x, plan, w_gate, w_up, w_down, layer):
    tile_expert, tile_first, n_valid, sorted_token, _ = plan
    t, d = x.shape
    f = w_gate.shape[3]
    n_tiles = tile_expert.shape[0]
    wspec = lambda a, b: pl.BlockSpec((1, 1, a, b), lambda i, te, *_: (layer, te[i], 0, 0))
    blocks = (2 * 3 * _nbytes((d, f), F32) + 3 * _nbytes((d, f), BF16) + 2 * _nbytes((MOE_TILE, d), F32)
              + 2 * _nbytes((MOE_TILE, d), F32) + 4 * _nbytes((MOE_TILE, f), F32))
    return pl.pallas_call(
        _moe_expert_body,
        grid_spec=pltpu.PrefetchScalarGridSpec(
            num_scalar_prefetch=4,
            grid=(n_tiles,),
            in_specs=[pl.BlockSpec(memory_space=pl.ANY), wspec(d, f), wspec(d, f), wspec(f, d)],
            out_specs=pl.BlockSpec((MOE_TILE, d), lambda i, *_: (i, 0)),
            scratch_shapes=[pltpu.VMEM((2, MOE_TILE // V7X_SUBLANES, V7X_SUBLANES, d), F32),
                            pltpu.SemaphoreType.DMA((2,)),
                            pltpu.VMEM((d, f), BF16), pltpu.VMEM((d, f), BF16), pltpu.VMEM((f, d), BF16)]),
        out_shape=jax.ShapeDtypeStruct((n_tiles * MOE_TILE, d), F32),
        compiler_params=pltpu.CompilerParams(dimension_semantics=("arbitrary",),
                                             vmem_limit_bytes=int(min(blocks + COMPILER_SCRATCH_BYTES, V7X_SCOPED_VMEM_CAP))),
        name="moe_experts",
    )(tile_expert, tile_first, n_valid, sorted_token, x, w_gate, w_up, w_down)


def _moe_combine_body(pos_ref, x_ref, w_ref, g_ref, b_ref, y_hbm, o_ref, ob_ref, ybuf, sem, *, n_tok, split_blocks):
    i = pl.program_id(0)
    tc = COMBINE_TILE

    sub = V7X_SUBLANES

    def gather_copy(p, slot, k, r8, j):
        return pltpu.make_async_copy(y_hbm.at[pl.ds(p, 1)], ybuf.at[slot, k, r8, pl.ds(j, 1)], sem.at[slot])

    def start_gather(tile, slot):
        def body(r8, c):
            for k in range(TOP_K):
                for j in range(sub):
                    gather_copy(pos_ref[k * n_tok + tile * tc + r8 * sub + j], slot, k, r8, j).start(priority=j % 2)
            return c

        lax.fori_loop(0, tc // sub, body, 0)

    @pl.when(i == 0)
    def _():
        start_gather(0, 0)

    @pl.when(i + 1 < pl.num_programs(0))
    def _():
        start_gather(i + 1, (i + 1) % 2)

    slot = i % 2
    for k in range(TOP_K):
        for r8 in range(tc // sub):
            pltpu.make_async_copy(y_hbm.at[pl.ds(0, sub)], ybuf.at[slot, k, r8], sem.at[slot]).wait()
    z = ALPHA * x_ref[...]
    for k in range(TOP_K):
        z = z + w_ref[:, k:k + 1] * ybuf[slot, k].reshape(tc, x_ref.shape[1])
    mu = jnp.mean(z, -1, keepdims=True)
    zc = z - mu
    var = jnp.mean(zc * zc, -1, keepdims=True)
    y = zc * lax.rsqrt(var + LN_EPS) * g_ref[...] + b_ref[...]
    if split_blocks is None:
        o_ref[...] = y
        ob_ref[...] = y.astype(BF16)
    else:
        @pl.when(i < split_blocks)
        def _():
            o_ref[...] = y

        @pl.when(i >= split_blocks)
        def _():
            ob_ref[...] = y


def _moe_combine_layernorm(x, y_slots, plan, gate_w, g, b, *, name, split=None):
    position_of_slot = plan[4]
    t, d = x.shape
    tc = COMBINE_TILE
    row = pl.BlockSpec((tc, d), lambda i, pos: (i, 0))
    vec = pl.BlockSpec((1, d), lambda i, pos: (0, 0))
    blocks = 2 * (3 * _nbytes((tc, d), F32) + _nbytes((tc, d), BF16)) + 2 * TOP_K * _nbytes((tc, d), F32)
    if split is None:
        sb = None
        out_specs = [row, row]
        out_shape = [jax.ShapeDtypeStruct((t, d), F32), jax.ShapeDtypeStruct((t, d), BF16)]
    else:
        sb = split // tc
        out_specs = [pl.BlockSpec((tc, d), lambda i, pos: (jnp.minimum(i, sb - 1), 0)),
                     pl.BlockSpec((tc, d), lambda i, pos: (jnp.maximum(i - sb, 0), 0))]
        out_shape = [jax.ShapeDtypeStruct((split, d), F32), jax.ShapeDtypeStruct((t - split, d), F32)]
    return pl.pallas_call(
        functools.partial(_moe_combine_body, n_tok=t, split_blocks=sb),
        grid_spec=pltpu.PrefetchScalarGridSpec(
            num_scalar_prefetch=1,
            grid=(t // tc,),
            in_specs=[row, pl.BlockSpec((tc, TOP_K), lambda i, pos: (i, 0)), vec, vec,
                      pl.BlockSpec(memory_space=pl.ANY)],
            out_specs=out_specs,
            scratch_shapes=[pltpu.VMEM((2, TOP_K, tc // V7X_SUBLANES, V7X_SUBLANES, d), F32),
                            pltpu.SemaphoreType.DMA((2,))]),
        out_shape=out_shape,
        compiler_params=pltpu.CompilerParams(dimension_semantics=("arbitrary",),
                                             vmem_limit_bytes=int(min(blocks + COMPILER_SCRATCH_BYTES, V7X_SCOPED_VMEM_CAP))),
        name=name,
    )(position_of_slot, x, gate_w, g.reshape(1, d), b.reshape(1, d), y_slots)


TOKEN_TILE = 1088
IN_PROJ_TN = 1024
LN_TILE = 512


def _moe_layernorm(x, eidx, gate_w, layer, w_gate, w_up, w_down, g, b, split=None):
    plan = _moe_plan(eidx)
    y_slots = _moe_experts(x, plan, w_gate, w_up, w_down, layer)
    return _moe_combine_layernorm(x, y_slots, plan, gate_w.T, g, b, name=f"moe_combine_ln_{layer}", split=split)


def kernel(x_prompt, x_sample, state_hgrn, state_ssm, state_conv, state_s5_re, state_s5_im, w_in0, hg_lb_logits, hg_norm_w, conv_w, conv_b, dt_bias, a_log, m_d, m_norm_w, w_out0, s5_a_re, s5_a_im, s5_log_dt, s5_b_re, s5_b_im, s5_c_re, s5_c_im, s5_d, glu_w_a, glu_b_a, glu_w_b, glu_b_b, w_router, b_router, w_gate, w_up, w_down, ln1_g, ln1_b, ln2_g, ln2_b):
    bp, lp, d = x_prompt.shape
    bs, ls, _ = x_sample.shape
    tp, ts = bp * lp, bs * ls
    tm = TOKEN_TILE
    xp2, xs2 = x_prompt.reshape(tp, d), x_sample.reshape(ts, d)
    w_dt = jnp.pad(w_in0[0, :, IN0_MAIN:], ((0, 0), (0, V7X_LANES - M_HEADS))).astype(BF16)
    x0b, dt = _join_bf16(xp2, xs2, w_dt, tm=LN_TILE)
    lower_bounds = jnp.cumsum(jax.nn.softmax(hg_lb_logits.astype(F32), axis=0), axis=0)
    pad_lanes = lambda v: jnp.pad(v, (0, V7X_LANES - v.shape[0])).reshape(1, V7X_LANES)

    proj = _matmul_w32(x0b, jnp.swapaxes(w_in0, 1, 2), 0, tm=tm, tn=IN_PROJ_TN, n_cols=IN0_MAIN,
                       transposed=True, name="in_proj")
    lb0 = lower_bounds[0]
    e2 = _head_expand_matrix()
    ssd_consts = (conv_w[0], conv_b[0].reshape(1, -1), pad_lanes(dt_bias[0]), pad_lanes(a_log[0]),
                  jnp.repeat(m_d[0], M_HEADDIM).reshape(1, M_DI), m_norm_w[0].reshape(1, M_DI), e2)
    hgrn_prompt = functools.partial(_hgrn_prompt, proj, lb0, hg_norm_w[0], batch=bp, seq=lp)
    mixed, hg_p = lax.cond(_hgrn_lower_bound_is_safe(lb0), functools.partial(hgrn_prompt, guarded=False),
                           functools.partial(hgrn_prompt, guarded=True))
    mixed, hg_s = _hgrn_sample(proj, tp, state_hgrn[0], lb0, hg_norm_w[0], batch=bs, seq=ls, prev=mixed)
    mixed, ssm_p, conv_p = _ssd_prompt(proj, dt, *ssd_consts, batch=bp, seq=lp, prev=mixed)
    mixed, ssm_s, conv_s = _ssd_sample(proj, tp, dt, state_ssm[0], state_conv[0], *ssd_consts, batch=bs, seq=ls,
                                       prev=mixed)
    mix = _matmul_w32(mixed, w_out0, 0, tm=tm, tn=512, name="out_proj")
    x1, eidx, gate_w = _res_layernorm_router(xp2, mix, ln1_g[0], ln1_b[0], w_router, b_router, tm=LN_TILE,
                                             name="ln1_router_0", x_tail=xs2)
    x2, x2b = _moe_layernorm(x1, eidx, gate_w, 0, w_gate, w_up, w_down, ln2_g[0], ln2_b[0])

    ab_re, ab_im, wb_re, wb_im, wc_re, wc_im = _s5_discretize(
        s5_a_re[0], s5_a_im[0], s5_log_dt[0], s5_b_re[0], s5_b_im[0], s5_c_re[0], s5_c_im[0])
    s5_consts = (ab_re, ab_im, wb_re, wb_im, wc_re, wc_im, s5_d[0])
    zeros = jnp.zeros((bp, S5_NBLK, S5_TILES, V7X_LANES), F32)
    act, s5r_p, s5i_p = _s5_fused(x2b, x2, zeros, zeros, *s5_consts, row0=0, batch=bp, seq=lp,
                                  rows=S5_PROMPT_ROWS, n_seq=1, name="s5_prompt")
    act, s5r_s, s5i_s = _s5_fused(x2b, x2, _s5_slab_layout(state_s5_re[0].reshape(bs, S5_STATE)),
                                  _s5_slab_layout(state_s5_im[0].reshape(bs, S5_STATE)), *s5_consts,
                                  row0=tp, batch=bs, seq=ls, rows=S5_SAMPLE_ROWS, n_seq=S5_SAMPLE_ROWS // ls,
                                  prev=act, name="s5_sample")
    mix1 = _glu(act, glu_w_a, glu_w_b, glu_b_a, glu_b_b, 0, tm=tm, tn=512)
    x3, eidx, gate_w = _res_layernorm_router(x2, mix1, ln1_g[1], ln1_b[1], w_router, b_router, tm=LN_TILE,
                                             name="ln1_router_1")
    y_p, y_s = _moe_layernorm(x3, eidx, gate_w, 1, w_gate, w_up, w_down, ln2_g[1], ln2_b[1], split=tp)

    s5_state = lambda a, b: _s5_from_slab_layout(a).reshape(1, b, S5_GROUPS, S5_P)
    return (y_p.reshape(bp, lp, d), y_s.reshape(bs, ls, d),
            hg_p[None], hg_s[None], ssm_p[None], ssm_s[None], conv_p[None], conv_s[None],
            s5_state(s5r_p, bp), s5_state(s5r_s, bs), s5_state(s5i_p, bp), s5_state(s5i_s, bs))
```

```python
import functools
import math

import jax
import jax.numpy as jnp
from jax import lax
from jax.experimental import pallas as pl
from jax.experimental.pallas import tpu as pltpu

F32 = jnp.float32
BF16 = jnp.bfloat16
HIGHEST = lax.Precision.HIGHEST

D_MODEL = 2048
DEPTH = 2
HG_HEADS = 16
HG_DK = 128
HG_DV = 128
HG_KW = HG_HEADS * HG_DK
HG_VW = HG_HEADS * HG_DV
HG_CHUNK = 64
M_DI = 2048
M_HEADDIM = 64
M_HEADS = 32
M_GROUPS = 8
M_HPG = 4
M_STATE = 128
M_CONV = 4
M_CONV_DIM = M_DI + 2 * M_GROUPS * M_STATE
M_CHUNK = 128
IN0_MAIN = 2 * HG_KW + 2 * HG_VW + M_DI + M_CONV_DIM
S5_GROUP = 16
S5_GROUPS = 128
S5_P = 64
S5_STATE = S5_GROUPS * S5_P
N_EXPERTS = 16
N_EXP_GROUPS = 4
EXP_PER_GROUP = 4
TOP_K = 2
D_EXPERT = 512
ALPHA = (2 * DEPTH) ** 0.25
LN_EPS = 1e-5
RMS_EPS = 1e-6

V7X_LANES = 128
V7X_SUBLANES = 8
V7X_SCOPED_VMEM_CAP = 60000 * 1024
COMPILER_SCRATCH_BYTES = 16 * 1024 * 1024


def _params(semantics, block_bytes):
    limit = min(2 * block_bytes + COMPILER_SCRATCH_BYTES, V7X_SCOPED_VMEM_CAP)
    return pltpu.CompilerParams(dimension_semantics=semantics, vmem_limit_bytes=int(limit))


def _nbytes(shape, dtype):
    return math.prod(shape) * jnp.dtype(dtype).itemsize


def _silu(x):
    return x * jax.nn.sigmoid(x)


def _dot(a, b, dims, precision=None):
    return lax.dot_general(a, b, (dims, ((), ())), precision=precision, preferred_element_type=F32)


NN = ((1,), (0,))
NT = ((1,), (1,))
TN = ((0,), (0,))

MIX0_WIDTH = HG_VW + M_DI


def _shared_output(body, n_in, prev):
    if prev is None:
        return body, [], [], {}

    def with_prev(*refs):
        return body(*refs[:n_in], *refs[n_in + 1:])

    return with_prev, [pl.BlockSpec(memory_space=pl.ANY)], [prev], {n_in: 0}


def _join_bf16_body(a_ref, b_ref, w_ref, o_ref, p_ref, *, first_blocks):
    x = jnp.where(pl.program_id(0) < first_blocks, a_ref[...], b_ref[...]).astype(o_ref.dtype)
    o_ref[...] = x
    p_ref[...] = _dot(x, w_ref[...], NN)


def _join_bf16(a, b, w, *, tm):
    d = a.shape[1]
    na, nb = a.shape[0] // tm, b.shape[0] // tm
    rows = a.shape[0] + b.shape[0]
    blocks = 2 * _nbytes((tm, d), F32) + _nbytes((tm, d), BF16) + _nbytes(w.shape, BF16) + _nbytes((tm, w.shape[1]), F32)
    return pl.pallas_call(
        functools.partial(_join_bf16_body, first_blocks=na),
        grid=(na + nb,),
        in_specs=[pl.BlockSpec((tm, d), lambda i: (jnp.minimum(i, na - 1), 0)),
                  pl.BlockSpec((tm, d), lambda i: (jnp.maximum(i - na, 0), 0)),
                  pl.BlockSpec(w.shape, lambda i: (0, 0))],
        out_specs=[pl.BlockSpec((tm, d), lambda i: (i, 0)), pl.BlockSpec((tm, w.shape[1]), lambda i: (i, 0))],
        out_shape=[jax.ShapeDtypeStruct((rows, d), BF16), jax.ShapeDtypeStruct((rows, w.shape[1]), F32)],
        compiler_params=_params(("arbitrary",), blocks),
        name="join_bf16",
    )(a, b, w)


def _mm_w32_body(x_ref, w_ref, o_ref, wb_scr):
    @pl.when(pl.program_id(1) == 0)
    def _():
        wb_scr[...] = w_ref[0].astype(BF16)

    o_ref[...] = _dot(x_ref[...], wb_scr[...], NN).astype(o_ref.dtype)


def _mm_w32t_body(x_ref, w_ref, o_ref, wb_scr):
    @pl.when(pl.program_id(1) == 0)
    def _():
        wb_scr[...] = w_ref[0].astype(BF16)

    o_ref[...] = _dot(x_ref[...], wb_scr[...], NT).astype(o_ref.dtype)


def _matmul_w32(x, w, layer, *, tm, tn, n_cols=None, transposed=False, name):
    m, k = x.shape
    n = n_cols if n_cols is not None else w.shape[1 if transposed else 2]
    blocks = _nbytes((tm, k), BF16) + _nbytes((k, tn), F32) + _nbytes((tm, tn), F32) + _nbytes((k, tn), BF16)
    if transposed:
        body, wblock, wspec = _mm_w32t_body, (tn, k), pl.BlockSpec((1, tn, k), lambda j, i: (layer, j, 0))
    else:
        body, wblock, wspec = _mm_w32_body, (k, tn), pl.BlockSpec((1, k, tn), lambda j, i: (layer, 0, j))
    return pl.pallas_call(
        body,
        grid=(n // tn, m // tm),
        in_specs=[pl.BlockSpec((tm, k), lambda j, i: (i, 0)), wspec],
        out_specs=pl.BlockSpec((tm, tn), lambda j, i: (i, j)),
        out_shape=jax.ShapeDtypeStruct((m, n), F32),
        scratch_shapes=[pltpu.VMEM(wblock, BF16)],
        compiler_params=_params(("arbitrary", "arbitrary"), blocks),
        name=name,
    )(x, w)


def _hgrn_gates(q, f, lb):
    fg = lb + (1.0 - lb) * jax.nn.sigmoid(f)
    return _silu(q), 1.0 - fg, jnp.log(fg)


HG_SAFE_EXPONENT = 80.0


def _hgrn_exact_intra(q, k, cum, v_ref, heads, intra_scr, q_scr, k_scr, cum_scr):
    c = cum.shape[0]
    q_scr[...] = q
    k_scr[...] = k
    cum_scr[...] = cum
    unsafe = jnp.max(jnp.abs(cum - cum[c // 2 - 1:c // 2, :])) > HG_SAFE_EXPONENT

    @pl.when(unsafe)
    def _():
        srow = lax.broadcasted_iota(jnp.int32, (c, 1), 0)
        sub = V7X_SUBLANES

        def row_tile(t8, carry):
            base = pl.multiple_of(t8 * sub, sub)
            cum_t = cum_scr[pl.ds(base, sub), :]
            q_t = q_scr[pl.ds(base, sub), :]
            rows_out = [[] for _ in heads]
            for j in range(sub):
                decay = jnp.exp(jnp.minimum(cum_t[j:j + 1, :] - cum_scr[...], 0.0))
                w = jnp.where(srow <= base + j, q_t[j:j + 1, :] * k_scr[...] * decay, 0.0)
                for h, sl in enumerate(heads):
                    coef = jnp.sum(w[:, sl], axis=-1, keepdims=True)
                    rows_out[h].append(jnp.sum(coef * v_ref[:, sl], axis=0, keepdims=True))
            for h, sl in enumerate(heads):
                intra_scr[pl.ds(base, sub), sl] = jnp.concatenate(rows_out[h], axis=0)
            return carry

        lax.fori_loop(0, c // sub, row_tile, 0)

    return unsafe


def _hgrn_prompt_body(q_ref, f_ref, v_ref, g_ref, lb_ref, nw_ref, o_ref, sfin_ref, st_scr, *guard_scr, guarded):
    n = pl.program_id(1)
    c = HG_CHUNK

    @pl.when(n == 0)
    def _():
        st_scr[...] = jnp.zeros_like(st_scr)
        if guarded:
            guard_scr[0][...] = jnp.zeros_like(guard_scr[0])

    row = lax.broadcasted_iota(jnp.int32, (c, c), 0)
    col = lax.broadcasted_iota(jnp.int32, (c, c), 1)
    causal = row >= col
    tri = causal.astype(F32)
    heads = [slice(h * HG_DK, (h + 1) * HG_DK) for h in range(HG_HEADS)]
    q, k, logf = _hgrn_gates(q_ref[...], f_ref[...], lb_ref[...])
    cum = _dot(tri, logf, NN, precision=HIGHEST)
    if guarded:
        unsafe = _hgrn_exact_intra(q, k, cum, v_ref, heads, *guard_scr)
        q, k, cum = (r[...] for r in guard_scr[1:])
    v = v_ref[...].astype(BF16)
    mid = cum[c // 2 - 1:c // 2, :]
    last = cum[c - 1:c, :]
    qm32 = q * jnp.exp(cum - mid)
    km32 = k * jnp.exp(mid - cum)
    qm = qm32.astype(BF16)
    km = km32.astype(BF16)
    if guarded:
        qi = (q * jnp.exp(cum)).astype(BF16)
        kl = (k * jnp.exp(last - cum)).astype(BF16)
    else:
        qi = (qm32 * jnp.exp(mid)).astype(BF16)
        kl = (km32 * jnp.exp(last - mid)).astype(BF16)
    a_last = jnp.exp(last)
    gate = nw_ref[...] * _silu(g_ref[...])
    scores = [jnp.where(causal, _dot(qm[:, sl], km[:, sl], NT), 0.0).astype(BF16) for sl in heads]
    sts = [st_scr[h] for h in range(HG_HEADS)]
    if guarded:
        outs = [_dot(qi[:, sl], sts[h].astype(BF16), NT)
                + jnp.where(unsafe, guard_scr[0][:, sl], _dot(scores[h], v[:, sl], NN)) for h, sl in enumerate(heads)]
    else:
        outs = [_dot(qi[:, sl], sts[h].astype(BF16), NT) + _dot(scores[h], v[:, sl], NN)
                for h, sl in enumerate(heads)]
    for h, sl in enumerate(heads):
        st_scr[h] = a_last[:, sl] * sts[h] + _dot(v[:, sl], kl[:, sl], TN)
    for h, sl in enumerate(heads):
        o = outs[h]
        o = o * lax.rsqrt(jnp.mean(o * o, -1, keepdims=True) + RMS_EPS)
        o_ref[:, sl] = (o * gate[:, sl]).astype(o_ref.dtype)

    @pl.when(n == pl.num_programs(1) - 1)
    def _():
        sfin_ref[0] = st_scr[...]


def _hgrn_lower_bound_is_safe(lb):
    return jnp.min(lb) >= math.exp(-HG_SAFE_EXPONENT / (HG_CHUNK // 2))


def _hgrn_prompt(proj, lb, nw, *, batch, seq, guarded, prev=None):
    c = HG_CHUNK
    nchunk = seq // c
    blk = lambda j: pl.BlockSpec((c, HG_KW), lambda b, n: (b * nchunk + n, j))
    vec = pl.BlockSpec((1, HG_KW), lambda b, n: (0, 0))
    blocks = 4 * _nbytes((c, HG_KW), F32) + _nbytes((c, HG_VW), BF16) + 2 * _nbytes((HG_HEADS, HG_DV, HG_DK), F32)
    guard_scratch = [pltpu.VMEM((c, HG_KW), F32)] * 4 if guarded else []
    body, xspecs, xops, alias = _shared_output(functools.partial(_hgrn_prompt_body, guarded=guarded), 6, prev)
    o, st = pl.pallas_call(
        body,
        grid=(batch, nchunk),
        in_specs=[blk(0), blk(1), blk(2), blk(3), vec, vec] + xspecs,
        out_specs=[pl.BlockSpec((c, HG_VW), lambda b, n: (b * nchunk + n, 0)),
                   pl.BlockSpec((1, HG_HEADS, HG_DV, HG_DK), lambda b, n: (b, 0, 0, 0))],
        out_shape=[jax.ShapeDtypeStruct((proj.shape[0], MIX0_WIDTH), BF16),
                   jax.ShapeDtypeStruct((batch, HG_HEADS, HG_DV, HG_DK), F32)],
        scratch_shapes=[pltpu.VMEM((HG_HEADS, HG_DV, HG_DK), F32)] + guard_scratch,
        input_output_aliases=alias,
        compiler_params=_params(("arbitrary", "arbitrary"), blocks),
        name="hgrn_prompt_guarded" if guarded else "hgrn_prompt",
    )(proj, proj, proj, proj, lb.reshape(1, HG_KW), nw.reshape(1, HG_VW), *xops)
    return o, jnp.swapaxes(st, -1, -2)


HG_SAMPLE_BB = 8


def _hgrn_sample_body(q_ref, f_ref, v_ref, g_ref, lb_ref, nw_ref, s_ref, o_ref, so_ref, *, seq):
    rows = 2 * seq
    assert rows == V7X_SUBLANES
    row = lax.broadcasted_iota(jnp.int32, (rows, rows), 0)
    col = lax.broadcasted_iota(jnp.int32, (rows, rows), 1)
    causal = (row >= col) & ((row // seq) == (col // seq))
    rvec = lax.broadcasted_iota(jnp.int32, (rows, 1), 0)
    r16 = lax.broadcasted_iota(jnp.int32, (2 * rows, HG_DV), 0)
    ones_rows = jnp.where((r16 == rows) | (r16 == rows + 1), 1.0, 0.0).astype(BF16)

    heads = [slice(h * HG_DK, (h + 1) * HG_DK) for h in range(HG_HEADS)]
    in_seq = rvec % seq

    def pair(p, carry):
        r0 = pl.multiple_of(p * rows, rows)
        tile = pl.ds(r0, rows)
        q, k, logf = _hgrn_gates(q_ref[tile, :], f_ref[tile, :], lb_ref[...])
        v = v_ref[tile, :].astype(BF16)
        cum = logf
        step = 1
        while step < seq:
            cum = cum + jnp.where(in_seq >= step, _shift_rows(cum, step), 0.0)
            step *= 2
        ecum = jnp.exp(cum)
        qi = (q * ecum).astype(BF16)
        km = (k / ecum).astype(BF16)
        gate = nw_ref[...] * _silu(g_ref[tile, :])
        v16 = jnp.concatenate([v, jnp.zeros_like(v)], axis=0)
        lhs = []
        for j in range(2):
            mine = (rvec // seq) == j
            last = cum[(j + 1) * seq - 1:(j + 1) * seq, :]
            a = jnp.exp(last)
            a_hi = a.astype(BF16).astype(F32)
            kl = jnp.where(mine, k * jnp.exp(last - cum), 0.0)
            lhs.append(jnp.concatenate([kl, a_hi, a - a_hi, jnp.zeros((rows - 2, HG_KW), F32)], axis=0).astype(BF16))
        scores = [jnp.where(causal, _dot(qi[:, sl], km[:, sl], NT), 0.0).astype(BF16) for sl in heads]
        outs = [_dot(scores[h], v[:, sl], NN) for h, sl in enumerate(heads)]
        for j in range(2):
            b = 2 * p + j
            mine = (rvec // seq) == j
            s0 = [s_ref[b, h] for h in range(HG_HEADS)]
            inter = [_dot(qi[:, sl], s0[h].astype(BF16), NN) for h, sl in enumerate(heads)]
            both = [_dot(lhs[j][:, sl], jnp.concatenate([v16[:, sl], ones_rows], axis=1), TN)
                    for sl in heads]
            for h in range(HG_HEADS):
                outs[h] = outs[h] + jnp.where(mine, inter[h], 0.0)
                so_ref[b, h] = both[h][:, HG_DV:] * s0[h] + both[h][:, :HG_DV]
        for h, sl in enumerate(heads):
            o = outs[h]
            o = o * lax.rsqrt(jnp.mean(o * o, -1, keepdims=True) + RMS_EPS)
            o_ref[tile, sl] = (o * gate[:, sl]).astype(o_ref.dtype)
        return carry

    lax.fori_loop(0, HG_SAMPLE_BB // 2, pair, 0)


def _hgrn_sample(proj, row0, state, lb, nw, *, batch, seq, prev=None):
    bb = HG_SAMPLE_BB
    tr = bb * seq
    blk0 = row0 // tr
    blk = lambda j: pl.BlockSpec((tr, HG_KW), lambda i: (blk0 + i, j))
    vec = pl.BlockSpec((1, HG_KW), lambda i: (0, 0))
    sblk = pl.BlockSpec((bb, HG_HEADS, HG_DK, HG_DV), lambda i: (i, 0, 0, 0))
    blocks = 4 * _nbytes((tr, HG_KW), F32) + 2 * _nbytes((bb, HG_HEADS, HG_DK, HG_DV), F32)
    body, xspecs, xops, alias = _shared_output(functools.partial(_hgrn_sample_body, seq=seq), 7, prev)
    return pl.pallas_call(
        body,
        grid=(batch // bb,),
        in_specs=[blk(0), blk(1), blk(2), blk(3), vec, vec, sblk] + xspecs,
        out_specs=[pl.BlockSpec((tr, HG_VW), lambda i: (blk0 + i, 0)), sblk],
        out_shape=[jax.ShapeDtypeStruct((proj.shape[0], MIX0_WIDTH), BF16),
                   jax.ShapeDtypeStruct(state.shape, F32)],
        input_output_aliases=alias,
        compiler_params=_params(("arbitrary",), blocks),
        name="hgrn_sample",
    )(proj, proj, proj, proj, lb.reshape(1, HG_KW), nw.reshape(1, HG_VW), state, *xops)


def _head_expand_matrix():
    r = jnp.arange(2 * V7X_LANES)[:, None] % V7X_LANES
    c = jnp.arange(M_DI)[None, :] // M_HEADDIM
    return (r == c).astype(BF16)


def _expand_heads(coef, e2):
    hi = coef.astype(BF16)
    lo = (coef - hi.astype(F32)).astype(BF16)
    return _dot(jnp.concatenate([hi, lo], axis=1), e2, NN)


def _softplus(x):
    return jnp.maximum(x, 0.0) + jnp.log(1.0 + jnp.exp(-jnp.abs(x)))


def _group_rmsnorm(y, width):
    outs = []
    for g in range(y.shape[1] // width):
        yg = y[:, g * width:(g + 1) * width]
        outs.append(yg * lax.rsqrt(jnp.mean(yg * yg, -1, keepdims=True) + RMS_EPS))
    return jnp.concatenate(outs, axis=1)


def _ssd_prompt_body(z_ref, x_ref, bc_ref, dt_ref, cwx_ref, cwb_ref, cbx_ref, cbb_ref, dtb_ref, alog_ref,
                     dexp_ref, nw_ref, e2_ref, y_ref, sfin_ref, conv_ref, s_scr, cx_scr, cb_scr):
    n = pl.program_id(1)
    c = M_CHUNK
    tail = V7X_SUBLANES

    @pl.when(n == 0)
    def _():
        s_scr[...] = jnp.zeros_like(s_scr)
        cx_scr[...] = jnp.zeros_like(cx_scr)
        cb_scr[...] = jnp.zeros_like(cb_scr)

    def conv(u, carry_scr, w_ref, b_ref):
        ext = jnp.concatenate([carry_scr[...], u], axis=0)
        acc = b_ref[...] + w_ref[M_CONV - 1:M_CONV, :] * u
        for j in range(1, M_CONV):
            acc = acc + w_ref[M_CONV - 1 - j:M_CONV - j, :] * pltpu.roll(ext, j, 0)[tail:, :]
        carry_scr[...] = u[c - tail:, :]
        return _silu(acc)

    ux = x_ref[...]
    ubc = bc_ref[...]
    xc = conv(ux, cx_scr, cwx_ref, cbx_ref)
    bcc = conv(ubc, cb_scr, cwb_ref, cbb_ref)
    ngn = M_GROUPS * M_STATE

    row = lax.broadcasted_iota(jnp.int32, (c, c), 0)
    col = lax.broadcasted_iota(jnp.int32, (c, c), 1)
    causal = row >= col
    tri = causal.astype(F32)
    e2 = e2_ref[...]

    dtp = _softplus(dt_ref[...] + dtb_ref[...])
    da = dtp * (-jnp.exp(alog_ref[...]))
    cum = _dot(tri, da, NN, precision=HIGHEST)
    cum_t = cum.T
    last = cum[c - 1:c, :]
    xdt = xc * _expand_heads(dtp, e2)
    xend = (xdt * _expand_heads(jnp.exp(last - cum), e2)).astype(BF16)
    ecum = _expand_heads(jnp.exp(cum), e2)
    xdt_b = xdt.astype(BF16)
    gw = M_HPG * M_HEADDIM

    ys = []
    for g in range(M_GROUPS):
        bg = bcc[:, g * M_STATE:(g + 1) * M_STATE].astype(BF16)
        cg = bcc[:, ngn + g * M_STATE:ngn + (g + 1) * M_STATE].astype(BF16)
        cb = _dot(cg, bg, NT)
        sg = s_scr[g * M_HPG:(g + 1) * M_HPG].reshape(gw, M_STATE)
        y_inter = _dot(cg, sg.astype(BF16), NT) * ecum[:, g * gw:(g + 1) * gw]
        upd = _dot(xend[:, g * gw:(g + 1) * gw], bg, TN)
        parts = []
        for hh in range(M_HPG):
            h = g * M_HPG + hh
            decay = jnp.exp(jnp.where(causal, cum[:, h:h + 1] - cum_t[h:h + 1, :], -jnp.inf))
            m = (cb * decay).astype(BF16)
            parts.append(_dot(m, xdt_b[:, h * M_HEADDIM:(h + 1) * M_HEADDIM], NN))
            s_scr[h] = jnp.exp(last[:, h:h + 1]) * s_scr[h] + upd[hh * M_HEADDIM:(hh + 1) * M_HEADDIM, :]
        ys.append(jnp.concatenate(parts, axis=1) + y_inter)
    y = jnp.concatenate(ys, axis=1) + dexp_ref[...] * xc
    y = y * _silu(z_ref[...])
    y_ref[...] = (_group_rmsnorm(y, gw) * nw_ref[...]).astype(y_ref.dtype)

    @pl.when(n == pl.num_programs(1) - 1)
    def _():
        sfin_ref[0] = s_scr[...]
        conv_ref[0, :, 0:M_DI] = ux[c - (M_CONV - 1):, :]
        conv_ref[0, :, M_DI:] = ubc[c - (M_CONV - 1):, :]


def _ssd_prompt(proj, dt, conv_w, conv_b, dt_bias, a_log, d_exp, norm_w, e2, *, batch, seq, prev=None):
    c = M_CHUNK
    nchunk = seq // c
    tok = lambda j: pl.BlockSpec((c, M_DI), lambda b, n: (b * nchunk + n, j))
    const = lambda shape, j=0: pl.BlockSpec(shape, lambda b, n: (0, j))
    blocks = (3 * _nbytes((c, M_DI), F32) + _nbytes((c, M_DI), BF16) + _nbytes((2 * V7X_LANES, M_DI), BF16)
              + 2 * _nbytes((M_HEADS, M_HEADDIM, M_STATE), F32) + 12 * _nbytes((c, M_DI), F32))
    body, xspecs, xops, alias = _shared_output(_ssd_prompt_body, 13, prev)
    return pl.pallas_call(
        body,
        grid=(batch, nchunk),
        in_specs=[tok(4), tok(5), tok(6),
                  pl.BlockSpec((c, V7X_LANES), lambda b, n: (b * nchunk + n, 0)),
                  const((M_CONV, M_DI), 0), const((M_CONV, M_DI), 1), const((1, M_DI), 0), const((1, M_DI), 1),
                  const((1, V7X_LANES)), const((1, V7X_LANES)), const((1, M_DI)), const((1, M_DI)),
                  const((2 * V7X_LANES, M_DI))] + xspecs,
        out_specs=[pl.BlockSpec((c, M_DI), lambda b, n: (b * nchunk + n, HG_VW // M_DI)),
                   pl.BlockSpec((1, M_HEADS, M_HEADDIM, M_STATE), lambda b, n: (b, 0, 0, 0)),
                   pl.BlockSpec((1, M_CONV - 1, M_CONV_DIM), lambda b, n: (b, 0, 0))],
        out_shape=[jax.ShapeDtypeStruct((proj.shape[0], MIX0_WIDTH), BF16),
                   jax.ShapeDtypeStruct((batch, M_HEADS, M_HEADDIM, M_STATE), F32),
                   jax.ShapeDtypeStruct((batch, M_CONV - 1, M_CONV_DIM), F32)],
        scratch_shapes=[pltpu.VMEM((M_HEADS, M_HEADDIM, M_STATE), F32),
                        pltpu.VMEM((V7X_SUBLANES, M_DI), F32), pltpu.VMEM((V7X_SUBLANES, M_DI), F32)],
        input_output_aliases=alias,
        compiler_params=_params(("arbitrary", "arbitrary"), blocks),
        name="ssd_prompt",
    )(proj, proj, proj, dt, conv_w, conv_w, conv_b, conv_b, dt_bias, a_log, d_exp, norm_w, e2, *xops)


SSD_SAMPLE_BB = 8


def _shift_rows(a, j):
    return a if j == 0 else pltpu.roll(a, j, 0)


def _ssd_sample_body(z_ref, x_ref, bc_ref, dt_ref, s_ref, cs_ref, cwx_ref, cwb_ref, cbx_ref, cbb_ref, dtb_ref,
                     alog_ref, dexp_ref, nw_ref, e2_ref, y_ref, so_ref, co_ref, *, seq):
    rows = V7X_SUBLANES
    hist = M_CONV - 1
    assert rows == 2 * seq and hist <= seq and hist <= rows - seq
    rvec = lax.broadcasted_iota(jnp.int32, (rows, 1), 0)
    valid = rvec < seq
    lane = lax.broadcasted_iota(jnp.int32, (1, V7X_LANES), 1)
    r16 = lax.broadcasted_iota(jnp.int32, (2 * rows, M_STATE), 0)
    ones_rows = jnp.where((r16 == rows) | (r16 == rows + 1), 1.0, 0.0).astype(BF16)
    e2 = e2_ref[...]
    a_neg = -jnp.exp(alog_ref[...])
    gw = M_HPG * M_HEADDIM
    ngn = M_GROUPS * M_STATE

    def conv(u8, buf, w_ref, b_ref):
        buf8 = jnp.concatenate([buf, jnp.zeros((rows - hist, buf.shape[1]), F32)], axis=0)
        ext = jnp.where(valid, u8, _shift_rows(buf8, rows - hist))
        acc = b_ref[...] + w_ref[hist:hist + 1, :] * ext
        for j in range(1, M_CONV):
            acc = acc + w_ref[hist - j:hist - j + 1, :] * _shift_rows(ext, j)
        new_hist = _shift_rows(ext, rows - (seq - hist))[0:hist, :]
        return _silu(acc), new_hist

    def one(b, u_x, u_bc, z8, dt8):
        xc, nhx = conv(u_x, cs_ref[b, :, 0:M_DI], cwx_ref, cbx_ref)
        bcc, nhb = conv(u_bc, cs_ref[b, :, M_DI:], cwb_ref, cbb_ref)
        co_ref[b, :, 0:M_DI] = nhx
        co_ref[b, :, M_DI:] = nhb
        dtp = jnp.where(valid, _softplus(dt8 + dtb_ref[...]), 0.0)
        cum = dtp * a_neg
        k = 1
        while k < seq:
            cum = cum + jnp.where(rvec >= k, _shift_rows(cum, k), 0.0)
            k *= 2
        last = cum[seq - 1:seq, :]
        bmat = bcc[:, :ngn]
        cmat = bcc[:, ngn:]
        coefs = [dtp, jnp.exp(cum), jnp.exp(last - cum), jnp.broadcast_to(jnp.exp(last), (rows, V7X_LANES))]
        for j in range(seq):
            prod = cmat * _shift_rows(bmat, j)
            cbh = jnp.zeros((rows, V7X_LANES), F32)
            for g in range(M_GROUPS):
                cbg = jnp.sum(prod[:, g * M_STATE:(g + 1) * M_STATE], axis=-1, keepdims=True)
                cbh = jnp.where((lane // M_HPG) == g, cbg, cbh)
            coefs.append(cbh * jnp.exp(cum - _shift_rows(cum, j)))
        wide = _expand_heads(jnp.concatenate(coefs, axis=0), e2)
        part = lambda n: wide[n * rows:(n + 1) * rows, :]
        xdt = xc * part(0)
        ecum = part(1)
        xs = xdt * part(2)
        dec = part(3)[0:1, :]
        y = dexp_ref[...] * xc
        for j in range(seq):
            y = y + part(4 + j) * _shift_rows(xdt, j)
        dec_hi = dec.astype(BF16).astype(F32)
        lhs = jnp.concatenate([xs, dec_hi, dec - dec_hi, jnp.zeros((rows - 2, M_DI), F32)], axis=0).astype(BF16)
        parts = []
        for g in range(M_GROUPS):
            bg = bmat[:, g * M_STATE:(g + 1) * M_STATE].astype(BF16)
            cg = cmat[:, g * M_STATE:(g + 1) * M_STATE].astype(BF16)
            s0 = s_ref[b, g * M_HPG:(g + 1) * M_HPG].reshape(gw, M_STATE)
            parts.append(_dot(cg, s0.astype(BF16), NT) * ecum[:, g * gw:(g + 1) * gw])
            rhs = jnp.concatenate([jnp.concatenate([bg, jnp.zeros((rows, M_STATE), BF16)], axis=0), ones_rows], axis=1)
            both = _dot(lhs[:, g * gw:(g + 1) * gw], rhs, TN)
            s_new = both[:, M_STATE:] * s0 + both[:, :M_STATE]
            so_ref[b, g * M_HPG:(g + 1) * M_HPG] = s_new.reshape(M_HPG, M_HEADDIM, M_STATE)
        y = (y + jnp.concatenate(parts, axis=1)) * _silu(z8)
        return _group_rmsnorm(y, gw) * nw_ref[...]

    def pair(p, carry):
        r0 = pl.multiple_of(p * rows, rows)
        tiles = [ref[pl.ds(r0, rows), :] for ref in (x_ref, bc_ref, z_ref, dt_ref)]
        ys = []
        for j in range(2):
            ys.append(one(2 * p + j, *[_shift_rows(t, j * (rows - seq)) for t in tiles]))
        y_ref[pl.ds(r0, rows), :] = jnp.where(valid, ys[0], _shift_rows(ys[1], seq)).astype(y_ref.dtype)
        return carry

    lax.fori_loop(0, SSD_SAMPLE_BB // 2, pair, 0)


def _ssd_sample(proj, row0, dt, state, conv_state, conv_w, conv_b, dt_bias, a_log, d_exp, norm_w, e2, *,
                batch, seq, prev=None):
    bb = SSD_SAMPLE_BB
    tr = bb * seq
    tok = lambda j: pl.BlockSpec((tr, M_DI), lambda i: (row0 // tr + i, j))
    const = lambda shape, j=0: pl.BlockSpec(shape, lambda i: (0, j))
    sblk = pl.BlockSpec((bb, M_HEADS, M_HEADDIM, M_STATE), lambda i: (i, 0, 0, 0))
    cblk = pl.BlockSpec((bb, M_CONV - 1, M_CONV_DIM), lambda i: (i, 0, 0))
    blocks = (3 * _nbytes((tr, M_DI), F32) + 2 * _nbytes((bb, M_HEADS, M_HEADDIM, M_STATE), F32)
              + 2 * _nbytes((bb, V7X_SUBLANES, M_CONV_DIM), F32) + _nbytes((2 * V7X_LANES, M_DI), BF16))
    body, xspecs, xops, alias = _shared_output(functools.partial(_ssd_sample_body, seq=seq), 15, prev)
    return pl.pallas_call(
        body,
        grid=(batch // bb,),
        in_specs=[tok(4), tok(5), tok(6),
                  pl.BlockSpec((tr, V7X_LANES), lambda i: (row0 // tr + i, 0)),
                  sblk, cblk,
                  const((M_CONV, M_DI), 0), const((M_CONV, M_DI), 1), const((1, M_DI), 0), const((1, M_DI), 1),
                  const((1, V7X_LANES)), const((1, V7X_LANES)), const((1, M_DI)), const((1, M_DI)),
                  const((2 * V7X_LANES, M_DI))] + xspecs,
        out_specs=[pl.BlockSpec((tr, M_DI), lambda i: (row0 // tr + i, HG_VW // M_DI)), sblk, cblk],
        out_shape=[jax.ShapeDtypeStruct((proj.shape[0], MIX0_WIDTH), BF16),
                   jax.ShapeDtypeStruct(state.shape, F32),
                   jax.ShapeDtypeStruct(conv_state.shape, F32)],
        input_output_aliases=alias,
        compiler_params=_params(("arbitrary",), blocks),
        name="ssd_sample",
    )(proj, proj, proj, dt, state, conv_state, conv_w, conv_w, conv_b, conv_b, dt_bias, a_log, d_exp, norm_w, e2, *xops)


S5_TILES = 8
S5_TILE_IN = D_MODEL // S5_TILES
S5_TILE_ST = S5_STATE // S5_TILES


def _s5_discretize(a_re, a_im, log_dt, b_re, b_im, c_re, c_im):
    lam_re = jnp.minimum(a_re, -1e-4)
    lam_im = a_im
    dt = jnp.exp(log_dt)[:, None]
    mag = jnp.exp(lam_re * dt)
    ab_re = mag * jnp.cos(lam_im * dt)
    ab_im = mag * jnp.sin(lam_im * dt)
    den = lam_re * lam_re + lam_im * lam_im
    zr = ((ab_re - 1.0) * lam_re + ab_im * lam_im) / den
    zi = (ab_im * lam_re - (ab_re - 1.0) * lam_im) / den
    bb_re = zr[..., None] * b_re - zi[..., None] * b_im
    bb_im = zr[..., None] * b_im + zi[..., None] * b_re
    gpt = S5_GROUPS // S5_TILES

    def block_diag(blocks, rows, cols):
        rep = jnp.tile(jnp.eye(cols, dtype=F32), (1, gpt))
        wide = jnp.einsum('krc,cn->krn', blocks, rep, precision=HIGHEST)
        own = (jnp.arange(gpt * rows)[:, None] // rows) == (jnp.arange(gpt * cols)[None, :] // cols)
        return jnp.where(own[None], wide, 0.0).astype(BF16)

    def pack_b(bb):
        return block_diag(jnp.swapaxes(bb, 1, 2).reshape(S5_TILES, S5_TILE_IN, S5_P), S5_GROUP, S5_P)

    def pack_c(c):
        return block_diag(jnp.swapaxes(c, 1, 2).reshape(S5_TILES, S5_TILE_ST, S5_GROUP), S5_P, S5_GROUP)

    slab = lambda a: _s5_slab_layout(a.reshape(S5_STATE))
    return slab(ab_re), slab(ab_im), pack_b(bb_re), pack_b(bb_im), pack_c(c_re), pack_c(c_im)


S5_PROMPT_ROWS = 256
S5_SAMPLE_ROWS = 128
S5_NBLK = S5_TILE_ST // V7X_LANES


def _s5_pitch(rows):
    assert rows % V7X_SUBLANES == 0
    return rows + V7X_SUBLANES // 2


def _s5_slab_layout(a):
    lead = a.shape[:-1]
    return jnp.swapaxes(a.reshape(lead + (S5_TILES, S5_NBLK, V7X_LANES)), -3, -2)


def _s5_from_slab_layout(a):
    lead = a.shape[:-3]
    return jnp.swapaxes(a, -3, -2).reshape(lead + (S5_STATE,))


def _s5_fused_body(xb_ref, x_ref, h0r_ref, h0i_ref, ar_ref, ai_ref, wbr_ref, wbi_ref, wcr_ref, wci_ref, d_ref, *rest,
                   n_seq, steps):
    act_ref, fr_ref, fi_ref, bur, bui, cr, ci = rest[-7:]
    hrs, his = bur, bui
    n = pl.program_id(1)
    rows = n_seq * steps
    S5_PITCH = _s5_pitch(rows)
    for s in range(S5_TILES):
        u = xb_ref[:, s * S5_TILE_IN:(s + 1) * S5_TILE_IN]
        br = _dot(u, wbr_ref[s], NN)
        bi = _dot(u, wbi_ref[s], NN)
        for j in range(S5_NBLK):
            bur[j, s * S5_PITCH:s * S5_PITCH + rows, :] = br[:, j * V7X_LANES:(j + 1) * V7X_LANES]
            bui[j, s * S5_PITCH:s * S5_PITCH + rows, :] = bi[:, j * V7X_LANES:(j + 1) * V7X_LANES]

    @pl.when(n == 0)
    def _():
        cr[...] = h0r_ref[...]
        ci[...] = h0i_ref[...]

    ar = [ar_ref[j] for j in range(S5_NBLK)]
    ai = [ai_ref[j] for j in range(S5_NBLK)]

    def seq_body(q, carry):
        def step(t, h):
            hr, hi = h
            idx = pl.ds(q * steps + t, S5_TILES, stride=S5_PITCH)
            nr, ni = [], []
            for j in range(S5_NBLK):
                nr.append(ar[j] * hr[j] - ai[j] * hi[j] + bur[j, idx, :])
                ni.append(ar[j] * hi[j] + ai[j] * hr[j] + bui[j, idx, :])
                hrs[j, idx, :] = nr[j]
                his[j, idx, :] = ni[j]
            return tuple(nr), tuple(ni)

        h0 = (tuple(cr[q, j] for j in range(S5_NBLK)), tuple(ci[q, j] for j in range(S5_NBLK)))
        hr, hi = lax.fori_loop(0, steps, step, h0, unroll=min(steps, 4))
        for j in range(S5_NBLK):
            cr[q, j] = hr[j]
            ci[q, j] = hi[j]
        return carry

    lax.fori_loop(0, n_seq, seq_body, 0)

    for s in range(S5_TILES):
        seg = slice(s * S5_PITCH, s * S5_PITCH + rows)
        hre = jnp.concatenate([hrs[j, seg, :] for j in range(S5_NBLK)], axis=1).astype(BF16)
        him = jnp.concatenate([his[j, seg, :] for j in range(S5_NBLK)], axis=1).astype(BF16)
        y = _dot(hre, wcr_ref[s], NN) - _dot(him, wci_ref[s], NN)
        cs = slice(s * S5_TILE_IN, (s + 1) * S5_TILE_IN)
        act_ref[:, cs] = jax.nn.gelu(y + d_ref[:, cs] * x_ref[:, cs]).astype(act_ref.dtype)

    @pl.when(n == pl.num_programs(1) - 1)
    def _():
        fr_ref[...] = cr[...]
        fi_ref[...] = ci[...]


def _s5_fused(xb, x, h0_re, h0_im, ab_re, ab_im, wb_re, wb_im, wc_re, wc_im, d_skip, *, row0, batch, seq, rows, n_seq,
              prev=None, name):
    t, d = x.shape
    steps = rows // n_seq
    assert n_seq == 1 or steps == seq
    S5_PITCH = _s5_pitch(rows)
    tok = lambda width: pl.BlockSpec((rows, width), lambda i, n: ((row0 + i * n_seq * seq) // rows + n, 0))
    st = pl.BlockSpec((n_seq, S5_NBLK, S5_TILES, V7X_LANES), lambda i, n: (i, 0, 0, 0))
    const = lambda shape: pl.BlockSpec(shape, lambda i, n: (0,) * len(shape), pipeline_mode=pl.Buffered(1))
    slab = pltpu.VMEM((S5_NBLK, S5_TILES * S5_PITCH, V7X_LANES), F32)
    stv = pltpu.VMEM((n_seq, S5_NBLK, S5_TILES, V7X_LANES), F32)
    blocks = (2 * (_nbytes((rows, d), BF16) * 2 + _nbytes((rows, d), F32))
              + 2 * 6 * _nbytes((n_seq, S5_STATE), F32) + 4 * _nbytes(wb_re.shape, BF16)
              + 2 * _nbytes((S5_NBLK, S5_TILES * S5_PITCH, V7X_LANES), F32))
    keep = [] if prev is None else [prev]
    n_in = 11
    return pl.pallas_call(
        functools.partial(_s5_fused_body, n_seq=n_seq, steps=steps),
        grid=(batch // n_seq, seq // steps),
        in_specs=[tok(d), tok(d), st, st, const(ab_re.shape), const(ab_im.shape), const(wb_re.shape),
                  const(wb_im.shape), const(wc_re.shape), const(wc_im.shape), const((1, d))]
                 + [pl.BlockSpec(memory_space=pl.ANY)] * len(keep),
        out_specs=[tok(d), st, st],
        out_shape=[jax.ShapeDtypeStruct((t, d), BF16), jax.ShapeDtypeStruct(h0_re.shape, F32),
                   jax.ShapeDtypeStruct(h0_im.shape, F32)],
        scratch_shapes=[slab, slab, stv, stv],
        input_output_aliases={n_in + k: k for k in range(len(keep))},
        compiler_params=pltpu.CompilerParams(dimension_semantics=("arbitrary", "arbitrary"),
                                             vmem_limit_bytes=int(min(blocks + COMPILER_SCRATCH_BYTES, V7X_SCOPED_VMEM_CAP))),
        name=name,
    )(xb, x, h0_re, h0_im, ab_re, ab_im, wb_re, wb_im, wc_re, wc_im, d_skip.reshape(1, d), *keep)


def _glu_body(a_ref, wa_ref, wb_ref, ba_ref, bb_ref, o_ref, wa_scr, wb_scr):
    @pl.when(pl.program_id(1) == 0)
    def _():
        wa_scr[...] = wa_ref[0].astype(BF16)
        wb_scr[...] = wb_ref[0].astype(BF16)

    a = a_ref[...]
    o_ref[...] = (_dot(a, wa_scr[...], NN) + ba_ref[...]) * jax.nn.sigmoid(_dot(a, wb_scr[...], NN) + bb_ref[...])


def _glu(a, wa, wb, ba, bb, layer, *, tm, tn):
    t, k = a.shape
    n = wa.shape[2]
    wspec = pl.BlockSpec((1, k, tn), lambda j, i: (layer, 0, j))
    bspec = pl.BlockSpec((1, tn), lambda j, i: (layer, j))
    blocks = _nbytes((tm, k), BF16) + 2 * _nbytes((k, tn), F32) + _nbytes((tm, tn), F32) + 2 * _nbytes((k, tn), BF16)
    return pl.pallas_call(
        _glu_body,
        grid=(n // tn, t // tm),
        in_specs=[pl.BlockSpec((tm, k), lambda j, i: (i, 0)), wspec, wspec, bspec, bspec],
        out_specs=pl.BlockSpec((tm, tn), lambda j, i: (i, j)),
        out_shape=jax.ShapeDtypeStruct((t, n), F32),
        scratch_shapes=[pltpu.VMEM((k, tn), BF16)] * 2,
        compiler_params=_params(("arbitrary", "arbitrary"), blocks),
        name="glu",
    )(a, wa, wb, ba, bb)


def _route(x, w_ref, b_ref, e_ref, g_ref):
    w = w_ref[...]
    w_hi = w.astype(BF16)
    w_lo = (w - w_hi.astype(F32)).astype(BF16)
    x_hi = x.astype(BF16)
    x_lo = (x - x_hi.astype(F32)).astype(BF16)
    logits = _dot(w_hi, x_hi, NT) + _dot(w_hi, x_lo, NT) + _dot(w_lo, x_hi, NT) + b_ref[...]
    rows = [logits[e:e + 1, :] for e in range(N_EXPERTS)]
    m = functools.reduce(jnp.maximum, rows)
    ex = [jnp.exp(r - m) for r in rows]
    z = functools.reduce(jnp.add, ex)
    p = [v / z for v in ex]

    def top2_sum(a, b, c, d):
        hi1, lo1, hi2, lo2 = jnp.maximum(a, b), jnp.minimum(a, b), jnp.maximum(c, d), jnp.minimum(c, d)
        return jnp.maximum(hi1, hi2) + jnp.maximum(jnp.minimum(hi1, hi2), jnp.maximum(lo1, lo2))

    assert EXP_PER_GROUP == 4
    best = top2_sum(*p[0:EXP_PER_GROUP])
    gsel = jnp.zeros_like(best, dtype=jnp.int32)
    for g in range(1, N_EXP_GROUPS):
        s = top2_sum(*p[g * EXP_PER_GROUP:(g + 1) * EXP_PER_GROUP])
        better = s > best
        gsel = jnp.where(better, g, gsel)
        best = jnp.where(better, s, best)
    inner = []
    for i in range(EXP_PER_GROUP):
        v = p[i]
        for g in range(1, N_EXP_GROUPS):
            v = jnp.where(gsel == g, p[g * EXP_PER_GROUP + i], v)
        inner.append(v)

    def first_argmax(vals, skip=None):
        bv = bi = None
        for i, v in enumerate(vals):
            v = v if skip is None else jnp.where(skip == i, -jnp.inf, v)
            if bv is None:
                bv, bi = v, jnp.zeros_like(gsel)
            else:
                better = v > bv
                bi = jnp.where(better, i, bi)
                bv = jnp.where(better, v, bv)
        return bv, bi

    p1, i1 = first_argmax(inner)
    p2, i2 = first_argmax(inner, skip=i1)
    tot = p1 + p2
    e1 = gsel * EXP_PER_GROUP + i1
    e2 = gsel * EXP_PER_GROUP + i2
    e_ref[0:1, :] = e1
    e_ref[1:2, :] = e2
    g_ref[0:1, :] = p1 / tot
    g_ref[1:2, :] = p2 / tot


def _ln_router_body(x_ref, x2_ref, mix_ref, g_ref, b_ref, w_ref, br_ref, o_ref, e_ref, gw_ref, *, first_blocks):
    x = x_ref[...] if first_blocks is None else jnp.where(pl.program_id(0) < first_blocks, x_ref[...], x2_ref[...])
    z = ALPHA * x + mix_ref[...]
    mu = jnp.mean(z, -1, keepdims=True)
    zc = z - mu
    var = jnp.mean(zc * zc, -1, keepdims=True)
    y = zc * lax.rsqrt(var + LN_EPS) * g_ref[...] + b_ref[...]
    o_ref[...] = y
    _route(y, w_ref, br_ref, e_ref, gw_ref)


def _res_layernorm_router(x, mix, g, b, w_router, b_router, *, tm, name, x_tail=None):
    t, d = mix.shape
    row = pl.BlockSpec((tm, d), lambda i: (i, 0))
    vec = pl.BlockSpec((1, d), lambda i: (0, 0))
    sel = pl.BlockSpec((TOP_K, tm), lambda i: (0, i))
    if x_tail is None:
        nb, xspecs, xs = None, [row, vec], [x, g.reshape(1, d)]
    else:
        nb = x.shape[0] // tm
        xspecs = [pl.BlockSpec((tm, d), lambda i: (jnp.minimum(i, nb - 1), 0)),
                  pl.BlockSpec((tm, d), lambda i: (jnp.maximum(i - nb, 0), 0))]
        xs = [x, x_tail]
    blocks = 4 * _nbytes((tm, d), F32) + _nbytes((N_EXPERTS, d), F32) + 2 * _nbytes((V7X_SUBLANES, tm), F32)
    return pl.pallas_call(
        functools.partial(_ln_router_body, first_blocks=nb),
        grid=(t // tm,),
        in_specs=xspecs + [row, vec, vec, pl.BlockSpec((N_EXPERTS, d), lambda i: (0, 0)),
                           pl.BlockSpec((N_EXPERTS, 1), lambda i: (0, 0))],
        out_specs=[row, sel, sel],
        out_shape=[jax.ShapeDtypeStruct((t, d), F32), jax.ShapeDtypeStruct((TOP_K, t), jnp.int32),
                   jax.ShapeDtypeStruct((TOP_K, t), F32)],
        compiler_params=_params(("arbitrary",), blocks),
        name=name,
    )(*xs, mix, g.reshape(1, d), b.reshape(1, d), w_router.T, b_router.reshape(N_EXPERTS, 1))


MOE_TILE = 256
COMBINE_TILE = 256


def _moe_plan(eidx):
    k, t = eidx.shape
    ns = k * t
    n_tiles = ns // MOE_TILE + N_EXPERTS
    e_flat = eidx.reshape(ns)
    order = jnp.argsort(e_flat, stable=True).astype(jnp.int32)
    rank = jnp.argsort(order).astype(jnp.int32)
    experts = jnp.arange(N_EXPERTS, dtype=jnp.int32)[None, :]
    slot_is = e_flat[:, None] == experts
    counts = jnp.sum(slot_is.astype(jnp.int32), axis=0)
    start = jnp.cumsum(counts) - counts
    tiles = (counts + MOE_TILE - 1) // MOE_TILE
    tile_end = jnp.cumsum(tiles)
    tile_start = tile_end - tiles
    tile_ids = jnp.arange(n_tiles, dtype=jnp.int32)
    tile_expert = jnp.minimum(jnp.sum((tile_ids[:, None] >= tile_end[None, :]).astype(jnp.int32), axis=1),
                              N_EXPERTS - 1)
    pick = lambda onehot, table: jnp.sum(jnp.where(onehot, table[None, :], 0), axis=1)
    tile_is = tile_expert[:, None] == experts
    tile_first = pick(tile_is, start) + (tile_ids - pick(tile_is, tile_start)) * MOE_TILE
    position_of_slot = pick(slot_is, tile_start) * MOE_TILE + rank - pick(slot_is, start)
    sorted_token = jnp.pad(order % t, (0, MOE_TILE))
    return (tile_expert.astype(jnp.int32), tile_first.astype(jnp.int32), tile_end[-1:].astype(jnp.int32),
            sorted_token.astype(jnp.int32), position_of_slot.astype(jnp.int32))


def _moe_expert_body(te_ref, first_ref, nt_ref, tok_ref, x_hbm, wg_ref, wu_ref, wd_ref, y_ref, xbuf, sem, wg_scr,
                     wu_scr, wd_scr):
    i = pl.program_id(0)
    nt = nt_ref[0]
    tm = MOE_TILE
    sub = V7X_SUBLANES

    def gather_copy(tok, slot, r8, j):
        return pltpu.make_async_copy(x_hbm.at[pl.ds(tok, 1)], xbuf.at[slot, r8, pl.ds(j, 1)], sem.at[slot])

    def start_gather(tile, slot):
        first = first_ref[tile]

        def body(r8, c):
            for j in range(sub):
                gather_copy(tok_ref[first + r8 * sub + j], slot, r8, j).start(priority=j % 2)
            return c

        lax.fori_loop(0, tm // sub, body, 0)

    @pl.when(i == 0)
    def _():
        start_gather(0, 0)

    @pl.when(i + 1 < nt)
    def _():
        start_gather(i + 1, (i + 1) % 2)

    @pl.when(i < nt)
    def _():
        slot = i % 2
        for r8 in range(tm // sub):
            pltpu.make_async_copy(x_hbm.at[pl.ds(0, sub)], xbuf.at[slot, r8], sem.at[slot]).wait()

        @pl.when((i == 0) | (te_ref[i] != te_ref[jnp.maximum(i - 1, 0)]))
        def _():
            wg_scr[...] = wg_ref[0, 0].astype(BF16)
            wu_scr[...] = wu_ref[0, 0].astype(BF16)
            wd_scr[...] = wd_ref[0, 0].astype(BF16)

        x = xbuf[slot].reshape(tm, x_hbm.shape[1]).astype(BF16)
        act = (_silu(_dot(x, wg_scr[...], NN)) * _dot(x, wu_scr[...], NN)).astype(BF16)
        y_ref[...] = _dot(act, wd_scr[...], NN)

    @pl.when(i >= nt)
    def _():
        y_ref[...] = jnp.zeros_like(y_ref)


def _moe_experts(x, plan, w_gate, w_up, w_down, layer):
    tile_expert, tile_first, n_valid, sorted_token, _ = plan
    t, d = x.shape
    f = w_gate.shape[3]
    n_tiles = tile_expert.shape[0]
    wspec = lambda a, b: pl.BlockSpec((1, 1, a, b), lambda i, te, *_: (layer, te[i], 0, 0))
    blocks = (2 * 3 * _nbytes((d, f), F32) + 3 * _nbytes((d, f), BF16) + 2 * _nbytes((MOE_TILE, d), F32)
              + 2 * _nbytes((MOE_TILE, d), F32) + 4 * _nbytes((MOE_TILE, f), F32))
    return pl.pallas_call(
        _moe_expert_body,
        grid_spec=pltpu.PrefetchScalarGridSpec(
            num_scalar_prefetch=4,
            grid=(n_tiles,),
            in_specs=[pl.BlockSpec(memory_space=pl.ANY), wspec(d, f), wspec(d, f), wspec(f, d)],
            out_specs=pl.BlockSpec((MOE_TILE, d), lambda i, *_: (i, 0)),
            scratch_shapes=[pltpu.VMEM((2, MOE_TILE // V7X_SUBLANES, V7X_SUBLANES, d), F32),
                            pltpu.SemaphoreType.DMA((2,)),
                            pltpu.VMEM((d, f), BF16), pltpu.VMEM((d, f), BF16), pltpu.VMEM((f, d), BF16)]),
        out_shape=jax.ShapeDtypeStruct((n_tiles * MOE_TILE, d), F32),
        compiler_params=pltpu.CompilerParams(dimension_semantics=("arbitrary",),
                                             vmem_limit_bytes=int(min(blocks + COMPILER_SCRATCH_BYTES, V7X_SCOPED_VMEM_CAP))),
        name="moe_experts",
    )(tile_expert, tile_first, n_valid, sorted_token, x, w_gate, w_up, w_down)


def _moe_combine_body(pos_ref, x_ref, w_ref, g_ref, b_ref, y_hbm, o_ref, ob_ref, ybuf, sem, *, n_tok, split_blocks):
    i = pl.program_id(0)
    tc = COMBINE_TILE

    sub = V7X_SUBLANES

    def gather_copy(p, slot, k, r8, j):
        return pltpu.make_async_copy(y_hbm.at[pl.ds(p, 1)], ybuf.at[slot, k, r8, pl.ds(j, 1)], sem.at[slot])

    def start_gather(tile, slot):
        def body(r8, c):
            for k in range(TOP_K):
                for j in range(sub):
                    gather_copy(pos_ref[k * n_tok + tile * tc + r8 * sub + j], slot, k, r8, j).start(priority=j % 2)
            return c

        lax.fori_loop(0, tc // sub, body, 0)

    @pl.when(i == 0)
    def _():
        start_gather(0, 0)

    @pl.when(i + 1 < pl.num_programs(0))
    def _():
        start_gather(i + 1, (i + 1) % 2)

    slot = i % 2
    for k in range(TOP_K):
        for r8 in range(tc // sub):
            pltpu.make_async_copy(y_hbm.at[pl.ds(0, sub)], ybuf.at[slot, k, r8], sem.at[slot]).wait()
    z = ALPHA * x_ref[...]
    for k in range(TOP_K):
        z = z + w_ref[:, k:k + 1] * ybuf[slot, k].reshape(tc, x_ref.shape[1])
    mu = jnp.mean(z, -1, keepdims=True)
    zc = z - mu
    var = jnp.mean(zc * zc, -1, keepdims=True)
    y = zc * lax.rsqrt(var + LN_EPS) * g_ref[...] + b_ref[...]
    if split_blocks is None:
        o_ref[...] = y
        ob_ref[...] = y.astype(BF16)
    else:
        @pl.when(i < split_blocks)
        def _():
            o_ref[...] = y

        @pl.when(i >= split_blocks)
        def _():
            ob_ref[...] = y


def _moe_combine_layernorm(x, y_slots, plan, gate_w, g, b, *, name, split=None):
    position_of_slot = plan[4]
    t, d = x.shape
    tc = COMBINE_TILE
    row = pl.BlockSpec((tc, d), lambda i, pos: (i, 0))
    vec = pl.BlockSpec((1, d), lambda i, pos: (0, 0))
    blocks = 2 * (3 * _nbytes((tc, d), F32) + _nbytes((tc, d), BF16)) + 2 * TOP_K * _nbytes((tc, d), F32)
    if split is None:
        sb = None
        out_specs = [row, row]
        out_shape = [jax.ShapeDtypeStruct((t, d), F32), jax.ShapeDtypeStruct((t, d), BF16)]
    else:
        sb = split // tc
        out_specs = [pl.BlockSpec((tc, d), lambda i, pos: (jnp.minimum(i, sb - 1), 0)),
                     pl.BlockSpec((tc, d), lambda i, pos: (jnp.maximum(i - sb, 0), 0))]
        out_shape = [jax.ShapeDtypeStruct((split, d), F32), jax.ShapeDtypeStruct((t - split, d), F32)]
    return pl.pallas_call(
        functools.partial(_moe_combine_body, n_tok=t, split_blocks=sb),
        grid_spec=pltpu.PrefetchScalarGridSpec(
            num_scalar_prefetch=1,
            grid=(t // tc,),
            in_specs=[row, pl.BlockSpec((tc, TOP_K), lambda i, pos: (i, 0)), vec, vec,
                      pl.BlockSpec(memory_space=pl.ANY)],
            out_specs=out_specs,
            scratch_shapes=[pltpu.VMEM((2, TOP_K, tc // V7X_SUBLANES, V7X_SUBLANES, d), F32),
                            pltpu.SemaphoreType.DMA((2,))]),
        out_shape=out_shape,
        compiler_params=pltpu.CompilerParams(dimension_semantics=("arbitrary",),
                                             vmem_limit_bytes=int(min(blocks + COMPILER_SCRATCH_BYTES, V7X_SCOPED_VMEM_CAP))),
        name=name,
    )(position_of_slot, x, gate_w, g.reshape(1, d), b.reshape(1, d), y_slots)


TOKEN_TILE = 1088
IN_PROJ_TN = 1024
LN_TILE = 512


def _moe_layernorm(x, eidx, gate_w, layer, w_gate, w_up, w_down, g, b, split=None):
    plan = _moe_plan(eidx)
    y_slots = _moe_experts(x, plan, w_gate, w_up, w_down, layer)
    return _moe_combine_layernorm(x, y_slots, plan, gate_w.T, g, b, name=f"moe_combine_ln_{layer}", split=split)


def kernel(x_prompt, x_sample, state_hgrn, state_ssm, state_conv, state_s5_re, state_s5_im, w_in0, hg_lb_logits, hg_norm_w, conv_w, conv_b, dt_bias, a_log, m_d, m_norm_w, w_out0, s5_a_re, s5_a_im, s5_log_dt, s5_b_re, s5_b_im, s5_c_re, s5_c_im, s5_d, glu_w_a, glu_b_a, glu_w_b, glu_b_b, w_router, b_router, w_gate, w_up, w_down, ln1_g, ln1_b, ln2_g, ln2_b):
    bp, lp, d = x_prompt.shape
    bs, ls, _ = x_sample.shape
    tp, ts = bp * lp, bs * ls
    tm = TOKEN_TILE
    xp2, xs2 = x_prompt.reshape(tp, d), x_sample.reshape(ts, d)
    w_dt = jnp.pad(w_in0[0, :, IN0_MAIN:], ((0, 0), (0, V7X_LANES - M_HEADS))).astype(BF16)
    x0b, dt = _join_bf16(xp2, xs2, w_dt, tm=LN_TILE)
    lower_bounds = jnp.cumsum(jax.nn.softmax(hg_lb_logits.astype(F32), axis=0), axis=0)
    pad_lanes = lambda v: jnp.pad(v, (0, V7X_LANES - v.shape[0])).reshape(1, V7X_LANES)

    proj = _matmul_w32(x0b, jnp.swapaxes(w_in0, 1, 2), 0, tm=tm, tn=IN_PROJ_TN, n_cols=IN0_MAIN,
                       transposed=True, name="in_proj")
    lb0 = lower_bounds[0]
    e2 = _head_expand_matrix()
    ssd_consts = (conv_w[0], conv_b[0].reshape(1, -1), pad_lanes(dt_bias[0]), pad_lanes(a_log[0]),
                  jnp.repeat(m_d[0], M_HEADDIM).reshape(1, M_DI), m_norm_w[0].reshape(1, M_DI), e2)
    hgrn_prompt = functools.partial(_hgrn_prompt, proj, lb0, hg_norm_w[0], batch=bp, seq=lp)
    mixed, hg_p = lax.cond(_hgrn_lower_bound_is_safe(lb0), functools.partial(hgrn_prompt, guarded=False),
                           functools.partial(hgrn_prompt, guarded=True))
    mixed, hg_s = _hgrn_sample(proj, tp, state_hgrn[0], lb0, hg_norm_w[0], batch=bs, seq=ls, prev=mixed)
    mixed, ssm_p, conv_p = _ssd_prompt(proj, dt, *ssd_consts, batch=bp, seq=lp, prev=mixed)
    mixed, ssm_s, conv_s = _ssd_sample(proj, tp, dt, state_ssm[0], state_conv[0], *ssd_consts, batch=bs, seq=ls,
                                       prev=mixed)
    mix = _matmul_w32(mixed, w_out0, 0, tm=tm, tn=512, name="out_proj")
    x1, eidx, gate_w = _res_layernorm_router(xp2, mix, ln1_g[0], ln1_b[0], w_router, b_router, tm=LN_TILE,
                                             name="ln1_router_0", x_tail=xs2)
    x2, x2b = _moe_layernorm(x1, eidx, gate_w, 0, w_gate, w_up, w_down, ln2_g[0], ln2_b[0])

    ab_re, ab_im, wb_re, wb_im, wc_re, wc_im = _s5_discretize(
        s5_a_re[0], s5_a_im[0], s5_log_dt[0], s5_b_re[0], s5_b_im[0], s5_c_re[0], s5_c_im[0])
    s5_consts = (ab_re, ab_im, wb_re, wb_im, wc_re, wc_im, s5_d[0])
    zeros = jnp.zeros((bp, S5_NBLK, S5_TILES, V7X_LANES), F32)
    act, s5r_p, s5i_p = _s5_fused(x2b, x2, zeros, zeros, *s5_consts, row0=0, batch=bp, seq=lp,
                                  rows=S5_PROMPT_ROWS, n_seq=1, name="s5_prompt")
    act, s5r_s, s5i_s = _s5_fused(x2b, x2, _s5_slab_layout(state_s5_re[0].reshape(bs, S5_STATE)),
                                  _s5_slab_layout(state_s5_im[0].reshape(bs, S5_STATE)), *s5_consts,
                                  row0=tp, batch=bs, seq=ls, rows=S5_SAMPLE_ROWS, n_seq=S5_SAMPLE_ROWS // ls,
                                  prev=act, name="s5_sample")
    mix1 = _glu(act, glu_w_a, glu_w_b, glu_b_a, glu_b_b, 0, tm=tm, tn=512)
    x3, eidx, gate_w = _res_layernorm_router(x2, mix1, ln1_g[1], ln1_b[1], w_router, b_router, tm=LN_TILE,
                                             name="ln1_router_1")
    y_p, y_s = _moe_layernorm(x3, eidx, gate_w, 1, w_gate, w_up, w_down, ln2_g[1], ln2_b[1], split=tp)

    s5_state = lambda a, b: _s5_from_slab_layout(a).reshape(1, b, S5_GROUPS, S5_P)
    return (y_p.reshape(bp, lp, d), y_s.reshape(bs, ls, d),
            hg_p[None], hg_s[None], ssm_p[None], ssm_s[None], conv_p[None], conv_s[None],
            s5_state(s5r_p, bp), s5_state(s5r_s, bs), s5_state(s5i_p, bp), s5_state(s5i_s, bs))
```
